```python
import jax, jax.numpy as jnp
from jax import lax
import numpy as np

D_MODEL = 1024
BATCH = 4
SEQ = 8192
DEPTH = 2

CHUNK = 64
EPS = 1e-6

A_HEADS = 8
A_HEAD_DIM = 64
A_WIDTH = A_HEADS * A_HEAD_DIM
A_LEFT_CHUNKS = 8
A_MAX_REL = 128

B_HEADS = 8
B_KEY_DIM = 64
B_VAL_DIM = 64
B_KWIDTH = B_HEADS * B_KEY_DIM
B_VWIDTH = B_HEADS * B_VAL_DIM

C_HEADS = 8
C_Q_RANK = 256
C_KV_RANK = 128
C_NOPE = 64
C_ROPE = 32
C_V = 64
C_WIDTH = C_HEADS * C_V
C_QBLOCK = 128
ROPE_BASE = 10000.0

N_BRANCH = 3
BRANCH_WIDTH = 512
D_FF = 4 * D_MODEL

IN_SPLIT_SIZES = (A_WIDTH, A_WIDTH, A_WIDTH,
                  B_KWIDTH, B_KWIDTH, B_VWIDTH, B_VWIDTH,
                  C_Q_RANK, C_KV_RANK, C_ROPE,
                  N_BRANCH * D_MODEL)
IN_COLS = sum(IN_SPLIT_SIZES)

kernel_name = 'hybrid_gated_streaming_block'


def rmsnorm(x, g):
    x32 = x.astype(jnp.float32)
    y = x32 * lax.rsqrt(jnp.mean(jnp.square(x32), axis=-1, keepdims=True) + EPS)
    return (y * g.astype(jnp.float32)).astype(x.dtype)


def rope(x, cos, sin):
    x1, x2 = jnp.split(x.astype(jnp.float32), 2, axis=-1)
    return jnp.concatenate([x1 * cos - x2 * sin, x2 * cos + x1 * sin], axis=-1).astype(x.dtype)


def chunked_relpos_attention(q, k, v, rel_table):
    bsz, seq, heads, dh = q.shape
    n_chunks = seq // CHUNK
    pad = A_LEFT_CHUNKS * CHUNK
    band = pad + CHUNK
    k_pad = jnp.pad(k, ((0, 0), (pad, 0), (0, 0), (0, 0)))
    v_pad = jnp.pad(v, ((0, 0), (pad, 0), (0, 0), (0, 0)))
    qi = jnp.arange(CHUNK)[:, None]
    kj = jnp.arange(band)[None, :]
    rel = jnp.clip(qi + pad - kj, -A_MAX_REL, A_MAX_REL) + A_MAX_REL
    bias = rel_table.astype(jnp.float32)[:, rel]
    q_chunks = q.reshape(bsz, n_chunks, CHUNK, heads, dh).transpose(1, 0, 2, 3, 4)
    scale = dh ** -0.5

    def one_chunk(args):
        c, q_blk = args
        k_band = lax.dynamic_slice_in_dim(k_pad, c * CHUNK, band, axis=1)
        v_band = lax.dynamic_slice_in_dim(v_pad, c * CHUNK, band, axis=1)
        s = jnp.einsum('bqhd,bkhd->bhqk', q_blk, k_band).astype(jnp.float32) * scale + bias
        valid = kj >= pad - c * CHUNK
        s = jnp.where(valid[None, None], s, -jnp.inf)
        p = jax.nn.softmax(s, axis=-1).astype(v.dtype)
        return jnp.einsum('bhqk,bkhd->bqhd', p, v_band)

    out = lax.map(one_chunk, (jnp.arange(n_chunks), q_chunks))
    return out.transpose(1, 0, 2, 3, 4).reshape(bsz, seq, heads * dh)


def hgrn2_recurrence(q, f_logit, i_val, g, lower_bound, norm_g):
    bsz, seq, _ = q.shape
    n_chunks = seq // CHUNK
    f32 = jnp.float32

    def to_chunks(t, d):
        return t.astype(f32).reshape(bsz, n_chunks, CHUNK, B_HEADS, d).transpose(1, 0, 3, 2, 4)

    z = to_chunks(f_logit, B_KEY_DIM)
    lb = lower_bound.reshape(1, B_HEADS, 1, B_KEY_DIM)
    log_f = jnp.logaddexp(jnp.log(lb), jnp.log1p(-lb) + jax.nn.log_sigmoid(z))
    k = (1.0 - lb) * jax.nn.sigmoid(-z)
    qc = jax.nn.silu(to_chunks(q, B_KEY_DIM))
    vc = to_chunks(i_val, B_VAL_DIM)
    causal = jnp.tril(jnp.ones((CHUNK, CHUNK), dtype=bool))[None, None, :, :, None]

    def step(state, blk):
        q_b, k_b, lf_b, v_b = blk
        cum = jnp.cumsum(lf_b, axis=2)
        diff = cum[:, :, :, None, :] - cum[:, :, None, :, :]
        decay = jnp.exp(jnp.where(causal, diff, -jnp.inf))
        scores = jnp.einsum('bhid,bhjd,bhijd->bhij', q_b, k_b, decay)
        o = (jnp.einsum('bhij,bhjv->bhiv', scores, v_b)
             + jnp.einsum('bhid,bhdv->bhiv', q_b * jnp.exp(cum), state))
        last = cum[:, :, -1:, :]
        state = (jnp.exp(last[:, :, 0, :])[..., None] * state
                 + jnp.einsum('bhjd,bhjv->bhdv', k_b * jnp.exp(last - cum), v_b))
        return state, o

    s0 = jnp.zeros((bsz, B_HEADS, B_KEY_DIM, B_VAL_DIM), f32)
    _, o = lax.scan(step, s0, (qc, k, log_f, vc))
    o = o.transpose(1, 0, 3, 2, 4).reshape(bsz, seq, B_HEADS, B_VAL_DIM)
    o = rmsnorm(o, norm_g) * jax.nn.silu(g.astype(f32).reshape(bsz, seq, B_HEADS, B_VAL_DIM))
    return o.reshape(bsz, seq, B_VWIDTH).astype(q.dtype)


def mla_attention(c_q, c_kv, k_rope_raw, positions, q_norm_g, kv_norm_g, w_uq, w_ukv):
    bsz, seq, _ = c_q.shape
    q = (rmsnorm(c_q, q_norm_g) @ w_uq).reshape(bsz, seq, C_HEADS, C_NOPE + C_ROPE)
    kv = (rmsnorm(c_kv, kv_norm_g) @ w_ukv).reshape(bsz, seq, C_HEADS, C_NOPE + C_V)
    q_nope, q_rope = q[..., :C_NOPE], q[..., C_NOPE:]
    k_nope, v = kv[..., :C_NOPE], kv[..., C_NOPE:]
    inv_freq = ROPE_BASE ** (-jnp.arange(0, C_ROPE, 2, dtype=jnp.float32) / C_ROPE)
    ang = positions.astype(jnp.float32)[..., None] * inv_freq
    cos, sin = jnp.cos(ang)[:, :, None, :], jnp.sin(ang)[:, :, None, :]
    q_rope = rope(q_rope, cos, sin)
    k_rope = rope(k_rope_raw[:, :, None, :], cos, sin)[:, :, 0, :]
    n_blocks = seq // C_QBLOCK

    def blocks(t):
        return t.reshape(bsz, n_blocks, C_QBLOCK, C_HEADS, t.shape[-1]).transpose(1, 0, 2, 3, 4)

    key_chunk = jnp.arange(seq) // CHUNK
    scale = (C_NOPE + C_ROPE) ** -0.5

    def one_block(args):
        blk, qn, qr = args
        s = (jnp.einsum('bqhd,bkhd->bhqk', qn, k_nope)
             + jnp.einsum('bqhr,bkr->bhqk', qr, k_rope)).astype(jnp.float32) * scale
        q_chunk = (blk * C_QBLOCK + jnp.arange(C_QBLOCK)) // CHUNK
        mask = key_chunk[None, :] <= q_chunk[:, None]
        s = jnp.where(mask[None, None], s, -jnp.inf)
        p = jax.nn.softmax(s, axis=-1).astype(v.dtype)
        return jnp.einsum('bhqk,bkhd->bqhd', p, v)

    out = lax.map(one_block, (jnp.arange(n_blocks), blocks(q_nope), blocks(q_rope)))
    return out.transpose(1, 0, 2, 3, 4).reshape(bsz, seq, C_WIDTH)


def setup_inputs(seed: int = 0) -> dict:
    key = jax.random.key(seed)
    ks = jax.random.split(key, 20)
    f32 = jnp.float32

    def nrm(k, shape, fan_in):
        return jax.random.normal(k, shape, f32) * (fan_in ** -0.5)

    def gain(k, shape):
        return 1.0 + 0.05 * jax.random.normal(k, shape, f32)

    x = jax.random.normal(ks[0], (BATCH, SEQ, D_MODEL), f32)
    offset = jax.random.randint(ks[1], (BATCH, 1), 0, 4096, dtype=jnp.int32)
    positions = offset + jnp.arange(SEQ, dtype=jnp.int32)[None, :]
    return {
        'x': x,
        'positions': positions,
        'norm_mix_g': gain(ks[2], (DEPTH, D_MODEL)),
        'w_in': nrm(ks[3], (DEPTH, D_MODEL, IN_COLS), D_MODEL),
        'rel_bias': 0.2 * jax.random.normal(ks[4], (DEPTH, A_HEADS, 2 * A_MAX_REL + 1), f32),
        'hgrn_lb_logits': jax.random.normal(ks[5], (DEPTH, B_KWIDTH), f32),
        'hgrn_norm_g': gain(ks[6], (DEPTH, B_VAL_DIM)),
        'mla_q_norm_g': gain(ks[7], (DEPTH, C_Q_RANK)),
        'mla_kv_norm_g': gain(ks[8], (DEPTH, C_KV_RANK)),
        'mla_w_uq': nrm(ks[9], (DEPTH, C_Q_RANK, C_HEADS * (C_NOPE + C_ROPE)), C_Q_RANK),
        'mla_w_ukv': nrm(ks[10], (DEPTH, C_KV_RANK, C_HEADS * (C_NOPE + C_V)), C_KV_RANK),
        'w_branch': nrm(ks[11], (DEPTH, N_BRANCH, BRANCH_WIDTH, D_MODEL), BRANCH_WIDTH),
        'w_out': nrm(ks[12], (DEPTH, D_MODEL, D_MODEL), D_MODEL),
        'norm_ffn_g': gain(ks[13], (DEPTH, D_MODEL)),
        'w_ff1': nrm(ks[14], (DEPTH, D_MODEL, D_FF), D_MODEL),
        'w_ff2': nrm(ks[15], (DEPTH, D_FF, D_MODEL), D_FF),
        'final_norm_g': gain(ks[16], (D_MODEL,)),
    }


def reference(x, positions, norm_mix_g, w_in, rel_bias, hgrn_lb_logits, hgrn_norm_g,
              mla_q_norm_g, mla_kv_norm_g, mla_w_uq, mla_w_ukv, w_branch, w_out,
              norm_ffn_g, w_ff1, w_ff2, final_norm_g):
    bsz, seq, _ = x.shape
    p_lb = jax.nn.softmax(hgrn_lb_logits.astype(jnp.float32), axis=0)
    lb_all = jnp.cumsum(p_lb, axis=0)
    lb_all = lb_all - lb_all[0:1]

    for l in range(DEPTH):
        h = rmsnorm(x, norm_mix_g[l])
        proj = h @ w_in[l]
        parts = []
        start = 0
        for size in IN_SPLIT_SIZES:
            parts.append(proj[..., start:start + size])
            start += size
        a_q, a_k, a_v, b_q, b_f, b_i, b_g, c_q, c_kv, c_kr, gate_logits = parts

        y_a = chunked_relpos_attention(
            a_q.reshape(bsz, seq, A_HEADS, A_HEAD_DIM),
            a_k.reshape(bsz, seq, A_HEADS, A_HEAD_DIM),
            a_v.reshape(bsz, seq, A_HEADS, A_HEAD_DIM),
            rel_bias[l])
        y_b = hgrn2_recurrence(b_q, b_f, b_i, b_g, lb_all[l], hgrn_norm_g[l])
        y_c = mla_attention(c_q, c_kv, c_kr, positions, mla_q_norm_g[l], mla_kv_norm_g[l],
                            mla_w_uq[l], mla_w_ukv[l])

        branches = jnp.stack([y_a.astype(h.dtype), y_b.astype(h.dtype), y_c.astype(h.dtype)], axis=2)
        up = jnp.einsum('bsnw,nwd->bsnd', branches, w_branch[l])
        gates = jax.nn.sigmoid(gate_logits.reshape(bsz, seq, N_BRANCH, D_MODEL))
        merged = jnp.sum(gates * up, axis=2)
        x = x + merged @ w_out[l]

        h2 = rmsnorm(x, norm_ffn_g[l])
        x = x + jnp.square(jax.nn.relu(h2 @ w_ff1[l])) @ w_ff2[l]

    return rmsnorm(x, final_norm_g)
```

```python
import functools
import math

import jax
import jax.numpy as jnp
import numpy as np
from jax import lax
from jax.experimental import pallas as pl
from jax.experimental.pallas import tpu as pltpu

F32 = jnp.float32
BF16 = jnp.bfloat16

D_MODEL = 1024
CHUNK = 64
EPS = 1e-6
N_HEADS = 8
HEAD_DIM = 64
LANES = 128
N_PAIRS = N_HEADS * HEAD_DIM // LANES
WIDTH = N_HEADS * HEAD_DIM

A_LEFT_CHUNKS = 8
A_MAX_REL = 128
A_QTILE = 2 * CHUNK
A_BAND = (A_LEFT_CHUNKS + 2) * CHUNK
A_PAD = A_LEFT_CHUNKS * CHUNK

C_Q_RANK = 256
C_KV_RANK = 128
C_ROPE = 32
C_NOPE = 64
C_SLOT = LANES
ROPE_BASE = 10000.0
C_TILE = 512

D_FF = 4 * D_MODEL
N_BRANCH = 3

SUB = 16
N_SUB = CHUNK // SUB
NEG_BIG = -1e30
LOG2E = 1.4426950408889634

VMEM_LIMIT = 48 * 1024 * 1024


def _cparams(sem):
    return pltpu.CompilerParams(dimension_semantics=sem, vmem_limit_bytes=VMEM_LIMIT)


def _rms(x, g):
    return x * lax.rsqrt(jnp.mean(x * x, axis=-1, keepdims=True) + EPS) * g


def _norm_matmul_kernel(x_ref, g_ref, w_ref, o_ref, h_ref):
    @pl.when(pl.program_id(1) == 0)
    def _():
        h_ref[...] = _rms(x_ref[...], g_ref[...]).astype(BF16)

    o_ref[...] = jnp.dot(h_ref[...], w_ref[...], preferred_element_type=F32).astype(o_ref.dtype)


def norm_matmul(x, g, w, out_dtype, tm, tn, name):
    t, d = x.shape
    n = w.shape[1]
    return pl.pallas_call(
        _norm_matmul_kernel,
        out_shape=jax.ShapeDtypeStruct((t, n), out_dtype),
        grid=(t // tm, n // tn),
        in_specs=[
            pl.BlockSpec((tm, d), lambda i, j: (i, 0)),
            pl.BlockSpec((1, d), lambda i, j: (0, 0)),
            pl.BlockSpec((d, tn), lambda i, j: (0, j)),
        ],
        out_specs=pl.BlockSpec((tm, tn), lambda i, j: (i, j)),
        scratch_shapes=[pltpu.VMEM((tm, d), BF16)],
        compiler_params=_cparams(("parallel", "arbitrary")),
        name=name,
    )(x, g.reshape(1, d), w)


def _trig_kernel(pos_ref, invf_ref, c_ref, s_ref):
    ang = pos_ref[...].astype(F32) * invf_ref[...]
    c_ref[...] = jnp.cos(ang)
    s_ref[...] = jnp.sin(ang)


def rope_tables(positions, tm):
    t = positions.size
    inv_freq = ROPE_BASE ** (-jnp.arange(0, C_ROPE, 2, dtype=F32) / C_ROPE)
    half = C_ROPE // 2
    invf = jnp.zeros((LANES,), F32)
    invf = invf.at[C_NOPE:C_NOPE + half].set(inv_freq).at[C_NOPE + half:C_NOPE + C_ROPE].set(inv_freq)
    return pl.pallas_call(
        _trig_kernel,
        out_shape=(jax.ShapeDtypeStruct((t, LANES), F32),) * 2,
        grid=(t // tm,),
        in_specs=[
            pl.BlockSpec((tm, 1), lambda i: (i, 0)),
            pl.BlockSpec((1, LANES), lambda i: (0, 0)),
        ],
        out_specs=(pl.BlockSpec((tm, LANES), lambda i: (i, 0)),) * 2,
        compiler_params=_cparams(("parallel",)),
        name="rope_tables",
    )(positions.reshape(t, 1), invf.reshape(1, LANES))


def _band_attn_kernel(q_ref, k_ref, v_ref, bias_ref, o_ref):
    c2 = pl.program_id(2)
    start = pl.multiple_of(jnp.maximum(c2 * A_QTILE - A_PAD, 0), A_QTILE)
    kwin = k_ref[pl.ds(start, A_BAND), :]
    vwin = v_ref[pl.ds(start, A_BAND), :]
    q = q_ref[...].astype(F32) * (HEAD_DIM ** -0.5)
    lane = lax.broadcasted_iota(jnp.int32, (A_QTILE, LANES), 1)
    outs = []
    for e in range(2):
        qe = (jnp.where(lane < HEAD_DIM, q, 0.0) if e == 0 else jnp.where(lane < HEAD_DIM, 0.0, q)).astype(BF16)
        s = lax.dot_general(qe, kwin, (((1,), (1,)), ((), ())), preferred_element_type=F32)
        s = s + bias_ref[0, e]
        m = jnp.max(s, axis=-1, keepdims=True)
        p = jnp.exp(s - m)
        l = jnp.sum(p, axis=-1, keepdims=True)
        pv = jnp.dot(p.astype(BF16), vwin, preferred_element_type=F32)
        outs.append(pv / l)
    o_ref[...] = jnp.where(lane < HEAD_DIM, outs[0], outs[1]).astype(o_ref.dtype)


def band_attention(proj, bias, bsz, seq, q_col, k_col, v_col):
    t = bsz * seq
    n_qt = seq // A_QTILE
    n_shift = A_PAD // A_QTILE
    return pl.pallas_call(
        _band_attn_kernel,
        out_shape=jax.ShapeDtypeStruct((t, WIDTH), BF16),
        grid=(bsz, N_PAIRS, n_qt),
        in_specs=[
            pl.BlockSpec((A_QTILE, LANES), lambda b, p, c: (b * n_qt + c, q_col + p)),
            pl.BlockSpec((seq, LANES), lambda b, p, c: (b, k_col + p)),
            pl.BlockSpec((seq, LANES), lambda b, p, c: (b, v_col + p)),
            pl.BlockSpec((1, 2, A_QTILE, A_BAND), lambda b, p, c: (jnp.minimum(c, n_shift), p, 0, 0)),
        ],
        out_specs=pl.BlockSpec((A_QTILE, LANES), lambda b, p, c: (b * n_qt + c, p)),
        compiler_params=_cparams(("parallel", "parallel", "arbitrary")),
        name="band_attention",
    )(proj, proj, proj, bias)


def band_bias_table(rel_table):
    n_shift = A_PAD // A_QTILE
    i = np.arange(A_QTILE)[:, None]
    j = np.arange(A_BAND)[None, :]
    idx, valid = [], []
    for sh in range(n_shift + 1):
        d = sh * A_QTILE
        idx.append(np.clip(d + i - j, -A_MAX_REL, A_MAX_REL) + A_MAX_REL)
        gap = (d + i) // CHUNK - j // CHUNK
        valid.append((gap >= 0) & (gap <= A_LEFT_CHUNKS))
    idx = jnp.asarray(np.stack(idx))
    valid = jnp.asarray(np.stack(valid))
    b = rel_table.astype(F32)[:, idx]
    b = jnp.where(valid[None], b, NEG_BIG)
    return b.transpose(1, 0, 2, 3)


def _split3(x):
    hi = x.astype(BF16)
    r1 = x - hi.astype(F32)
    mid = r1.astype(BF16)
    lo = (r1 - mid.astype(F32)).astype(BF16)
    return hi, mid, lo


def _hgrn_kernel(q_ref, f_ref, i_ref, g_ref, loglb_ref, log1mlb_ref, onemlb_ref, ng_ref,
                 o_ref, st_ref, *, chunks_per_step):
    @pl.when(pl.program_id(2) == 0)
    def _():
        st_ref[...] = jnp.zeros_like(st_ref)

    log_lb = loglb_ref[...]
    log1m_lb = log1mlb_ref[...]
    one_m_lb = onemlb_ref[...]
    norm_g = ng_ref[...]

    lane_c = lax.broadcasted_iota(jnp.int32, (CHUNK, LANES), 1)
    row_c = lax.broadcasted_iota(jnp.int32, (CHUNK, LANES), 0)
    head0_c = lane_c < HEAD_DIM
    r64 = lax.broadcasted_iota(jnp.int32, (CHUNK, CHUNK), 0)
    c64 = lax.broadcasted_iota(jnp.int32, (CHUNK, CHUNK), 1)
    tril = (c64 <= r64).astype(BF16)
    rl = lax.broadcasted_iota(jnp.int32, (LANES, LANES), 0)
    cl = lax.broadcasted_iota(jnp.int32, (LANES, LANES), 1)
    same_head = (rl // HEAD_DIM) == (cl // HEAD_DIM)
    bd = same_head.astype(BF16)
    ri = lax.broadcasted_iota(jnp.int32, (CHUNK, CHUNK * SUB), 0)
    ci = lax.broadcasted_iota(jnp.int32, (CHUNK, CHUNK * SUB), 1)
    pick = (ci // SUB == ri).astype(BF16)
    row_s = lax.broadcasted_iota(jnp.int32, (SUB, LANES), 0)

    def chunk_body(c, carry):
        r0 = pl.multiple_of(c * CHUNK, CHUNK)
        rows = pl.ds(r0, CHUNK)
        z = f_ref[rows, :]
        qraw = q_ref[rows, :].astype(F32)
        v = i_ref[rows, :].astype(F32)
        graw = g_ref[rows, :].astype(F32)

        log_sig = jnp.minimum(z, 0.0) - jnp.log1p(jnp.exp(-jnp.abs(z)))
        bterm = log1m_lb + log_sig
        log_f = jnp.maximum(log_lb, bterm) + jnp.log1p(jnp.exp(-jnp.abs(log_lb - bterm)))
        ks = one_m_lb * (1.0 / (1.0 + jnp.exp(z)))
        qs = qraw * (1.0 / (1.0 + jnp.exp(-qraw)))

        hi, mid, lo = _split3(log_f)
        cum = (jnp.dot(tril, hi, preferred_element_type=F32)
               + jnp.dot(tril, mid, preferred_element_type=F32)
               + jnp.dot(tril, lo, preferred_element_type=F32))

        zero_row = jnp.zeros((1, LANES), F32)
        c_start = [zero_row] + [cum[SUB * i - 1:SUB * i, :] for i in range(1, N_SUB)]
        c_end = [cum[SUB * i + SUB - 1:SUB * i + SUB, :] for i in range(N_SUB)]
        cstart_full = jnp.concatenate([jnp.broadcast_to(r, (SUB, LANES)) for r in c_start], axis=0)
        cend_full = jnp.concatenate([jnp.broadcast_to(r, (SUB, LANES)) for r in c_end], axis=0)

        q1 = qs * jnp.exp(cum - cstart_full)
        k2 = ks * jnp.exp(cend_full - cum)
        k2b = k2.astype(BF16)
        vb0 = jnp.where(head0_c, v, 0.0).astype(BF16)
        vb1 = jnp.where(head0_c, 0.0, v).astype(BF16)

        sc = [None, None]
        for j in range(N_SUB - 1):
            parts = []
            for i in range(N_SUB):
                if i > j:
                    parts.append(jnp.broadcast_to(jnp.exp(c_start[i] - c_end[j]), (SUB, LANES)))
                else:
                    parts.append(jnp.zeros((SUB, LANES), F32))
            qj = q1 * jnp.concatenate(parts, axis=0)
            col_in_j = (c64 >= j * SUB) & (c64 < (j + 1) * SUB)
            for e in range(2):
                qje = (jnp.where(head0_c, qj, 0.0) if e == 0 else jnp.where(head0_c, 0.0, qj)).astype(BF16)
                s = lax.dot_general(qje, k2b, (((1,), (1,)), ((), ())), preferred_element_type=F32)
                s = jnp.where(col_in_j, s, 0.0)
                sc[e] = s if sc[e] is None else sc[e] + s
        o = (jnp.dot(sc[0].astype(BF16), vb0, preferred_element_type=F32)
             + jnp.dot(sc[1].astype(BF16), vb1, preferred_element_type=F32))

        w_rows = []
        for blk in range(N_SUB):
            cum_b = cum[SUB * blk:SUB * (blk + 1), :]
            ks_b = ks[SUB * blk:SUB * (blk + 1), :]
            for i in range(SUB):
                r = SUB * blk + i
                diff = jnp.minimum(cum[r:r + 1, :] - cum_b, 0.0)
                w = jnp.exp(diff) * (ks_b * qs[r:r + 1, :])
                w_rows.append(jnp.where(row_s <= i, w, 0.0).astype(BF16))
        w_all = jnp.concatenate(w_rows, axis=0)
        sb = jnp.dot(w_all, bd, preferred_element_type=F32)
        v_rep = jnp.concatenate(
            [v[SUB * (r // SUB):SUB * (r // SUB + 1), :] for r in range(CHUNK)], axis=0)
        o = o + jnp.dot(pick, (sb * v_rep).astype(BF16), preferred_element_type=F32)

        st = st_ref[...]
        q_state = (q1 * jnp.exp(cstart_full)).astype(BF16)
        o = o + lax.dot_general(q_state, st.astype(BF16), (((1,), (1,)), ((), ())),
                                preferred_element_type=F32)
        c_last = c_end[N_SUB - 1]
        k_end = (k2 * jnp.exp(c_last - cend_full)).astype(BF16)
        upd = lax.dot_general(v.astype(BF16), k_end, (((0,), (0,)), ((), ())),
                              preferred_element_type=F32)
        st_ref[...] = st * jnp.exp(c_last) + jnp.where(same_head, upd, 0.0)

        osq = o * o
        ms0 = jnp.sum(jnp.where(head0_c, osq, 0.0), axis=-1, keepdims=True)
        ms1 = jnp.sum(jnp.where(head0_c, 0.0, osq), axis=-1, keepdims=True)
        ms = jnp.where(head0_c, ms0, ms1) * (1.0 / HEAD_DIM)
        y = o * lax.rsqrt(ms + EPS) * norm_g
        y = y * (graw * (1.0 / (1.0 + jnp.exp(-graw))))
        o_ref[rows, :] = y.astype(o_ref.dtype)
        return carry

    lax.fori_loop(0, chunks_per_step, chunk_body, 0)


def hgrn2(proj16, proj32, lb, norm_g, bsz, seq, q_col, i_col, g_col, f_col, rows_per_step):
    t = bsz * seq
    n_steps = seq // rows_per_step
    log_lb = jnp.log(lb).reshape(N_PAIRS, 1, LANES)
    log1m_lb = jnp.log1p(-lb).reshape(N_PAIRS, 1, LANES)
    one_m_lb = (1.0 - lb).reshape(N_PAIRS, 1, LANES)
    ng = jnp.tile(norm_g.astype(F32), LANES // HEAD_DIM).reshape(1, LANES)

    def act(col):
        return pl.BlockSpec((rows_per_step, LANES), lambda b, p, s: (b * n_steps + s, col + p))

    def par():
        return pl.BlockSpec((None, 1, LANES), lambda b, p, s: (p, 0, 0))

    return pl.pallas_call(
        functools.partial(_hgrn_kernel, chunks_per_step=rows_per_step // CHUNK),
        out_shape=jax.ShapeDtypeStruct((t, WIDTH), BF16),
        grid=(bsz, N_PAIRS, n_steps),
        in_specs=[act(q_col), act(f_col), act(i_col), act(g_col), par(), par(), par(),
                  pl.BlockSpec((1, LANES), lambda b, p, s: (0, 0))],
        out_specs=pl.BlockSpec((rows_per_step, LANES), lambda b, p, s: (b * n_steps + s, p)),
        scratch_shapes=[pltpu.VMEM((LANES, LANES), F32)],
        compiler_params=_cparams(("parallel", "parallel", "arbitrary")),
        name="hgrn2",
    )(proj16, proj32, proj16, proj16, log_lb, log1m_lb, one_m_lb, ng)


def _mla_prep_kernel(pc_ref, ct_ref, st_ref, gq_ref, gkv_ref, wq1_ref, wq2_ref, wk_ref, wv_ref,
                     q_ref, k_ref, v_ref):
    pc = pc_ref[...]
    qn = _rms(pc[:, :C_Q_RANK], gq_ref[...]).astype(BF16)
    kvn = _rms(pc[:, C_Q_RANK:C_Q_RANK + C_KV_RANK], gkv_ref[...]).astype(BF16)
    kr = pc[:, C_Q_RANK + C_KV_RANK:C_Q_RANK + C_KV_RANK + C_SLOT]
    kr_sw = pc[:, C_Q_RANK + C_KV_RANK + C_SLOT:C_Q_RANK + C_KV_RANK + 2 * C_SLOT]
    ct = ct_ref[...]
    st = st_ref[...]
    a = jnp.dot(qn, wq1_ref[...], preferred_element_type=F32)
    b = jnp.dot(qn, wq2_ref[...], preferred_element_type=F32)
    kn = jnp.dot(kvn, wk_ref[...], preferred_element_type=F32)
    v_ref[...] = jnp.dot(kvn, wv_ref[...], preferred_element_type=F32).astype(v_ref.dtype)
    kr_rot = kr * ct + kr_sw * st
    for h in range(N_HEADS):
        sl = slice(h * C_SLOT, (h + 1) * C_SLOT)
        q_ref[:, sl] = (a[:, sl] * ct + b[:, sl] * st).astype(q_ref.dtype)
        k_ref[:, sl] = (kn[:, sl] + kr_rot).astype(k_ref.dtype)


def mla_prep(proj32, ctab, stab, gq, gkv, wq1, wq2, wk, wv, tm):
    t = proj32.shape[0]
    c_in = C_Q_RANK + C_KV_RANK + 2 * C_SLOT

    def full(a):
        return pl.BlockSpec(a.shape, lambda i: (0,) * a.ndim)

    gq = gq.reshape(1, -1)
    gkv = gkv.reshape(1, -1)
    return pl.pallas_call(
        _mla_prep_kernel,
        out_shape=(jax.ShapeDtypeStruct((t, N_HEADS * C_SLOT), BF16),
                   jax.ShapeDtypeStruct((t, N_HEADS * C_SLOT), BF16),
                   jax.ShapeDtypeStruct((t, WIDTH), BF16)),
        grid=(t // tm,),
        in_specs=[pl.BlockSpec((tm, c_in), lambda i: (i, 0)),
                  pl.BlockSpec((tm, LANES), lambda i: (i, 0)),
                  pl.BlockSpec((tm, LANES), lambda i: (i, 0)),
                  full(gq), full(gkv), full(wq1), full(wq2), full(wk), full(wv)],
        out_specs=(pl.BlockSpec((tm, N_HEADS * C_SLOT), lambda i: (i, 0)),
                   pl.BlockSpec((tm, N_HEADS * C_SLOT), lambda i: (i, 0)),
                   pl.BlockSpec((tm, WIDTH), lambda i: (i, 0))),
        compiler_params=_cparams(("parallel",)),
        name="mla_prep",
    )(proj32, ctab, stab, gq, gkv, wq1, wq2, wk, wv)


def _mla_flash_kernel(q_ref, k_ref, v_ref, o_ref, acc_ref, m_ref, l_ref):
    qi = pl.program_id(2)
    c = (C_NOPE + C_ROPE) ** -0.5 * LOG2E
    n_rep = C_TILE // LANES

    m_ref[...] = jnp.full_like(m_ref, -jnp.inf)
    l_ref[...] = jnp.zeros_like(l_ref)
    acc_ref[...] = jnp.zeros_like(acc_ref)

    def tile(j, masked):
        rows = pl.ds(pl.multiple_of(j * C_TILE, C_TILE), C_TILE)
        v = v_ref[rows, :]
        for e in range(2):
            k = k_ref[rows, e * C_SLOT:(e + 1) * C_SLOT]
            s = lax.dot_general(q_ref[:, e * C_SLOT:(e + 1) * C_SLOT], k, (((1,), (1,)), ((), ())),
                                preferred_element_type=F32)
            if masked:
                r = lax.broadcasted_iota(jnp.int32, (C_TILE, C_TILE), 0) // CHUNK
                cc = lax.broadcasted_iota(jnp.int32, (C_TILE, C_TILE), 1) // CHUNK
                s = jnp.where(cc <= r, s, NEG_BIG)
            m_prev = m_ref[e]
            m_new = jnp.maximum(m_prev, jnp.max(s, axis=-1, keepdims=True))
            p = jnp.exp2((s - jnp.concatenate([m_new] * n_rep, axis=1)) * c)
            alpha = jnp.exp2((m_prev - m_new) * c)
            l_ref[e] = alpha * l_ref[e] + jnp.sum(p, axis=-1, keepdims=True)
            acc_ref[e] = alpha * acc_ref[e] + jnp.dot(p.astype(BF16), v, preferred_element_type=F32)
            m_ref[e] = m_new

    def body(j, carry):
        tile(j, False)
        return carry

    lax.fori_loop(0, qi, body, 0)
    tile(qi, True)

    lane = lax.broadcasted_iota(jnp.int32, (C_TILE, LANES), 1)
    o0 = acc_ref[0] / l_ref[0]
    o1 = acc_ref[1] / l_ref[1]
    o_ref[...] = jnp.where(lane < HEAD_DIM, o0, o1).astype(o_ref.dtype)


def mla_flash(q, k, v, bsz, seq):
    t = bsz * seq
    n_qt = seq // C_TILE
    return pl.pallas_call(
        _mla_flash_kernel,
        out_shape=jax.ShapeDtypeStruct((t, WIDTH), BF16),
        grid=(bsz, N_PAIRS, n_qt),
        in_specs=[
            pl.BlockSpec((C_TILE, 2 * C_SLOT), lambda b, p, i: (b * n_qt + i, p)),
            pl.BlockSpec((seq, 2 * C_SLOT), lambda b, p, i: (b, p)),
            pl.BlockSpec((seq, LANES), lambda b, p, i: (b, p)),
        ],
        out_specs=pl.BlockSpec((C_TILE, LANES), lambda b, p, i: (b * n_qt + i, p)),
        scratch_shapes=[pltpu.VMEM((2, C_TILE, LANES), F32),
                        pltpu.VMEM((2, C_TILE, LANES), F32),
                        pltpu.VMEM((2, C_TILE, LANES), F32)],
        compiler_params=_cparams(("parallel", "parallel", "arbitrary")),
        name="mla_flash",
    )(q, k, v)


def _merge_kernel(x_ref, ya_ref, yb_ref, yc_ref, g_ref, wbr_ref, wout_ref, o_ref):
    merged = None
    for n, y_ref in enumerate((ya_ref, yb_ref, yc_ref)):
        up = jnp.dot(y_ref[...], wbr_ref[n], preferred_element_type=F32)
        gl = g_ref[:, n * D_MODEL:(n + 1) * D_MODEL].astype(F32)
        term = (1.0 / (1.0 + jnp.exp(-gl))) * up
        merged = term if merged is None else merged + term
    o_ref[...] = x_ref[...] + jnp.dot(merged.astype(BF16), wout_ref[...], preferred_element_type=F32)


def merge_out(x, ya, yb, yc, proj16, gate_blk, wbr, wout, tm):
    t = x.shape[0]
    return pl.pallas_call(
        _merge_kernel,
        out_shape=jax.ShapeDtypeStruct((t, D_MODEL), F32),
        grid=(t // tm,),
        in_specs=[
            pl.BlockSpec((tm, D_MODEL), lambda i: (i, 0)),
            pl.BlockSpec((tm, WIDTH), lambda i: (i, 0)),
            pl.BlockSpec((tm, WIDTH), lambda i: (i, 0)),
            pl.BlockSpec((tm, WIDTH), lambda i: (i, 0)),
            pl.BlockSpec((tm, N_BRANCH * D_MODEL), lambda i: (i, gate_blk)),
            pl.BlockSpec(wbr.shape, lambda i: (0, 0, 0)),
            pl.BlockSpec(wout.shape, lambda i: (0, 0)),
        ],
        out_specs=pl.BlockSpec((tm, D_MODEL), lambda i: (i, 0)),
        compiler_params=_cparams(("parallel",)),
        name="merge_out",
    )(x, ya, yb, yc, proj16, wbr, wout)


def _ffn_kernel(x_ref, g_ref, w1_ref, w2_ref, gf_ref, o_ref, h_ref, acc_ref, *, final_norm):
    k = pl.program_id(1)

    @pl.when(k == 0)
    def _():
        h_ref[...] = _rms(x_ref[...], g_ref[...]).astype(BF16)
        acc_ref[...] = jnp.zeros_like(acc_ref)

    u = jnp.maximum(jnp.dot(h_ref[...], w1_ref[...], preferred_element_type=F32), 0.0)
    acc_ref[...] += jnp.dot((u * u).astype(BF16), w2_ref[...], preferred_element_type=F32)

    @pl.when(k == pl.num_programs(1) - 1)
    def _():
        y = x_ref[...] + acc_ref[...]
        if final_norm:
            y = _rms(y, gf_ref[...])
        o_ref[...] = y


def ffn(x, g, w1, w2, gf, final_norm, tm, tf):
    t, d = x.shape
    dff = w1.shape[1]
    return pl.pallas_call(
        functools.partial(_ffn_kernel, final_norm=final_norm),
        out_shape=jax.ShapeDtypeStruct((t, d), F32),
        grid=(t // tm, dff // tf),
        in_specs=[
            pl.BlockSpec((tm, d), lambda i, k: (i, 0)),
            pl.BlockSpec((1, d), lambda i, k: (0, 0)),
            pl.BlockSpec((d, tf), lambda i, k: (0, k)),
            pl.BlockSpec((tf, d), lambda i, k: (k, 0)),
            pl.BlockSpec((1, d), lambda i, k: (0, 0)),
        ],
        out_specs=pl.BlockSpec((tm, d), lambda i, k: (i, 0)),
        scratch_shapes=[pltpu.VMEM((tm, d), BF16), pltpu.VMEM((tm, d), F32)],
        compiler_params=_cparams(("parallel", "arbitrary")),
        name="ffn",
    )(x, g.reshape(1, d), w1, w2, gf.reshape(1, d))


COL_AQ, COL_AK, COL_AV = 0, 4, 8
COL_BQ, COL_BI, COL_BG = 12, 16, 20
GATE_BLK = 1
COL32_BF = (C_Q_RANK + C_KV_RANK + 2 * C_SLOT) // LANES


def _prep_w_in(w):
    o = 0
    parts = {}
    for name, size in (("aq", 512), ("ak", 512), ("av", 512), ("bq", 512), ("bf", 512), ("bi", 512),
                       ("bg", 512), ("cq", C_Q_RANK), ("ckv", C_KV_RANK), ("ckr", C_ROPE),
                       ("gate", N_BRANCH * D_MODEL)):
        parts[name] = w[:, o:o + size]
        o += size
    w16 = jnp.concatenate([parts[n] for n in ("aq", "ak", "av", "bq", "bi", "bg", "gate")], axis=1)
    half = C_ROPE // 2
    x1, x2 = parts["ckr"][:, :half], parts["ckr"][:, half:]
    zl = jnp.zeros((w.shape[0], C_NOPE), w.dtype)
    zr = jnp.zeros((w.shape[0], C_SLOT - C_NOPE - C_ROPE), w.dtype)
    kr_slot = jnp.concatenate([zl, x1, x2, zr], axis=1)
    kr_sw = jnp.concatenate([zl, -x2, x1, zr], axis=1)
    w32 = jnp.concatenate([parts["cq"], parts["ckv"], kr_slot, kr_sw, parts["bf"]], axis=1)
    return w16.astype(BF16), w32.astype(BF16)


def _prep_w_mla(w_uq, w_ukv):
    half = C_ROPE // 2
    per_q = C_NOPE + C_ROPE
    wq = w_uq.reshape(C_Q_RANK, N_HEADS, per_q)
    nope, x1, x2 = wq[..., :C_NOPE], wq[..., C_NOPE:C_NOPE + half], wq[..., C_NOPE + half:]
    zpad = jnp.zeros((C_Q_RANK, N_HEADS, C_SLOT - per_q), w_uq.dtype)
    wq1 = jnp.concatenate([nope, x1, x2, zpad], axis=-1).reshape(C_Q_RANK, N_HEADS * C_SLOT)
    wq2 = jnp.concatenate([jnp.zeros_like(nope), -x2, x1, zpad], axis=-1).reshape(C_Q_RANK, N_HEADS * C_SLOT)
    wkv = w_ukv.reshape(C_KV_RANK, N_HEADS, C_NOPE + HEAD_DIM)
    k_nope, v = wkv[..., :C_NOPE], wkv[..., C_NOPE:]
    wk = jnp.concatenate([k_nope, jnp.zeros((C_KV_RANK, N_HEADS, C_SLOT - C_NOPE), w_ukv.dtype)],
                         axis=-1).reshape(C_KV_RANK, N_HEADS * C_SLOT)
    wv = v.reshape(C_KV_RANK, WIDTH)
    return wq1.astype(BF16), wq2.astype(BF16), wk.astype(BF16), wv.astype(BF16)


def kernel(x, positions, norm_mix_g, w_in, rel_bias, hgrn_lb_logits, hgrn_norm_g, mla_q_norm_g,
           mla_kv_norm_g, mla_w_uq, mla_w_ukv, w_branch, w_out, norm_ffn_g, w_ff1, w_ff2, final_norm_g):
    bsz, seq, d = x.shape
    depth = w_in.shape[0]
    t = bsz * seq
    assert d == D_MODEL and seq % C_TILE == 0 and seq >= A_BAND
    tm = min(1024, t)
    hgrn_rows = min(512, seq)

    p_lb = jax.nn.softmax(hgrn_lb_logits.astype(F32), axis=0)
    lb_all = jnp.cumsum(p_lb, axis=0)
    lb_all = lb_all - lb_all[0:1]

    ctab, stab = rope_tables(positions, tm)
    xf = x.reshape(t, d)
    for l in range(depth):
        w16, w32 = _prep_w_in(w_in[l])
        wq1, wq2, wk, wv = _prep_w_mla(mla_w_uq[l], mla_w_ukv[l])
        proj16 = norm_matmul(xf, norm_mix_g[l], w16, BF16, tm, 1024, "in_proj_bf16")
        proj32 = norm_matmul(xf, norm_mix_g[l], w32, F32, tm, w32.shape[1], "in_proj_f32")

        y_a = band_attention(proj16, band_bias_table(rel_bias[l]), bsz, seq, COL_AQ, COL_AK, COL_AV)
        y_b = hgrn2(proj16, proj32, lb_all[l], hgrn_norm_g[l], bsz, seq,
                    COL_BQ, COL_BI, COL_BG, COL32_BF, hgrn_rows)
        q_c, k_c, v_c = mla_prep(proj32, ctab, stab, mla_q_norm_g[l], mla_kv_norm_g[l],
                                 wq1, wq2, wk, wv, tm)
        y_c = mla_flash(q_c, k_c, v_c, bsz, seq)

        xf = merge_out(xf, y_a, y_b, y_c, proj16, GATE_BLK, w_branch[l].astype(BF16),
                       w_out[l].astype(BF16), min(512, t))
        xf = ffn(xf, norm_ffn_g[l], w_ff1[l].astype(BF16), w_ff2[l].astype(BF16), final_norm_g,
                 l == depth - 1, tm, 1024)
    return xf.reshape(bsz, seq, d)
```

```python
import functools

import jax
import jax.numpy as jnp
from jax import lax
from jax.experimental import pallas as pl
from jax.experimental.pallas import tpu as pltpu

F32 = jnp.float32
BF16 = jnp.bfloat16

D_MODEL = 1024
CHUNK = 64
EPS = 1e-6
N_HEADS = 8
HEAD_DIM = 64
LANES = 128
N_PAIRS = N_HEADS * HEAD_DIM // LANES
WIDTH = N_HEADS * HEAD_DIM

A_LEFT_CHUNKS = 8
A_MAX_REL = 128
A_QTILE = 2 * CHUNK
A_BAND = (A_LEFT_CHUNKS + 2) * CHUNK
A_PAD = A_LEFT_CHUNKS * CHUNK

C_Q_RANK = 256
C_KV_RANK = 128
C_ROPE = 32
C_NOPE = 64
C_SLOT = LANES
ROPE_BASE = 10000.0
C_TILE = 512

D_FF = 4 * D_MODEL
N_BRANCH = 3

SUB = 16
N_SUB = CHUNK // SUB
NEG_BIG = -1e30
LOG2E = 1.4426950408889634

VMEM_LIMIT = 48 * 1024 * 1024


def _cparams(sem):
    return pltpu.CompilerParams(dimension_semantics=sem, vmem_limit_bytes=VMEM_LIMIT)


def _rms(x, g):
    return x * lax.rsqrt(jnp.mean(x * x, axis=-1, keepdims=True) + EPS) * g


def _norm_matmul_kernel(x_ref, g_ref, w_ref, o_ref, h_ref):
    @pl.when(pl.program_id(1) == 0)
    def _():
        h_ref[...] = _rms(x_ref[...], g_ref[...]).astype(BF16)

    o_ref[...] = jnp.dot(h_ref[...], w_ref[...], preferred_element_type=F32).astype(o_ref.dtype)


def norm_matmul(x, g, w, out_dtype, tm, tn, name):
    t, d = x.shape
    n = w.shape[1]
    return pl.pallas_call(
        _norm_matmul_kernel,
        out_shape=jax.ShapeDtypeStruct((t, n), out_dtype),
        grid=(t // tm, n // tn),
        in_specs=[
            pl.BlockSpec((tm, d), lambda i, j: (i, 0)),
            pl.BlockSpec((1, d), lambda i, j: (0, 0)),
            pl.BlockSpec((d, tn), lambda i, j: (0, j)),
        ],
        out_specs=pl.BlockSpec((tm, tn), lambda i, j: (i, j)),
        scratch_shapes=[pltpu.VMEM((tm, d), BF16)],
        compiler_params=_cparams(("parallel", "arbitrary")),
        name=name,
    )(x, g.reshape(1, d), w)


def _trig_kernel(pos_ref, invf_ref, c_ref, s_ref):
    ang = pos_ref[...].astype(F32) * invf_ref[...]
    c_ref[...] = jnp.cos(ang)
    s_ref[...] = jnp.sin(ang)


def rope_tables(positions, tm):
    t = positions.size
    inv_freq = ROPE_BASE ** (-jnp.arange(0, C_ROPE, 2, dtype=F32) / C_ROPE)
    half = C_ROPE // 2
    invf = jnp.zeros((LANES,), F32)
    invf = invf.at[C_NOPE:C_NOPE + half].set(inv_freq).at[C_NOPE + half:C_NOPE + C_ROPE].set(inv_freq)
    return pl.pallas_call(
        _trig_kernel,
        out_shape=(jax.ShapeDtypeStruct((t, LANES), F32),) * 2,
        grid=(t // tm,),
        in_specs=[
            pl.BlockSpec((tm, 1), lambda i: (i, 0)),
            pl.BlockSpec((1, LANES), lambda i: (0, 0)),
        ],
        out_specs=(pl.BlockSpec((tm, LANES), lambda i: (i, 0)),) * 2,
        compiler_params=_cparams(("parallel",)),
        name="rope_tables",
    )(positions.reshape(t, 1), invf.reshape(1, LANES))


def _band_attn_kernel(q_ref, k_ref, v_ref, bias_ref, o_ref):
    c2 = pl.program_id(2)
    start = pl.multiple_of(jnp.maximum(c2 * A_QTILE - A_PAD, 0), A_QTILE)
    kwin = k_ref[pl.ds(start, A_BAND), :]
    vwin = v_ref[pl.ds(start, A_BAND), :]
    q = q_ref[...].astype(F32) * (HEAD_DIM ** -0.5)
    lane = lax.broadcasted_iota(jnp.int32, (A_QTILE, LANES), 1)
    outs = []
    for e in range(2):
        qe = (jnp.where(lane < HEAD_DIM, q, 0.0) if e == 0 else jnp.where(lane < HEAD_DIM, 0.0, q)).astype(BF16)
        s = lax.dot_general(qe, kwin, (((1,), (1,)), ((), ())), preferred_element_type=F32)
        s = s + bias_ref[0, e]
        m = jnp.max(s, axis=-1, keepdims=True)
        p = jnp.exp(s - m)
        l = jnp.sum(p, axis=-1, keepdims=True)
        pv = jnp.dot(p.astype(BF16), vwin, preferred_element_type=F32)
        outs.append(pv / l)
    o_ref[...] = jnp.where(lane < HEAD_DIM, outs[0], outs[1]).astype(o_ref.dtype)


def band_attention(proj, bias, bsz, seq, q_col, k_col, v_col):
    t = bsz * seq
    n_qt = seq // A_QTILE
    n_shift = A_PAD // A_QTILE
    return pl.pallas_call(
        _band_attn_kernel,
        out_shape=jax.ShapeDtypeStruct((t, WIDTH), BF16),
        grid=(bsz, N_PAIRS, n_qt),
        in_specs=[
            pl.BlockSpec((A_QTILE, LANES), lambda b, p, c: (b * n_qt + c, q_col + p)),
            pl.BlockSpec((seq, LANES), lambda b, p, c: (b, k_col + p)),
            pl.BlockSpec((seq, LANES), lambda b, p, c: (b, v_col + p)),
            pl.BlockSpec((1, 2, A_QTILE, A_BAND), lambda b, p, c: (jnp.minimum(c, n_shift), p, 0, 0)),
        ],
        out_specs=pl.BlockSpec((A_QTILE, LANES), lambda b, p, c: (b * n_qt + c, p)),
        compiler_params=_cparams(("parallel", "parallel", "arbitrary")),
        name="band_attention",
    )(proj, proj, proj, bias)


A_ROLL = 768


def _band_bias_kernel(r_ref, o_ref):
    d = pl.program_id(0) * A_QTILE
    prof = jnp.broadcast_to(r_ref[...], (A_QTILE, A_ROLL))
    b = pltpu.roll(prof, 0, 1, stride=1, stride_axis=0)[:, :A_BAND]
    i = lax.broadcasted_iota(jnp.int32, (A_QTILE, A_BAND), 0)
    j = lax.broadcasted_iota(jnp.int32, (A_QTILE, A_BAND), 1)
    gap = (d + i) // CHUNK - j // CHUNK
    o_ref[...] = jnp.where((gap >= 0) & (gap <= A_LEFT_CHUNKS), b, NEG_BIG)


def band_bias_table(rel_table):
    n_shift = A_PAD // A_QTILE
    heads = rel_table.shape[0]
    pad = A_ROLL + A_PAD
    ext = jnp.pad(rel_table.astype(F32)[:, ::-1], ((0, 0), (pad, pad)), mode="edge")

    def seg(d, u0, n):
        o = u0 - d + A_MAX_REL + pad
        return ext[:, o:o + n]

    n_neg = A_ROLL - (A_BAND + 1)
    prof = jnp.stack([jnp.concatenate([seg(s * A_QTILE, 0, A_BAND + 1), seg(s * A_QTILE, -n_neg, n_neg)], axis=1)
                      for s in range(n_shift + 1)])
    return pl.pallas_call(
        _band_bias_kernel,
        out_shape=jax.ShapeDtypeStruct((n_shift + 1, heads, A_QTILE, A_BAND), F32),
        grid=(n_shift + 1, heads),
        in_specs=[pl.BlockSpec((None, None, 1, A_ROLL), lambda s, h: (s, h, 0, 0))],
        out_specs=pl.BlockSpec((None, None, A_QTILE, A_BAND), lambda s, h: (s, h, 0, 0)),
        compiler_params=_cparams(("parallel", "parallel")),
        name="band_bias",
    )(prof.reshape(n_shift + 1, heads, 1, A_ROLL))


def _split3(x):
    hi = x.astype(BF16)
    r1 = x - hi.astype(F32)
    mid = r1.astype(BF16)
    lo = (r1 - mid.astype(F32)).astype(BF16)
    return hi, mid, lo


def _hgrn_kernel(q_ref, f_ref, i_ref, g_ref, loglb_ref, log1mlb_ref, onemlb_ref, ng_ref,
                 o_ref, st_ref, *, chunks_per_step):
    @pl.when(pl.program_id(2) == 0)
    def _():
        st_ref[...] = jnp.zeros_like(st_ref)

    log_lb = loglb_ref[...]
    log1m_lb = log1mlb_ref[...]
    one_m_lb = onemlb_ref[...]
    norm_g = ng_ref[...]

    lane_c = lax.broadcasted_iota(jnp.int32, (CHUNK, LANES), 1)
    row_c = lax.broadcasted_iota(jnp.int32, (CHUNK, LANES), 0)
    head0_c = lane_c < HEAD_DIM
    r64 = lax.broadcasted_iota(jnp.int32, (CHUNK, CHUNK), 0)
    c64 = lax.broadcasted_iota(jnp.int32, (CHUNK, CHUNK), 1)
    tril = (c64 <= r64).astype(BF16)
    rl = lax.broadcasted_iota(jnp.int32, (LANES, LANES), 0)
    cl = lax.broadcasted_iota(jnp.int32, (LANES, LANES), 1)
    same_head = (rl // HEAD_DIM) == (cl // HEAD_DIM)
    bd = same_head.astype(BF16)
    ri = lax.broadcasted_iota(jnp.int32, (CHUNK, CHUNK * SUB), 0)
    ci = lax.broadcasted_iota(jnp.int32, (CHUNK, CHUNK * SUB), 1)
    pick = (ci // SUB == ri).astype(BF16)
    row_s = lax.broadcasted_iota(jnp.int32, (SUB, LANES), 0)

    def chunk_body(c, carry):
        r0 = pl.multiple_of(c * CHUNK, CHUNK)
        rows = pl.ds(r0, CHUNK)
        z = f_ref[rows, :]
        qraw = q_ref[rows, :].astype(F32)
        v = i_ref[rows, :].astype(F32)
        graw = g_ref[rows, :].astype(F32)

        log_sig = jnp.minimum(z, 0.0) - jnp.log1p(jnp.exp(-jnp.abs(z)))
        bterm = log1m_lb + log_sig
        log_f = jnp.maximum(log_lb, bterm) + jnp.log1p(jnp.exp(-jnp.abs(log_lb - bterm)))
        ks = one_m_lb * (1.0 / (1.0 + jnp.exp(z)))
        qs = qraw * (1.0 / (1.0 + jnp.exp(-qraw)))

        hi, mid, lo = _split3(log_f)
        cum = (jnp.dot(tril, hi, preferred_element_type=F32)
               + jnp.dot(tril, mid, preferred_element_type=F32)
               + jnp.dot(tril, lo, preferred_element_type=F32))

        zero_row = jnp.zeros((1, LANES), F32)
        c_start = [zero_row] + [cum[SUB * i - 1:SUB * i, :] for i in range(1, N_SUB)]
        c_end = [cum[SUB * i + SUB - 1:SUB * i + SUB, :] for i in range(N_SUB)]
        cstart_full = jnp.concatenate([jnp.broadcast_to(r, (SUB, LANES)) for r in c_start], axis=0)
        cend_full = jnp.concatenate([jnp.broadcast_to(r, (SUB, LANES)) for r in c_end], axis=0)

        q1 = qs * jnp.exp(cum - cstart_full)
        k2 = ks * jnp.exp(cend_full - cum)
        k2b = k2.astype(BF16)
        vb0 = jnp.where(head0_c, v, 0.0).astype(BF16)
        vb1 = jnp.where(head0_c, 0.0, v).astype(BF16)

        sc = [None, None]
        for j in range(N_SUB - 1):
            parts = []
            for i in range(N_SUB):
                if i > j:
                    parts.append(jnp.broadcast_to(jnp.exp(c_start[i] - c_end[j]), (SUB, LANES)))
                else:
                    parts.append(jnp.zeros((SUB, LANES), F32))
            qj = q1 * jnp.concatenate(parts, axis=0)
            col_in_j = (c64 >= j * SUB) & (c64 < (j + 1) * SUB)
            for e in range(2):
                qje = (jnp.where(head0_c, qj, 0.0) if e == 0 else jnp.where(head0_c, 0.0, qj)).astype(BF16)
                s = lax.dot_general(qje, k2b, (((1,), (1,)), ((), ())), preferred_element_type=F32)
                s = jnp.where(col_in_j, s, 0.0)
                sc[e] = s if sc[e] is None else sc[e] + s
        o = (jnp.dot(sc[0].astype(BF16), vb0, preferred_element_type=F32)
             + jnp.dot(sc[1].astype(BF16), vb1, preferred_element_type=F32))

        w_rows = []
        for blk in range(N_SUB):
            cum_b = cum[SUB * blk:SUB * (blk + 1), :]
            ks_b = ks[SUB * blk:SUB * (blk + 1), :]
            for i in range(SUB):
                r = SUB * blk + i
                diff = jnp.minimum(cum[r:r + 1, :] - cum_b, 0.0)
                w = jnp.exp(diff) * (ks_b * qs[r:r + 1, :])
                w_rows.append(jnp.where(row_s <= i, w, 0.0).astype(BF16))
        w_all = jnp.concatenate(w_rows, axis=0)
        sb = jnp.dot(w_all, bd, preferred_element_type=F32)
        v_rep = jnp.concatenate(
            [v[SUB * (r // SUB):SUB * (r // SUB + 1), :] for r in range(CHUNK)], axis=0)
        o = o + jnp.dot(pick, (sb * v_rep).astype(BF16), preferred_element_type=F32)

        st = st_ref[...]
        q_state = (q1 * jnp.exp(cstart_full)).astype(BF16)
        o = o + lax.dot_general(q_state, st.astype(BF16), (((1,), (1,)), ((), ())),
                                preferred_element_type=F32)
        c_last = c_end[N_SUB - 1]
        k_end = (k2 * jnp.exp(c_last - cend_full)).astype(BF16)
        upd = lax.dot_general(v.astype(BF16), k_end, (((0,), (0,)), ((), ())),
                              preferred_element_type=F32)
        st_ref[...] = st * jnp.exp(c_last) + jnp.where(same_head, upd, 0.0)

        osq = o * o
        ms0 = jnp.sum(jnp.where(head0_c, osq, 0.0), axis=-1, keepdims=True)
        ms1 = jnp.sum(jnp.where(head0_c, 0.0, osq), axis=-1, keepdims=True)
        ms = jnp.where(head0_c, ms0, ms1) * (1.0 / HEAD_DIM)
        y = o * lax.rsqrt(ms + EPS) * norm_g
        y = y * (graw * (1.0 / (1.0 + jnp.exp(-graw))))
        o_ref[rows, :] = y.astype(o_ref.dtype)
        return carry

    lax.fori_loop(0, chunks_per_step, chunk_body, 0)


def hgrn2(proj16, proj32, lb, norm_g, bsz, seq, q_col, i_col, g_col, f_col, rows_per_step):
    t = bsz * seq
    n_steps = seq // rows_per_step
    log_lb = jnp.log(lb).reshape(N_PAIRS, 1, LANES)
    log1m_lb = jnp.log1p(-lb).reshape(N_PAIRS, 1, LANES)
    one_m_lb = (1.0 - lb).reshape(N_PAIRS, 1, LANES)
    ng = jnp.tile(norm_g.astype(F32), LANES // HEAD_DIM).reshape(1, LANES)

    def act(col):
        return pl.BlockSpec((rows_per_step, LANES), lambda b, p, s: (b * n_steps + s, col + p))

    def par():
        return pl.BlockSpec((None, 1, LANES), lambda b, p, s: (p, 0, 0))

    return pl.pallas_call(
        functools.partial(_hgrn_kernel, chunks_per_step=rows_per_step // CHUNK),
        out_shape=jax.ShapeDtypeStruct((t, WIDTH), BF16),
        grid=(bsz, N_PAIRS, n_steps),
        in_specs=[act(q_col), act(f_col), act(i_col), act(g_col), par(), par(), par(),
                  pl.BlockSpec((1, LANES), lambda b, p, s: (0, 0))],
        out_specs=pl.BlockSpec((rows_per_step, LANES), lambda b, p, s: (b * n_steps + s, p)),
        scratch_shapes=[pltpu.VMEM((LANES, LANES), F32)],
        compiler_params=_cparams(("parallel", "parallel", "arbitrary")),
        name="hgrn2",
    )(proj16, proj32, proj16, proj16, log_lb, log1m_lb, one_m_lb, ng)


def _mla_prep_kernel(pc_ref, ct_ref, st_ref, gq_ref, gkv_ref, wq1_ref, wq2_ref, wk_ref, wv_ref,
                     q_ref, k_ref, v_ref):
    pc = pc_ref[...]
    qn = _rms(pc[:, :C_Q_RANK], gq_ref[...]).astype(BF16)
    kvn = _rms(pc[:, C_Q_RANK:C_Q_RANK + C_KV_RANK], gkv_ref[...]).astype(BF16)
    kr = pc[:, C_Q_RANK + C_KV_RANK:C_Q_RANK + C_KV_RANK + C_SLOT]
    kr_sw = pc[:, C_Q_RANK + C_KV_RANK + C_SLOT:C_Q_RANK + C_KV_RANK + 2 * C_SLOT]
    ct = ct_ref[...]
    st = st_ref[...]
    a = jnp.dot(qn, wq1_ref[...], preferred_element_type=F32)
    b = jnp.dot(qn, wq2_ref[...], preferred_element_type=F32)
    kn = jnp.dot(kvn, wk_ref[...], preferred_element_type=F32)
    v_ref[...] = jnp.dot(kvn, wv_ref[...], preferred_element_type=F32).astype(v_ref.dtype)
    kr_rot = kr * ct + kr_sw * st
    for h in range(N_HEADS):
        sl = slice(h * C_SLOT, (h + 1) * C_SLOT)
        q_ref[:, sl] = (a[:, sl] * ct + b[:, sl] * st).astype(q_ref.dtype)
        k_ref[:, sl] = (kn[:, sl] + kr_rot).astype(k_ref.dtype)


def mla_prep(proj32, ctab, stab, gq, gkv, wq1, wq2, wk, wv, tm):
    t = proj32.shape[0]
    c_in = C_Q_RANK + C_KV_RANK + 2 * C_SLOT

    def full(a):
        return pl.BlockSpec(a.shape, lambda i: (0,) * a.ndim)

    gq = gq.reshape(1, -1)
    gkv = gkv.reshape(1, -1)
    return pl.pallas_call(
        _mla_prep_kernel,
        out_shape=(jax.ShapeDtypeStruct((t, N_HEADS * C_SLOT), BF16),
                   jax.ShapeDtypeStruct((t, N_HEADS * C_SLOT), BF16),
                   jax.ShapeDtypeStruct((t, WIDTH), BF16)),
        grid=(t // tm,),
        in_specs=[pl.BlockSpec((tm, c_in), lambda i: (i, 0)),
                  pl.BlockSpec((tm, LANES), lambda i: (i, 0)),
                  pl.BlockSpec((tm, LANES), lambda i: (i, 0)),
                  full(gq), full(gkv), full(wq1), full(wq2), full(wk), full(wv)],
        out_specs=(pl.BlockSpec((tm, N_HEADS * C_SLOT), lambda i: (i, 0)),
                   pl.BlockSpec((tm, N_HEADS * C_SLOT), lambda i: (i, 0)),
                   pl.BlockSpec((tm, WIDTH), lambda i: (i, 0))),
        compiler_params=_cparams(("parallel",)),
        name="mla_prep",
    )(proj32, ctab, stab, gq, gkv, wq1, wq2, wk, wv)


def _mla_flash_kernel(q_ref, k_ref, v_ref, o_ref, acc_ref, m_ref, l_ref):
    qi = pl.program_id(2)
    c = (C_NOPE + C_ROPE) ** -0.5 * LOG2E
    n_rep = C_TILE // LANES

    m_ref[...] = jnp.full_like(m_ref, -jnp.inf)
    l_ref[...] = jnp.zeros_like(l_ref)
    acc_ref[...] = jnp.zeros_like(acc_ref)

    def tile(j, masked):
        rows = pl.ds(pl.multiple_of(j * C_TILE, C_TILE), C_TILE)
        v = v_ref[rows, :]
        for e in range(2):
            k = k_ref[rows, e * C_SLOT:(e + 1) * C_SLOT]
            s = lax.dot_general(q_ref[:, e * C_SLOT:(e + 1) * C_SLOT], k, (((1,), (1,)), ((), ())),
                                preferred_element_type=F32)
            if masked:
                r = lax.broadcasted_iota(jnp.int32, (C_TILE, C_TILE), 0) // CHUNK
                cc = lax.broadcasted_iota(jnp.int32, (C_TILE, C_TILE), 1) // CHUNK
                s = jnp.where(cc <= r, s, NEG_BIG)
            m_prev = m_ref[e]
            m_new = jnp.maximum(m_prev, jnp.max(s, axis=-1, keepdims=True))
            p = jnp.exp2((s - jnp.concatenate([m_new] * n_rep, axis=1)) * c)
            alpha = jnp.exp2((m_prev - m_new) * c)
            l_ref[e] = alpha * l_ref[e] + jnp.sum(p, axis=-1, keepdims=True)
            acc_ref[e] = alpha * acc_ref[e] + jnp.dot(p.astype(BF16), v, preferred_element_type=F32)
            m_ref[e] = m_new

    def body(j, carry):
        tile(j, False)
        return carry

    lax.fori_loop(0, qi, body, 0)
    tile(qi, True)

    lane = lax.broadcasted_iota(jnp.int32, (C_TILE, LANES), 1)
    o0 = acc_ref[0] / l_ref[0]
    o1 = acc_ref[1] / l_ref[1]
    o_ref[...] = jnp.where(lane < HEAD_DIM, o0, o1).astype(o_ref.dtype)


def mla_flash(q, k, v, bsz, seq):
    t = bsz * seq
    n_qt = seq // C_TILE
    return pl.pallas_call(
        _mla_flash_kernel,
        out_shape=jax.ShapeDtypeStruct((t, WIDTH), BF16),
        grid=(bsz, N_PAIRS, n_qt),
        in_specs=[
            pl.BlockSpec((C_TILE, 2 * C_SLOT), lambda b, p, i: (b * n_qt + i, p)),
            pl.BlockSpec((seq, 2 * C_SLOT), lambda b, p, i: (b, p)),
            pl.BlockSpec((seq, LANES), lambda b, p, i: (b, p)),
        ],
        out_specs=pl.BlockSpec((C_TILE, LANES), lambda b, p, i: (b * n_qt + i, p)),
        scratch_shapes=[pltpu.VMEM((2, C_TILE, LANES), F32),
                        pltpu.VMEM((2, C_TILE, LANES), F32),
                        pltpu.VMEM((2, C_TILE, LANES), F32)],
        compiler_params=_cparams(("parallel", "parallel", "arbitrary")),
        name="mla_flash",
    )(q, k, v)


def _merge_kernel(x_ref, ya_ref, yb_ref, yc_ref, g_ref, wbr_ref, wout_ref, o_ref):
    merged = None
    for n, y_ref in enumerate((ya_ref, yb_ref, yc_ref)):
        up = jnp.dot(y_ref[...], wbr_ref[n], preferred_element_type=F32)
        gl = g_ref[:, n * D_MODEL:(n + 1) * D_MODEL].astype(F32)
        term = (1.0 / (1.0 + jnp.exp(-gl))) * up
        merged = term if merged is None else merged + term
    o_ref[...] = x_ref[...] + jnp.dot(merged.astype(BF16), wout_ref[...], preferred_element_type=F32)


def merge_out(x, ya, yb, yc, proj16, gate_blk, wbr, wout, tm):
    t = x.shape[0]
    return pl.pallas_call(
        _merge_kernel,
        out_shape=jax.ShapeDtypeStruct((t, D_MODEL), F32),
        grid=(t // tm,),
        in_specs=[
            pl.BlockSpec((tm, D_MODEL), lambda i: (i, 0)),
            pl.BlockSpec((tm, WIDTH), lambda i: (i, 0)),
            pl.BlockSpec((tm, WIDTH), lambda i: (i, 0)),
            pl.BlockSpec((tm, WIDTH), lambda i: (i, 0)),
            pl.BlockSpec((tm, N_BRANCH * D_MODEL), lambda i: (i, gate_blk)),
            pl.BlockSpec(wbr.shape, lambda i: (0, 0, 0)),
            pl.BlockSpec(wout.shape, lambda i: (0, 0)),
        ],
        out_specs=pl.BlockSpec((tm, D_MODEL), lambda i: (i, 0)),
        compiler_params=_cparams(("parallel",)),
        name="merge_out",
    )(x, ya, yb, yc, proj16, wbr, wout)


def _ffn_kernel(x_ref, g_ref, w1_ref, w2_ref, gf_ref, o_ref, h_ref, acc_ref, *, final_norm):
    k = pl.program_id(1)

    @pl.when(k == 0)
    def _():
        h_ref[...] = _rms(x_ref[...], g_ref[...]).astype(BF16)
        acc_ref[...] = jnp.zeros_like(acc_ref)

    u = jnp.maximum(jnp.dot(h_ref[...], w1_ref[...], preferred_element_type=F32), 0.0)
    acc_ref[...] += jnp.dot((u * u).astype(BF16), w2_ref[...], preferred_element_type=F32)

    @pl.when(k == pl.num_programs(1) - 1)
    def _():
        y = x_ref[...] + acc_ref[...]
        if final_norm:
            y = _rms(y, gf_ref[...])
        o_ref[...] = y


def ffn(x, g, w1, w2, gf, final_norm, tm, tf):
    t, d = x.shape
    dff = w1.shape[1]
    return pl.pallas_call(
        functools.partial(_ffn_kernel, final_norm=final_norm),
        out_shape=jax.ShapeDtypeStruct((t, d), F32),
        grid=(t // tm, dff // tf),
        in_specs=[
            pl.BlockSpec((tm, d), lambda i, k: (i, 0)),
            pl.BlockSpec((1, d), lambda i, k: (0, 0)),
            pl.BlockSpec((d, tf), lambda i, k: (0, k)),
            pl.BlockSpec((tf, d), lambda i, k: (k, 0)),
            pl.BlockSpec((1, d), lambda i, k: (0, 0)),
        ],
        out_specs=pl.BlockSpec((tm, d), lambda i, k: (i, 0)),
        scratch_shapes=[pltpu.VMEM((tm, d), BF16), pltpu.VMEM((tm, d), F32)],
        compiler_params=_cparams(("parallel", "arbitrary")),
        name="ffn",
    )(x, g.reshape(1, d), w1, w2, gf.reshape(1, d))


COL_AQ, COL_AK, COL_AV = 0, 4, 8
COL_BQ, COL_BI, COL_BG = 12, 16, 20
GATE_BLK = 1
COL32_BF = (C_Q_RANK + C_KV_RANK + 2 * C_SLOT) // LANES


def _prep_w_in(w):
    o = 0
    parts = {}
    for name, size in (("aq", 512), ("ak", 512), ("av", 512), ("bq", 512), ("bf", 512), ("bi", 512),
                       ("bg", 512), ("cq", C_Q_RANK), ("ckv", C_KV_RANK), ("ckr", C_ROPE),
                       ("gate", N_BRANCH * D_MODEL)):
        parts[name] = w[:, o:o + size]
        o += size
    w16 = jnp.concatenate([parts[n] for n in ("aq", "ak", "av", "bq", "bi", "bg", "gate")], axis=1)
    half = C_ROPE // 2
    x1, x2 = parts["ckr"][:, :half], parts["ckr"][:, half:]
    zl = jnp.zeros((w.shape[0], C_NOPE), w.dtype)
    zr = jnp.zeros((w.shape[0], C_SLOT - C_NOPE - C_ROPE), w.dtype)
    kr_slot = jnp.concatenate([zl, x1, x2, zr], axis=1)
    kr_sw = jnp.concatenate([zl, -x2, x1, zr], axis=1)
    w32 = jnp.concatenate([parts["cq"], parts["ckv"], kr_slot, kr_sw, parts["bf"]], axis=1)
    return w16.astype(BF16), w32.astype(BF16)


def _prep_w_mla(w_uq, w_ukv):
    half = C_ROPE // 2
    per_q = C_NOPE + C_ROPE
    wq = w_uq.reshape(C_Q_RANK, N_HEADS, per_q)
    nope, x1, x2 = wq[..., :C_NOPE], wq[..., C_NOPE:C_NOPE + half], wq[..., C_NOPE + half:]
    zpad = jnp.zeros((C_Q_RANK, N_HEADS, C_SLOT - per_q), w_uq.dtype)
    wq1 = jnp.concatenate([nope, x1, x2, zpad], axis=-1).reshape(C_Q_RANK, N_HEADS * C_SLOT)
    wq2 = jnp.concatenate([jnp.zeros_like(nope), -x2, x1, zpad], axis=-1).reshape(C_Q_RANK, N_HEADS * C_SLOT)
    wkv = w_ukv.reshape(C_KV_RANK, N_HEADS, C_NOPE + HEAD_DIM)
    k_nope, v = wkv[..., :C_NOPE], wkv[..., C_NOPE:]
    wk = jnp.concatenate([k_nope, jnp.zeros((C_KV_RANK, N_HEADS, C_SLOT - C_NOPE), w_ukv.dtype)],
                         axis=-1).reshape(C_KV_RANK, N_HEADS * C_SLOT)
    wv = v.reshape(C_KV_RANK, WIDTH)
    return wq1.astype(BF16), wq2.astype(BF16), wk.astype(BF16), wv.astype(BF16)


def kernel(x, positions, norm_mix_g, w_in, rel_bias, hgrn_lb_logits, hgrn_norm_g, mla_q_norm_g,
           mla_kv_norm_g, mla_w_uq, mla_w_ukv, w_branch, w_out, norm_ffn_g, w_ff1, w_ff2, final_norm_g):
    bsz, seq, d = x.shape
    depth = w_in.shape[0]
    t = bsz * seq
    assert d == D_MODEL and seq % C_TILE == 0 and seq >= A_BAND
    tm = min(1024, t)
    hgrn_rows = min(512, seq)

    p_lb = jax.nn.softmax(hgrn_lb_logits.astype(F32), axis=0)
    lb_all = jnp.cumsum(p_lb, axis=0)
    lb_all = lb_all - lb_all[0:1]

    ctab, stab = rope_tables(positions, tm)
    xf = x.reshape(t, d)
    for l in range(depth):
        w16, w32 = _prep_w_in(w_in[l])
        wq1, wq2, wk, wv = _prep_w_mla(mla_w_uq[l], mla_w_ukv[l])
        proj16 = norm_matmul(xf, norm_mix_g[l], w16, BF16, tm, 1024, "in_proj_bf16")
        proj32 = norm_matmul(xf, norm_mix_g[l], w32, F32, tm, w32.shape[1], "in_proj_f32")

        y_a = band_attention(proj16, band_bias_table(rel_bias[l]), bsz, seq, COL_AQ, COL_AK, COL_AV)
        y_b = hgrn2(proj16, proj32, lb_all[l], hgrn_norm_g[l], bsz, seq,
                    COL_BQ, COL_BI, COL_BG, COL32_BF, hgrn_rows)
        q_c, k_c, v_c = mla_prep(proj32, ctab, stab, mla_q_norm_g[l], mla_kv_norm_g[l],
                                 wq1, wq2, wk, wv, tm)
        y_c = mla_flash(q_c, k_c, v_c, bsz, seq)

        xf = merge_out(xf, y_a, y_b, y_c, proj16, GATE_BLK, w_branch[l].astype(BF16),
                       w_out[l].astype(BF16), min(512, t))
        xf = ffn(xf, norm_ffn_g[l], w_ff1[l].astype(BF16), w_ff2[l].astype(BF16), final_norm_g,
                 l == depth - 1, tm, 1024)
    return xf.reshape(bsz, seq, d)
```

```python
import functools

import jax
import jax.numpy as jnp
from jax import lax
from jax.experimental import pallas as pl
from jax.experimental.pallas import tpu as pltpu

F32 = jnp.float32
BF16 = jnp.bfloat16

D_MODEL = 1024
CHUNK = 64
EPS = 1e-6
N_HEADS = 8
HEAD_DIM = 64
LANES = 128
N_PAIRS = N_HEADS * HEAD_DIM // LANES
WIDTH = N_HEADS * HEAD_DIM

A_LEFT_CHUNKS = 8
A_MAX_REL = 128
A_QTILE = 2 * CHUNK
A_BAND = (A_LEFT_CHUNKS + 2) * CHUNK
A_PAD = A_LEFT_CHUNKS * CHUNK
A_TILES_PER_STEP = 4

C_Q_RANK = 256
C_KV_RANK = 128
C_ROPE = 32
C_NOPE = 64
C_SLOT = LANES
ROPE_BASE = 10000.0
C_TILE = 512

D_FF = 4 * D_MODEL
N_BRANCH = 3

SUB = 16
N_SUB = CHUNK // SUB
HGRN_UNROLL = 2
NEG_BIG = -1e30
LOG2E = 1.4426950408889634

VMEM_LIMIT = 48 * 1024 * 1024
_NT = (((1,), (1,)), ((), ()))


def _cparams(sem):
    return pltpu.CompilerParams(dimension_semantics=sem, vmem_limit_bytes=VMEM_LIMIT)


def _rms(x, g):
    return x * lax.rsqrt(jnp.mean(x * x, axis=-1, keepdims=True) + EPS) * g


def _norm_matmul_kernel(x_ref, g_ref, w_ref, o_ref, h_ref):
    @pl.when(pl.program_id(1) == 0)
    def _():
        h_ref[...] = _rms(x_ref[...], g_ref[...]).astype(BF16)

    o_ref[...] = jnp.dot(h_ref[...], w_ref[...], preferred_element_type=F32).astype(o_ref.dtype)


def norm_matmul(x, g, w, out_dtype, tm, tn, name):
    t, d = x.shape
    n = w.shape[1]
    return pl.pallas_call(
        _norm_matmul_kernel,
        out_shape=jax.ShapeDtypeStruct((t, n), out_dtype),
        grid=(t // tm, n // tn),
        in_specs=[
            pl.BlockSpec((tm, d), lambda i, j: (i, 0)),
            pl.BlockSpec((1, d), lambda i, j: (0, 0)),
            pl.BlockSpec((d, tn), lambda i, j: (0, j)),
        ],
        out_specs=pl.BlockSpec((tm, tn), lambda i, j: (i, j)),
        scratch_shapes=[pltpu.VMEM((tm, d), BF16)],
        compiler_params=_cparams(("parallel", "arbitrary")),
        name=name,
    )(x, g.reshape(1, d), w)


def _trig_kernel(pos_ref, invf_ref, c_ref, s_ref, ct_ref, st_ref):
    ang = pos_ref[...].astype(F32) * invf_ref[...]
    c = jnp.cos(ang)
    s = jnp.sin(ang)
    c_ref[...] = c
    s_ref[...] = s
    ct_ref[...] = c.T
    st_ref[...] = s.T


def rope_tables(positions, tm):
    t = positions.size
    inv_freq = ROPE_BASE ** (-jnp.arange(0, C_ROPE, 2, dtype=F32) / C_ROPE)
    half = C_ROPE // 2
    invf = jnp.zeros((LANES,), F32)
    invf = invf.at[C_NOPE:C_NOPE + half].set(inv_freq).at[C_NOPE + half:C_NOPE + C_ROPE].set(inv_freq)
    return pl.pallas_call(
        _trig_kernel,
        out_shape=(jax.ShapeDtypeStruct((t, LANES), F32),) * 2 + (jax.ShapeDtypeStruct((LANES, t), F32),) * 2,
        grid=(t // tm,),
        in_specs=[
            pl.BlockSpec((tm, 1), lambda i: (i, 0)),
            pl.BlockSpec((1, LANES), lambda i: (0, 0)),
        ],
        out_specs=(pl.BlockSpec((tm, LANES), lambda i: (i, 0)),) * 2
        + (pl.BlockSpec((LANES, tm), lambda i: (0, i)),) * 2,
        compiler_params=_cparams(("parallel",)),
        name="rope_tables",
    )(positions.reshape(t, 1), invf.reshape(1, LANES))


def _band_attn_kernel(q_ref, k_ref, v_ref, bias_ref, o_ref):
    lane = lax.broadcasted_iota(jnp.int32, (A_QTILE, LANES), 1)
    n_shift = A_PAD // A_QTILE
    starts, scores = [], []
    for t in range(A_TILES_PER_STEP):
        c2 = pl.program_id(2) * A_TILES_PER_STEP + t
        start = pl.multiple_of(jnp.maximum(c2 * A_QTILE - A_PAD, 0), A_QTILE)
        shift = jnp.minimum(c2, n_shift)
        kwin = k_ref[pl.ds(start, A_BAND), :]
        q = q_ref[t * A_QTILE:(t + 1) * A_QTILE, :].astype(F32) * (HEAD_DIM ** -0.5)
        qst = jnp.concatenate([jnp.where(lane < HEAD_DIM, q, 0.0), jnp.where(lane < HEAD_DIM, 0.0, q)],
                              axis=0).astype(BF16)
        s = lax.dot_general(qst, kwin, _NT, preferred_element_type=F32)
        starts.append(start)
        scores.append(s + bias_ref[shift, 0])
    for t in range(A_TILES_PER_STEP):
        s = scores[t]
        vwin = v_ref[pl.ds(starts[t], A_BAND), :]
        m = jnp.max(s, axis=-1, keepdims=True)
        p = jnp.exp(s - m)
        l = jnp.sum(p, axis=-1, keepdims=True)
        pv = jnp.dot(p.astype(BF16), vwin, preferred_element_type=F32) / l
        o_ref[t * A_QTILE:(t + 1) * A_QTILE, :] = jnp.where(lane < HEAD_DIM, pv[:A_QTILE], pv[A_QTILE:]).astype(o_ref.dtype)


def band_attention(proj, bias, bsz, seq, q_col, k_col, v_col):
    t = bsz * seq
    rows = A_QTILE * A_TILES_PER_STEP
    n_steps = seq // rows
    n_shift = A_PAD // A_QTILE
    return pl.pallas_call(
        _band_attn_kernel,
        out_shape=jax.ShapeDtypeStruct((t, WIDTH), BF16),
        grid=(bsz, N_PAIRS, n_steps),
        in_specs=[
            pl.BlockSpec((rows, LANES), lambda b, p, c: (b * n_steps + c, q_col + p)),
            pl.BlockSpec((seq, LANES), lambda b, p, c: (b, k_col + p)),
            pl.BlockSpec((seq, LANES), lambda b, p, c: (b, v_col + p)),
            pl.BlockSpec((n_shift + 1, 1, 2 * A_QTILE, A_BAND), lambda b, p, c: (0, p, 0, 0)),
        ],
        out_specs=pl.BlockSpec((rows, LANES), lambda b, p, c: (b * n_steps + c, p)),
        compiler_params=_cparams(("parallel", "parallel", "arbitrary")),
        name="band_attention",
    )(proj, proj, proj, bias.reshape(n_shift + 1, N_PAIRS, 2 * A_QTILE, A_BAND))


A_ROLL = 768


def _band_bias_kernel(r_ref, o_ref):
    d = pl.program_id(0) * A_QTILE
    prof = jnp.broadcast_to(r_ref[...], (A_QTILE, A_ROLL))
    b = pltpu.roll(prof, 0, 1, stride=1, stride_axis=0)[:, :A_BAND]
    i = lax.broadcasted_iota(jnp.int32, (A_QTILE, A_BAND), 0)
    j = lax.broadcasted_iota(jnp.int32, (A_QTILE, A_BAND), 1)
    gap = (d + i) // CHUNK - j // CHUNK
    o_ref[...] = jnp.where((gap >= 0) & (gap <= A_LEFT_CHUNKS), b, NEG_BIG)


def band_bias_table(rel_table):
    n_shift = A_PAD // A_QTILE
    heads = rel_table.shape[0]
    pad = A_ROLL + A_PAD
    ext = jnp.pad(rel_table.astype(F32)[:, ::-1], ((0, 0), (pad, pad)), mode="edge")

    def seg(d, u0, n):
        o = u0 - d + A_MAX_REL + pad
        return ext[:, o:o + n]

    n_neg = A_ROLL - (A_BAND + 1)
    prof = jnp.stack([jnp.concatenate([seg(s * A_QTILE, 0, A_BAND + 1), seg(s * A_QTILE, -n_neg, n_neg)], axis=1)
                      for s in range(n_shift + 1)])
    return pl.pallas_call(
        _band_bias_kernel,
        out_shape=jax.ShapeDtypeStruct((n_shift + 1, heads, A_QTILE, A_BAND), F32),
        grid=(n_shift + 1, heads),
        in_specs=[pl.BlockSpec((None, None, 1, A_ROLL), lambda s, h: (s, h, 0, 0))],
        out_specs=pl.BlockSpec((None, None, A_QTILE, A_BAND), lambda s, h: (s, h, 0, 0)),
        compiler_params=_cparams(("parallel", "parallel")),
        name="band_bias",
    )(prof.reshape(n_shift + 1, heads, 1, A_ROLL))


def _split3(x):
    hi = x.astype(BF16)
    r1 = x - hi.astype(F32)
    mid = r1.astype(BF16)
    lo = (r1 - mid.astype(F32)).astype(BF16)
    return hi, mid, lo


def _hgrn_kernel(q_ref, f_ref, i_ref, g_ref, loglb_ref, log1mlb_ref, onemlb_ref, ng_ref,
                 o_ref, st_ref, *, chunks_per_step):
    @pl.when(pl.program_id(2) == 0)
    def _():
        st_ref[...] = jnp.zeros_like(st_ref)

    log_lb = loglb_ref[...]
    log1m_lb = log1mlb_ref[...]
    one_m_lb = onemlb_ref[...]
    norm_g = ng_ref[...]

    lane_c = lax.broadcasted_iota(jnp.int32, (CHUNK, LANES), 1)
    row_c = lax.broadcasted_iota(jnp.int32, (CHUNK, LANES), 0)
    head0_c = lane_c < HEAD_DIM
    r64 = lax.broadcasted_iota(jnp.int32, (CHUNK, CHUNK), 0)
    c64 = lax.broadcasted_iota(jnp.int32, (CHUNK, CHUNK), 1)
    tril = (c64 <= r64).astype(BF16)
    rl = lax.broadcasted_iota(jnp.int32, (LANES, LANES), 0)
    cl = lax.broadcasted_iota(jnp.int32, (LANES, LANES), 1)
    same_head = (rl // HEAD_DIM) == (cl // HEAD_DIM)
    bd = same_head.astype(BF16)
    ri = lax.broadcasted_iota(jnp.int32, (CHUNK, CHUNK * SUB), 0)
    ci = lax.broadcasted_iota(jnp.int32, (CHUNK, CHUNK * SUB), 1)
    pick = (ci // SUB == ri).astype(BF16)
    row_s = lax.broadcasted_iota(jnp.int32, (SUB, LANES), 0)

    def chunk_body(c, carry):
        r0 = pl.multiple_of(c * CHUNK, CHUNK)
        rows = pl.ds(r0, CHUNK)
        z = f_ref[rows, :]
        qraw = q_ref[rows, :].astype(F32)
        v = i_ref[rows, :].astype(F32)
        graw = g_ref[rows, :].astype(F32)

        log_sig = jnp.minimum(z, 0.0) - jnp.log(1.0 + jnp.exp(-jnp.abs(z)))
        bterm = log1m_lb + log_sig
        log_f = jnp.maximum(log_lb, bterm) + jnp.log(1.0 + jnp.exp(-jnp.abs(log_lb - bterm)))
        ks = one_m_lb * (1.0 / (1.0 + jnp.exp(z)))
        qs = qraw * (1.0 / (1.0 + jnp.exp(-qraw)))

        cum3 = jnp.dot(tril, jnp.concatenate(_split3(log_f), axis=1), preferred_element_type=F32)
        cum = cum3[:, :LANES] + cum3[:, LANES:2 * LANES] + cum3[:, 2 * LANES:]

        zero_row = jnp.zeros((1, LANES), F32)
        c_start = [zero_row] + [cum[SUB * i - 1:SUB * i, :] for i in range(1, N_SUB)]
        c_end = [cum[SUB * i + SUB - 1:SUB * i + SUB, :] for i in range(N_SUB)]
        cstart_full = jnp.concatenate([jnp.broadcast_to(r, (SUB, LANES)) for r in c_start], axis=0)
        cend_full = jnp.concatenate([jnp.broadcast_to(r, (SUB, LANES)) for r in c_end], axis=0)

        q1 = qs * jnp.exp(cum - cstart_full)
        k2 = ks * jnp.exp(cend_full - cum)
        k2b = k2.astype(BF16)
        vb0 = jnp.where(head0_c, v, 0.0).astype(BF16)
        vb1 = jnp.where(head0_c, 0.0, v).astype(BF16)

        sc = [None, None]
        for j in range(N_SUB - 1):
            parts = []
            for i in range(N_SUB):
                if i > j:
                    parts.append(jnp.broadcast_to(jnp.exp(c_start[i] - c_end[j]), (SUB, LANES)))
                else:
                    parts.append(jnp.zeros((SUB, LANES), F32))
            qj = q1 * jnp.concatenate(parts, axis=0)
            col_in_j = (c64 >= j * SUB) & (c64 < (j + 1) * SUB)
            for e in range(2):
                qje = (jnp.where(head0_c, qj, 0.0) if e == 0 else jnp.where(head0_c, 0.0, qj)).astype(BF16)
                s = lax.dot_general(qje, k2b, _NT, preferred_element_type=F32)
                s = jnp.where(col_in_j, s, 0.0)
                sc[e] = s if sc[e] is None else sc[e] + s
        o = (jnp.dot(sc[0].astype(BF16), vb0, preferred_element_type=F32)
             + jnp.dot(sc[1].astype(BF16), vb1, preferred_element_type=F32))

        w_rows = []
        for blk in range(N_SUB):
            cum_b = cum[SUB * blk:SUB * (blk + 1), :]
            ks_b = ks[SUB * blk:SUB * (blk + 1), :]
            for i in range(SUB):
                r = SUB * blk + i
                diff = jnp.minimum(cum[r:r + 1, :] - cum_b, 0.0)
                w = jnp.exp(diff) * (ks_b * qs[r:r + 1, :])
                w_rows.append(jnp.where(row_s <= i, w, 0.0).astype(BF16))
        w_all = jnp.concatenate(w_rows, axis=0)
        sb = jnp.dot(w_all, bd, preferred_element_type=F32)
        v_rep = jnp.concatenate(
            [v[SUB * (r // SUB):SUB * (r // SUB + 1), :] for r in range(CHUNK)], axis=0)
        o = o + jnp.dot(pick, (sb * v_rep).astype(BF16), preferred_element_type=F32)

        st = st_ref[...]
        q_state = (q1 * jnp.exp(cstart_full)).astype(BF16)
        o = o + lax.dot_general(q_state, st.astype(BF16), (((1,), (1,)), ((), ())),
                                preferred_element_type=F32)
        c_last = c_end[N_SUB - 1]
        k_end = (k2 * jnp.exp(c_last - cend_full)).astype(BF16)
        upd = lax.dot_general(v.astype(BF16), k_end, (((0,), (0,)), ((), ())),
                              preferred_element_type=F32)
        st_ref[...] = st * jnp.exp(c_last) + jnp.where(same_head, upd, 0.0)

        osq = o * o
        ms0 = jnp.sum(jnp.where(head0_c, osq, 0.0), axis=-1, keepdims=True)
        ms1 = jnp.sum(jnp.where(head0_c, 0.0, osq), axis=-1, keepdims=True)
        ms = jnp.where(head0_c, ms0, ms1) * (1.0 / HEAD_DIM)
        y = o * lax.rsqrt(ms + EPS) * norm_g
        y = y * (graw * (1.0 / (1.0 + jnp.exp(-graw))))
        o_ref[rows, :] = y.astype(o_ref.dtype)
        return carry

    lax.fori_loop(0, chunks_per_step, chunk_body, 0, unroll=HGRN_UNROLL)


def hgrn2(proj16, proj32, lb, norm_g, bsz, seq, q_col, i_col, g_col, f_col, rows_per_step):
    t = bsz * seq
    n_steps = seq // rows_per_step
    log_lb = jnp.log(lb).reshape(N_PAIRS, 1, LANES)
    log1m_lb = jnp.log1p(-lb).reshape(N_PAIRS, 1, LANES)
    one_m_lb = (1.0 - lb).reshape(N_PAIRS, 1, LANES)
    ng = jnp.tile(norm_g.astype(F32), LANES // HEAD_DIM).reshape(1, LANES)

    def act(col):
        return pl.BlockSpec((rows_per_step, LANES), lambda b, p, s: (b * n_steps + s, col + p))

    def par():
        return pl.BlockSpec((None, 1, LANES), lambda b, p, s: (p, 0, 0))

    return pl.pallas_call(
        functools.partial(_hgrn_kernel, chunks_per_step=rows_per_step // CHUNK),
        out_shape=jax.ShapeDtypeStruct((t, WIDTH), BF16),
        grid=(bsz, N_PAIRS, n_steps),
        in_specs=[act(q_col), act(f_col), act(i_col), act(g_col), par(), par(), par(),
                  pl.BlockSpec((1, LANES), lambda b, p, s: (0, 0))],
        out_specs=pl.BlockSpec((rows_per_step, LANES), lambda b, p, s: (b * n_steps + s, p)),
        scratch_shapes=[pltpu.VMEM((LANES, LANES), F32)],
        compiler_params=_cparams(("parallel", "parallel", "arbitrary")),
        name="hgrn2",
    )(proj16, proj32, proj16, proj16, log_lb, log1m_lb, one_m_lb, ng)


def _mla_prep_kernel(pc_ref, ct_ref, st_ref, ctt_ref, stt_ref, gq_ref, gkv_ref,
                     wq1t_ref, wq2t_ref, wk_ref, wvt_ref, qt_ref, k_ref, vt_ref):
    pc = pc_ref[...]
    qn = _rms(pc[:, :C_Q_RANK], gq_ref[...]).astype(BF16)
    kvn = _rms(pc[:, C_Q_RANK:C_Q_RANK + C_KV_RANK], gkv_ref[...]).astype(BF16)
    kr = pc[:, C_Q_RANK + C_KV_RANK:C_Q_RANK + C_KV_RANK + C_SLOT]
    kr_sw = pc[:, C_Q_RANK + C_KV_RANK + C_SLOT:C_Q_RANK + C_KV_RANK + 2 * C_SLOT]
    a_t = lax.dot_general(wq1t_ref[...], qn, _NT, preferred_element_type=F32)
    b_t = lax.dot_general(wq2t_ref[...], qn, _NT, preferred_element_type=F32)
    kn = jnp.dot(kvn, wk_ref[...], preferred_element_type=F32)
    vt_ref[...] = lax.dot_general(wvt_ref[...], kvn, _NT, preferred_element_type=F32).astype(vt_ref.dtype)
    kr_rot = kr * ct_ref[...] + kr_sw * st_ref[...]
    ctt = ctt_ref[...]
    stt = stt_ref[...]
    for h in range(N_HEADS):
        sl = slice(h * C_SLOT, (h + 1) * C_SLOT)
        qt_ref[sl, :] = (a_t[sl, :] * ctt + b_t[sl, :] * stt).astype(qt_ref.dtype)
        k_ref[:, sl] = (kn[:, sl] + kr_rot).astype(k_ref.dtype)


def mla_prep(proj32, tabs, gq, gkv, wq1t, wq2t, wk, wvt, tm):
    t = proj32.shape[0]
    c_in = C_Q_RANK + C_KV_RANK + 2 * C_SLOT
    ctab, stab, ctab_t, stab_t = tabs

    def full(a):
        return pl.BlockSpec(a.shape, lambda i: (0,) * a.ndim)

    gq = gq.reshape(1, -1)
    gkv = gkv.reshape(1, -1)
    return pl.pallas_call(
        _mla_prep_kernel,
        out_shape=(jax.ShapeDtypeStruct((N_HEADS * C_SLOT, t), BF16),
                   jax.ShapeDtypeStruct((t, N_HEADS * C_SLOT), BF16),
                   jax.ShapeDtypeStruct((WIDTH, t), BF16)),
        grid=(t // tm,),
        in_specs=[pl.BlockSpec((tm, c_in), lambda i: (i, 0)),
                  pl.BlockSpec((tm, LANES), lambda i: (i, 0)),
                  pl.BlockSpec((tm, LANES), lambda i: (i, 0)),
                  pl.BlockSpec((LANES, tm), lambda i: (0, i)),
                  pl.BlockSpec((LANES, tm), lambda i: (0, i)),
                  full(gq), full(gkv), full(wq1t), full(wq2t), full(wk), full(wvt)],
        out_specs=(pl.BlockSpec((N_HEADS * C_SLOT, tm), lambda i: (0, i)),
                   pl.BlockSpec((tm, N_HEADS * C_SLOT), lambda i: (i, 0)),
                   pl.BlockSpec((WIDTH, tm), lambda i: (0, i))),
        compiler_params=_cparams(("parallel",)),
        name="mla_prep",
    )(proj32, ctab, stab, ctab_t, stab_t, gq, gkv, wq1t, wq2t, wk, wvt)


def _mla_flash_kernel(qt_ref, k_ref, vt_ref, o_ref, acc_ref, m_ref, l_ref, s_ref):
    qi = pl.program_id(2)
    c = (C_NOPE + C_ROPE) ** -0.5 * LOG2E

    m_ref[...] = jnp.full_like(m_ref, -jnp.inf)
    l_ref[...] = jnp.zeros_like(l_ref)
    acc_ref[...] = jnp.zeros_like(acc_ref)

    def scores(j, e):
        off = pl.multiple_of(j * C_TILE, C_TILE)
        k = k_ref[pl.ds(off, C_TILE), e * C_SLOT:(e + 1) * C_SLOT]
        return jnp.dot(k, qt_ref[e * C_SLOT:(e + 1) * C_SLOT, :], preferred_element_type=F32)

    def softmax_pv(j, e, s):
        off = pl.multiple_of(j * C_TILE, C_TILE)
        m_prev = m_ref[e]
        m_new = jnp.maximum(m_prev, jnp.max(s, axis=0, keepdims=True))
        p = jnp.exp2((s - m_new) * c)
        alpha = jnp.exp2((m_prev - m_new) * c)
        l_ref[e] = alpha * l_ref[e] + jnp.sum(p, axis=0, keepdims=True)
        vt = vt_ref[e * HEAD_DIM:(e + 1) * HEAD_DIM, pl.ds(off, C_TILE)]
        acc_ref[e] = alpha * acc_ref[e] + jnp.dot(vt, p.astype(BF16), preferred_element_type=F32)
        m_ref[e] = m_new

    for e in range(2):
        s_ref[e] = scores(0, e)

    def body(j, carry):
        for e in range(2):
            s = s_ref[e]
            s_ref[e] = scores(j + 1, e)
            softmax_pv(j, e, s)
        return carry

    lax.fori_loop(0, qi, body, 0)
    kc = lax.broadcasted_iota(jnp.int32, (C_TILE, C_TILE), 0) // CHUNK
    qc = lax.broadcasted_iota(jnp.int32, (C_TILE, C_TILE), 1) // CHUNK
    for e in range(2):
        softmax_pv(qi, e, jnp.where(kc <= qc, s_ref[e], NEG_BIG))

    o_t = jnp.concatenate([acc_ref[0] / l_ref[0], acc_ref[1] / l_ref[1]], axis=0)
    o_ref[...] = o_t.T.astype(o_ref.dtype)


def mla_flash(qt, k, vt, bsz, seq):
    t = bsz * seq
    n_qt = seq // C_TILE
    return pl.pallas_call(
        _mla_flash_kernel,
        out_shape=jax.ShapeDtypeStruct((t, WIDTH), BF16),
        grid=(bsz, N_PAIRS, n_qt),
        in_specs=[
            pl.BlockSpec((2 * C_SLOT, C_TILE), lambda b, p, i: (p, b * n_qt + i)),
            pl.BlockSpec((seq, 2 * C_SLOT), lambda b, p, i: (b, p)),
            pl.BlockSpec((LANES, seq), lambda b, p, i: (p, b)),
        ],
        out_specs=pl.BlockSpec((C_TILE, LANES), lambda b, p, i: (b * n_qt + i, p)),
        scratch_shapes=[pltpu.VMEM((2, HEAD_DIM, C_TILE), F32),
                        pltpu.VMEM((2, 1, C_TILE), F32),
                        pltpu.VMEM((2, 1, C_TILE), F32),
                        pltpu.VMEM((2, C_TILE, C_TILE), F32)],
        compiler_params=_cparams(("parallel", "parallel", "arbitrary")),
        name="mla_flash",
    )(qt, k, vt)


def _merge_kernel(x_ref, ya_ref, yb_ref, yc_ref, g_ref, wbr_ref, wout_ref, o_ref):
    merged = None
    for n, y_ref in enumerate((ya_ref, yb_ref, yc_ref)):
        up = jnp.dot(y_ref[...], wbr_ref[n], preferred_element_type=F32)
        gl = g_ref[:, n * D_MODEL:(n + 1) * D_MODEL].astype(F32)
        term = (1.0 / (1.0 + jnp.exp(-gl))) * up
        merged = term if merged is None else merged + term
    o_ref[...] = x_ref[...] + jnp.dot(merged.astype(BF16), wout_ref[...], preferred_element_type=F32)


def merge_out(x, ya, yb, yc, proj16, gate_blk, wbr, wout, tm):
    t = x.shape[0]
    return pl.pallas_call(
        _merge_kernel,
        out_shape=jax.ShapeDtypeStruct((t, D_MODEL), F32),
        grid=(t // tm,),
        in_specs=[
            pl.BlockSpec((tm, D_MODEL), lambda i: (i, 0)),
            pl.BlockSpec((tm, WIDTH), lambda i: (i, 0)),
            pl.BlockSpec((tm, WIDTH), lambda i: (i, 0)),
            pl.BlockSpec((tm, WIDTH), lambda i: (i, 0)),
            pl.BlockSpec((tm, N_BRANCH * D_MODEL), lambda i: (i, gate_blk)),
            pl.BlockSpec(wbr.shape, lambda i: (0, 0, 0)),
            pl.BlockSpec(wout.shape, lambda i: (0, 0)),
        ],
        out_specs=pl.BlockSpec((tm, D_MODEL), lambda i: (i, 0)),
        compiler_params=_cparams(("parallel",)),
        name="merge_out",
    )(x, ya, yb, yc, proj16, wbr, wout)


def _ffn_kernel(x_ref, g_ref, w1_ref, w2_ref, gf_ref, o_ref, h_ref, acc_ref, *, final_norm):
    k = pl.program_id(1)

    @pl.when(k == 0)
    def _():
        h_ref[...] = _rms(x_ref[...], g_ref[...]).astype(BF16)
        acc_ref[...] = jnp.zeros_like(acc_ref)

    u = jnp.maximum(jnp.dot(h_ref[...], w1_ref[...], preferred_element_type=F32), 0.0)
    acc_ref[...] += jnp.dot((u * u).astype(BF16), w2_ref[...], preferred_element_type=F32)

    @pl.when(k == pl.num_programs(1) - 1)
    def _():
        y = x_ref[...] + acc_ref[...]
        if final_norm:
            y = _rms(y, gf_ref[...])
        o_ref[...] = y


def ffn(x, g, w1, w2, gf, final_norm, tm, tf):
    t, d = x.shape
    dff = w1.shape[1]
    return pl.pallas_call(
        functools.partial(_ffn_kernel, final_norm=final_norm),
        out_shape=jax.ShapeDtypeStruct((t, d), F32),
        grid=(t // tm, dff // tf),
        in_specs=[
            pl.BlockSpec((tm, d), lambda i, k: (i, 0)),
            pl.BlockSpec((1, d), lambda i, k: (0, 0)),
            pl.BlockSpec((d, tf), lambda i, k: (0, k)),
            pl.BlockSpec((tf, d), lambda i, k: (k, 0)),
            pl.BlockSpec((1, d), lambda i, k: (0, 0)),
        ],
        out_specs=pl.BlockSpec((tm, d), lambda i, k: (i, 0)),
        scratch_shapes=[pltpu.VMEM((tm, d), BF16), pltpu.VMEM((tm, d), F32)],
        compiler_params=_cparams(("parallel", "arbitrary")),
        name="ffn",
    )(x, g.reshape(1, d), w1, w2, gf.reshape(1, d))


COL_AQ, COL_AK, COL_AV = 0, 4, 8
COL_BQ, COL_BI, COL_BG = 12, 16, 20
GATE_BLK = 1
COL32_BF = (C_Q_RANK + C_KV_RANK + 2 * C_SLOT) // LANES


def _prep_w_in(w):
    o = 0
    parts = {}
    for name, size in (("aq", 512), ("ak", 512), ("av", 512), ("bq", 512), ("bf", 512), ("bi", 512),
                       ("bg", 512), ("cq", C_Q_RANK), ("ckv", C_KV_RANK), ("ckr", C_ROPE),
                       ("gate", N_BRANCH * D_MODEL)):
        parts[name] = w[:, o:o + size]
        o += size
    w16 = jnp.concatenate([parts[n] for n in ("aq", "ak", "av", "bq", "bi", "bg", "gate")], axis=1)
    half = C_ROPE // 2
    x1, x2 = parts["ckr"][:, :half], parts["ckr"][:, half:]
    zl = jnp.zeros((w.shape[0], C_NOPE), w.dtype)
    zr = jnp.zeros((w.shape[0], C_SLOT - C_NOPE - C_ROPE), w.dtype)
    kr_slot = jnp.concatenate([zl, x1, x2, zr], axis=1)
    kr_sw = jnp.concatenate([zl, -x2, x1, zr], axis=1)
    w32 = jnp.concatenate([parts["cq"], parts["ckv"], kr_slot, kr_sw, parts["bf"]], axis=1)
    return w16.astype(BF16), w32.astype(BF16)


def _prep_w_mla(w_uq, w_ukv):
    half = C_ROPE // 2
    per_q = C_NOPE + C_ROPE
    wq = w_uq.reshape(C_Q_RANK, N_HEADS, per_q)
    nope, x1, x2 = wq[..., :C_NOPE], wq[..., C_NOPE:C_NOPE + half], wq[..., C_NOPE + half:]
    zpad = jnp.zeros((C_Q_RANK, N_HEADS, C_SLOT - per_q), w_uq.dtype)
    wq1 = jnp.concatenate([nope, x1, x2, zpad], axis=-1).reshape(C_Q_RANK, N_HEADS * C_SLOT)
    wq2 = jnp.concatenate([jnp.zeros_like(nope), -x2, x1, zpad], axis=-1).reshape(C_Q_RANK, N_HEADS * C_SLOT)
    wkv = w_ukv.reshape(C_KV_RANK, N_HEADS, C_NOPE + HEAD_DIM)
    k_nope, v = wkv[..., :C_NOPE], wkv[..., C_NOPE:]
    wk = jnp.concatenate([k_nope, jnp.zeros((C_KV_RANK, N_HEADS, C_SLOT - C_NOPE), w_ukv.dtype)],
                         axis=-1).reshape(C_KV_RANK, N_HEADS * C_SLOT)
    wv = v.reshape(C_KV_RANK, WIDTH)
    return wq1.T.astype(BF16), wq2.T.astype(BF16), wk.astype(BF16), wv.T.astype(BF16)


def kernel(x, positions, norm_mix_g, w_in, rel_bias, hgrn_lb_logits, hgrn_norm_g, mla_q_norm_g,
           mla_kv_norm_g, mla_w_uq, mla_w_ukv, w_branch, w_out, norm_ffn_g, w_ff1, w_ff2, final_norm_g):
    bsz, seq, d = x.shape
    depth = w_in.shape[0]
    t = bsz * seq
    assert d == D_MODEL and seq % C_TILE == 0 and seq >= A_BAND
    tm = min(1024, t)
    hgrn_rows = min(512, seq)

    p_lb = jax.nn.softmax(hgrn_lb_logits.astype(F32), axis=0)
    lb_all = jnp.cumsum(p_lb, axis=0)
    lb_all = lb_all - lb_all[0:1]

    tabs = rope_tables(positions, tm)
    xf = x.reshape(t, d)
    for l in range(depth):
        w16, w32 = _prep_w_in(w_in[l])
        wq1t, wq2t, wk, wvt = _prep_w_mla(mla_w_uq[l], mla_w_ukv[l])
        proj16 = norm_matmul(xf, norm_mix_g[l], w16, BF16, tm, 1024, "in_proj_bf16")
        proj32 = norm_matmul(xf, norm_mix_g[l], w32, F32, tm, w32.shape[1], "in_proj_f32")

        y_a = band_attention(proj16, band_bias_table(rel_bias[l]), bsz, seq, COL_AQ, COL_AK, COL_AV)
        y_b = hgrn2(proj16, proj32, lb_all[l], hgrn_norm_g[l], bsz, seq,
                    COL_BQ, COL_BI, COL_BG, COL32_BF, hgrn_rows)
        qt_c, k_c, vt_c = mla_prep(proj32, tabs, mla_q_norm_g[l], mla_kv_norm_g[l], wq1t, wq2t, wk, wvt, tm)
        y_c = mla_flash(qt_c, k_c, vt_c, bsz, seq)

        xf = merge_out(xf, y_a, y_b, y_c, proj16, GATE_BLK, w_branch[l].astype(BF16),
                       w_out[l].astype(BF16), min(512, t))
        xf = ffn(xf, norm_ffn_g[l], w_ff1[l].astype(BF16), w_ff2[l].astype(BF16), final_norm_g,
                 l == depth - 1, tm, 1024)
    return xf.reshape(bsz, seq, d)
```

```python
import functools

import jax
import jax.numpy as jnp
from jax import lax
from jax.experimental import pallas as pl
from jax.experimental.pallas import tpu as pltpu

F32 = jnp.float32
BF16 = jnp.bfloat16

D_MODEL = 1024
CHUNK = 64
EPS = 1e-6
N_HEADS = 8
HEAD_DIM = 64
LANES = 128
N_PAIRS = N_HEADS * HEAD_DIM // LANES
WIDTH = N_HEADS * HEAD_DIM

A_LEFT_CHUNKS = 8
A_MAX_REL = 128
A_QTILE = 2 * CHUNK
A_BAND = (A_LEFT_CHUNKS + 2) * CHUNK
A_PAD = A_LEFT_CHUNKS * CHUNK
A_TILES_PER_STEP = 4

C_Q_RANK = 256
C_KV_RANK = 128
C_ROPE = 32
C_NOPE = 64
C_SLOT = LANES
ROPE_BASE = 10000.0
C_TILE = 512
ONES_ROWS = 16

D_FF = 4 * D_MODEL
N_BRANCH = 3

SUB = 16
N_SUB = CHUNK // SUB
NEG_BIG = -1e30
LOG2E = 1.4426950408889634

VMEM_LIMIT = 48 * 1024 * 1024
_NT = (((1,), (1,)), ((), ()))


def _cparams(sem):
    return pltpu.CompilerParams(dimension_semantics=sem, vmem_limit_bytes=VMEM_LIMIT)


def _rms(x, g):
    return x * lax.rsqrt(jnp.mean(x * x, axis=-1, keepdims=True) + EPS) * g


def _norm_matmul_kernel(x_ref, g_ref, w_ref, o_ref, h_ref):
    @pl.when(pl.program_id(1) == 0)
    def _():
        h_ref[...] = _rms(x_ref[...], g_ref[...]).astype(BF16)

    o_ref[...] = jnp.dot(h_ref[...], w_ref[...], preferred_element_type=F32).astype(o_ref.dtype)


def norm_matmul(x, g, w, out_dtype, tm, tn, name):
    t, d = x.shape
    n = w.shape[1]
    return pl.pallas_call(
        _norm_matmul_kernel,
        out_shape=jax.ShapeDtypeStruct((t, n), out_dtype),
        grid=(t // tm, n // tn),
        in_specs=[
            pl.BlockSpec((tm, d), lambda i, j: (i, 0)),
            pl.BlockSpec((1, d), lambda i, j: (0, 0)),
            pl.BlockSpec((d, tn), lambda i, j: (0, j)),
        ],
        out_specs=pl.BlockSpec((tm, tn), lambda i, j: (i, j)),
        scratch_shapes=[pltpu.VMEM((tm, d), BF16)],
        compiler_params=_cparams(("parallel", "arbitrary")),
        name=name,
    )(x, g.reshape(1, d), w)


def _trig_kernel(pos_ref, invf_ref, c_ref, s_ref, ct_ref, st_ref):
    ang = pos_ref[...].astype(F32) * invf_ref[...]
    c = jnp.cos(ang)
    s = jnp.sin(ang)
    c_ref[...] = c
    s_ref[...] = s
    ct_ref[...] = c.T
    st_ref[...] = s.T


def rope_tables(positions, tm):
    t = positions.size
    inv_freq = ROPE_BASE ** (-jnp.arange(0, C_ROPE, 2, dtype=F32) / C_ROPE)
    half = C_ROPE // 2
    invf = jnp.zeros((LANES,), F32)
    invf = invf.at[C_NOPE:C_NOPE + half].set(inv_freq).at[C_NOPE + half:C_NOPE + C_ROPE].set(inv_freq)
    return pl.pallas_call(
        _trig_kernel,
        out_shape=(jax.ShapeDtypeStruct((t, LANES), F32),) * 2 + (jax.ShapeDtypeStruct((LANES, t), F32),) * 2,
        grid=(t // tm,),
        in_specs=[
            pl.BlockSpec((tm, 1), lambda i: (i, 0)),
            pl.BlockSpec((1, LANES), lambda i: (0, 0)),
        ],
        out_specs=(pl.BlockSpec((tm, LANES), lambda i: (i, 0)),) * 2
        + (pl.BlockSpec((LANES, tm), lambda i: (0, i)),) * 2,
        compiler_params=_cparams(("parallel",)),
        name="rope_tables",
    )(positions.reshape(t, 1), invf.reshape(1, LANES))


def _band_attn_kernel(q_ref, k_ref, v_ref, bias_ref, o_ref):
    lane = lax.broadcasted_iota(jnp.int32, (A_QTILE, LANES), 1)
    n_shift = A_PAD // A_QTILE
    starts, scores = [], []
    for t in range(A_TILES_PER_STEP):
        c2 = pl.program_id(2) * A_TILES_PER_STEP + t
        start = pl.multiple_of(jnp.maximum(c2 * A_QTILE - A_PAD, 0), A_QTILE)
        shift = jnp.minimum(c2, n_shift)
        kwin = k_ref[pl.ds(start, A_BAND), :]
        q = q_ref[t * A_QTILE:(t + 1) * A_QTILE, :].astype(F32) * (HEAD_DIM ** -0.5)
        qst = jnp.concatenate([jnp.where(lane < HEAD_DIM, q, 0.0), jnp.where(lane < HEAD_DIM, 0.0, q)],
                              axis=0).astype(BF16)
        s = lax.dot_general(qst, kwin, _NT, preferred_element_type=F32)
        starts.append(start)
        scores.append(s + bias_ref[shift, 0])
    for t in range(A_TILES_PER_STEP):
        s = scores[t]
        vwin = v_ref[pl.ds(starts[t], A_BAND), :]
        m = jnp.max(s, axis=-1, keepdims=True)
        p = jnp.exp(s - m)
        l = jnp.sum(p, axis=-1, keepdims=True)
        pv = jnp.dot(p.astype(BF16), vwin, preferred_element_type=F32) / l
        o_ref[t * A_QTILE:(t + 1) * A_QTILE, :] = jnp.where(lane < HEAD_DIM, pv[:A_QTILE], pv[A_QTILE:]).astype(o_ref.dtype)


def band_attention(proj, bias, bsz, seq, q_col, k_col, v_col):
    t = bsz * seq
    rows = A_QTILE * A_TILES_PER_STEP
    n_steps = seq // rows
    n_shift = A_PAD // A_QTILE
    return pl.pallas_call(
        _band_attn_kernel,
        out_shape=jax.ShapeDtypeStruct((t, WIDTH), BF16),
        grid=(bsz, N_PAIRS, n_steps),
        in_specs=[
            pl.BlockSpec((rows, LANES), lambda b, p, c: (b * n_steps + c, q_col + p)),
            pl.BlockSpec((seq, LANES), lambda b, p, c: (b, k_col + p)),
            pl.BlockSpec((seq, LANES), lambda b, p, c: (b, v_col + p)),
            pl.BlockSpec((n_shift + 1, 1, 2 * A_QTILE, A_BAND), lambda b, p, c: (0, p, 0, 0)),
        ],
        out_specs=pl.BlockSpec((rows, LANES), lambda b, p, c: (b * n_steps + c, p)),
        compiler_params=_cparams(("parallel", "parallel", "arbitrary")),
        name="band_attention",
    )(proj, proj, proj, bias.reshape(n_shift + 1, N_PAIRS, 2 * A_QTILE, A_BAND))


A_ROLL = 768


def _band_bias_kernel(r_ref, o_ref):
    d = pl.program_id(0) * A_QTILE
    prof = jnp.broadcast_to(r_ref[...], (A_QTILE, A_ROLL))
    b = pltpu.roll(prof, 0, 1, stride=1, stride_axis=0)[:, :A_BAND]
    i = lax.broadcasted_iota(jnp.int32, (A_QTILE, A_BAND), 0)
    j = lax.broadcasted_iota(jnp.int32, (A_QTILE, A_BAND), 1)
    gap = (d + i) // CHUNK - j // CHUNK
    o_ref[...] = jnp.where((gap >= 0) & (gap <= A_LEFT_CHUNKS), b, NEG_BIG)


def band_bias_table(rel_table):
    n_shift = A_PAD // A_QTILE
    heads = rel_table.shape[0]
    pad = A_ROLL + A_PAD
    ext = jnp.pad(rel_table.astype(F32)[:, ::-1], ((0, 0), (pad, pad)), mode="edge")

    def seg(d, u0, n):
        o = u0 - d + A_MAX_REL + pad
        return ext[:, o:o + n]

    n_neg = A_ROLL - (A_BAND + 1)
    prof = jnp.stack([jnp.concatenate([seg(s * A_QTILE, 0, A_BAND + 1), seg(s * A_QTILE, -n_neg, n_neg)], axis=1)
                      for s in range(n_shift + 1)])
    return pl.pallas_call(
        _band_bias_kernel,
        out_shape=jax.ShapeDtypeStruct((n_shift + 1, heads, A_QTILE, A_BAND), F32),
        grid=(n_shift + 1, heads),
        in_specs=[pl.BlockSpec((None, None, 1, A_ROLL), lambda s, h: (s, h, 0, 0))],
        out_specs=pl.BlockSpec((None, None, A_QTILE, A_BAND), lambda s, h: (s, h, 0, 0)),
        compiler_params=_cparams(("parallel", "parallel")),
        name="band_bias",
    )(prof.reshape(n_shift + 1, heads, 1, A_ROLL))


def _split3(x):
    hi = x.astype(BF16)
    r1 = x - hi.astype(F32)
    mid = r1.astype(BF16)
    lo = (r1 - mid.astype(F32)).astype(BF16)
    return hi, mid, lo


def _hgrn_kernel(q_ref, f_ref, i_ref, g_ref, loglb_ref, log1mlb_ref, ng_ref, o_ref, st_ref, *, n_chunks):
    @pl.when(pl.program_id(2) == 0)
    def _():
        st_ref[...] = jnp.zeros_like(st_ref)

    log_lb = loglb_ref[...]
    log1m_lb = log1mlb_ref[...]
    norm_g = ng_ref[...]
    rows = n_chunks * CHUNK
    n_blk = rows // SUB

    head0 = lax.broadcasted_iota(jnp.int32, (rows, LANES), 1) < HEAD_DIM
    head0_c = lax.broadcasted_iota(jnp.int32, (CHUNK, LANES), 1) < HEAD_DIM
    r64 = lax.broadcasted_iota(jnp.int32, (CHUNK, CHUNK), 0)
    c64 = lax.broadcasted_iota(jnp.int32, (CHUNK, CHUNK), 1)
    tril = (c64 <= r64).astype(BF16)
    c64s = lax.broadcasted_iota(jnp.int32, (2 * CHUNK, CHUNK), 1)
    rl = lax.broadcasted_iota(jnp.int32, (LANES, LANES), 0)
    cl = lax.broadcasted_iota(jnp.int32, (LANES, LANES), 1)
    same_head = (rl // HEAD_DIM) == (cl // HEAD_DIM)
    bd = same_head.astype(BF16)
    ri = lax.broadcasted_iota(jnp.int32, (CHUNK, CHUNK * SUB), 0)
    ci = lax.broadcasted_iota(jnp.int32, (CHUNK, CHUNK * SUB), 1)
    pick = (ci // SUB == ri).astype(BF16)
    row_s = lax.broadcasted_iota(jnp.int32, (SUB, LANES), 0)

    def chunk(x, c):
        return x[c * CHUNK:(c + 1) * CHUNK]

    z = f_ref[...]
    qraw = q_ref[...].astype(F32)
    v = i_ref[...].astype(F32)
    graw = g_ref[...].astype(F32)

    log_sig = jnp.minimum(z, 0.0) - jnp.log(1.0 + jnp.exp(-jnp.abs(z)))
    bterm = log1m_lb + log_sig
    log_f = jnp.maximum(log_lb, bterm) + jnp.log(1.0 + jnp.exp(-jnp.abs(log_lb - bterm)))
    log_k = bterm - z
    qs = qraw * (1.0 / (1.0 + jnp.exp(-qraw)))

    x3 = jnp.concatenate(_split3(log_f), axis=1)
    cum3 = [jnp.dot(tril, chunk(x3, c), preferred_element_type=F32) for c in range(n_chunks)]
    cum = jnp.concatenate([t[:, :LANES] + t[:, LANES:2 * LANES] + t[:, 2 * LANES:] for t in cum3], axis=0)

    zero_row = jnp.zeros((1, LANES), F32)
    c_end = [cum[SUB * b + SUB - 1:SUB * b + SUB, :] for b in range(n_blk)]
    c_start = [zero_row if b % N_SUB == 0 else c_end[b - 1] for b in range(n_blk)]
    c_last = [c_end[c * N_SUB + N_SUB - 1] for c in range(n_chunks)]

    def rows_of(blocks):
        return jnp.concatenate([jnp.broadcast_to(r, (SUB, LANES)) for r in blocks], axis=0)

    cstart_full = rows_of(c_start)
    cend_full = rows_of(c_end)
    clast_full = jnp.concatenate([jnp.broadcast_to(r, (CHUNK, LANES)) for r in c_last], axis=0)
    q1 = qs * jnp.exp(cum - cstart_full)
    lk = log_k - cum
    k2b = jnp.exp(cend_full + lk).astype(BF16)
    k_end = jnp.exp(clast_full + lk).astype(BF16)
    q_state = (qs * jnp.exp(cum)).astype(BF16)
    vb = v.astype(BF16)

    s_cross = []
    for j in range(N_SUB - 1):
        d_rows = []
        for b in range(n_blk):
            if b % N_SUB > j:
                d_rows.append(jnp.exp(c_start[b] - c_end[(b // N_SUB) * N_SUB + j]))
            else:
                d_rows.append(zero_row)
        qj = q1 * rows_of(d_rows)
        q0 = jnp.where(head0, qj, 0.0).astype(BF16)
        q1h = jnp.where(head0, 0.0, qj).astype(BF16)
        for c in range(n_chunks):
            qst = jnp.concatenate([chunk(q0, c), chunk(q1h, c)], axis=0)
            s_cross.append(lax.dot_general(qst, chunk(k2b, c), _NT, preferred_element_type=F32))
    o_cross = []
    for c in range(n_chunks):
        sc = None
        for j in range(N_SUB - 1):
            col_in_j = (c64s >= j * SUB) & (c64s < (j + 1) * SUB)
            s = jnp.where(col_in_j, s_cross[j * n_chunks + c], 0.0)
            sc = s if sc is None else sc + s
        o2 = jnp.dot(sc.astype(BF16), chunk(vb, c), preferred_element_type=F32)
        o_cross.append(jnp.where(head0_c, o2[:CHUNK], o2[CHUNK:]))

    a = cum - log_k
    w_rows = []
    for b in range(n_blk):
        a_b = a[SUB * b:SUB * (b + 1), :]
        for i in range(SUB):
            r = SUB * b + i
            arg = jnp.where(row_s <= i, cum[r:r + 1, :] - a_b, NEG_BIG)
            w_rows.append((jnp.exp(arg) * qs[r:r + 1, :]).astype(BF16))
    w_all = jnp.concatenate(w_rows, axis=0)
    sb = jnp.dot(w_all, bd, preferred_element_type=F32)
    o_diag = []
    for c in range(n_chunks):
        v_rep = jnp.concatenate([v[SUB * (r // SUB):SUB * (r // SUB + 1), :]
                                 for r in range(c * CHUNK, (c + 1) * CHUNK)], axis=0)
        sb_c = sb[c * CHUNK * SUB:(c + 1) * CHUNK * SUB]
        o_diag.append(jnp.dot(pick, (sb_c * v_rep).astype(BF16), preferred_element_type=F32))

    upd = [jnp.where(same_head,
                     lax.dot_general(chunk(vb, c), chunk(k_end, c), (((0,), (0,)), ((), ())),
                                     preferred_element_type=F32), 0.0) for c in range(n_chunks)]
    st = st_ref[...]
    o_state = []
    for c in range(n_chunks):
        o_state.append(lax.dot_general(chunk(q_state, c), st.astype(BF16), _NT, preferred_element_type=F32))
        st = st * jnp.exp(c_last[c]) + upd[c]
    st_ref[...] = st

    o = jnp.concatenate([o_cross[c] + o_diag[c] + o_state[c] for c in range(n_chunks)], axis=0)
    osq = o * o
    ms0 = jnp.sum(jnp.where(head0, osq, 0.0), axis=-1, keepdims=True)
    ms1 = jnp.sum(jnp.where(head0, 0.0, osq), axis=-1, keepdims=True)
    ms = jnp.where(head0, ms0, ms1) * (1.0 / HEAD_DIM)
    y = o * lax.rsqrt(ms + EPS) * norm_g
    y = y * (graw * (1.0 / (1.0 + jnp.exp(-graw))))
    o_ref[...] = y.astype(o_ref.dtype)


def hgrn2(proj16, proj32, lb, norm_g, bsz, seq, q_col, i_col, g_col, f_col, rows_per_step):
    t = bsz * seq
    n_steps = seq // rows_per_step
    log_lb = jnp.log(lb).reshape(N_PAIRS, 1, LANES)
    log1m_lb = jnp.log1p(-lb).reshape(N_PAIRS, 1, LANES)
    ng = jnp.tile(norm_g.astype(F32), LANES // HEAD_DIM).reshape(1, LANES)

    def act(col):
        return pl.BlockSpec((rows_per_step, LANES), lambda b, p, s: (b * n_steps + s, col + p))

    def par():
        return pl.BlockSpec((None, 1, LANES), lambda b, p, s: (p, 0, 0))

    return pl.pallas_call(
        functools.partial(_hgrn_kernel, n_chunks=rows_per_step // CHUNK),
        out_shape=jax.ShapeDtypeStruct((t, WIDTH), BF16),
        grid=(bsz, N_PAIRS, n_steps),
        in_specs=[act(q_col), act(f_col), act(i_col), act(g_col), par(), par(),
                  pl.BlockSpec((1, LANES), lambda b, p, s: (0, 0))],
        out_specs=pl.BlockSpec((rows_per_step, LANES), lambda b, p, s: (b * n_steps + s, p)),
        scratch_shapes=[pltpu.VMEM((LANES, LANES), F32)],
        compiler_params=_cparams(("parallel", "parallel", "arbitrary")),
        name="hgrn2",
    )(proj16, proj32, proj16, proj16, log_lb, log1m_lb, ng)


def _mla_prep_kernel(pc_ref, ct_ref, st_ref, ctt_ref, stt_ref, gq_ref, gkv_ref,
                     wq1t_ref, wq2t_ref, wk_ref, wvt_ref, qt_ref, k_ref, vt_ref):
    pc = pc_ref[...]
    qn = _rms(pc[:, :C_Q_RANK], gq_ref[...]).astype(BF16)
    kvn = _rms(pc[:, C_Q_RANK:C_Q_RANK + C_KV_RANK], gkv_ref[...]).astype(BF16)
    kr = pc[:, C_Q_RANK + C_KV_RANK:C_Q_RANK + C_KV_RANK + C_SLOT]
    kr_sw = pc[:, C_Q_RANK + C_KV_RANK + C_SLOT:C_Q_RANK + C_KV_RANK + 2 * C_SLOT]
    a_t = lax.dot_general(wq1t_ref[...], qn, _NT, preferred_element_type=F32)
    b_t = lax.dot_general(wq2t_ref[...], qn, _NT, preferred_element_type=F32)
    kn = jnp.dot(kvn, wk_ref[...], preferred_element_type=F32)
    vt_ref[...] = lax.dot_general(wvt_ref[...], kvn, _NT, preferred_element_type=F32).astype(vt_ref.dtype)
    kr_rot = kr * ct_ref[...] + kr_sw * st_ref[...]
    ctt = ctt_ref[...]
    stt = stt_ref[...]
    for h in range(N_HEADS):
        sl = slice(h * C_SLOT, (h + 1) * C_SLOT)
        qt_ref[sl, :] = (a_t[sl, :] * ctt + b_t[sl, :] * stt).astype(qt_ref.dtype)
        k_ref[:, sl] = (kn[:, sl] + kr_rot).astype(k_ref.dtype)


def mla_prep(proj32, tabs, gq, gkv, wq1t, wq2t, wk, wvt, tm):
    t = proj32.shape[0]
    c_in = C_Q_RANK + C_KV_RANK + 2 * C_SLOT
    ctab, stab, ctab_t, stab_t = tabs

    def full(a):
        return pl.BlockSpec(a.shape, lambda i: (0,) * a.ndim)

    gq = gq.reshape(1, -1)
    gkv = gkv.reshape(1, -1)
    return pl.pallas_call(
        _mla_prep_kernel,
        out_shape=(jax.ShapeDtypeStruct((N_HEADS * C_SLOT, t), BF16),
                   jax.ShapeDtypeStruct((t, N_HEADS * C_SLOT), BF16),
                   jax.ShapeDtypeStruct((WIDTH, t), BF16)),
        grid=(t // tm,),
        in_specs=[pl.BlockSpec((tm, c_in), lambda i: (i, 0)),
                  pl.BlockSpec((tm, LANES), lambda i: (i, 0)),
                  pl.BlockSpec((tm, LANES), lambda i: (i, 0)),
                  pl.BlockSpec((LANES, tm), lambda i: (0, i)),
                  pl.BlockSpec((LANES, tm), lambda i: (0, i)),
                  full(gq), full(gkv), full(wq1t), full(wq2t), full(wk), full(wvt)],
        out_specs=(pl.BlockSpec((N_HEADS * C_SLOT, tm), lambda i: (0, i)),
                   pl.BlockSpec((tm, N_HEADS * C_SLOT), lambda i: (i, 0)),
                   pl.BlockSpec((WIDTH, tm), lambda i: (0, i))),
        compiler_params=_cparams(("parallel",)),
        name="mla_prep",
    )(proj32, ctab, stab, ctab_t, stab_t, gq, gkv, wq1t, wq2t, wk, wvt)


def _mla_flash_kernel(qt_ref, k_ref, vt_ref, o_ref, acc_ref, m_ref, l_ref, s_ref):
    qi = pl.program_id(2)
    c = (C_NOPE + C_ROPE) ** -0.5 * LOG2E

    m_ref[...] = jnp.full_like(m_ref, -jnp.inf)
    l_ref[...] = jnp.zeros_like(l_ref)
    acc_ref[...] = jnp.zeros_like(acc_ref)

    def scores(j, e):
        off = pl.multiple_of(j * C_TILE, C_TILE)
        k = k_ref[pl.ds(off, C_TILE), e * C_SLOT:(e + 1) * C_SLOT]
        return jnp.dot(k, qt_ref[e * C_SLOT:(e + 1) * C_SLOT, :], preferred_element_type=F32)

    def softmax_pv(j, e, s):
        off = pl.multiple_of(j * C_TILE, C_TILE)
        m_prev = m_ref[e]
        m_new = jnp.maximum(m_prev, jnp.max(s, axis=0, keepdims=True))
        p = jnp.exp2((s - m_new) * c).astype(BF16)
        alpha = jnp.exp2((m_prev - m_new) * c)
        vt = vt_ref[e * HEAD_DIM:(e + 1) * HEAD_DIM, pl.ds(off, C_TILE)]
        vt_aug = jnp.concatenate([vt, jnp.ones((ONES_ROWS, C_TILE), BF16)], axis=0)
        pv = jnp.dot(vt_aug, p, preferred_element_type=F32)
        l_ref[e] = alpha * l_ref[e] + pv[HEAD_DIM:HEAD_DIM + 1]
        acc_ref[e] = alpha * acc_ref[e] + pv[:HEAD_DIM]
        m_ref[e] = m_new

    for e in range(2):
        s_ref[e] = scores(0, e)

    def body(j, carry):
        for e in range(2):
            s = s_ref[e]
            s_ref[e] = scores(j + 1, e)
            softmax_pv(j, e, s)
        return carry

    lax.fori_loop(0, qi, body, 0)
    kc = lax.broadcasted_iota(jnp.int32, (C_TILE, C_TILE), 0) // CHUNK
    qc = lax.broadcasted_iota(jnp.int32, (C_TILE, C_TILE), 1) // CHUNK
    for e in range(2):
        softmax_pv(qi, e, jnp.where(kc <= qc, s_ref[e], NEG_BIG))

    o_t = jnp.concatenate([acc_ref[0] / l_ref[0], acc_ref[1] / l_ref[1]], axis=0)
    o_ref[...] = o_t.T.astype(o_ref.dtype)


def mla_flash(qt, k, vt, bsz, seq):
    t = bsz * seq
    n_qt = seq // C_TILE
    return pl.pallas_call(
        _mla_flash_kernel,
        out_shape=jax.ShapeDtypeStruct((t, WIDTH), BF16),
        grid=(bsz, N_PAIRS, n_qt),
        in_specs=[
            pl.BlockSpec((2 * C_SLOT, C_TILE), lambda b, p, i: (p, b * n_qt + i)),
            pl.BlockSpec((seq, 2 * C_SLOT), lambda b, p, i: (b, p)),
            pl.BlockSpec((LANES, seq), lambda b, p, i: (p, b)),
        ],
        out_specs=pl.BlockSpec((C_TILE, LANES), lambda b, p, i: (b * n_qt + i, p)),
        scratch_shapes=[pltpu.VMEM((2, HEAD_DIM, C_TILE), F32),
                        pltpu.VMEM((2, 1, C_TILE), F32),
                        pltpu.VMEM((2, 1, C_TILE), F32),
                        pltpu.VMEM((2, C_TILE, C_TILE), F32)],
        compiler_params=_cparams(("parallel", "parallel", "arbitrary")),
        name="mla_flash",
    )(qt, k, vt)


def _merge_kernel(x_ref, ya_ref, yb_ref, yc_ref, g_ref, wbr_ref, wout_ref, o_ref):
    merged = None
    for n, y_ref in enumerate((ya_ref, yb_ref, yc_ref)):
        up = jnp.dot(y_ref[...], wbr_ref[n], preferred_element_type=F32)
        gl = g_ref[:, n * D_MODEL:(n + 1) * D_MODEL].astype(F32)
        term = (1.0 / (1.0 + jnp.exp(-gl))) * up
        merged = term if merged is None else merged + term
    o_ref[...] = x_ref[...] + jnp.dot(merged.astype(BF16), wout_ref[...], preferred_element_type=F32)


def merge_out(x, ya, yb, yc, proj16, gate_blk, wbr, wout, tm):
    t = x.shape[0]
    return pl.pallas_call(
        _merge_kernel,
        out_shape=jax.ShapeDtypeStruct((t, D_MODEL), F32),
        grid=(t // tm,),
        in_specs=[
            pl.BlockSpec((tm, D_MODEL), lambda i: (i, 0)),
            pl.BlockSpec((tm, WIDTH), lambda i: (i, 0)),
            pl.BlockSpec((tm, WIDTH), lambda i: (i, 0)),
            pl.BlockSpec((tm, WIDTH), lambda i: (i, 0)),
            pl.BlockSpec((tm, N_BRANCH * D_MODEL), lambda i: (i, gate_blk)),
            pl.BlockSpec(wbr.shape, lambda i: (0, 0, 0)),
            pl.BlockSpec(wout.shape, lambda i: (0, 0)),
        ],
        out_specs=pl.BlockSpec((tm, D_MODEL), lambda i: (i, 0)),
        compiler_params=_cparams(("parallel",)),
        name="merge_out",
    )(x, ya, yb, yc, proj16, wbr, wout)


def _ffn_kernel(x_ref, g_ref, w1_ref, w2_ref, gf_ref, o_ref, h_ref, acc_ref, *, final_norm):
    k = pl.program_id(1)

    @pl.when(k == 0)
    def _():
        h_ref[...] = _rms(x_ref[...], g_ref[...]).astype(BF16)
        acc_ref[...] = jnp.zeros_like(acc_ref)

    u = jnp.maximum(jnp.dot(h_ref[...], w1_ref[...], preferred_element_type=F32), 0.0)
    acc_ref[...] += jnp.dot((u * u).astype(BF16), w2_ref[...], preferred_element_type=F32)

    @pl.when(k == pl.num_programs(1) - 1)
    def _():
        y = x_ref[...] + acc_ref[...]
        if final_norm:
            y = _rms(y, gf_ref[...])
        o_ref[...] = y


def ffn(x, g, w1, w2, gf, final_norm, tm, tf):
    t, d = x.shape
    dff = w1.shape[1]
    return pl.pallas_call(
        functools.partial(_ffn_kernel, final_norm=final_norm),
        out_shape=jax.ShapeDtypeStruct((t, d), F32),
        grid=(t // tm, dff // tf),
        in_specs=[
            pl.BlockSpec((tm, d), lambda i, k: (i, 0)),
            pl.BlockSpec((1, d), lambda i, k: (0, 0)),
            pl.BlockSpec((d, tf), lambda i, k: (0, k)),
            pl.BlockSpec((tf, d), lambda i, k: (k, 0)),
            pl.BlockSpec((1, d), lambda i, k: (0, 0)),
        ],
        out_specs=pl.BlockSpec((tm, d), lambda i, k: (i, 0)),
        scratch_shapes=[pltpu.VMEM((tm, d), BF16), pltpu.VMEM((tm, d), F32)],
        compiler_params=_cparams(("parallel", "arbitrary")),
        name="ffn",
    )(x, g.reshape(1, d), w1, w2, gf.reshape(1, d))


COL_AQ, COL_AK, COL_AV = 0, 4, 8
COL_BQ, COL_BI, COL_BG = 12, 16, 20
GATE_BLK = 1
COL32_BF = (C_Q_RANK + C_KV_RANK + 2 * C_SLOT) // LANES


def _prep_w_in(w):
    o = 0
    parts = {}
    for name, size in (("aq", 512), ("ak", 512), ("av", 512), ("bq", 512), ("bf", 512), ("bi", 512),
                       ("bg", 512), ("cq", C_Q_RANK), ("ckv", C_KV_RANK), ("ckr", C_ROPE),
                       ("gate", N_BRANCH * D_MODEL)):
        parts[name] = w[:, o:o + size]
        o += size
    w16 = jnp.concatenate([parts[n] for n in ("aq", "ak", "av", "bq", "bi", "bg", "gate")], axis=1)
    half = C_ROPE // 2
    x1, x2 = parts["ckr"][:, :half], parts["ckr"][:, half:]
    zl = jnp.zeros((w.shape[0], C_NOPE), w.dtype)
    zr = jnp.zeros((w.shape[0], C_SLOT - C_NOPE - C_ROPE), w.dtype)
    kr_slot = jnp.concatenate([zl, x1, x2, zr], axis=1)
    kr_sw = jnp.concatenate([zl, -x2, x1, zr], axis=1)
    w32 = jnp.concatenate([parts["cq"], parts["ckv"], kr_slot, kr_sw, parts["bf"]], axis=1)
    return w16.astype(BF16), w32.astype(BF16)


def _prep_w_mla(w_uq, w_ukv):
    half = C_ROPE // 2
    per_q = C_NOPE + C_ROPE
    wq = w_uq.reshape(C_Q_RANK, N_HEADS, per_q)
    nope, x1, x2 = wq[..., :C_NOPE], wq[..., C_NOPE:C_NOPE + half], wq[..., C_NOPE + half:]
    zpad = jnp.zeros((C_Q_RANK, N_HEADS, C_SLOT - per_q), w_uq.dtype)
    wq1 = jnp.concatenate([nope, x1, x2, zpad], axis=-1).reshape(C_Q_RANK, N_HEADS * C_SLOT)
    wq2 = jnp.concatenate([jnp.zeros_like(nope), -x2, x1, zpad], axis=-1).reshape(C_Q_RANK, N_HEADS * C_SLOT)
    wkv = w_ukv.reshape(C_KV_RANK, N_HEADS, C_NOPE + HEAD_DIM)
    k_nope, v = wkv[..., :C_NOPE], wkv[..., C_NOPE:]
    wk = jnp.concatenate([k_nope, jnp.zeros((C_KV_RANK, N_HEADS, C_SLOT - C_NOPE), w_ukv.dtype)],
                         axis=-1).reshape(C_KV_RANK, N_HEADS * C_SLOT)
    wv = v.reshape(C_KV_RANK, WIDTH)
    return wq1.T.astype(BF16), wq2.T.astype(BF16), wk.astype(BF16), wv.T.astype(BF16)


def kernel(x, positions, norm_mix_g, w_in, rel_bias, hgrn_lb_logits, hgrn_norm_g, mla_q_norm_g,
           mla_kv_norm_g, mla_w_uq, mla_w_ukv, w_branch, w_out, norm_ffn_g, w_ff1, w_ff2, final_norm_g):
    bsz, seq, d = x.shape
    depth = w_in.shape[0]
    t = bsz * seq
    assert d == D_MODEL and seq % C_TILE == 0 and seq >= A_BAND
    tm = min(1024, t)
    hgrn_rows = min(512, seq)

    p_lb = jax.nn.softmax(hgrn_lb_logits.astype(F32), axis=0)
    lb_all = jnp.cumsum(p_lb, axis=0)
    lb_all = lb_all - lb_all[0:1]

    tabs = rope_tables(positions, tm)
    xf = x.reshape(t, d)
    for l in range(depth):
        w16, w32 = _prep_w_in(w_in[l])
        wq1t, wq2t, wk, wvt = _prep_w_mla(mla_w_uq[l], mla_w_ukv[l])
        proj16 = norm_matmul(xf, norm_mix_g[l], w16, BF16, tm, 1024, "in_proj_bf16")
        proj32 = norm_matmul(xf, norm_mix_g[l], w32, F32, tm, w32.shape[1], "in_proj_f32")

        y_a = band_attention(proj16, band_bias_table(rel_bias[l]), bsz, seq, COL_AQ, COL_AK, COL_AV)
        y_b = hgrn2(proj16, proj32, lb_all[l], hgrn_norm_g[l], bsz, seq,
                    COL_BQ, COL_BI, COL_BG, COL32_BF, hgrn_rows)
        qt_c, k_c, vt_c = mla_prep(proj32, tabs, mla_q_norm_g[l], mla_kv_norm_g[l], wq1t, wq2t, wk, wvt, tm)
        y_c = mla_flash(qt_c, k_c, vt_c, bsz, seq)

        xf = merge_out(xf, y_a, y_b, y_c, proj16, GATE_BLK, w_branch[l].astype(BF16),
                       w_out[l].astype(BF16), min(512, t))
        xf = ffn(xf, norm_ffn_g[l], w_ff1[l].astype(BF16), w_ff2[l].astype(BF16), final_norm_g,
                 l == depth - 1, tm, 1024)
    return xf.reshape(bsz, seq, d)
```

```python
import functools

import jax
import jax.numpy as jnp
from jax import lax
from jax.experimental import pallas as pl
from jax.experimental.pallas import tpu as pltpu

F32 = jnp.float32
BF16 = jnp.bfloat16

D_MODEL = 1024
CHUNK = 64
EPS = 1e-6
N_HEADS = 8
HEAD_DIM = 64
LANES = 128
N_PAIRS = N_HEADS * HEAD_DIM // LANES
WIDTH = N_HEADS * HEAD_DIM

A_LEFT_CHUNKS = 8
A_MAX_REL = 128
A_QTILE = 2 * CHUNK
A_BAND = (A_LEFT_CHUNKS + 2) * CHUNK
A_PAD = A_LEFT_CHUNKS * CHUNK
A_TILES_PER_STEP = 8

C_Q_RANK = 256
C_KV_RANK = 128
C_ROPE = 32
C_NOPE = 64
C_SLOT = LANES
ROPE_BASE = 10000.0
C_TILE = 512
ONES_ROWS = 16
LOG2E = 1.4426950408889634
C_QSCALE = (C_NOPE + C_ROPE) ** -0.5 * LOG2E

D_FF = 4 * D_MODEL
N_BRANCH = 3

SUB = 16
N_SUB = CHUNK // SUB
NEG_BIG = -1e30

VMEM_LIMIT = 48 * 1024 * 1024
_NT = (((1,), (1,)), ((), ()))


def _cparams(sem):
    return pltpu.CompilerParams(dimension_semantics=sem, vmem_limit_bytes=VMEM_LIMIT)


def _resident(a):
    return pl.BlockSpec(a.shape, lambda *_: (0,) * a.ndim, pipeline_mode=pl.Buffered(1))


def _rms(x, g):
    return x * lax.rsqrt(jnp.mean(x * x, axis=-1, keepdims=True) + EPS) * g


def _norm_matmul_kernel(x_ref, g_ref, w_ref, o_ref, *, tn):
    h = _rms(x_ref[...], g_ref[...]).astype(BF16)
    for j in range(w_ref.shape[1] // tn):
        cols = slice(j * tn, (j + 1) * tn)
        o_ref[:, cols] = jnp.dot(h, w_ref[:, cols], preferred_element_type=F32).astype(o_ref.dtype)


def norm_matmul(x, g, w, out_dtype, tm, tn, name):
    t, d = x.shape
    n = w.shape[1]
    return pl.pallas_call(
        functools.partial(_norm_matmul_kernel, tn=tn),
        out_shape=jax.ShapeDtypeStruct((t, n), out_dtype),
        grid=(t // tm,),
        in_specs=[
            pl.BlockSpec((tm, d), lambda i: (i, 0)),
            pl.BlockSpec((1, d), lambda i: (0, 0)),
            _resident(w),
        ],
        out_specs=pl.BlockSpec((tm, n), lambda i: (i, 0)),
        compiler_params=_cparams(("parallel",)),
        name=name,
    )(x, g.reshape(1, d), w)


def _trig_kernel(pos_ref, invf_ref, c_ref, s_ref, ct_ref, st_ref):
    ang = pos_ref[...].astype(F32) * invf_ref[...]
    c = jnp.cos(ang)
    s = jnp.sin(ang)
    c_ref[...] = c
    s_ref[...] = s
    ct_ref[...] = c.T
    st_ref[...] = s.T


def rope_tables(positions, tm):
    t = positions.size
    inv_freq = ROPE_BASE ** (-jnp.arange(0, C_ROPE, 2, dtype=F32) / C_ROPE)
    half = C_ROPE // 2
    invf = jnp.zeros((LANES,), F32)
    invf = invf.at[C_NOPE:C_NOPE + half].set(inv_freq).at[C_NOPE + half:C_NOPE + C_ROPE].set(inv_freq)
    return pl.pallas_call(
        _trig_kernel,
        out_shape=(jax.ShapeDtypeStruct((t, LANES), F32),) * 2 + (jax.ShapeDtypeStruct((LANES, t), F32),) * 2,
        grid=(t // tm,),
        in_specs=[
            pl.BlockSpec((tm, 1), lambda i: (i, 0)),
            pl.BlockSpec((1, LANES), lambda i: (0, 0)),
        ],
        out_specs=(pl.BlockSpec((tm, LANES), lambda i: (i, 0)),) * 2
        + (pl.BlockSpec((LANES, tm), lambda i: (0, i)),) * 2,
        compiler_params=_cparams(("parallel",)),
        name="rope_tables",
    )(positions.reshape(t, 1), invf.reshape(1, LANES))


def _band_attn_kernel(q_ref, k_ref, v_ref, bias_ref, o_ref):
    lane = lax.broadcasted_iota(jnp.int32, (A_QTILE, LANES), 1)
    n_shift = A_PAD // A_QTILE
    starts, scores = [], []
    for t in range(A_TILES_PER_STEP):
        c2 = pl.program_id(2) * A_TILES_PER_STEP + t
        start = pl.multiple_of(jnp.maximum(c2 * A_QTILE - A_PAD, 0), A_QTILE)
        shift = jnp.minimum(c2, n_shift)
        kwin = k_ref[pl.ds(start, A_BAND), :]
        q = q_ref[t * A_QTILE:(t + 1) * A_QTILE, :].astype(F32) * (HEAD_DIM ** -0.5)
        qst = jnp.concatenate([jnp.where(lane < HEAD_DIM, q, 0.0), jnp.where(lane < HEAD_DIM, 0.0, q)],
                              axis=0).astype(BF16)
        s = lax.dot_general(qst, kwin, _NT, preferred_element_type=F32)
        starts.append(start)
        scores.append(s + bias_ref[shift, 0])
    for t in range(A_TILES_PER_STEP):
        s = scores[t]
        vwin = v_ref[pl.ds(starts[t], A_BAND), :]
        m = jnp.max(s, axis=-1, keepdims=True)
        p = jnp.exp(s - m)
        l = jnp.sum(p, axis=-1, keepdims=True)
        pv = jnp.dot(p.astype(BF16), vwin, preferred_element_type=F32) / l
        o_ref[t * A_QTILE:(t + 1) * A_QTILE, :] = jnp.where(lane < HEAD_DIM, pv[:A_QTILE], pv[A_QTILE:]).astype(o_ref.dtype)


def band_attention(proj, bias, bsz, seq, q_col, k_col, v_col):
    t = bsz * seq
    rows = A_QTILE * A_TILES_PER_STEP
    n_steps = seq // rows
    n_shift = A_PAD // A_QTILE
    return pl.pallas_call(
        _band_attn_kernel,
        out_shape=jax.ShapeDtypeStruct((t, WIDTH), BF16),
        grid=(bsz, N_PAIRS, n_steps),
        in_specs=[
            pl.BlockSpec((rows, LANES), lambda b, p, c: (b * n_steps + c, q_col + p)),
            pl.BlockSpec((seq, LANES), lambda b, p, c: (b, k_col + p)),
            pl.BlockSpec((seq, LANES), lambda b, p, c: (b, v_col + p)),
            pl.BlockSpec((n_shift + 1, 1, 2 * A_QTILE, A_BAND), lambda b, p, c: (0, p, 0, 0)),
        ],
        out_specs=pl.BlockSpec((rows, LANES), lambda b, p, c: (b * n_steps + c, p)),
        compiler_params=_cparams(("parallel", "parallel", "arbitrary")),
        name="band_attention",
    )(proj, proj, proj, bias.reshape(n_shift + 1, N_PAIRS, 2 * A_QTILE, A_BAND))


A_ROLL = 768


def _band_bias_kernel(r_ref, o_ref):
    d = pl.program_id(0) * A_QTILE
    prof = jnp.broadcast_to(r_ref[...], (A_QTILE, A_ROLL))
    b = pltpu.roll(prof, 0, 1, stride=1, stride_axis=0)[:, :A_BAND]
    i = lax.broadcasted_iota(jnp.int32, (A_QTILE, A_BAND), 0)
    j = lax.broadcasted_iota(jnp.int32, (A_QTILE, A_BAND), 1)
    gap = (d + i) // CHUNK - j // CHUNK
    o_ref[...] = jnp.where((gap >= 0) & (gap <= A_LEFT_CHUNKS), b, NEG_BIG)


def band_bias_table(rel_table):
    n_shift = A_PAD // A_QTILE
    heads = rel_table.shape[0]
    pad = A_ROLL + A_PAD
    ext = jnp.pad(rel_table.astype(F32)[:, ::-1], ((0, 0), (pad, pad)), mode="edge")

    def seg(d, u0, n):
        o = u0 - d + A_MAX_REL + pad
        return ext[:, o:o + n]

    n_neg = A_ROLL - (A_BAND + 1)
    prof = jnp.stack([jnp.concatenate([seg(s * A_QTILE, 0, A_BAND + 1), seg(s * A_QTILE, -n_neg, n_neg)], axis=1)
                      for s in range(n_shift + 1)])
    return pl.pallas_call(
        _band_bias_kernel,
        out_shape=jax.ShapeDtypeStruct((n_shift + 1, heads, A_QTILE, A_BAND), F32),
        grid=(n_shift + 1, heads),
        in_specs=[pl.BlockSpec((None, None, 1, A_ROLL), lambda s, h: (s, h, 0, 0))],
        out_specs=pl.BlockSpec((None, None, A_QTILE, A_BAND), lambda s, h: (s, h, 0, 0)),
        compiler_params=_cparams(("parallel", "parallel")),
        name="band_bias",
    )(prof.reshape(n_shift + 1, heads, 1, A_ROLL))


def _split3(x):
    hi = x.astype(BF16)
    r1 = x - hi.astype(F32)
    mid = r1.astype(BF16)
    lo = (r1 - mid.astype(F32)).astype(BF16)
    return hi, mid, lo


def _hgrn_kernel(q_ref, f_ref, i_ref, g_ref, loglb_ref, log1mlb_ref, ng_ref, o_ref, st_ref, *, n_chunks):
    @pl.when(pl.program_id(2) == 0)
    def _():
        st_ref[...] = jnp.zeros_like(st_ref)

    log_lb = loglb_ref[...]
    log1m_lb = log1mlb_ref[...]
    norm_g = ng_ref[...]
    rows = n_chunks * CHUNK
    n_blk = rows // SUB

    head0 = lax.broadcasted_iota(jnp.int32, (rows, LANES), 1) < HEAD_DIM
    head0_c = lax.broadcasted_iota(jnp.int32, (CHUNK, LANES), 1) < HEAD_DIM
    r64 = lax.broadcasted_iota(jnp.int32, (CHUNK, CHUNK), 0)
    c64 = lax.broadcasted_iota(jnp.int32, (CHUNK, CHUNK), 1)
    tril = (c64 <= r64).astype(BF16)
    c64s = lax.broadcasted_iota(jnp.int32, (2 * CHUNK, CHUNK), 1)
    rl = lax.broadcasted_iota(jnp.int32, (LANES, LANES), 0)
    cl = lax.broadcasted_iota(jnp.int32, (LANES, LANES), 1)
    same_head = (rl // HEAD_DIM) == (cl // HEAD_DIM)
    bd = same_head.astype(BF16)
    ri = lax.broadcasted_iota(jnp.int32, (CHUNK, CHUNK * SUB), 0)
    ci = lax.broadcasted_iota(jnp.int32, (CHUNK, CHUNK * SUB), 1)
    pick = (ci // SUB == ri).astype(BF16)
    row_s = lax.broadcasted_iota(jnp.int32, (SUB, LANES), 0)

    def chunk(x, c):
        return x[c * CHUNK:(c + 1) * CHUNK]

    z = f_ref[...]
    qraw = q_ref[...].astype(F32)
    v = i_ref[...].astype(F32)
    graw = g_ref[...].astype(F32)

    log_sig = jnp.minimum(z, 0.0) - jnp.log(1.0 + jnp.exp(-jnp.abs(z)))
    bterm = log1m_lb + log_sig
    log_f = jnp.maximum(log_lb, bterm) + jnp.log(1.0 + jnp.exp(-jnp.abs(log_lb - bterm)))
    log_k = bterm - z
    qs = qraw * (1.0 / (1.0 + jnp.exp(-qraw)))

    x3 = jnp.concatenate(_split3(log_f), axis=1)
    cum3 = [jnp.dot(tril, chunk(x3, c), preferred_element_type=F32) for c in range(n_chunks)]
    cum = jnp.concatenate([t[:, :LANES] + t[:, LANES:2 * LANES] + t[:, 2 * LANES:] for t in cum3], axis=0)

    zero_row = jnp.zeros((1, LANES), F32)
    c_end = [cum[SUB * b + SUB - 1:SUB * b + SUB, :] for b in range(n_blk)]
    c_start = [zero_row if b % N_SUB == 0 else c_end[b - 1] for b in range(n_blk)]
    c_last = [c_end[c * N_SUB + N_SUB - 1] for c in range(n_chunks)]

    def rows_of(blocks):
        return jnp.concatenate([jnp.broadcast_to(r, (SUB, LANES)) for r in blocks], axis=0)

    cstart_full = rows_of(c_start)
    cend_full = rows_of(c_end)
    clast_full = jnp.concatenate([jnp.broadcast_to(r, (CHUNK, LANES)) for r in c_last], axis=0)
    q1 = qs * jnp.exp(cum - cstart_full)
    lk = log_k - cum
    k2b = jnp.exp(cend_full + lk).astype(BF16)
    k_end = jnp.exp(clast_full + lk).astype(BF16)
    q_state = (qs * jnp.exp(cum)).astype(BF16)
    vb = v.astype(BF16)

    s_cross = []
    for j in range(N_SUB - 1):
        d_rows = []
        for b in range(n_blk):
            if b % N_SUB > j:
                d_rows.append(jnp.exp(c_start[b] - c_end[(b // N_SUB) * N_SUB + j]))
            else:
                d_rows.append(zero_row)
        qj = q1 * rows_of(d_rows)
        q0 = jnp.where(head0, qj, 0.0).astype(BF16)
        q1h = jnp.where(head0, 0.0, qj).astype(BF16)
        for c in range(n_chunks):
            qst = jnp.concatenate([chunk(q0, c), chunk(q1h, c)], axis=0)
            s_cross.append(lax.dot_general(qst, chunk(k2b, c), _NT, preferred_element_type=F32))
    o_cross = []
    for c in range(n_chunks):
        sc = None
        for j in range(N_SUB - 1):
            col_in_j = (c64s >= j * SUB) & (c64s < (j + 1) * SUB)
            s = jnp.where(col_in_j, s_cross[j * n_chunks + c], 0.0)
            sc = s if sc is None else sc + s
        o2 = jnp.dot(sc.astype(BF16), chunk(vb, c), preferred_element_type=F32)
        o_cross.append(jnp.where(head0_c, o2[:CHUNK], o2[CHUNK:]))

    a = cum - log_k
    w_rows = []
    for b in range(n_blk):
        a_b = a[SUB * b:SUB * (b + 1), :]
        for i in range(SUB):
            r = SUB * b + i
            arg = jnp.where(row_s <= i, cum[r:r + 1, :] - a_b, NEG_BIG)
            w_rows.append((jnp.exp(arg) * qs[r:r + 1, :]).astype(BF16))
    w_all = jnp.concatenate(w_rows, axis=0)
    sb = jnp.dot(w_all, bd, preferred_element_type=F32)
    o_diag = []
    for c in range(n_chunks):
        v_rep = jnp.concatenate([v[SUB * (r // SUB):SUB * (r // SUB + 1), :]
                                 for r in range(c * CHUNK, (c + 1) * CHUNK)], axis=0)
        sb_c = sb[c * CHUNK * SUB:(c + 1) * CHUNK * SUB]
        o_diag.append(jnp.dot(pick, (sb_c * v_rep).astype(BF16), preferred_element_type=F32))

    upd = [jnp.where(same_head,
                     lax.dot_general(chunk(vb, c), chunk(k_end, c), (((0,), (0,)), ((), ())),
                                     preferred_element_type=F32), 0.0) for c in range(n_chunks)]
    st = st_ref[...]
    o_state = []
    for c in range(n_chunks):
        o_state.append(lax.dot_general(chunk(q_state, c), st.astype(BF16), _NT, preferred_element_type=F32))
        st = st * jnp.exp(c_last[c]) + upd[c]
    st_ref[...] = st

    o = jnp.concatenate([o_cross[c] + o_diag[c] + o_state[c] for c in range(n_chunks)], axis=0)
    osq = o * o
    ms0 = jnp.sum(jnp.where(head0, osq, 0.0), axis=-1, keepdims=True)
    ms1 = jnp.sum(jnp.where(head0, 0.0, osq), axis=-1, keepdims=True)
    ms = jnp.where(head0, ms0, ms1) * (1.0 / HEAD_DIM)
    y = o * lax.rsqrt(ms + EPS) * norm_g
    y = y * (graw * (1.0 / (1.0 + jnp.exp(-graw))))
    o_ref[...] = y.astype(o_ref.dtype)


def hgrn2(proj16, proj32, lb, norm_g, bsz, seq, q_col, i_col, g_col, f_col, rows_per_step):
    t = bsz * seq
    n_steps = seq // rows_per_step
    log_lb = jnp.log(lb).reshape(N_PAIRS, 1, LANES)
    log1m_lb = jnp.log1p(-lb).reshape(N_PAIRS, 1, LANES)
    ng = jnp.tile(norm_g.astype(F32), LANES // HEAD_DIM).reshape(1, LANES)

    def act(col):
        return pl.BlockSpec((rows_per_step, LANES), lambda b, p, s: (b * n_steps + s, col + p))

    def par():
        return pl.BlockSpec((None, 1, LANES), lambda b, p, s: (p, 0, 0))

    return pl.pallas_call(
        functools.partial(_hgrn_kernel, n_chunks=rows_per_step // CHUNK),
        out_shape=jax.ShapeDtypeStruct((t, WIDTH), BF16),
        grid=(bsz, N_PAIRS, n_steps),
        in_specs=[act(q_col), act(f_col), act(i_col), act(g_col), par(), par(),
                  pl.BlockSpec((1, LANES), lambda b, p, s: (0, 0))],
        out_specs=pl.BlockSpec((rows_per_step, LANES), lambda b, p, s: (b * n_steps + s, p)),
        scratch_shapes=[pltpu.VMEM((LANES, LANES), F32)],
        compiler_params=_cparams(("parallel", "parallel", "arbitrary")),
        name="hgrn2",
    )(proj16, proj32, proj16, proj16, log_lb, log1m_lb, ng)


def _mla_prep_kernel(pc_ref, ct_ref, st_ref, ctt_ref, stt_ref, gq_ref, gkv_ref,
                     wq1t_ref, wq2t_ref, wk_ref, wvt_ref, qt_ref, k_ref, vt_ref):
    pc = pc_ref[...]
    qn = _rms(pc[:, :C_Q_RANK], gq_ref[...]).astype(BF16)
    kvn = _rms(pc[:, C_Q_RANK:C_Q_RANK + C_KV_RANK], gkv_ref[...]).astype(BF16)
    kr = pc[:, C_Q_RANK + C_KV_RANK:C_Q_RANK + C_KV_RANK + C_SLOT]
    kr_sw = pc[:, C_Q_RANK + C_KV_RANK + C_SLOT:C_Q_RANK + C_KV_RANK + 2 * C_SLOT]
    a_t = lax.dot_general(wq1t_ref[...], qn, _NT, preferred_element_type=F32)
    b_t = lax.dot_general(wq2t_ref[...], qn, _NT, preferred_element_type=F32)
    kn = jnp.dot(kvn, wk_ref[...], preferred_element_type=F32)
    vt_ref[...] = lax.dot_general(wvt_ref[...], kvn, _NT, preferred_element_type=F32).astype(vt_ref.dtype)
    kr_rot = kr * ct_ref[...] + kr_sw * st_ref[...]
    ctt = ctt_ref[...]
    stt = stt_ref[...]
    for h in range(N_HEADS):
        sl = slice(h * C_SLOT, (h + 1) * C_SLOT)
        qt_ref[sl, :] = (a_t[sl, :] * ctt + b_t[sl, :] * stt).astype(qt_ref.dtype)
        k_ref[:, sl] = (kn[:, sl] + kr_rot).astype(k_ref.dtype)


def mla_prep(proj32, tabs, gq, gkv, wq1t, wq2t, wk, wvt, tm):
    t = proj32.shape[0]
    c_in = C_Q_RANK + C_KV_RANK + 2 * C_SLOT
    ctab, stab, ctab_t, stab_t = tabs

    def full(a):
        return pl.BlockSpec(a.shape, lambda i: (0,) * a.ndim)

    gq = gq.reshape(1, -1)
    gkv = gkv.reshape(1, -1)
    return pl.pallas_call(
        _mla_prep_kernel,
        out_shape=(jax.ShapeDtypeStruct((N_HEADS * C_SLOT, t), BF16),
                   jax.ShapeDtypeStruct((t, N_HEADS * C_SLOT), BF16),
                   jax.ShapeDtypeStruct((WIDTH, t), BF16)),
        grid=(t // tm,),
        in_specs=[pl.BlockSpec((tm, c_in), lambda i: (i, 0)),
                  pl.BlockSpec((tm, LANES), lambda i: (i, 0)),
                  pl.BlockSpec((tm, LANES), lambda i: (i, 0)),
                  pl.BlockSpec((LANES, tm), lambda i: (0, i)),
                  pl.BlockSpec((LANES, tm), lambda i: (0, i)),
                  full(gq), full(gkv), full(wq1t), full(wq2t), full(wk), full(wvt)],
        out_specs=(pl.BlockSpec((N_HEADS * C_SLOT, tm), lambda i: (0, i)),
                   pl.BlockSpec((tm, N_HEADS * C_SLOT), lambda i: (i, 0)),
                   pl.BlockSpec((WIDTH, tm), lambda i: (0, i))),
        compiler_params=_cparams(("parallel",)),
        name="mla_prep",
    )(proj32, ctab, stab, ctab_t, stab_t, gq, gkv, wq1t, wq2t, wk, wvt)


def _mla_flash_kernel(qt_ref, k_ref, vt_ref, o_ref, acc_ref, m_ref, l_ref, s_ref):
    qi = pl.program_id(2)
    c = C_QSCALE

    m_ref[...] = jnp.full_like(m_ref, -jnp.inf)
    l_ref[...] = jnp.zeros_like(l_ref)
    acc_ref[...] = jnp.zeros_like(acc_ref)

    def scores(j, e):
        off = pl.multiple_of(j * C_TILE, C_TILE)
        k = k_ref[pl.ds(off, C_TILE), e * C_SLOT:(e + 1) * C_SLOT]
        return jnp.dot(k, qt_ref[e * C_SLOT:(e + 1) * C_SLOT, :], preferred_element_type=F32)

    def softmax_pv(j, e, s):
        off = pl.multiple_of(j * C_TILE, C_TILE)
        m_prev = m_ref[e]
        m_new = jnp.maximum(m_prev, jnp.max(s, axis=0, keepdims=True))
        p = jnp.exp2((s - m_new) * c).astype(BF16)
        alpha = jnp.exp2((m_prev - m_new) * c)
        vt = vt_ref[e * HEAD_DIM:(e + 1) * HEAD_DIM, pl.ds(off, C_TILE)]
        vt_aug = jnp.concatenate([vt, jnp.ones((ONES_ROWS, C_TILE), BF16)], axis=0)
        pv = jnp.dot(vt_aug, p, preferred_element_type=F32)
        l_ref[e] = alpha * l_ref[e] + pv[HEAD_DIM:HEAD_DIM + 1]
        acc_ref[e] = alpha * acc_ref[e] + pv[:HEAD_DIM]
        m_ref[e] = m_new

    for e in range(2):
        s_ref[e] = scores(0, e)

    def body(j, carry):
        for e in range(2):
            s = s_ref[e]
            s_ref[e] = scores(j + 1, e)
            softmax_pv(j, e, s)
        return carry

    lax.fori_loop(0, qi, body, 0)
    kc = lax.broadcasted_iota(jnp.int32, (C_TILE, C_TILE), 0) // CHUNK
    qc = lax.broadcasted_iota(jnp.int32, (C_TILE, C_TILE), 1) // CHUNK
    for e in range(2):
        softmax_pv(qi, e, jnp.where(kc <= qc, s_ref[e], NEG_BIG))

    o_t = jnp.concatenate([acc_ref[0] / l_ref[0], acc_ref[1] / l_ref[1]], axis=0)
    o_ref[...] = o_t.T.astype(o_ref.dtype)


def mla_flash(qt, k, vt, bsz, seq):
    t = bsz * seq
    n_qt = seq // C_TILE
    return pl.pallas_call(
        _mla_flash_kernel,
        out_shape=jax.ShapeDtypeStruct((t, WIDTH), BF16),
        grid=(bsz, N_PAIRS, n_qt),
        in_specs=[
            pl.BlockSpec((2 * C_SLOT, C_TILE), lambda b, p, i: (p, b * n_qt + i)),
            pl.BlockSpec((seq, 2 * C_SLOT), lambda b, p, i: (b, p)),
            pl.BlockSpec((LANES, seq), lambda b, p, i: (p, b)),
        ],
        out_specs=pl.BlockSpec((C_TILE, LANES), lambda b, p, i: (b * n_qt + i, p)),
        scratch_shapes=[pltpu.VMEM((2, HEAD_DIM, C_TILE), F32),
                        pltpu.VMEM((2, 1, C_TILE), F32),
                        pltpu.VMEM((2, 1, C_TILE), F32),
                        pltpu.VMEM((2, C_TILE, C_TILE), F32)],
        compiler_params=_cparams(("parallel", "parallel", "arbitrary")),
        name="mla_flash",
    )(qt, k, vt)


def _merge_kernel(x_ref, ya_ref, yb_ref, yc_ref, g_ref, wbr_ref, wout_ref, o_ref):
    merged = None
    for n, y_ref in enumerate((ya_ref, yb_ref, yc_ref)):
        up = jnp.dot(y_ref[...], wbr_ref[n], preferred_element_type=F32)
        gl = g_ref[:, n * D_MODEL:(n + 1) * D_MODEL].astype(F32)
        term = (1.0 / (1.0 + jnp.exp(-gl))) * up
        merged = term if merged is None else merged + term
    o_ref[...] = x_ref[...] + jnp.dot(merged.astype(BF16), wout_ref[...], preferred_element_type=F32)


def merge_out(x, ya, yb, yc, proj16, gate_blk, wbr, wout, tm):
    t = x.shape[0]
    return pl.pallas_call(
        _merge_kernel,
        out_shape=jax.ShapeDtypeStruct((t, D_MODEL), F32),
        grid=(t // tm,),
        in_specs=[
            pl.BlockSpec((tm, D_MODEL), lambda i: (i, 0)),
            pl.BlockSpec((tm, WIDTH), lambda i: (i, 0)),
            pl.BlockSpec((tm, WIDTH), lambda i: (i, 0)),
            pl.BlockSpec((tm, WIDTH), lambda i: (i, 0)),
            pl.BlockSpec((tm, N_BRANCH * D_MODEL), lambda i: (i, gate_blk)),
            _resident(wbr),
            _resident(wout),
        ],
        out_specs=pl.BlockSpec((tm, D_MODEL), lambda i: (i, 0)),
        compiler_params=_cparams(("parallel",)),
        name="merge_out",
    )(x, ya, yb, yc, proj16, wbr, wout)


def _ffn_kernel(x_ref, g_ref, w1_ref, w2_ref, gf_ref, o_ref, *, final_norm, tf):
    x = x_ref[...]
    h = _rms(x, g_ref[...]).astype(BF16)
    acc = None
    for k in range(w1_ref.shape[1] // tf):
        u = jnp.maximum(jnp.dot(h, w1_ref[:, k * tf:(k + 1) * tf], preferred_element_type=F32), 0.0)
        part = jnp.dot((u * u).astype(BF16), w2_ref[k * tf:(k + 1) * tf, :], preferred_element_type=F32)
        acc = part if acc is None else acc + part
    y = x + acc
    if final_norm:
        y = _rms(y, gf_ref[...])
    o_ref[...] = y


def ffn(x, g, w1, w2, gf, final_norm, tm, tf):
    t, d = x.shape
    return pl.pallas_call(
        functools.partial(_ffn_kernel, final_norm=final_norm, tf=tf),
        out_shape=jax.ShapeDtypeStruct((t, d), F32),
        grid=(t // tm,),
        in_specs=[
            pl.BlockSpec((tm, d), lambda i: (i, 0)),
            pl.BlockSpec((1, d), lambda i: (0, 0)),
            _resident(w1),
            _resident(w2),
            pl.BlockSpec((1, d), lambda i: (0, 0)),
        ],
        out_specs=pl.BlockSpec((tm, d), lambda i: (i, 0)),
        compiler_params=_cparams(("parallel",)),
        name="ffn",
    )(x, g.reshape(1, d), w1, w2, gf.reshape(1, d))


COL_AQ, COL_AK, COL_AV = 0, 4, 8
COL_BQ, COL_BI, COL_BG = 12, 16, 20
GATE_BLK = 1
COL32_BF = (C_Q_RANK + C_KV_RANK + 2 * C_SLOT) // LANES


def _prep_w_in(w):
    o = 0
    parts = {}
    for name, size in (("aq", 512), ("ak", 512), ("av", 512), ("bq", 512), ("bf", 512), ("bi", 512),
                       ("bg", 512), ("cq", C_Q_RANK), ("ckv", C_KV_RANK), ("ckr", C_ROPE),
                       ("gate", N_BRANCH * D_MODEL)):
        parts[name] = w[:, o:o + size]
        o += size
    w16 = jnp.concatenate([parts[n] for n in ("aq", "ak", "av", "bq", "bi", "bg", "gate")], axis=1)
    half = C_ROPE // 2
    x1, x2 = parts["ckr"][:, :half], parts["ckr"][:, half:]
    zl = jnp.zeros((w.shape[0], C_NOPE), w.dtype)
    zr = jnp.zeros((w.shape[0], C_SLOT - C_NOPE - C_ROPE), w.dtype)
    kr_slot = jnp.concatenate([zl, x1, x2, zr], axis=1)
    kr_sw = jnp.concatenate([zl, -x2, x1, zr], axis=1)
    w32 = jnp.concatenate([parts["cq"], parts["ckv"], kr_slot, kr_sw, parts["bf"]], axis=1)
    return w16.astype(BF16), w32.astype(BF16)


def _prep_w_mla(w_uq, w_ukv):
    half = C_ROPE // 2
    per_q = C_NOPE + C_ROPE
    wq = w_uq.reshape(C_Q_RANK, N_HEADS, per_q)
    nope, x1, x2 = wq[..., :C_NOPE], wq[..., C_NOPE:C_NOPE + half], wq[..., C_NOPE + half:]
    zpad = jnp.zeros((C_Q_RANK, N_HEADS, C_SLOT - per_q), w_uq.dtype)
    wq1 = jnp.concatenate([nope, x1, x2, zpad], axis=-1).reshape(C_Q_RANK, N_HEADS * C_SLOT)
    wq2 = jnp.concatenate([jnp.zeros_like(nope), -x2, x1, zpad], axis=-1).reshape(C_Q_RANK, N_HEADS * C_SLOT)
    wkv = w_ukv.reshape(C_KV_RANK, N_HEADS, C_NOPE + HEAD_DIM)
    k_nope, v = wkv[..., :C_NOPE], wkv[..., C_NOPE:]
    wk = jnp.concatenate([k_nope, jnp.zeros((C_KV_RANK, N_HEADS, C_SLOT - C_NOPE), w_ukv.dtype)],
                         axis=-1).reshape(C_KV_RANK, N_HEADS * C_SLOT)
    wv = v.reshape(C_KV_RANK, WIDTH)
    return wq1.T.astype(BF16), wq2.T.astype(BF16), wk.astype(BF16), wv.T.astype(BF16)


def kernel(x, positions, norm_mix_g, w_in, rel_bias, hgrn_lb_logits, hgrn_norm_g, mla_q_norm_g,
           mla_kv_norm_g, mla_w_uq, mla_w_ukv, w_branch, w_out, norm_ffn_g, w_ff1, w_ff2, final_norm_g):
    bsz, seq, d = x.shape
    depth = w_in.shape[0]
    t = bsz * seq
    assert d == D_MODEL and seq % C_TILE == 0 and seq >= A_BAND
    tm = min(1024, t)
    hgrn_rows = min(512, seq)

    p_lb = jax.nn.softmax(hgrn_lb_logits.astype(F32), axis=0)
    lb_all = jnp.cumsum(p_lb, axis=0)
    lb_all = lb_all - lb_all[0:1]

    tabs = rope_tables(positions, tm)
    xf = x.reshape(t, d)
    for l in range(depth):
        w16, w32 = _prep_w_in(w_in[l])
        wq1t, wq2t, wk, wvt = _prep_w_mla(mla_w_uq[l], mla_w_ukv[l])
        proj16 = norm_matmul(xf, norm_mix_g[l], w16, BF16, min(512, t), 1024, "in_proj_bf16")
        proj32 = norm_matmul(xf, norm_mix_g[l], w32, F32, tm, w32.shape[1], "in_proj_f32")

        y_a = band_attention(proj16, band_bias_table(rel_bias[l]), bsz, seq, COL_AQ, COL_AK, COL_AV)
        y_b = hgrn2(proj16, proj32, lb_all[l], hgrn_norm_g[l], bsz, seq,
                    COL_BQ, COL_BI, COL_BG, COL32_BF, hgrn_rows)
        qt_c, k_c, vt_c = mla_prep(proj32, tabs, mla_q_norm_g[l], mla_kv_norm_g[l], wq1t, wq2t, wk, wvt, tm)
        y_c = mla_flash(qt_c, k_c, vt_c, bsz, seq)

        xf = merge_out(xf, y_a, y_b, y_c, proj16, GATE_BLK, w_branch[l].astype(BF16),
                       w_out[l].astype(BF16), tm)
        xf = ffn(xf, norm_ffn_g[l], w_ff1[l].astype(BF16), w_ff2[l].astype(BF16), final_norm_g,
                 l == depth - 1, tm, 1024)
    return xf.reshape(bsz, seq, d)
```

```python
import functools

import jax
import jax.numpy as jnp
from jax import lax
from jax.experimental import pallas as pl
from jax.experimental.pallas import tpu as pltpu

F32 = jnp.float32
BF16 = jnp.bfloat16

D_MODEL = 1024
CHUNK = 64
EPS = 1e-6
N_HEADS = 8
HEAD_DIM = 64
LANES = 128
N_PAIRS = N_HEADS * HEAD_DIM // LANES
WIDTH = N_HEADS * HEAD_DIM

A_LEFT_CHUNKS = 8
A_MAX_REL = 128
A_QTILE = 2 * CHUNK
A_BAND = (A_LEFT_CHUNKS + 2) * CHUNK
A_PAD = A_LEFT_CHUNKS * CHUNK
A_TILES_PER_STEP = 8

C_Q_RANK = 256
C_KV_RANK = 128
C_ROPE = 32
C_NOPE = 64
C_SLOT = LANES
ROPE_BASE = 10000.0
C_TILE = 512
ONES_ROWS = 16
LOG2E = 1.4426950408889634
C_QSCALE = (C_NOPE + C_ROPE) ** -0.5 * LOG2E

D_FF = 4 * D_MODEL
N_BRANCH = 3

SUB = 8
N_SUB = CHUNK // SUB
NEG_BIG = -1e30

VMEM_LIMIT = 48 * 1024 * 1024
_NT = (((1,), (1,)), ((), ()))


def _cparams(sem):
    return pltpu.CompilerParams(dimension_semantics=sem, vmem_limit_bytes=VMEM_LIMIT)


def _resident(a):
    return pl.BlockSpec(a.shape, lambda *_: (0,) * a.ndim, pipeline_mode=pl.Buffered(1))


def _rms(x, g):
    return x * lax.rsqrt(jnp.mean(x * x, axis=-1, keepdims=True) + EPS) * g


def _norm_matmul_kernel(x_ref, g_ref, wa_ref, wb_ref, oa_ref, ob_ref, *, tn):
    h = _rms(x_ref[...], g_ref[...]).astype(BF16)
    for j in range(wa_ref.shape[1] // tn):
        cols = slice(j * tn, (j + 1) * tn)
        oa_ref[:, cols] = jnp.dot(h, wa_ref[:, cols], preferred_element_type=F32).astype(oa_ref.dtype)
    ob_ref[...] = jnp.dot(h, wb_ref[...], preferred_element_type=F32).astype(ob_ref.dtype)


def norm_matmul(x, g, wa, wb, dtype_a, dtype_b, tm, tn, name):
    t, d = x.shape
    na, nb = wa.shape[1], wb.shape[1]
    return pl.pallas_call(
        functools.partial(_norm_matmul_kernel, tn=tn),
        out_shape=(jax.ShapeDtypeStruct((t, na), dtype_a), jax.ShapeDtypeStruct((t, nb), dtype_b)),
        grid=(t // tm,),
        in_specs=[
            pl.BlockSpec((tm, d), lambda i: (i, 0)),
            pl.BlockSpec((1, d), lambda i: (0, 0)),
            _resident(wa),
            _resident(wb),
        ],
        out_specs=(pl.BlockSpec((tm, na), lambda i: (i, 0)), pl.BlockSpec((tm, nb), lambda i: (i, 0))),
        compiler_params=_cparams(("parallel",)),
        name=name,
    )(x, g.reshape(1, d), wa, wb)


def _trig_kernel(pos_ref, invf_ref, c_ref, s_ref, ct_ref, st_ref):
    ang = pos_ref[...].astype(F32) * invf_ref[...]
    c = jnp.cos(ang)
    s = jnp.sin(ang)
    c_ref[...] = c
    s_ref[...] = s
    ct_ref[...] = c.T
    st_ref[...] = s.T


def rope_tables(positions, tm):
    t = positions.size
    inv_freq = ROPE_BASE ** (-jnp.arange(0, C_ROPE, 2, dtype=F32) / C_ROPE)
    half = C_ROPE // 2
    invf = jnp.zeros((LANES,), F32)
    invf = invf.at[C_NOPE:C_NOPE + half].set(inv_freq).at[C_NOPE + half:C_NOPE + C_ROPE].set(inv_freq)
    return pl.pallas_call(
        _trig_kernel,
        out_shape=(jax.ShapeDtypeStruct((t, LANES), F32),) * 2 + (jax.ShapeDtypeStruct((LANES, t), F32),) * 2,
        grid=(t // tm,),
        in_specs=[
            pl.BlockSpec((tm, 1), lambda i: (i, 0)),
            pl.BlockSpec((1, LANES), lambda i: (0, 0)),
        ],
        out_specs=(pl.BlockSpec((tm, LANES), lambda i: (i, 0)),) * 2
        + (pl.BlockSpec((LANES, tm), lambda i: (0, i)),) * 2,
        compiler_params=_cparams(("parallel",)),
        name="rope_tables",
    )(positions.reshape(t, 1), invf.reshape(1, LANES))


def _band_attn_kernel(q_ref, k_ref, v_ref, bias_ref, o_ref):
    lane = lax.broadcasted_iota(jnp.int32, (A_QTILE, LANES), 1)
    n_shift = A_PAD // A_QTILE
    starts, scores = [], []
    for t in range(A_TILES_PER_STEP):
        c2 = pl.program_id(2) * A_TILES_PER_STEP + t
        start = pl.multiple_of(jnp.maximum(c2 * A_QTILE - A_PAD, 0), A_QTILE)
        shift = jnp.minimum(c2, n_shift)
        kwin = k_ref[pl.ds(start, A_BAND), :]
        q = q_ref[t * A_QTILE:(t + 1) * A_QTILE, :].astype(F32) * (HEAD_DIM ** -0.5)
        qst = jnp.concatenate([jnp.where(lane < HEAD_DIM, q, 0.0), jnp.where(lane < HEAD_DIM, 0.0, q)],
                              axis=0).astype(BF16)
        s = lax.dot_general(qst, kwin, _NT, preferred_element_type=F32)
        starts.append(start)
        scores.append(s + bias_ref[shift, 0])
    for t in range(A_TILES_PER_STEP):
        s = scores[t]
        vwin = v_ref[pl.ds(starts[t], A_BAND), :]
        m = jnp.max(s, axis=-1, keepdims=True)
        p = jnp.exp(s - m)
        l = jnp.sum(p, axis=-1, keepdims=True)
        pv = jnp.dot(p.astype(BF16), vwin, preferred_element_type=F32) / l
        o_ref[t * A_QTILE:(t + 1) * A_QTILE, :] = jnp.where(lane < HEAD_DIM, pv[:A_QTILE], pv[A_QTILE:]).astype(o_ref.dtype)


def band_attention(proj, bias, bsz, seq, q_col, k_col, v_col):
    t = bsz * seq
    rows = A_QTILE * A_TILES_PER_STEP
    n_steps = seq // rows
    n_shift = A_PAD // A_QTILE
    return pl.pallas_call(
        _band_attn_kernel,
        out_shape=jax.ShapeDtypeStruct((t, WIDTH), BF16),
        grid=(bsz, N_PAIRS, n_steps),
        in_specs=[
            pl.BlockSpec((rows, LANES), lambda b, p, c: (b * n_steps + c, q_col + p)),
            pl.BlockSpec((seq, LANES), lambda b, p, c: (b, k_col + p)),
            pl.BlockSpec((seq, LANES), lambda b, p, c: (b, v_col + p)),
            pl.BlockSpec((n_shift + 1, 1, 2 * A_QTILE, A_BAND), lambda b, p, c: (0, p, 0, 0)),
        ],
        out_specs=pl.BlockSpec((rows, LANES), lambda b, p, c: (b * n_steps + c, p)),
        compiler_params=_cparams(("parallel", "parallel", "arbitrary")),
        name="band_attention",
    )(proj, proj, proj, bias.reshape(n_shift + 1, N_PAIRS, 2 * A_QTILE, A_BAND))


A_ROLL = 768


def _band_bias_kernel(r_ref, o_ref):
    d = pl.program_id(0) * A_QTILE
    prof = jnp.broadcast_to(r_ref[...], (A_QTILE, A_ROLL))
    b = pltpu.roll(prof, 0, 1, stride=1, stride_axis=0)[:, :A_BAND]
    i = lax.broadcasted_iota(jnp.int32, (A_QTILE, A_BAND), 0)
    j = lax.broadcasted_iota(jnp.int32, (A_QTILE, A_BAND), 1)
    gap = (d + i) // CHUNK - j // CHUNK
    o_ref[...] = jnp.where((gap >= 0) & (gap <= A_LEFT_CHUNKS), b, NEG_BIG)


def band_bias_table(rel_table):
    n_shift = A_PAD // A_QTILE
    heads = rel_table.shape[0]
    pad = A_ROLL + A_PAD
    ext = jnp.pad(rel_table.astype(F32)[:, ::-1], ((0, 0), (pad, pad)), mode="edge")

    def seg(d, u0, n):
        o = u0 - d + A_MAX_REL + pad
        return ext[:, o:o + n]

    n_neg = A_ROLL - (A_BAND + 1)
    prof = jnp.stack([jnp.concatenate([seg(s * A_QTILE, 0, A_BAND + 1), seg(s * A_QTILE, -n_neg, n_neg)], axis=1)
                      for s in range(n_shift + 1)])
    return pl.pallas_call(
        _band_bias_kernel,
        out_shape=jax.ShapeDtypeStruct((n_shift + 1, heads, A_QTILE, A_BAND), F32),
        grid=(n_shift + 1, heads),
        in_specs=[pl.BlockSpec((None, None, 1, A_ROLL), lambda s, h: (s, h, 0, 0))],
        out_specs=pl.BlockSpec((None, None, A_QTILE, A_BAND), lambda s, h: (s, h, 0, 0)),
        compiler_params=_cparams(("parallel", "parallel")),
        name="band_bias",
    )(prof.reshape(n_shift + 1, heads, 1, A_ROLL))


def _split3(x):
    hi = x.astype(BF16)
    r1 = x - hi.astype(F32)
    mid = r1.astype(BF16)
    lo = (r1 - mid.astype(F32)).astype(BF16)
    return hi, mid, lo


def _hgrn_kernel(q_ref, f_ref, i_ref, g_ref, loglb_ref, log1mlb_ref, ng_ref, o_ref, st_ref, *, n_chunks):
    @pl.when(pl.program_id(2) == 0)
    def _():
        st_ref[...] = jnp.zeros_like(st_ref)

    log_lb = loglb_ref[...]
    log1m_lb = log1mlb_ref[...]
    norm_g = ng_ref[...]
    rows = n_chunks * CHUNK
    n_blk = rows // SUB

    head0 = lax.broadcasted_iota(jnp.int32, (rows, LANES), 1) < HEAD_DIM
    head0_c = lax.broadcasted_iota(jnp.int32, (CHUNK, LANES), 1) < HEAD_DIM
    r64 = lax.broadcasted_iota(jnp.int32, (CHUNK, CHUNK), 0)
    c64 = lax.broadcasted_iota(jnp.int32, (CHUNK, CHUNK), 1)
    tril = (c64 <= r64).astype(BF16)
    c64s = lax.broadcasted_iota(jnp.int32, (2 * CHUNK, CHUNK), 1)
    rl = lax.broadcasted_iota(jnp.int32, (LANES, LANES), 0)
    cl = lax.broadcasted_iota(jnp.int32, (LANES, LANES), 1)
    same_head = (rl // HEAD_DIM) == (cl // HEAD_DIM)
    bd = same_head.astype(BF16)
    ri = lax.broadcasted_iota(jnp.int32, (CHUNK, CHUNK * SUB), 0)
    ci = lax.broadcasted_iota(jnp.int32, (CHUNK, CHUNK * SUB), 1)
    pick = (ci // SUB == ri).astype(BF16)
    row_s = lax.broadcasted_iota(jnp.int32, (SUB, LANES), 0)

    def chunk(x, c):
        return x[c * CHUNK:(c + 1) * CHUNK]

    z = f_ref[...]
    qraw = q_ref[...].astype(F32)
    v = i_ref[...].astype(F32)
    graw = g_ref[...].astype(F32)

    log_sig = jnp.minimum(z, 0.0) - jnp.log(1.0 + jnp.exp(-jnp.abs(z)))
    bterm = log1m_lb + log_sig
    log_f = jnp.maximum(log_lb, bterm) + jnp.log(1.0 + jnp.exp(-jnp.abs(log_lb - bterm)))
    log_k = bterm - z
    qs = qraw * (1.0 / (1.0 + jnp.exp(-qraw)))

    x3 = jnp.concatenate(_split3(log_f), axis=1)
    cum3 = [jnp.dot(tril, chunk(x3, c), preferred_element_type=F32) for c in range(n_chunks)]
    cum = jnp.concatenate([t[:, :LANES] + t[:, LANES:2 * LANES] + t[:, 2 * LANES:] for t in cum3], axis=0)

    zero_row = jnp.zeros((1, LANES), F32)
    c_end = [cum[SUB * b + SUB - 1:SUB * b + SUB, :] for b in range(n_blk)]
    c_start = [zero_row if b % N_SUB == 0 else c_end[b - 1] for b in range(n_blk)]
    c_last = [c_end[c * N_SUB + N_SUB - 1] for c in range(n_chunks)]

    def rows_of(blocks):
        return jnp.concatenate([jnp.broadcast_to(r, (SUB, LANES)) for r in blocks], axis=0)

    cstart_full = rows_of(c_start)
    cend_full = rows_of(c_end)
    clast_full = jnp.concatenate([jnp.broadcast_to(r, (CHUNK, LANES)) for r in c_last], axis=0)
    q1 = qs * jnp.exp(cum - cstart_full)
    lk = log_k - cum
    k2b = jnp.exp(cend_full + lk).astype(BF16)
    k_end = jnp.exp(clast_full + lk).astype(BF16)
    q_state = (qs * jnp.exp(cum)).astype(BF16)
    vb = v.astype(BF16)

    s_cross = []
    for j in range(N_SUB - 1):
        d_rows = []
        for b in range(n_blk):
            if b % N_SUB > j:
                d_rows.append(jnp.exp(c_start[b] - c_end[(b // N_SUB) * N_SUB + j]))
            else:
                d_rows.append(zero_row)
        qj = q1 * rows_of(d_rows)
        q0 = jnp.where(head0, qj, 0.0).astype(BF16)
        q1h = jnp.where(head0, 0.0, qj).astype(BF16)
        for c in range(n_chunks):
            qst = jnp.concatenate([chunk(q0, c), chunk(q1h, c)], axis=0)
            s_cross.append(lax.dot_general(qst, chunk(k2b, c), _NT, preferred_element_type=F32))
    o_cross = []
    for c in range(n_chunks):
        sc = None
        for j in range(N_SUB - 1):
            col_in_j = (c64s >= j * SUB) & (c64s < (j + 1) * SUB)
            s = jnp.where(col_in_j, s_cross[j * n_chunks + c], 0.0)
            sc = s if sc is None else sc + s
        o2 = jnp.dot(sc.astype(BF16), chunk(vb, c), preferred_element_type=F32)
        o_cross.append(jnp.where(head0_c, o2[:CHUNK], o2[CHUNK:]))

    a = cum - log_k
    w_rows = []
    for b in range(n_blk):
        a_b = a[SUB * b:SUB * (b + 1), :]
        for i in range(SUB):
            r = SUB * b + i
            arg = jnp.where(row_s <= i, cum[r:r + 1, :] - a_b, NEG_BIG)
            w_rows.append(jnp.exp(arg) * qs[r:r + 1, :])
    w_all = jnp.concatenate(w_rows, axis=0).astype(BF16)
    sb = jnp.dot(w_all, bd, preferred_element_type=F32)
    o_diag = []
    for c in range(n_chunks):
        v_rep = jnp.concatenate([v[SUB * (r // SUB):SUB * (r // SUB + 1), :]
                                 for r in range(c * CHUNK, (c + 1) * CHUNK)], axis=0)
        sb_c = sb[c * CHUNK * SUB:(c + 1) * CHUNK * SUB]
        o_diag.append(jnp.dot(pick, (sb_c * v_rep).astype(BF16), preferred_element_type=F32))

    upd = [jnp.where(same_head,
                     lax.dot_general(chunk(vb, c), chunk(k_end, c), (((0,), (0,)), ((), ())),
                                     preferred_element_type=F32), 0.0) for c in range(n_chunks)]
    st = st_ref[...]
    o_state = []
    for c in range(n_chunks):
        o_state.append(lax.dot_general(chunk(q_state, c), st.astype(BF16), _NT, preferred_element_type=F32))
        st = st * jnp.exp(c_last[c]) + upd[c]
    st_ref[...] = st

    o = jnp.concatenate([o_cross[c] + o_diag[c] + o_state[c] for c in range(n_chunks)], axis=0)
    osq = o * o
    ms0 = jnp.sum(jnp.where(head0, osq, 0.0), axis=-1, keepdims=True)
    ms1 = jnp.sum(jnp.where(head0, 0.0, osq), axis=-1, keepdims=True)
    ms = jnp.where(head0, ms0, ms1) * (1.0 / HEAD_DIM)
    y = o * lax.rsqrt(ms + EPS) * norm_g
    y = y * (graw * (1.0 / (1.0 + jnp.exp(-graw))))
    o_ref[...] = y.astype(o_ref.dtype)


def hgrn2(proj16, proj32, lb, norm_g, bsz, seq, q_col, i_col, g_col, f_col, rows_per_step):
    t = bsz * seq
    n_steps = seq // rows_per_step
    log_lb = jnp.log(lb).reshape(N_PAIRS, 1, LANES)
    log1m_lb = jnp.log1p(-lb).reshape(N_PAIRS, 1, LANES)
    ng = jnp.tile(norm_g.astype(F32), LANES // HEAD_DIM).reshape(1, LANES)

    def act(col):
        return pl.BlockSpec((rows_per_step, LANES), lambda b, p, s: (b * n_steps + s, col + p))

    def par():
        return pl.BlockSpec((None, 1, LANES), lambda b, p, s: (p, 0, 0))

    return pl.pallas_call(
        functools.partial(_hgrn_kernel, n_chunks=rows_per_step // CHUNK),
        out_shape=jax.ShapeDtypeStruct((t, WIDTH), BF16),
        grid=(bsz, N_PAIRS, n_steps),
        in_specs=[act(q_col), act(f_col), act(i_col), act(g_col), par(), par(),
                  pl.BlockSpec((1, LANES), lambda b, p, s: (0, 0))],
        out_specs=pl.BlockSpec((rows_per_step, LANES), lambda b, p, s: (b * n_steps + s, p)),
        scratch_shapes=[pltpu.VMEM((LANES, LANES), F32)],
        compiler_params=_cparams(("parallel", "parallel", "arbitrary")),
        name="hgrn2",
    )(proj16, proj32, proj16, proj16, log_lb, log1m_lb, ng)


def _mla_prep_kernel(pc_ref, ct_ref, st_ref, ctt_ref, stt_ref, gq_ref, gkv_ref,
                     wq1t_ref, wq2t_ref, wk_ref, wvt_ref, qt_ref, k_ref, vt_ref):
    pc = pc_ref[...]
    qn = _rms(pc[:, :C_Q_RANK], gq_ref[...]).astype(BF16)
    kvn = _rms(pc[:, C_Q_RANK:C_Q_RANK + C_KV_RANK], gkv_ref[...]).astype(BF16)
    kr = pc[:, C_Q_RANK + C_KV_RANK:C_Q_RANK + C_KV_RANK + C_SLOT]
    kr_sw = pc[:, C_Q_RANK + C_KV_RANK + C_SLOT:C_Q_RANK + C_KV_RANK + 2 * C_SLOT]
    a_t = lax.dot_general(wq1t_ref[...], qn, _NT, preferred_element_type=F32)
    b_t = lax.dot_general(wq2t_ref[...], qn, _NT, preferred_element_type=F32)
    kn = jnp.dot(kvn, wk_ref[...], preferred_element_type=F32)
    vt_ref[...] = lax.dot_general(wvt_ref[...], kvn, _NT, preferred_element_type=F32).astype(vt_ref.dtype)
    kr_rot = kr * ct_ref[...] + kr_sw * st_ref[...]
    ctt = ctt_ref[...]
    stt = stt_ref[...]
    for h in range(N_HEADS):
        sl = slice(h * C_SLOT, (h + 1) * C_SLOT)
        qt_ref[sl, :] = (a_t[sl, :] * ctt + b_t[sl, :] * stt).astype(qt_ref.dtype)
        k_ref[:, sl] = (kn[:, sl] + kr_rot).astype(k_ref.dtype)


def mla_prep(proj32, tabs, gq, gkv, wq1t, wq2t, wk, wvt, tm):
    t = proj32.shape[0]
    c_in = C_Q_RANK + C_KV_RANK + 2 * C_SLOT
    ctab, stab, ctab_t, stab_t = tabs

    def full(a):
        return pl.BlockSpec(a.shape, lambda i: (0,) * a.ndim)

    gq = gq.reshape(1, -1)
    gkv = gkv.reshape(1, -1)
    return pl.pallas_call(
        _mla_prep_kernel,
        out_shape=(jax.ShapeDtypeStruct((N_HEADS * C_SLOT, t), BF16),
                   jax.ShapeDtypeStruct((t, N_HEADS * C_SLOT), BF16),
                   jax.ShapeDtypeStruct((WIDTH, t), BF16)),
        grid=(t // tm,),
        in_specs=[pl.BlockSpec((tm, c_in), lambda i: (i, 0)),
                  pl.BlockSpec((tm, LANES), lambda i: (i, 0)),
                  pl.BlockSpec((tm, LANES), lambda i: (i, 0)),
                  pl.BlockSpec((LANES, tm), lambda i: (0, i)),
                  pl.BlockSpec((LANES, tm), lambda i: (0, i)),
                  full(gq), full(gkv), full(wq1t), full(wq2t), full(wk), full(wvt)],
        out_specs=(pl.BlockSpec((N_HEADS * C_SLOT, tm), lambda i: (0, i)),
                   pl.BlockSpec((tm, N_HEADS * C_SLOT), lambda i: (i, 0)),
                   pl.BlockSpec((WIDTH, tm), lambda i: (0, i))),
        compiler_params=_cparams(("parallel",)),
        name="mla_prep",
    )(proj32, ctab, stab, ctab_t, stab_t, gq, gkv, wq1t, wq2t, wk, wvt)


def _mla_flash_kernel(qt_ref, k_ref, vt_ref, o_ref, acc_ref, m_ref, l_ref, s_ref, *, n_qt):
    c = C_QSCALE

    def scores(qi, j, e):
        qoff = pl.multiple_of(qi * C_TILE, C_TILE)
        koff = pl.multiple_of(j * C_TILE, C_TILE)
        k = k_ref[pl.ds(koff, C_TILE), e * C_SLOT:(e + 1) * C_SLOT]
        qt = qt_ref[e * C_SLOT:(e + 1) * C_SLOT, pl.ds(qoff, C_TILE)]
        return jnp.dot(k, qt, preferred_element_type=F32)

    def softmax_pv(j, e, s):
        off = pl.multiple_of(j * C_TILE, C_TILE)
        m_prev = m_ref[e]
        m_new = jnp.maximum(m_prev, jnp.max(s, axis=0, keepdims=True))
        p = jnp.exp2((s - m_new) * c).astype(BF16)
        alpha = jnp.exp2((m_prev - m_new) * c)
        vt = vt_ref[e * HEAD_DIM:(e + 1) * HEAD_DIM, pl.ds(off, C_TILE)]
        vt_aug = jnp.concatenate([vt, jnp.ones((ONES_ROWS, C_TILE), BF16)], axis=0)
        pv = jnp.dot(vt_aug, p, preferred_element_type=F32)
        l_ref[e] = alpha * l_ref[e] + pv[HEAD_DIM:HEAD_DIM + 1]
        acc_ref[e] = alpha * acc_ref[e] + pv[:HEAD_DIM]
        m_ref[e] = m_new

    kc = lax.broadcasted_iota(jnp.int32, (C_TILE, C_TILE), 0) // CHUNK
    qc = lax.broadcasted_iota(jnp.int32, (C_TILE, C_TILE), 1) // CHUNK

    for e in range(2):
        s_ref[e] = scores(0, 0, e)

    def q_body(qi, carry):
        m_ref[...] = jnp.full_like(m_ref, -jnp.inf)
        l_ref[...] = jnp.zeros_like(l_ref)
        acc_ref[...] = jnp.zeros_like(acc_ref)

        def body(j, carry2):
            for e in range(2):
                s = s_ref[e]
                s_ref[e] = scores(qi, j + 1, e)
                softmax_pv(j, e, s)
            return carry2

        lax.fori_loop(0, qi, body, 0)
        q_next = jnp.minimum(qi + 1, n_qt - 1)
        for e in range(2):
            s = jnp.where(kc <= qc, s_ref[e], NEG_BIG)
            s_ref[e] = scores(q_next, 0, e)
            softmax_pv(qi, e, s)

        o_t = jnp.concatenate([acc_ref[0] / l_ref[0], acc_ref[1] / l_ref[1]], axis=0)
        o_ref[pl.ds(pl.multiple_of(qi * C_TILE, C_TILE), C_TILE), :] = o_t.T.astype(o_ref.dtype)
        return carry

    lax.fori_loop(0, n_qt, q_body, 0)


def mla_flash(qt, k, vt, bsz, seq):
    t = bsz * seq
    return pl.pallas_call(
        functools.partial(_mla_flash_kernel, n_qt=seq // C_TILE),
        out_shape=jax.ShapeDtypeStruct((t, WIDTH), BF16),
        grid=(bsz, N_PAIRS),
        in_specs=[
            pl.BlockSpec((2 * C_SLOT, seq), lambda b, p: (p, b)),
            pl.BlockSpec((seq, 2 * C_SLOT), lambda b, p: (b, p)),
            pl.BlockSpec((LANES, seq), lambda b, p: (p, b)),
        ],
        out_specs=pl.BlockSpec((seq, LANES), lambda b, p: (b, p)),
        scratch_shapes=[pltpu.VMEM((2, HEAD_DIM, C_TILE), F32),
                        pltpu.VMEM((2, 1, C_TILE), F32),
                        pltpu.VMEM((2, 1, C_TILE), F32),
                        pltpu.VMEM((2, C_TILE, C_TILE), F32)],
        compiler_params=_cparams(("parallel", "parallel")),
        name="mla_flash",
    )(qt, k, vt)


def _merge_kernel(x_ref, ya_ref, yb_ref, yc_ref, g_ref, wbr_ref, wout_ref, o_ref):
    merged = None
    for n, y_ref in enumerate((ya_ref, yb_ref, yc_ref)):
        up = jnp.dot(y_ref[...], wbr_ref[n], preferred_element_type=F32)
        gl = g_ref[:, n * D_MODEL:(n + 1) * D_MODEL].astype(F32)
        term = (1.0 / (1.0 + jnp.exp(-gl))) * up
        merged = term if merged is None else merged + term
    o_ref[...] = x_ref[...] + jnp.dot(merged.astype(BF16), wout_ref[...], preferred_element_type=F32)


def merge_out(x, ya, yb, yc, proj16, gate_blk, wbr, wout, tm):
    t = x.shape[0]
    return pl.pallas_call(
        _merge_kernel,
        out_shape=jax.ShapeDtypeStruct((t, D_MODEL), F32),
        grid=(t // tm,),
        in_specs=[
            pl.BlockSpec((tm, D_MODEL), lambda i: (i, 0)),
            pl.BlockSpec((tm, WIDTH), lambda i: (i, 0)),
            pl.BlockSpec((tm, WIDTH), lambda i: (i, 0)),
            pl.BlockSpec((tm, WIDTH), lambda i: (i, 0)),
            pl.BlockSpec((tm, N_BRANCH * D_MODEL), lambda i: (i, gate_blk)),
            _resident(wbr),
            _resident(wout),
        ],
        out_specs=pl.BlockSpec((tm, D_MODEL), lambda i: (i, 0)),
        compiler_params=_cparams(("parallel",)),
        name="merge_out",
    )(x, ya, yb, yc, proj16, wbr, wout)


def _ffn_kernel(x_ref, g_ref, w1_ref, w2_ref, gf_ref, o_ref, *, final_norm, tf):
    x = x_ref[...]
    h = _rms(x, g_ref[...]).astype(BF16)
    acc = None
    for k in range(w1_ref.shape[1] // tf):
        u = jnp.maximum(jnp.dot(h, w1_ref[:, k * tf:(k + 1) * tf], preferred_element_type=F32), 0.0)
        part = jnp.dot((u * u).astype(BF16), w2_ref[k * tf:(k + 1) * tf, :], preferred_element_type=F32)
        acc = part if acc is None else acc + part
    y = x + acc
    if final_norm:
        y = _rms(y, gf_ref[...])
    o_ref[...] = y


def ffn(x, g, w1, w2, gf, final_norm, tm, tf):
    t, d = x.shape
    return pl.pallas_call(
        functools.partial(_ffn_kernel, final_norm=final_norm, tf=tf),
        out_shape=jax.ShapeDtypeStruct((t, d), F32),
        grid=(t // tm,),
        in_specs=[
            pl.BlockSpec((tm, d), lambda i: (i, 0)),
            pl.BlockSpec((1, d), lambda i: (0, 0)),
            _resident(w1),
            _resident(w2),
            pl.BlockSpec((1, d), lambda i: (0, 0)),
        ],
        out_specs=pl.BlockSpec((tm, d), lambda i: (i, 0)),
        compiler_params=_cparams(("parallel",)),
        name="ffn",
    )(x, g.reshape(1, d), w1, w2, gf.reshape(1, d))


COL_AQ, COL_AK, COL_AV = 0, 4, 8
COL_BQ, COL_BI, COL_BG = 12, 16, 20
GATE_BLK = 1
COL32_BF = (C_Q_RANK + C_KV_RANK + 2 * C_SLOT) // LANES


def _prep_w_in(w):
    o = 0
    parts = {}
    for name, size in (("aq", 512), ("ak", 512), ("av", 512), ("bq", 512), ("bf", 512), ("bi", 512),
                       ("bg", 512), ("cq", C_Q_RANK), ("ckv", C_KV_RANK), ("ckr", C_ROPE),
                       ("gate", N_BRANCH * D_MODEL)):
        parts[name] = w[:, o:o + size]
        o += size
    w16 = jnp.concatenate([parts[n] for n in ("aq", "ak", "av", "bq", "bi", "bg", "gate")], axis=1)
    half = C_ROPE // 2
    x1, x2 = parts["ckr"][:, :half], parts["ckr"][:, half:]
    zl = jnp.zeros((w.shape[0], C_NOPE), w.dtype)
    zr = jnp.zeros((w.shape[0], C_SLOT - C_NOPE - C_ROPE), w.dtype)
    kr_slot = jnp.concatenate([zl, x1, x2, zr], axis=1)
    kr_sw = jnp.concatenate([zl, -x2, x1, zr], axis=1)
    w32 = jnp.concatenate([parts["cq"], parts["ckv"], kr_slot, kr_sw, parts["bf"]], axis=1)
    return w16.astype(BF16), w32.astype(BF16)


def _prep_w_mla(w_uq, w_ukv):
    half = C_ROPE // 2
    per_q = C_NOPE + C_ROPE
    wq = w_uq.reshape(C_Q_RANK, N_HEADS, per_q)
    nope, x1, x2 = wq[..., :C_NOPE], wq[..., C_NOPE:C_NOPE + half], wq[..., C_NOPE + half:]
    zpad = jnp.zeros((C_Q_RANK, N_HEADS, C_SLOT - per_q), w_uq.dtype)
    wq1 = jnp.concatenate([nope, x1, x2, zpad], axis=-1).reshape(C_Q_RANK, N_HEADS * C_SLOT)
    wq2 = jnp.concatenate([jnp.zeros_like(nope), -x2, x1, zpad], axis=-1).reshape(C_Q_RANK, N_HEADS * C_SLOT)
    wkv = w_ukv.reshape(C_KV_RANK, N_HEADS, C_NOPE + HEAD_DIM)
    k_nope, v = wkv[..., :C_NOPE], wkv[..., C_NOPE:]
    wk = jnp.concatenate([k_nope, jnp.zeros((C_KV_RANK, N_HEADS, C_SLOT - C_NOPE), w_ukv.dtype)],
                         axis=-1).reshape(C_KV_RANK, N_HEADS * C_SLOT)
    wv = v.reshape(C_KV_RANK, WIDTH)
    return wq1.T.astype(BF16), wq2.T.astype(BF16), wk.astype(BF16), wv.T.astype(BF16)


def kernel(x, positions, norm_mix_g, w_in, rel_bias, hgrn_lb_logits, hgrn_norm_g, mla_q_norm_g,
           mla_kv_norm_g, mla_w_uq, mla_w_ukv, w_branch, w_out, norm_ffn_g, w_ff1, w_ff2, final_norm_g):
    bsz, seq, d = x.shape
    depth = w_in.shape[0]
    t = bsz * seq
    assert d == D_MODEL and seq % C_TILE == 0 and seq >= A_BAND
    tm = min(1024, t)
    hgrn_rows = min(512, seq)

    p_lb = jax.nn.softmax(hgrn_lb_logits.astype(F32), axis=0)
    lb_all = jnp.cumsum(p_lb, axis=0)
    lb_all = lb_all - lb_all[0:1]

    tabs = rope_tables(positions, tm)
    xf = x.reshape(t, d)
    for l in range(depth):
        w16, w32 = _prep_w_in(w_in[l])
        wq1t, wq2t, wk, wvt = _prep_w_mla(mla_w_uq[l], mla_w_ukv[l])
        proj16, proj32 = norm_matmul(xf, norm_mix_g[l], w16, w32, BF16, F32, min(512, t), 1024, "in_proj")

        y_a = band_attention(proj16, band_bias_table(rel_bias[l]), bsz, seq, COL_AQ, COL_AK, COL_AV)
        y_b = hgrn2(proj16, proj32, lb_all[l], hgrn_norm_g[l], bsz, seq,
                    COL_BQ, COL_BI, COL_BG, COL32_BF, hgrn_rows)
        qt_c, k_c, vt_c = mla_prep(proj32, tabs, mla_q_norm_g[l], mla_kv_norm_g[l], wq1t, wq2t, wk, wvt, tm)
        y_c = mla_flash(qt_c, k_c, vt_c, bsz, seq)

        xf = merge_out(xf, y_a, y_b, y_c, proj16, GATE_BLK, w_branch[l].astype(BF16),
                       w_out[l].astype(BF16), tm)
        xf = ffn(xf, norm_ffn_g[l], w_ff1[l].astype(BF16), w_ff2[l].astype(BF16), final_norm_g,
                 l == depth - 1, tm, 1024)
    return xf.reshape(bsz, seq, d)
```

```python
import functools

import jax
import jax.numpy as jnp
from jax import lax
from jax.experimental import pallas as pl
from jax.experimental.pallas import tpu as pltpu

F32 = jnp.float32
BF16 = jnp.bfloat16

D_MODEL = 1024
CHUNK = 64
EPS = 1e-6
N_HEADS = 8
HEAD_DIM = 64
LANES = 128
N_PAIRS = N_HEADS * HEAD_DIM // LANES
WIDTH = N_HEADS * HEAD_DIM

A_LEFT_CHUNKS = 8
A_MAX_REL = 128
A_QTILE = 2 * CHUNK
A_BAND = (A_LEFT_CHUNKS + 2) * CHUNK
A_PAD = A_LEFT_CHUNKS * CHUNK
A_TILES_PER_STEP = 8

C_Q_RANK = 256
C_KV_RANK = 128
C_ROPE = 32
C_NOPE = 64
C_SLOT = LANES
C_LATENT = C_Q_RANK + C_KV_RANK + 2 * C_SLOT
ROPE_BASE = 10000.0
C_TILE = 512
ONES_ROWS = 16
LOG2E = 1.4426950408889634
C_QSCALE = (C_NOPE + C_ROPE) ** -0.5 * LOG2E

D_FF = 4 * D_MODEL
N_BRANCH = 3

SUB = 8
N_SUB = CHUNK // SUB
NEG_BIG = -1e30

VMEM_LIMIT = 48 * 1024 * 1024
_NT = (((1,), (1,)), ((), ()))


def _cparams(sem):
    return pltpu.CompilerParams(dimension_semantics=sem, vmem_limit_bytes=VMEM_LIMIT)


def _resident(a):
    return pl.BlockSpec(a.shape, lambda *_: (0,) * a.ndim, pipeline_mode=pl.Buffered(1))


def _rms(x, g):
    return x * lax.rsqrt(jnp.mean(x * x, axis=-1, keepdims=True) + EPS) * g


def _in_proj_kernel(x_ref, g_ref, wa_ref, wb_ref, ct_ref, st_ref, ctt_ref, stt_ref, gq_ref, gkv_ref,
                    wq1t_ref, wq2t_ref, wk_ref, wvt_ref, oa_ref, of_ref, qt_ref, k_ref, vt_ref, *, tn):
    h = _rms(x_ref[...], g_ref[...]).astype(BF16)
    for j in range(wa_ref.shape[1] // tn):
        cols = slice(j * tn, (j + 1) * tn)
        oa_ref[:, cols] = jnp.dot(h, wa_ref[:, cols], preferred_element_type=F32).astype(oa_ref.dtype)
    pb = jnp.dot(h, wb_ref[...], preferred_element_type=F32)
    of_ref[...] = pb[:, C_LATENT:]

    qn = _rms(pb[:, :C_Q_RANK], gq_ref[...]).astype(BF16)
    kvn = _rms(pb[:, C_Q_RANK:C_Q_RANK + C_KV_RANK], gkv_ref[...]).astype(BF16)
    kr = pb[:, C_Q_RANK + C_KV_RANK:C_Q_RANK + C_KV_RANK + C_SLOT]
    kr_sw = pb[:, C_Q_RANK + C_KV_RANK + C_SLOT:C_LATENT]
    a_t = lax.dot_general(wq1t_ref[...], qn, _NT, preferred_element_type=F32)
    b_t = lax.dot_general(wq2t_ref[...], qn, _NT, preferred_element_type=F32)
    kn = jnp.dot(kvn, wk_ref[...], preferred_element_type=F32)
    vt_ref[...] = lax.dot_general(wvt_ref[...], kvn, _NT, preferred_element_type=F32).astype(vt_ref.dtype)
    kr_rot = kr * ct_ref[...] + kr_sw * st_ref[...]
    ctt = ctt_ref[...]
    stt = stt_ref[...]
    for hd in range(N_HEADS):
        sl = slice(hd * C_SLOT, (hd + 1) * C_SLOT)
        qt_ref[sl, :] = (a_t[sl, :] * ctt + b_t[sl, :] * stt).astype(qt_ref.dtype)
        k_ref[:, sl] = (kn[:, sl] + kr_rot).astype(k_ref.dtype)


def in_proj(x, g, wa, wb, tabs, gq, gkv, wq1t, wq2t, wk, wvt, tm, tn):
    t, d = x.shape
    na = wa.shape[1]
    nf = wb.shape[1] - C_LATENT
    ctab, stab, ctab_t, stab_t = tabs
    gq = gq.reshape(1, -1)
    gkv = gkv.reshape(1, -1)
    rows = lambda n: pl.BlockSpec((tm, n), lambda i: (i, 0))
    cols = lambda n: pl.BlockSpec((n, tm), lambda i: (0, i))
    return pl.pallas_call(
        functools.partial(_in_proj_kernel, tn=tn),
        out_shape=(jax.ShapeDtypeStruct((t, na), BF16),
                   jax.ShapeDtypeStruct((t, nf), F32),
                   jax.ShapeDtypeStruct((N_HEADS * C_SLOT, t), BF16),
                   jax.ShapeDtypeStruct((t, N_HEADS * C_SLOT), BF16),
                   jax.ShapeDtypeStruct((WIDTH, t), BF16)),
        grid=(t // tm,),
        in_specs=[rows(d), pl.BlockSpec((1, d), lambda i: (0, 0)), _resident(wa), _resident(wb),
                  rows(LANES), rows(LANES), cols(LANES), cols(LANES),
                  _resident(gq), _resident(gkv), _resident(wq1t), _resident(wq2t), _resident(wk), _resident(wvt)],
        out_specs=(rows(na), rows(nf), cols(N_HEADS * C_SLOT), rows(N_HEADS * C_SLOT), cols(WIDTH)),
        compiler_params=_cparams(("parallel",)),
        name="in_proj",
    )(x, g.reshape(1, d), wa, wb, ctab, stab, ctab_t, stab_t, gq, gkv, wq1t, wq2t, wk, wvt)


def _trig_kernel(pos_ref, invf_ref, c_ref, s_ref, ct_ref, st_ref):
    ang = pos_ref[...].astype(F32) * invf_ref[...]
    c = jnp.cos(ang)
    s = jnp.sin(ang)
    c_ref[...] = c
    s_ref[...] = s
    ct_ref[...] = c.T
    st_ref[...] = s.T


def rope_tables(positions, tm):
    t = positions.size
    inv_freq = ROPE_BASE ** (-jnp.arange(0, C_ROPE, 2, dtype=F32) / C_ROPE)
    half = C_ROPE // 2
    invf = jnp.zeros((LANES,), F32)
    invf = invf.at[C_NOPE:C_NOPE + half].set(inv_freq).at[C_NOPE + half:C_NOPE + C_ROPE].set(inv_freq)
    return pl.pallas_call(
        _trig_kernel,
        out_shape=(jax.ShapeDtypeStruct((t, LANES), F32),) * 2 + (jax.ShapeDtypeStruct((LANES, t), F32),) * 2,
        grid=(t // tm,),
        in_specs=[
            pl.BlockSpec((tm, 1), lambda i: (i, 0)),
            pl.BlockSpec((1, LANES), lambda i: (0, 0)),
        ],
        out_specs=(pl.BlockSpec((tm, LANES), lambda i: (i, 0)),) * 2
        + (pl.BlockSpec((LANES, tm), lambda i: (0, i)),) * 2,
        compiler_params=_cparams(("parallel",)),
        name="rope_tables",
    )(positions.reshape(t, 1), invf.reshape(1, LANES))


def _band_attn_kernel(q_ref, k_ref, v_ref, bias_ref, o_ref):
    lane = lax.broadcasted_iota(jnp.int32, (A_QTILE, LANES), 1)
    n_shift = A_PAD // A_QTILE
    starts, scores = [], []
    for t in range(A_TILES_PER_STEP):
        c2 = pl.program_id(2) * A_TILES_PER_STEP + t
        start = pl.multiple_of(jnp.maximum(c2 * A_QTILE - A_PAD, 0), A_QTILE)
        shift = jnp.minimum(c2, n_shift)
        kwin = k_ref[pl.ds(start, A_BAND), :]
        q = q_ref[t * A_QTILE:(t + 1) * A_QTILE, :].astype(F32) * (HEAD_DIM ** -0.5)
        qst = jnp.concatenate([jnp.where(lane < HEAD_DIM, q, 0.0), jnp.where(lane < HEAD_DIM, 0.0, q)],
                              axis=0).astype(BF16)
        s = lax.dot_general(qst, kwin, _NT, preferred_element_type=F32)
        starts.append(start)
        scores.append(s + bias_ref[shift, 0])
    for t in range(A_TILES_PER_STEP):
        s = scores[t]
        vwin = v_ref[pl.ds(starts[t], A_BAND), :]
        m = jnp.max(s, axis=-1, keepdims=True)
        p = jnp.exp(s - m)
        l = jnp.sum(p, axis=-1, keepdims=True)
        pv = jnp.dot(p.astype(BF16), vwin, preferred_element_type=F32) / l
        o_ref[t * A_QTILE:(t + 1) * A_QTILE, :] = jnp.where(lane < HEAD_DIM, pv[:A_QTILE], pv[A_QTILE:]).astype(o_ref.dtype)


def band_attention(proj, bias, bsz, seq, q_col, k_col, v_col):
    t = bsz * seq
    rows = A_QTILE * A_TILES_PER_STEP
    n_steps = seq // rows
    n_shift = A_PAD // A_QTILE
    return pl.pallas_call(
        _band_attn_kernel,
        out_shape=jax.ShapeDtypeStruct((t, WIDTH), BF16),
        grid=(bsz, N_PAIRS, n_steps),
        in_specs=[
            pl.BlockSpec((rows, LANES), lambda b, p, c: (b * n_steps + c, q_col + p)),
            pl.BlockSpec((seq, LANES), lambda b, p, c: (b, k_col + p)),
            pl.BlockSpec((seq, LANES), lambda b, p, c: (b, v_col + p)),
            pl.BlockSpec((n_shift + 1, 1, 2 * A_QTILE, A_BAND), lambda b, p, c: (0, p, 0, 0)),
        ],
        out_specs=pl.BlockSpec((rows, LANES), lambda b, p, c: (b * n_steps + c, p)),
        compiler_params=_cparams(("parallel", "parallel", "arbitrary")),
        name="band_attention",
    )(proj, proj, proj, bias.reshape(n_shift + 1, N_PAIRS, 2 * A_QTILE, A_BAND))


A_ROLL = 768


def _band_bias_kernel(r_ref, o_ref):
    d = pl.program_id(0) * A_QTILE
    prof = jnp.broadcast_to(r_ref[...], (A_QTILE, A_ROLL))
    b = pltpu.roll(prof, 0, 1, stride=1, stride_axis=0)[:, :A_BAND]
    i = lax.broadcasted_iota(jnp.int32, (A_QTILE, A_BAND), 0)
    j = lax.broadcasted_iota(jnp.int32, (A_QTILE, A_BAND), 1)
    gap = (d + i) // CHUNK - j // CHUNK
    o_ref[...] = jnp.where((gap >= 0) & (gap <= A_LEFT_CHUNKS), b, NEG_BIG)


def band_bias_table(rel_table):
    n_shift = A_PAD // A_QTILE
    heads = rel_table.shape[0]
    pad = A_ROLL + A_PAD
    ext = jnp.pad(rel_table.astype(F32)[:, ::-1], ((0, 0), (pad, pad)), mode="edge")

    def seg(d, u0, n):
        o = u0 - d + A_MAX_REL + pad
        return ext[:, o:o + n]

    n_neg = A_ROLL - (A_BAND + 1)
    prof = jnp.stack([jnp.concatenate([seg(s * A_QTILE, 0, A_BAND + 1), seg(s * A_QTILE, -n_neg, n_neg)], axis=1)
                      for s in range(n_shift + 1)])
    return pl.pallas_call(
        _band_bias_kernel,
        out_shape=jax.ShapeDtypeStruct((n_shift + 1, heads, A_QTILE, A_BAND), F32),
        grid=(n_shift + 1, heads),
        in_specs=[pl.BlockSpec((None, None, 1, A_ROLL), lambda s, h: (s, h, 0, 0))],
        out_specs=pl.BlockSpec((None, None, A_QTILE, A_BAND), lambda s, h: (s, h, 0, 0)),
        compiler_params=_cparams(("parallel", "parallel")),
        name="band_bias",
    )(prof.reshape(n_shift + 1, heads, 1, A_ROLL))


def _split3(x):
    hi = x.astype(BF16)
    r1 = x - hi.astype(F32)
    mid = r1.astype(BF16)
    lo = (r1 - mid.astype(F32)).astype(BF16)
    return hi, mid, lo


def _hgrn_kernel(q_ref, f_ref, i_ref, g_ref, loglb_ref, log1mlb_ref, ng_ref, o_ref, st_ref, *, n_chunks):
    @pl.when(pl.program_id(2) == 0)
    def _():
        st_ref[...] = jnp.zeros_like(st_ref)

    log_lb = loglb_ref[...]
    log1m_lb = log1mlb_ref[...]
    norm_g = ng_ref[...]
    rows = n_chunks * CHUNK
    n_blk = rows // SUB

    head0 = lax.broadcasted_iota(jnp.int32, (rows, LANES), 1) < HEAD_DIM
    head0_c = lax.broadcasted_iota(jnp.int32, (CHUNK, LANES), 1) < HEAD_DIM
    r64 = lax.broadcasted_iota(jnp.int32, (CHUNK, CHUNK), 0)
    c64 = lax.broadcasted_iota(jnp.int32, (CHUNK, CHUNK), 1)
    tril = (c64 <= r64).astype(BF16)
    c64s = lax.broadcasted_iota(jnp.int32, (2 * CHUNK, CHUNK), 1)
    rl = lax.broadcasted_iota(jnp.int32, (LANES, LANES), 0)
    cl = lax.broadcasted_iota(jnp.int32, (LANES, LANES), 1)
    same_head = (rl // HEAD_DIM) == (cl // HEAD_DIM)
    bd = same_head.astype(BF16)
    ri = lax.broadcasted_iota(jnp.int32, (CHUNK, CHUNK * SUB), 0)
    ci = lax.broadcasted_iota(jnp.int32, (CHUNK, CHUNK * SUB), 1)
    pick = (ci // SUB == ri).astype(BF16)
    row_s = lax.broadcasted_iota(jnp.int32, (SUB, LANES), 0)

    def chunk(x, c):
        return x[c * CHUNK:(c + 1) * CHUNK]

    z = f_ref[...]
    qraw = q_ref[...].astype(F32)
    v = i_ref[...].astype(F32)
    graw = g_ref[...].astype(F32)

    log_sig = jnp.minimum(z, 0.0) - jnp.log(1.0 + jnp.exp(-jnp.abs(z)))
    bterm = log1m_lb + log_sig
    log_f = jnp.maximum(log_lb, bterm) + jnp.log(1.0 + jnp.exp(-jnp.abs(log_lb - bterm)))
    log_k = bterm - z
    qs = qraw * (1.0 / (1.0 + jnp.exp(-qraw)))

    x3 = jnp.concatenate(_split3(log_f), axis=1)
    cum3 = [jnp.dot(tril, chunk(x3, c), preferred_element_type=F32) for c in range(n_chunks)]
    cum = jnp.concatenate([t[:, :LANES] + t[:, LANES:2 * LANES] + t[:, 2 * LANES:] for t in cum3], axis=0)

    zero_row = jnp.zeros((1, LANES), F32)
    c_end = [cum[SUB * b + SUB - 1:SUB * b + SUB, :] for b in range(n_blk)]
    c_start = [zero_row if b % N_SUB == 0 else c_end[b - 1] for b in range(n_blk)]
    c_last = [c_end[c * N_SUB + N_SUB - 1] for c in range(n_chunks)]

    def rows_of(blocks):
        return jnp.concatenate([jnp.broadcast_to(r, (SUB, LANES)) for r in blocks], axis=0)

    cstart_full = rows_of(c_start)
    cend_full = rows_of(c_end)
    clast_full = jnp.concatenate([jnp.broadcast_to(r, (CHUNK, LANES)) for r in c_last], axis=0)
    q1 = qs * jnp.exp(cum - cstart_full)
    lk = log_k - cum
    k2b = jnp.exp(cend_full + lk).astype(BF16)
    k_end = jnp.exp(clast_full + lk).astype(BF16)
    q_state = (qs * jnp.exp(cum)).astype(BF16)
    vb = v.astype(BF16)

    s_cross = []
    for j in range(N_SUB - 1):
        d_rows = []
        for b in range(n_blk):
            if b % N_SUB > j:
                d_rows.append(jnp.exp(c_start[b] - c_end[(b // N_SUB) * N_SUB + j]))
            else:
                d_rows.append(zero_row)
        qj = q1 * rows_of(d_rows)
        q0 = jnp.where(head0, qj, 0.0).astype(BF16)
        q1h = jnp.where(head0, 0.0, qj).astype(BF16)
        for c in range(n_chunks):
            qst = jnp.concatenate([chunk(q0, c), chunk(q1h, c)], axis=0)
            s_cross.append(lax.dot_general(qst, chunk(k2b, c), _NT, preferred_element_type=F32))
    o_cross = []
    for c in range(n_chunks):
        sc = None
        for j in range(N_SUB - 1):
            col_in_j = (c64s >= j * SUB) & (c64s < (j + 1) * SUB)
            s = jnp.where(col_in_j, s_cross[j * n_chunks + c], 0.0)
            sc = s if sc is None else sc + s
        o2 = jnp.dot(sc.astype(BF16), chunk(vb, c), preferred_element_type=F32)
        o_cross.append(jnp.where(head0_c, o2[:CHUNK], o2[CHUNK:]))

    a = cum - log_k
    w_rows = []
    for b in range(n_blk):
        a_b = a[SUB * b:SUB * (b + 1), :]
        for i in range(SUB):
            r = SUB * b + i
            arg = jnp.where(row_s <= i, cum[r:r + 1, :] - a_b, NEG_BIG)
            w_rows.append(jnp.exp(arg) * qs[r:r + 1, :])
    w_all = jnp.concatenate(w_rows, axis=0).astype(BF16)
    sb = jnp.dot(w_all, bd, preferred_element_type=F32)
    o_diag = []
    for c in range(n_chunks):
        v_rep = jnp.concatenate([v[SUB * (r // SUB):SUB * (r // SUB + 1), :]
                                 for r in range(c * CHUNK, (c + 1) * CHUNK)], axis=0)
        sb_c = sb[c * CHUNK * SUB:(c + 1) * CHUNK * SUB]
        o_diag.append(jnp.dot(pick, (sb_c * v_rep).astype(BF16), preferred_element_type=F32))

    upd = [jnp.where(same_head,
                     lax.dot_general(chunk(vb, c), chunk(k_end, c), (((0,), (0,)), ((), ())),
                                     preferred_element_type=F32), 0.0) for c in range(n_chunks)]
    st = st_ref[...]
    o_state = []
    for c in range(n_chunks):
        o_state.append(lax.dot_general(chunk(q_state, c), st.astype(BF16), _NT, preferred_element_type=F32))
        st = st * jnp.exp(c_last[c]) + upd[c]
    st_ref[...] = st

    o = jnp.concatenate([o_cross[c] + o_diag[c] + o_state[c] for c in range(n_chunks)], axis=0)
    osq = o * o
    ms0 = jnp.sum(jnp.where(head0, osq, 0.0), axis=-1, keepdims=True)
    ms1 = jnp.sum(jnp.where(head0, 0.0, osq), axis=-1, keepdims=True)
    ms = jnp.where(head0, ms0, ms1) * (1.0 / HEAD_DIM)
    y = o * lax.rsqrt(ms + EPS) * norm_g
    y = y * (graw * (1.0 / (1.0 + jnp.exp(-graw))))
    o_ref[...] = y.astype(o_ref.dtype)


def hgrn2(proj16, proj32, lb, norm_g, bsz, seq, q_col, i_col, g_col, f_col, rows_per_step):
    t = bsz * seq
    n_steps = seq // rows_per_step
    log_lb = jnp.log(lb).reshape(N_PAIRS, 1, LANES)
    log1m_lb = jnp.log1p(-lb).reshape(N_PAIRS, 1, LANES)
    ng = jnp.tile(norm_g.astype(F32), LANES // HEAD_DIM).reshape(1, LANES)

    def act(col):
        return pl.BlockSpec((rows_per_step, LANES), lambda b, p, s: (b * n_steps + s, col + p))

    def par():
        return pl.BlockSpec((None, 1, LANES), lambda b, p, s: (p, 0, 0))

    return pl.pallas_call(
        functools.partial(_hgrn_kernel, n_chunks=rows_per_step // CHUNK),
        out_shape=jax.ShapeDtypeStruct((t, WIDTH), BF16),
        grid=(bsz, N_PAIRS, n_steps),
        in_specs=[act(q_col), act(f_col), act(i_col), act(g_col), par(), par(),
                  pl.BlockSpec((1, LANES), lambda b, p, s: (0, 0))],
        out_specs=pl.BlockSpec((rows_per_step, LANES), lambda b, p, s: (b * n_steps + s, p)),
        scratch_shapes=[pltpu.VMEM((LANES, LANES), F32)],
        compiler_params=_cparams(("parallel", "parallel", "arbitrary")),
        name="hgrn2",
    )(proj16, proj32, proj16, proj16, log_lb, log1m_lb, ng)


def _mla_flash_kernel(qt_ref, k_ref, vt_ref, o_ref, acc_ref, m_ref, l_ref, s_ref, *, n_qt):
    c = C_QSCALE

    def scores(qi, j, e):
        qoff = pl.multiple_of(qi * C_TILE, C_TILE)
        koff = pl.multiple_of(j * C_TILE, C_TILE)
        k = k_ref[pl.ds(koff, C_TILE), e * C_SLOT:(e + 1) * C_SLOT]
        qt = qt_ref[e * C_SLOT:(e + 1) * C_SLOT, pl.ds(qoff, C_TILE)]
        return jnp.dot(k, qt, preferred_element_type=F32)

    def softmax_pv(j, e, s):
        off = pl.multiple_of(j * C_TILE, C_TILE)
        m_prev = m_ref[e]
        m_new = jnp.maximum(m_prev, jnp.max(s, axis=0, keepdims=True))
        p = jnp.exp2((s - m_new) * c).astype(BF16)
        alpha = jnp.exp2((m_prev - m_new) * c)
        vt = vt_ref[e * HEAD_DIM:(e + 1) * HEAD_DIM, pl.ds(off, C_TILE)]
        vt_aug = jnp.concatenate([vt, jnp.ones((ONES_ROWS, C_TILE), BF16)], axis=0)
        pv = jnp.dot(vt_aug, p, preferred_element_type=F32)
        l_ref[e] = alpha * l_ref[e] + pv[HEAD_DIM:HEAD_DIM + 1]
        acc_ref[e] = alpha * acc_ref[e] + pv[:HEAD_DIM]
        m_ref[e] = m_new

    kc = lax.broadcasted_iota(jnp.int32, (C_TILE, C_TILE), 0) // CHUNK
    qc = lax.broadcasted_iota(jnp.int32, (C_TILE, C_TILE), 1) // CHUNK

    for e in range(2):
        s_ref[e] = scores(0, 0, e)

    def q_body(qi, carry):
        m_ref[...] = jnp.full_like(m_ref, -jnp.inf)
        l_ref[...] = jnp.zeros_like(l_ref)
        acc_ref[...] = jnp.zeros_like(acc_ref)

        def body(j, carry2):
            for e in range(2):
                s = s_ref[e]
                s_ref[e] = scores(qi, j + 1, e)
                softmax_pv(j, e, s)
            return carry2

        lax.fori_loop(0, qi, body, 0)
        q_next = jnp.minimum(qi + 1, n_qt - 1)
        for e in range(2):
            s = jnp.where(kc <= qc, s_ref[e], NEG_BIG)
            s_ref[e] = scores(q_next, 0, e)
            softmax_pv(qi, e, s)

        o_t = jnp.concatenate([acc_ref[0] / l_ref[0], acc_ref[1] / l_ref[1]], axis=0)
        o_ref[pl.ds(pl.multiple_of(qi * C_TILE, C_TILE), C_TILE), :] = o_t.T.astype(o_ref.dtype)
        return carry

    lax.fori_loop(0, n_qt, q_body, 0)


def mla_flash(qt, k, vt, bsz, seq):
    t = bsz * seq
    return pl.pallas_call(
        functools.partial(_mla_flash_kernel, n_qt=seq // C_TILE),
        out_shape=jax.ShapeDtypeStruct((t, WIDTH), BF16),
        grid=(bsz, N_PAIRS),
        in_specs=[
            pl.BlockSpec((2 * C_SLOT, seq), lambda b, p: (p, b)),
            pl.BlockSpec((seq, 2 * C_SLOT), lambda b, p: (b, p)),
            pl.BlockSpec((LANES, seq), lambda b, p: (p, b)),
        ],
        out_specs=pl.BlockSpec((seq, LANES), lambda b, p: (b, p)),
        scratch_shapes=[pltpu.VMEM((2, HEAD_DIM, C_TILE), F32),
                        pltpu.VMEM((2, 1, C_TILE), F32),
                        pltpu.VMEM((2, 1, C_TILE), F32),
                        pltpu.VMEM((2, C_TILE, C_TILE), F32)],
        compiler_params=_cparams(("parallel", "parallel")),
        name="mla_flash",
    )(qt, k, vt)


def _merge_kernel(x_ref, ya_ref, yb_ref, yc_ref, g_ref, wbr_ref, wout_ref, o_ref):
    merged = None
    for n, y_ref in enumerate((ya_ref, yb_ref, yc_ref)):
        up = jnp.dot(y_ref[...], wbr_ref[n], preferred_element_type=F32)
        gl = g_ref[:, n * D_MODEL:(n + 1) * D_MODEL].astype(F32)
        term = (1.0 / (1.0 + jnp.exp(-gl))) * up
        merged = term if merged is None else merged + term
    o_ref[...] = x_ref[...] + jnp.dot(merged.astype(BF16), wout_ref[...], preferred_element_type=F32)


def merge_out(x, ya, yb, yc, proj16, gate_blk, wbr, wout, tm):
    t = x.shape[0]
    return pl.pallas_call(
        _merge_kernel,
        out_shape=jax.ShapeDtypeStruct((t, D_MODEL), F32),
        grid=(t // tm,),
        in_specs=[
            pl.BlockSpec((tm, D_MODEL), lambda i: (i, 0)),
            pl.BlockSpec((tm, WIDTH), lambda i: (i, 0)),
            pl.BlockSpec((tm, WIDTH), lambda i: (i, 0)),
            pl.BlockSpec((tm, WIDTH), lambda i: (i, 0)),
            pl.BlockSpec((tm, N_BRANCH * D_MODEL), lambda i: (i, gate_blk)),
            _resident(wbr),
            _resident(wout),
        ],
        out_specs=pl.BlockSpec((tm, D_MODEL), lambda i: (i, 0)),
        compiler_params=_cparams(("parallel",)),
        name="merge_out",
    )(x, ya, yb, yc, proj16, wbr, wout)


def _ffn_kernel(x_ref, g_ref, w1_ref, w2_ref, gf_ref, o_ref, *, final_norm, tf):
    x = x_ref[...]
    h = _rms(x, g_ref[...]).astype(BF16)
    acc = None
    for k in range(w1_ref.shape[1] // tf):
        u = jnp.maximum(jnp.dot(h, w1_ref[:, k * tf:(k + 1) * tf], preferred_element_type=F32), 0.0)
        part = jnp.dot((u * u).astype(BF16), w2_ref[k * tf:(k + 1) * tf, :], preferred_element_type=F32)
        acc = part if acc is None else acc + part
    y = x + acc
    if final_norm:
        y = _rms(y, gf_ref[...])
    o_ref[...] = y


def ffn(x, g, w1, w2, gf, final_norm, tm, tf):
    t, d = x.shape
    return pl.pallas_call(
        functools.partial(_ffn_kernel, final_norm=final_norm, tf=tf),
        out_shape=jax.ShapeDtypeStruct((t, d), F32),
        grid=(t // tm,),
        in_specs=[
            pl.BlockSpec((tm, d), lambda i: (i, 0)),
            pl.BlockSpec((1, d), lambda i: (0, 0)),
            _resident(w1),
            _resident(w2),
            pl.BlockSpec((1, d), lambda i: (0, 0)),
        ],
        out_specs=pl.BlockSpec((tm, d), lambda i: (i, 0)),
        compiler_params=_cparams(("parallel",)),
        name="ffn",
    )(x, g.reshape(1, d), w1, w2, gf.reshape(1, d))


COL_AQ, COL_AK, COL_AV = 0, 4, 8
COL_BQ, COL_BI, COL_BG = 12, 16, 20
GATE_BLK = 1


def _prep_w_in(w):
    o = 0
    parts = {}
    for name, size in (("aq", 512), ("ak", 512), ("av", 512), ("bq", 512), ("bf", 512), ("bi", 512),
                       ("bg", 512), ("cq", C_Q_RANK), ("ckv", C_KV_RANK), ("ckr", C_ROPE),
                       ("gate", N_BRANCH * D_MODEL)):
        parts[name] = w[:, o:o + size]
        o += size
    w16 = jnp.concatenate([parts[n] for n in ("aq", "ak", "av", "bq", "bi", "bg", "gate")], axis=1)
    half = C_ROPE // 2
    x1, x2 = parts["ckr"][:, :half], parts["ckr"][:, half:]
    zl = jnp.zeros((w.shape[0], C_NOPE), w.dtype)
    zr = jnp.zeros((w.shape[0], C_SLOT - C_NOPE - C_ROPE), w.dtype)
    kr_slot = jnp.concatenate([zl, x1, x2, zr], axis=1)
    kr_sw = jnp.concatenate([zl, -x2, x1, zr], axis=1)
    w32 = jnp.concatenate([parts["cq"], parts["ckv"], kr_slot, kr_sw, parts["bf"]], axis=1)
    return w16.astype(BF16), w32.astype(BF16)


def _prep_w_mla(w_uq, w_ukv):
    half = C_ROPE // 2
    per_q = C_NOPE + C_ROPE
    wq = w_uq.reshape(C_Q_RANK, N_HEADS, per_q)
    nope, x1, x2 = wq[..., :C_NOPE], wq[..., C_NOPE:C_NOPE + half], wq[..., C_NOPE + half:]
    zpad = jnp.zeros((C_Q_RANK, N_HEADS, C_SLOT - per_q), w_uq.dtype)
    wq1 = jnp.concatenate([nope, x1, x2, zpad], axis=-1).reshape(C_Q_RANK, N_HEADS * C_SLOT)
    wq2 = jnp.concatenate([jnp.zeros_like(nope), -x2, x1, zpad], axis=-1).reshape(C_Q_RANK, N_HEADS * C_SLOT)
    wkv = w_ukv.reshape(C_KV_RANK, N_HEADS, C_NOPE + HEAD_DIM)
    k_nope, v = wkv[..., :C_NOPE], wkv[..., C_NOPE:]
    wk = jnp.concatenate([k_nope, jnp.zeros((C_KV_RANK, N_HEADS, C_SLOT - C_NOPE), w_ukv.dtype)],
                         axis=-1).reshape(C_KV_RANK, N_HEADS * C_SLOT)
    wv = v.reshape(C_KV_RANK, WIDTH)
    return wq1.T.astype(BF16), wq2.T.astype(BF16), wk.astype(BF16), wv.T.astype(BF16)


def kernel(x, positions, norm_mix_g, w_in, rel_bias, hgrn_lb_logits, hgrn_norm_g, mla_q_norm_g,
           mla_kv_norm_g, mla_w_uq, mla_w_ukv, w_branch, w_out, norm_ffn_g, w_ff1, w_ff2, final_norm_g):
    bsz, seq, d = x.shape
    depth = w_in.shape[0]
    t = bsz * seq
    assert d == D_MODEL and seq % C_TILE == 0 and seq >= A_BAND
    tm = min(1024, t)
    hgrn_rows = min(512, seq)

    p_lb = jax.nn.softmax(hgrn_lb_logits.astype(F32), axis=0)
    lb_all = jnp.cumsum(p_lb, axis=0)
    lb_all = lb_all - lb_all[0:1]

    tabs = rope_tables(positions, tm)
    xf = x.reshape(t, d)
    for l in range(depth):
        w16, w32 = _prep_w_in(w_in[l])
        wq1t, wq2t, wk, wvt = _prep_w_mla(mla_w_uq[l], mla_w_ukv[l])
        proj16, b_f, qt_c, k_c, vt_c = in_proj(xf, norm_mix_g[l], w16, w32, tabs, mla_q_norm_g[l], mla_kv_norm_g[l],
                                               wq1t, wq2t, wk, wvt, min(512, t), 1024)

        y_a = band_attention(proj16, band_bias_table(rel_bias[l]), bsz, seq, COL_AQ, COL_AK, COL_AV)
        y_b = hgrn2(proj16, b_f, lb_all[l], hgrn_norm_g[l], bsz, seq, COL_BQ, COL_BI, COL_BG, 0, hgrn_rows)
        y_c = mla_flash(qt_c, k_c, vt_c, bsz, seq)

        xf = merge_out(xf, y_a, y_b, y_c, proj16, GATE_BLK, w_branch[l].astype(BF16),
                       w_out[l].astype(BF16), tm)
        xf = ffn(xf, norm_ffn_g[l], w_ff1[l].astype(BF16), w_ff2[l].astype(BF16), final_norm_g,
                 l == depth - 1, tm, 1024)
    return xf.reshape(bsz, seq, d)
```

```python
import functools

import jax
import jax.numpy as jnp
from jax import lax
from jax.experimental import pallas as pl
from jax.experimental.pallas import tpu as pltpu

F32 = jnp.float32
BF16 = jnp.bfloat16

D_MODEL = 1024
CHUNK = 64
EPS = 1e-6
N_HEADS = 8
HEAD_DIM = 64
LANES = 128
N_PAIRS = N_HEADS * HEAD_DIM // LANES
WIDTH = N_HEADS * HEAD_DIM

A_LEFT_CHUNKS = 8
A_MAX_REL = 128
A_QTILE = 2 * CHUNK
A_BAND = (A_LEFT_CHUNKS + 2) * CHUNK
A_PAD = A_LEFT_CHUNKS * CHUNK
A_TILES_PER_STEP = 8

C_Q_RANK = 256
C_KV_RANK = 128
C_ROPE = 32
C_NOPE = 64
C_SLOT = LANES
C_LATENT = C_Q_RANK + C_KV_RANK + 2 * C_SLOT
ROPE_BASE = 10000.0
C_TILE = 512
ONES_ROWS = 16
LOG2E = 1.4426950408889634
C_QSCALE = (C_NOPE + C_ROPE) ** -0.5 * LOG2E

D_FF = 4 * D_MODEL
N_BRANCH = 3

SUB = 8
N_SUB = CHUNK // SUB
NEG_BIG = -1e30

VMEM_LIMIT = 48 * 1024 * 1024
VMEM_LIMIT_IN_PROJ = 58 * 1024 * 1024
_NT = (((1,), (1,)), ((), ()))


def _cparams(sem, vmem_limit=VMEM_LIMIT):
    return pltpu.CompilerParams(dimension_semantics=sem, vmem_limit_bytes=vmem_limit)


def _resident(a):
    return pl.BlockSpec(a.shape, lambda *_: (0,) * a.ndim, pipeline_mode=pl.Buffered(1))


def _rms(x, g):
    return x * lax.rsqrt(jnp.mean(x * x, axis=-1, keepdims=True) + EPS) * g


def _in_proj_kernel(x_ref, g_ref, wa_ref, wb_ref, ct_ref, st_ref, ctt_ref, stt_ref, gq_ref, gkv_ref,
                    wq1t_ref, wq2t_ref, wk_ref, wvt_ref, loglb_ref, log1mlb_ref, ng_ref,
                    oa_ref, yb_ref, qt_ref, k_ref, vt_ref, state_ref, *, tn, tiles_per_seq):
    @pl.when(pl.program_id(0) % tiles_per_seq == 0)
    def _():
        state_ref[...] = jnp.zeros_like(state_ref)

    h = _rms(x_ref[...], g_ref[...]).astype(BF16)

    def proj_tile(j):
        cols = slice(j * tn, (j + 1) * tn)
        t = jnp.dot(h, wa_ref[:, cols], preferred_element_type=F32)
        oa_ref[:, cols] = t.astype(oa_ref.dtype)
        return t

    tq = proj_tile(COL_BQ * LANES // tn)
    tig = proj_tile(COL_BI * LANES // tn)
    pb = jnp.dot(h, wb_ref[...], preferred_element_type=F32)
    q_off = COL_BQ * LANES % tn
    i_off = COL_BI * LANES % tn
    g_off = COL_BG * LANES - (COL_BI * LANES // tn) * tn
    y_out = []
    gens = []
    for p in range(N_PAIRS):
        lanes = lambda off: slice(off + p * LANES, off + (p + 1) * LANES)
        gens.append(_hgrn_stages(pb[:, lanes(C_LATENT)], tq[:, lanes(q_off)], tig[:, lanes(i_off)],
                                 tig[:, lanes(g_off)], loglb_ref[p], log1mlb_ref[p], ng_ref[...],
                                 state_ref.at[p], y_out))

    qn = _rms(pb[:, :C_Q_RANK], gq_ref[...]).astype(BF16)
    kvn = _rms(pb[:, C_Q_RANK:C_Q_RANK + C_KV_RANK], gkv_ref[...]).astype(BF16)
    kr = pb[:, C_Q_RANK + C_KV_RANK:C_Q_RANK + C_KV_RANK + C_SLOT]
    kr_sw = pb[:, C_Q_RANK + C_KV_RANK + C_SLOT:C_LATENT]
    kr_rot = kr * ct_ref[...] + kr_sw * st_ref[...]

    def mla_q():
        a_t = lax.dot_general(wq1t_ref[...], qn, _NT, preferred_element_type=F32)
        b_t = lax.dot_general(wq2t_ref[...], qn, _NT, preferred_element_type=F32)
        ctt = ctt_ref[...]
        stt = stt_ref[...]
        for hd in range(N_HEADS):
            sl = slice(hd * C_SLOT, (hd + 1) * C_SLOT)
            qt_ref[sl, :] = (a_t[sl, :] * ctt + b_t[sl, :] * stt).astype(qt_ref.dtype)

    def mla_kv():
        kn = jnp.dot(kvn, wk_ref[...], preferred_element_type=F32)
        vt_ref[...] = lax.dot_general(wvt_ref[...], kvn, _NT, preferred_element_type=F32).astype(vt_ref.dtype)
        for hd in range(N_HEADS):
            sl = slice(hd * C_SLOT, (hd + 1) * C_SLOT)
            k_ref[:, sl] = (kn[:, sl] + kr_rot).astype(k_ref.dtype)

    done = {COL_BQ * LANES // tn, COL_BI * LANES // tn}
    big = [functools.partial(proj_tile, j) for j in range(wa_ref.shape[1] // tn) if j not in done] + [mla_q, mla_kv]
    assert len(big) == HGRN_STAGES
    for work in big:
        work()
        for gen in gens:
            next(gen)
    for p in range(N_PAIRS):
        yb_ref[:, p * LANES:(p + 1) * LANES] = y_out[p].astype(yb_ref.dtype)


def in_proj(x, g, wa, wb, tabs, gq, gkv, wq1t, wq2t, wk, wvt, lb, hgrn_norm_g, seq, tm, tn):
    t, d = x.shape
    na = wa.shape[1]
    ctab, stab, ctab_t, stab_t = tabs
    gq = gq.reshape(1, -1)
    gkv = gkv.reshape(1, -1)
    log_lb = jnp.log(lb).reshape(N_PAIRS, 1, LANES)
    log1m_lb = jnp.log1p(-lb).reshape(N_PAIRS, 1, LANES)
    ng = jnp.tile(hgrn_norm_g.astype(F32), LANES // HEAD_DIM).reshape(1, LANES)
    rows = lambda n: pl.BlockSpec((tm, n), lambda i: (i, 0))
    cols = lambda n: pl.BlockSpec((n, tm), lambda i: (0, i))
    return pl.pallas_call(
        functools.partial(_in_proj_kernel, tn=tn, tiles_per_seq=seq // tm),
        out_shape=(jax.ShapeDtypeStruct((t, na), BF16),
                   jax.ShapeDtypeStruct((t, WIDTH), BF16),
                   jax.ShapeDtypeStruct((N_HEADS * C_SLOT, t), BF16),
                   jax.ShapeDtypeStruct((t, N_HEADS * C_SLOT), BF16),
                   jax.ShapeDtypeStruct((WIDTH, t), BF16)),
        grid=(t // tm,),
        in_specs=[rows(d), pl.BlockSpec((1, d), lambda i: (0, 0)), _resident(wa), _resident(wb),
                  rows(LANES), rows(LANES), cols(LANES), cols(LANES),
                  _resident(gq), _resident(gkv), _resident(wq1t), _resident(wq2t), _resident(wk), _resident(wvt),
                  _resident(log_lb), _resident(log1m_lb), _resident(ng)],
        out_specs=(rows(na), rows(WIDTH), cols(N_HEADS * C_SLOT), rows(N_HEADS * C_SLOT), cols(WIDTH)),
        scratch_shapes=[pltpu.VMEM((N_PAIRS, LANES, LANES), F32)],
        compiler_params=_cparams(("arbitrary",), VMEM_LIMIT_IN_PROJ),
        name="in_proj",
    )(x, g.reshape(1, d), wa, wb, ctab, stab, ctab_t, stab_t, gq, gkv, wq1t, wq2t, wk, wvt, log_lb, log1m_lb, ng)


def _trig_kernel(pos_ref, invf_ref, c_ref, s_ref, ct_ref, st_ref):
    ang = pos_ref[...].astype(F32) * invf_ref[...]
    c = jnp.cos(ang)
    s = jnp.sin(ang)
    c_ref[...] = c
    s_ref[...] = s
    ct_ref[...] = c.T
    st_ref[...] = s.T


def rope_tables(positions, tm):
    t = positions.size
    inv_freq = ROPE_BASE ** (-jnp.arange(0, C_ROPE, 2, dtype=F32) / C_ROPE)
    half = C_ROPE // 2
    invf = jnp.zeros((LANES,), F32)
    invf = invf.at[C_NOPE:C_NOPE + half].set(inv_freq).at[C_NOPE + half:C_NOPE + C_ROPE].set(inv_freq)
    return pl.pallas_call(
        _trig_kernel,
        out_shape=(jax.ShapeDtypeStruct((t, LANES), F32),) * 2 + (jax.ShapeDtypeStruct((LANES, t), F32),) * 2,
        grid=(t // tm,),
        in_specs=[
            pl.BlockSpec((tm, 1), lambda i: (i, 0)),
            pl.BlockSpec((1, LANES), lambda i: (0, 0)),
        ],
        out_specs=(pl.BlockSpec((tm, LANES), lambda i: (i, 0)),) * 2
        + (pl.BlockSpec((LANES, tm), lambda i: (0, i)),) * 2,
        compiler_params=_cparams(("parallel",)),
        name="rope_tables",
    )(positions.reshape(t, 1), invf.reshape(1, LANES))


def _band_attn_kernel(q_ref, k_ref, v_ref, bias_ref, o_ref):
    lane = lax.broadcasted_iota(jnp.int32, (A_QTILE, LANES), 1)
    n_shift = A_PAD // A_QTILE
    starts, scores = [], []
    for t in range(A_TILES_PER_STEP):
        c2 = pl.program_id(2) * A_TILES_PER_STEP + t
        start = pl.multiple_of(jnp.maximum(c2 * A_QTILE - A_PAD, 0), A_QTILE)
        shift = jnp.minimum(c2, n_shift)
        kwin = k_ref[pl.ds(start, A_BAND), :]
        q = q_ref[t * A_QTILE:(t + 1) * A_QTILE, :].astype(F32) * (HEAD_DIM ** -0.5)
        qst = jnp.concatenate([jnp.where(lane < HEAD_DIM, q, 0.0), jnp.where(lane < HEAD_DIM, 0.0, q)],
                              axis=0).astype(BF16)
        s = lax.dot_general(qst, kwin, _NT, preferred_element_type=F32)
        starts.append(start)
        scores.append(s + bias_ref[shift, 0])
    for t in range(A_TILES_PER_STEP):
        s = scores[t]
        vwin = v_ref[pl.ds(starts[t], A_BAND), :]
        m = jnp.max(s, axis=-1, keepdims=True)
        p = jnp.exp(s - m)
        l = jnp.sum(p, axis=-1, keepdims=True)
        pv = jnp.dot(p.astype(BF16), vwin, preferred_element_type=F32) / l
        o_ref[t * A_QTILE:(t + 1) * A_QTILE, :] = jnp.where(lane < HEAD_DIM, pv[:A_QTILE], pv[A_QTILE:]).astype(o_ref.dtype)


def band_attention(proj, bias, bsz, seq, q_col, k_col, v_col):
    t = bsz * seq
    rows = A_QTILE * A_TILES_PER_STEP
    n_steps = seq // rows
    n_shift = A_PAD // A_QTILE
    return pl.pallas_call(
        _band_attn_kernel,
        out_shape=jax.ShapeDtypeStruct((t, WIDTH), BF16),
        grid=(bsz, N_PAIRS, n_steps),
        in_specs=[
            pl.BlockSpec((rows, LANES), lambda b, p, c: (b * n_steps + c, q_col + p)),
            pl.BlockSpec((seq, LANES), lambda b, p, c: (b, k_col + p)),
            pl.BlockSpec((seq, LANES), lambda b, p, c: (b, v_col + p)),
            pl.BlockSpec((n_shift + 1, 1, 2 * A_QTILE, A_BAND), lambda b, p, c: (0, p, 0, 0)),
        ],
        out_specs=pl.BlockSpec((rows, LANES), lambda b, p, c: (b * n_steps + c, p)),
        compiler_params=_cparams(("parallel", "parallel", "arbitrary")),
        name="band_attention",
    )(proj, proj, proj, bias.reshape(n_shift + 1, N_PAIRS, 2 * A_QTILE, A_BAND))


A_ROLL = 768


def _band_bias_kernel(r_ref, o_ref):
    d = pl.program_id(0) * A_QTILE
    prof = jnp.broadcast_to(r_ref[...], (A_QTILE, A_ROLL))
    b = pltpu.roll(prof, 0, 1, stride=1, stride_axis=0)[:, :A_BAND]
    i = lax.broadcasted_iota(jnp.int32, (A_QTILE, A_BAND), 0)
    j = lax.broadcasted_iota(jnp.int32, (A_QTILE, A_BAND), 1)
    gap = (d + i) // CHUNK - j // CHUNK
    o_ref[...] = jnp.where((gap >= 0) & (gap <= A_LEFT_CHUNKS), b, NEG_BIG)


def band_bias_table(rel_table):
    n_shift = A_PAD // A_QTILE
    heads = rel_table.shape[0]
    pad = A_ROLL + A_PAD
    ext = jnp.pad(rel_table.astype(F32)[:, ::-1], ((0, 0), (pad, pad)), mode="edge")

    def seg(d, u0, n):
        o = u0 - d + A_MAX_REL + pad
        return ext[:, o:o + n]

    n_neg = A_ROLL - (A_BAND + 1)
    prof = jnp.stack([jnp.concatenate([seg(s * A_QTILE, 0, A_BAND + 1), seg(s * A_QTILE, -n_neg, n_neg)], axis=1)
                      for s in range(n_shift + 1)])
    return pl.pallas_call(
        _band_bias_kernel,
        out_shape=jax.ShapeDtypeStruct((n_shift + 1, heads, A_QTILE, A_BAND), F32),
        grid=(n_shift + 1, heads),
        in_specs=[pl.BlockSpec((None, None, 1, A_ROLL), lambda s, h: (s, h, 0, 0))],
        out_specs=pl.BlockSpec((None, None, A_QTILE, A_BAND), lambda s, h: (s, h, 0, 0)),
        compiler_params=_cparams(("parallel", "parallel")),
        name="band_bias",
    )(prof.reshape(n_shift + 1, heads, 1, A_ROLL))


def _split3(x):
    hi = x.astype(BF16)
    r1 = x - hi.astype(F32)
    mid = r1.astype(BF16)
    lo = (r1 - mid.astype(F32)).astype(BF16)
    return hi, mid, lo


HGRN_STAGES = 6


def _hgrn_stages(z, qraw, v, graw, log_lb, log1m_lb, norm_g, st_ref, y_out):
    rows = z.shape[0]
    n_chunks = rows // CHUNK
    n_blk = rows // SUB

    head0 = lax.broadcasted_iota(jnp.int32, (rows, LANES), 1) < HEAD_DIM
    head0_c = lax.broadcasted_iota(jnp.int32, (CHUNK, LANES), 1) < HEAD_DIM
    r64 = lax.broadcasted_iota(jnp.int32, (CHUNK, CHUNK), 0)
    c64 = lax.broadcasted_iota(jnp.int32, (CHUNK, CHUNK), 1)
    tril = (c64 <= r64).astype(BF16)
    c64s = lax.broadcasted_iota(jnp.int32, (2 * CHUNK, CHUNK), 1)
    rl = lax.broadcasted_iota(jnp.int32, (LANES, LANES), 0)
    cl = lax.broadcasted_iota(jnp.int32, (LANES, LANES), 1)
    same_head = (rl // HEAD_DIM) == (cl // HEAD_DIM)
    bd = same_head.astype(BF16)
    ri = lax.broadcasted_iota(jnp.int32, (CHUNK, CHUNK * SUB), 0)
    ci = lax.broadcasted_iota(jnp.int32, (CHUNK, CHUNK * SUB), 1)
    pick = (ci // SUB == ri).astype(BF16)
    row_s = lax.broadcasted_iota(jnp.int32, (SUB, LANES), 0)

    def chunk(x, c):
        return x[c * CHUNK:(c + 1) * CHUNK]

    log_sig = jnp.minimum(z, 0.0) - jnp.log(1.0 + jnp.exp(-jnp.abs(z)))
    bterm = log1m_lb + log_sig
    log_f = jnp.maximum(log_lb, bterm) + jnp.log(1.0 + jnp.exp(-jnp.abs(log_lb - bterm)))
    log_k = bterm - z
    qs = qraw * (1.0 / (1.0 + jnp.exp(-qraw)))

    x3 = jnp.concatenate(_split3(log_f), axis=1)
    cum3 = [jnp.dot(tril, chunk(x3, c), preferred_element_type=F32) for c in range(n_chunks)]
    yield
    cum = jnp.concatenate([t[:, :LANES] + t[:, LANES:2 * LANES] + t[:, 2 * LANES:] for t in cum3], axis=0)

    zero_row = jnp.zeros((1, LANES), F32)
    c_end = [cum[SUB * b + SUB - 1:SUB * b + SUB, :] for b in range(n_blk)]
    c_start = [zero_row if b % N_SUB == 0 else c_end[b - 1] for b in range(n_blk)]
    c_last = [c_end[c * N_SUB + N_SUB - 1] for c in range(n_chunks)]

    def rows_of(blocks):
        return jnp.concatenate([jnp.broadcast_to(r, (SUB, LANES)) for r in blocks], axis=0)

    cstart_full = rows_of(c_start)
    cend_full = rows_of(c_end)
    clast_full = jnp.concatenate([jnp.broadcast_to(r, (CHUNK, LANES)) for r in c_last], axis=0)
    q1 = qs * jnp.exp(cum - cstart_full)
    lk = log_k - cum
    k2b = jnp.exp(cend_full + lk).astype(BF16)
    k_end = jnp.exp(clast_full + lk).astype(BF16)
    q_state = (qs * jnp.exp(cum)).astype(BF16)
    vb = v.astype(BF16)

    s_cross = []
    for j in range(N_SUB - 1):
        d_rows = []
        for b in range(n_blk):
            if b % N_SUB > j:
                d_rows.append(jnp.exp(c_start[b] - c_end[(b // N_SUB) * N_SUB + j]))
            else:
                d_rows.append(zero_row)
        qj = q1 * rows_of(d_rows)
        q0 = jnp.where(head0, qj, 0.0).astype(BF16)
        q1h = jnp.where(head0, 0.0, qj).astype(BF16)
        for c in range(n_chunks):
            qst = jnp.concatenate([chunk(q0, c), chunk(q1h, c)], axis=0)
            s_cross.append(lax.dot_general(qst, chunk(k2b, c), _NT, preferred_element_type=F32))
    yield
    o_cross = []
    for c in range(n_chunks):
        sc = None
        for j in range(N_SUB - 1):
            col_in_j = (c64s >= j * SUB) & (c64s < (j + 1) * SUB)
            s = jnp.where(col_in_j, s_cross[j * n_chunks + c], 0.0)
            sc = s if sc is None else sc + s
        o2 = jnp.dot(sc.astype(BF16), chunk(vb, c), preferred_element_type=F32)
        o_cross.append(jnp.where(head0_c, o2[:CHUNK], o2[CHUNK:]))

    yield
    a = cum - log_k
    w_rows = []
    for b in range(n_blk):
        a_b = a[SUB * b:SUB * (b + 1), :]
        for i in range(SUB):
            r = SUB * b + i
            arg = jnp.where(row_s <= i, cum[r:r + 1, :] - a_b, NEG_BIG)
            w_rows.append(jnp.exp(arg) * qs[r:r + 1, :])
    w_all = jnp.concatenate(w_rows, axis=0).astype(BF16)
    sb = jnp.dot(w_all, bd, preferred_element_type=F32)
    yield
    o_diag = []
    for c in range(n_chunks):
        v_rep = jnp.concatenate([v[SUB * (r // SUB):SUB * (r // SUB + 1), :]
                                 for r in range(c * CHUNK, (c + 1) * CHUNK)], axis=0)
        sb_c = sb[c * CHUNK * SUB:(c + 1) * CHUNK * SUB]
        o_diag.append(jnp.dot(pick, (sb_c * v_rep).astype(BF16), preferred_element_type=F32))

    yield
    upd = [jnp.where(same_head,
                     lax.dot_general(chunk(vb, c), chunk(k_end, c), (((0,), (0,)), ((), ())),
                                     preferred_element_type=F32), 0.0) for c in range(n_chunks)]
    st = st_ref[...]
    o_state = []
    for c in range(n_chunks):
        o_state.append(lax.dot_general(chunk(q_state, c), st.astype(BF16), _NT, preferred_element_type=F32))
        st = st * jnp.exp(c_last[c]) + upd[c]
    st_ref[...] = st

    o = jnp.concatenate([o_cross[c] + o_diag[c] + o_state[c] for c in range(n_chunks)], axis=0)
    osq = o * o
    ms0 = jnp.sum(jnp.where(head0, osq, 0.0), axis=-1, keepdims=True)
    ms1 = jnp.sum(jnp.where(head0, 0.0, osq), axis=-1, keepdims=True)
    ms = jnp.where(head0, ms0, ms1) * (1.0 / HEAD_DIM)
    y = o * lax.rsqrt(ms + EPS) * norm_g
    y_out.append(y * (graw * (1.0 / (1.0 + jnp.exp(-graw)))))
    yield


def _mla_flash_kernel(qt_ref, k_ref, vt_ref, o_ref, acc_ref, m_ref, l_ref, s_ref, *, n_qt):
    c = C_QSCALE

    def scores(qi, j, e):
        qoff = pl.multiple_of(qi * C_TILE, C_TILE)
        koff = pl.multiple_of(j * C_TILE, C_TILE)
        k = k_ref[pl.ds(koff, C_TILE), e * C_SLOT:(e + 1) * C_SLOT]
        qt = qt_ref[e * C_SLOT:(e + 1) * C_SLOT, pl.ds(qoff, C_TILE)]
        return jnp.dot(k, qt, preferred_element_type=F32)

    def softmax_pv(j, e, s):
        off = pl.multiple_of(j * C_TILE, C_TILE)
        m_prev = m_ref[e]
        m_new = jnp.maximum(m_prev, jnp.max(s, axis=0, keepdims=True))
        p = jnp.exp2((s - m_new) * c).astype(BF16)
        alpha = jnp.exp2((m_prev - m_new) * c)
        vt = vt_ref[e * HEAD_DIM:(e + 1) * HEAD_DIM, pl.ds(off, C_TILE)]
        vt_aug = jnp.concatenate([vt, jnp.ones((ONES_ROWS, C_TILE), BF16)], axis=0)
        pv = jnp.dot(vt_aug, p, preferred_element_type=F32)
        l_ref[e] = alpha * l_ref[e] + pv[HEAD_DIM:HEAD_DIM + 1]
        acc_ref[e] = alpha * acc_ref[e] + pv[:HEAD_DIM]
        m_ref[e] = m_new

    kc = lax.broadcasted_iota(jnp.int32, (C_TILE, C_TILE), 0) // CHUNK
    qc = lax.broadcasted_iota(jnp.int32, (C_TILE, C_TILE), 1) // CHUNK

    for e in range(2):
        s_ref[e] = scores(0, 0, e)

    def q_body(qi, carry):
        m_ref[...] = jnp.full_like(m_ref, -jnp.inf)
        l_ref[...] = jnp.zeros_like(l_ref)
        acc_ref[...] = jnp.zeros_like(acc_ref)

        def body(j, carry2):
            for e in range(2):
                s = s_ref[e]
                s_ref[e] = scores(qi, j + 1, e)
                softmax_pv(j, e, s)
            return carry2

        lax.fori_loop(0, qi, body, 0)
        q_next = jnp.minimum(qi + 1, n_qt - 1)
        for e in range(2):
            s = jnp.where(kc <= qc, s_ref[e], NEG_BIG)
            s_ref[e] = scores(q_next, 0, e)
            softmax_pv(qi, e, s)

        o_t = jnp.concatenate([acc_ref[0] / l_ref[0], acc_ref[1] / l_ref[1]], axis=0)
        o_ref[pl.ds(pl.multiple_of(qi * C_TILE, C_TILE), C_TILE), :] = o_t.T.astype(o_ref.dtype)
        return carry

    lax.fori_loop(0, n_qt, q_body, 0)


def mla_flash(qt, k, vt, bsz, seq):
    t = bsz * seq
    return pl.pallas_call(
        functools.partial(_mla_flash_kernel, n_qt=seq // C_TILE),
        out_shape=jax.ShapeDtypeStruct((t, WIDTH), BF16),
        grid=(bsz, N_PAIRS),
        in_specs=[
            pl.BlockSpec((2 * C_SLOT, seq), lambda b, p: (p, b)),
            pl.BlockSpec((seq, 2 * C_SLOT), lambda b, p: (b, p)),
            pl.BlockSpec((LANES, seq), lambda b, p: (p, b)),
        ],
        out_specs=pl.BlockSpec((seq, LANES), lambda b, p: (b, p)),
        scratch_shapes=[pltpu.VMEM((2, HEAD_DIM, C_TILE), F32),
                        pltpu.VMEM((2, 1, C_TILE), F32),
                        pltpu.VMEM((2, 1, C_TILE), F32),
                        pltpu.VMEM((2, C_TILE, C_TILE), F32)],
        compiler_params=_cparams(("parallel", "parallel")),
        name="mla_flash",
    )(qt, k, vt)


def _merge_kernel(x_ref, ya_ref, yb_ref, yc_ref, g_ref, wbr_ref, wout_ref, o_ref):
    merged = None
    for n, y_ref in enumerate((ya_ref, yb_ref, yc_ref)):
        up = jnp.dot(y_ref[...], wbr_ref[n], preferred_element_type=F32)
        gl = g_ref[:, n * D_MODEL:(n + 1) * D_MODEL].astype(F32)
        term = (1.0 / (1.0 + jnp.exp(-gl))) * up
        merged = term if merged is None else merged + term
    o_ref[...] = x_ref[...] + jnp.dot(merged.astype(BF16), wout_ref[...], preferred_element_type=F32)


def merge_out(x, ya, yb, yc, proj16, gate_blk, wbr, wout, tm):
    t = x.shape[0]
    return pl.pallas_call(
        _merge_kernel,
        out_shape=jax.ShapeDtypeStruct((t, D_MODEL), F32),
        grid=(t // tm,),
        in_specs=[
            pl.BlockSpec((tm, D_MODEL), lambda i: (i, 0)),
            pl.BlockSpec((tm, WIDTH), lambda i: (i, 0)),
            pl.BlockSpec((tm, WIDTH), lambda i: (i, 0)),
            pl.BlockSpec((tm, WIDTH), lambda i: (i, 0)),
            pl.BlockSpec((tm, N_BRANCH * D_MODEL), lambda i: (i, gate_blk)),
            _resident(wbr),
            _resident(wout),
        ],
        out_specs=pl.BlockSpec((tm, D_MODEL), lambda i: (i, 0)),
        compiler_params=_cparams(("parallel",)),
        name="merge_out",
    )(x, ya, yb, yc, proj16, wbr, wout)


def _ffn_kernel(x_ref, g_ref, w1_ref, w2_ref, gf_ref, o_ref, *, final_norm, tf):
    x = x_ref[...]
    h = _rms(x, g_ref[...]).astype(BF16)
    acc = None
    for k in range(w1_ref.shape[1] // tf):
        u = jnp.maximum(jnp.dot(h, w1_ref[:, k * tf:(k + 1) * tf], preferred_element_type=F32), 0.0)
        part = jnp.dot((u * u).astype(BF16), w2_ref[k * tf:(k + 1) * tf, :], preferred_element_type=F32)
        acc = part if acc is None else acc + part
    y = x + acc
    if final_norm:
        y = _rms(y, gf_ref[...])
    o_ref[...] = y


def ffn(x, g, w1, w2, gf, final_norm, tm, tf):
    t, d = x.shape
    return pl.pallas_call(
        functools.partial(_ffn_kernel, final_norm=final_norm, tf=tf),
        out_shape=jax.ShapeDtypeStruct((t, d), F32),
        grid=(t // tm,),
        in_specs=[
            pl.BlockSpec((tm, d), lambda i: (i, 0)),
            pl.BlockSpec((1, d), lambda i: (0, 0)),
            _resident(w1),
            _resident(w2),
            pl.BlockSpec((1, d), lambda i: (0, 0)),
        ],
        out_specs=pl.BlockSpec((tm, d), lambda i: (i, 0)),
        compiler_params=_cparams(("parallel",)),
        name="ffn",
    )(x, g.reshape(1, d), w1, w2, gf.reshape(1, d))


COL_AQ, COL_AK, COL_AV = 0, 4, 8
COL_BQ, COL_BI, COL_BG = 12, 16, 20
GATE_BLK = 1


def _prep_w_in(w):
    o = 0
    parts = {}
    for name, size in (("aq", 512), ("ak", 512), ("av", 512), ("bq", 512), ("bf", 512), ("bi", 512),
                       ("bg", 512), ("cq", C_Q_RANK), ("ckv", C_KV_RANK), ("ckr", C_ROPE),
                       ("gate", N_BRANCH * D_MODEL)):
        parts[name] = w[:, o:o + size]
        o += size
    w16 = jnp.concatenate([parts[n] for n in ("aq", "ak", "av", "bq", "bi", "bg", "gate")], axis=1)
    half = C_ROPE // 2
    x1, x2 = parts["ckr"][:, :half], parts["ckr"][:, half:]
    zl = jnp.zeros((w.shape[0], C_NOPE), w.dtype)
    zr = jnp.zeros((w.shape[0], C_SLOT - C_NOPE - C_ROPE), w.dtype)
    kr_slot = jnp.concatenate([zl, x1, x2, zr], axis=1)
    kr_sw = jnp.concatenate([zl, -x2, x1, zr], axis=1)
    w32 = jnp.concatenate([parts["cq"], parts["ckv"], kr_slot, kr_sw, parts["bf"]], axis=1)
    return w16.astype(BF16), w32.astype(BF16)


def _prep_w_mla(w_uq, w_ukv):
    half = C_ROPE // 2
    per_q = C_NOPE + C_ROPE
    wq = w_uq.reshape(C_Q_RANK, N_HEADS, per_q)
    nope, x1, x2 = wq[..., :C_NOPE], wq[..., C_NOPE:C_NOPE + half], wq[..., C_NOPE + half:]
    zpad = jnp.zeros((C_Q_RANK, N_HEADS, C_SLOT - per_q), w_uq.dtype)
    wq1 = jnp.concatenate([nope, x1, x2, zpad], axis=-1).reshape(C_Q_RANK, N_HEADS * C_SLOT)
    wq2 = jnp.concatenate([jnp.zeros_like(nope), -x2, x1, zpad], axis=-1).reshape(C_Q_RANK, N_HEADS * C_SLOT)
    wkv = w_ukv.reshape(C_KV_RANK, N_HEADS, C_NOPE + HEAD_DIM)
    k_nope, v = wkv[..., :C_NOPE], wkv[..., C_NOPE:]
    wk = jnp.concatenate([k_nope, jnp.zeros((C_KV_RANK, N_HEADS, C_SLOT - C_NOPE), w_ukv.dtype)],
                         axis=-1).reshape(C_KV_RANK, N_HEADS * C_SLOT)
    wv = v.reshape(C_KV_RANK, WIDTH)
    return wq1.T.astype(BF16), wq2.T.astype(BF16), wk.astype(BF16), wv.T.astype(BF16)


def kernel(x, positions, norm_mix_g, w_in, rel_bias, hgrn_lb_logits, hgrn_norm_g, mla_q_norm_g,
           mla_kv_norm_g, mla_w_uq, mla_w_ukv, w_branch, w_out, norm_ffn_g, w_ff1, w_ff2, final_norm_g):
    bsz, seq, d = x.shape
    depth = w_in.shape[0]
    t = bsz * seq
    assert d == D_MODEL and seq % C_TILE == 0 and seq >= A_BAND
    tm = min(1024, t)

    p_lb = jax.nn.softmax(hgrn_lb_logits.astype(F32), axis=0)
    lb_all = jnp.cumsum(p_lb, axis=0)
    lb_all = lb_all - lb_all[0:1]

    tabs = rope_tables(positions, tm)
    xf = x.reshape(t, d)
    for l in range(depth):
        w16, w32 = _prep_w_in(w_in[l])
        wq1t, wq2t, wk, wvt = _prep_w_mla(mla_w_uq[l], mla_w_ukv[l])
        proj16, y_b, qt_c, k_c, vt_c = in_proj(xf, norm_mix_g[l], w16, w32, tabs, mla_q_norm_g[l], mla_kv_norm_g[l],
                                               wq1t, wq2t, wk, wvt, lb_all[l], hgrn_norm_g[l], seq, min(512, seq), 1024)

        y_a = band_attention(proj16, band_bias_table(rel_bias[l]), bsz, seq, COL_AQ, COL_AK, COL_AV)
        y_c = mla_flash(qt_c, k_c, vt_c, bsz, seq)

        xf = merge_out(xf, y_a, y_b, y_c, proj16, GATE_BLK, w_branch[l].astype(BF16),
                       w_out[l].astype(BF16), tm)
        xf = ffn(xf, norm_ffn_g[l], w_ff1[l].astype(BF16), w_ff2[l].astype(BF16), final_norm_g,
                 l == depth - 1, tm, 1024)
    return xf.reshape(bsz, seq, d)
```

```python
import functools

import jax
import jax.numpy as jnp
from jax import lax
from jax.experimental import pallas as pl
from jax.experimental.pallas import tpu as pltpu

F32 = jnp.float32
BF16 = jnp.bfloat16

D_MODEL = 1024
CHUNK = 64
EPS = 1e-6
N_HEADS = 8
HEAD_DIM = 64
LANES = 128
N_PAIRS = N_HEADS * HEAD_DIM // LANES
WIDTH = N_HEADS * HEAD_DIM

A_LEFT_CHUNKS = 8
A_MAX_REL = 128
A_QTILE = 2 * CHUNK
A_BAND = (A_LEFT_CHUNKS + 2) * CHUNK
A_PAD = A_LEFT_CHUNKS * CHUNK
A_TILES_PER_STEP = 8

C_Q_RANK = 256
C_KV_RANK = 128
C_ROPE = 32
C_NOPE = 64
C_SLOT = LANES
C_LATENT = C_Q_RANK + C_KV_RANK + 2 * C_SLOT
ROPE_BASE = 10000.0
C_TILE = 512
ONES_ROWS = 16
LOG2E = 1.4426950408889634
C_QSCALE = (C_NOPE + C_ROPE) ** -0.5 * LOG2E

D_FF = 4 * D_MODEL
N_BRANCH = 3

SUB = 8
N_SUB = CHUNK // SUB
NEG_BIG = -1e30

VMEM_LIMIT = 48 * 1024 * 1024
VMEM_LIMIT_IN_PROJ = 58 * 1024 * 1024
_NT = (((1,), (1,)), ((), ()))


def _cparams(sem, vmem_limit=VMEM_LIMIT):
    return pltpu.CompilerParams(dimension_semantics=sem, vmem_limit_bytes=vmem_limit)


def _resident(a):
    return pl.BlockSpec(a.shape, lambda *_: (0,) * a.ndim, pipeline_mode=pl.Buffered(1))


def _rms(x, g):
    return x * lax.rsqrt(jnp.mean(x * x, axis=-1, keepdims=True) + EPS) * g


def _in_proj_kernel(x_ref, g_ref, wa_ref, wb_ref, ct_ref, st_ref, ctt_ref, stt_ref, gq_ref, gkv_ref,
                    wq1t_ref, wq2t_ref, wk_ref, wvt_ref, loglb_ref, log1mlb_ref, ng_ref,
                    oa_ref, yb_ref, qt_ref, k_ref, vt_ref, state_ref, *, tn, tiles_per_seq):
    @pl.when(pl.program_id(0) % tiles_per_seq == 0)
    def _():
        state_ref[...] = jnp.zeros_like(state_ref)

    h = _rms(x_ref[...], g_ref[...]).astype(BF16)

    def proj_tile(j):
        cols = slice(j * tn, (j + 1) * tn)
        t = jnp.dot(h, wa_ref[:, cols], preferred_element_type=F32)
        oa_ref[:, cols] = t.astype(oa_ref.dtype)
        return t

    tq = proj_tile(COL_BQ * LANES // tn)
    tig = proj_tile(COL_BI * LANES // tn)
    pb = jnp.dot(h, wb_ref[...], preferred_element_type=F32)
    q_off = COL_BQ * LANES % tn
    i_off = COL_BI * LANES % tn
    g_off = COL_BG * LANES - (COL_BI * LANES // tn) * tn
    y_out = []
    gens = []
    gw = HGRN_GROUP * LANES
    for p in range(N_PAIRS // HGRN_GROUP):
        lanes = lambda off: slice(off + p * gw, off + (p + 1) * gw)
        gens.append(_hgrn_stages(pb[:, lanes(C_LATENT)], tq[:, lanes(q_off)], tig[:, lanes(i_off)],
                                 tig[:, lanes(g_off)], loglb_ref[p], log1mlb_ref[p], ng_ref[...],
                                 state_ref.at[p], y_out))

    qn = _rms(pb[:, :C_Q_RANK], gq_ref[...]).astype(BF16)
    kvn = _rms(pb[:, C_Q_RANK:C_Q_RANK + C_KV_RANK], gkv_ref[...]).astype(BF16)
    kr = pb[:, C_Q_RANK + C_KV_RANK:C_Q_RANK + C_KV_RANK + C_SLOT]
    kr_sw = pb[:, C_Q_RANK + C_KV_RANK + C_SLOT:C_LATENT]
    kr_rot = kr * ct_ref[...] + kr_sw * st_ref[...]

    def mla_q():
        a_t = lax.dot_general(wq1t_ref[...], qn, _NT, preferred_element_type=F32)
        b_t = lax.dot_general(wq2t_ref[...], qn, _NT, preferred_element_type=F32)
        ctt = ctt_ref[...]
        stt = stt_ref[...]
        for hd in range(N_HEADS):
            sl = slice(hd * C_SLOT, (hd + 1) * C_SLOT)
            qt_ref[sl, :] = (a_t[sl, :] * ctt + b_t[sl, :] * stt).astype(qt_ref.dtype)

    def mla_kv():
        kn = jnp.dot(kvn, wk_ref[...], preferred_element_type=F32)
        vt_ref[...] = lax.dot_general(wvt_ref[...], kvn, _NT, preferred_element_type=F32).astype(vt_ref.dtype)
        for hd in range(N_HEADS):
            sl = slice(hd * C_SLOT, (hd + 1) * C_SLOT)
            k_ref[:, sl] = (kn[:, sl] + kr_rot).astype(k_ref.dtype)

    done = {COL_BQ * LANES // tn, COL_BI * LANES // tn}
    big = [functools.partial(proj_tile, j) for j in range(wa_ref.shape[1] // tn) if j not in done] + [mla_q, mla_kv]
    assert len(big) == HGRN_STAGES
    for work in big:
        work()
        for gen in gens:
            next(gen)
    yb_ref[...] = jnp.concatenate(y_out, axis=1).astype(yb_ref.dtype)


def in_proj(x, g, wa, wb, tabs, gq, gkv, wq1t, wq2t, wk, wvt, lb, hgrn_norm_g, seq, tm, tn):
    t, d = x.shape
    na = wa.shape[1]
    ctab, stab, ctab_t, stab_t = tabs
    gq = gq.reshape(1, -1)
    gkv = gkv.reshape(1, -1)
    gw = HGRN_GROUP * LANES
    log_lb = jnp.log(lb).reshape(-1, 1, gw)
    log1m_lb = jnp.log1p(-lb).reshape(-1, 1, gw)
    ng = jnp.tile(hgrn_norm_g.astype(F32), gw // HEAD_DIM).reshape(1, gw)
    rows = lambda n: pl.BlockSpec((tm, n), lambda i: (i, 0))
    cols = lambda n: pl.BlockSpec((n, tm), lambda i: (0, i))
    return pl.pallas_call(
        functools.partial(_in_proj_kernel, tn=tn, tiles_per_seq=seq // tm),
        out_shape=(jax.ShapeDtypeStruct((t, na), BF16),
                   jax.ShapeDtypeStruct((t, WIDTH), BF16),
                   jax.ShapeDtypeStruct((N_HEADS * C_SLOT, t), BF16),
                   jax.ShapeDtypeStruct((t, N_HEADS * C_SLOT), BF16),
                   jax.ShapeDtypeStruct((WIDTH, t), BF16)),
        grid=(t // tm,),
        in_specs=[rows(d), pl.BlockSpec((1, d), lambda i: (0, 0)), _resident(wa), _resident(wb),
                  rows(LANES), rows(LANES), cols(LANES), cols(LANES),
                  _resident(gq), _resident(gkv), _resident(wq1t), _resident(wq2t), _resident(wk), _resident(wvt),
                  _resident(log_lb), _resident(log1m_lb), _resident(ng)],
        out_specs=(rows(na), rows(WIDTH), cols(N_HEADS * C_SLOT), rows(N_HEADS * C_SLOT), cols(WIDTH)),
        scratch_shapes=[pltpu.VMEM((N_PAIRS // HGRN_GROUP, gw, gw), F32)],
        compiler_params=_cparams(("arbitrary",), VMEM_LIMIT_IN_PROJ),
        name="in_proj",
    )(x, g.reshape(1, d), wa, wb, ctab, stab, ctab_t, stab_t, gq, gkv, wq1t, wq2t, wk, wvt, log_lb, log1m_lb, ng)


def _trig_kernel(pos_ref, invf_ref, c_ref, s_ref, ct_ref, st_ref):
    ang = pos_ref[...].astype(F32) * invf_ref[...]
    c = jnp.cos(ang)
    s = jnp.sin(ang)
    c_ref[...] = c
    s_ref[...] = s
    ct_ref[...] = c.T
    st_ref[...] = s.T


def rope_tables(positions, tm):
    t = positions.size
    inv_freq = ROPE_BASE ** (-jnp.arange(0, C_ROPE, 2, dtype=F32) / C_ROPE)
    half = C_ROPE // 2
    invf = jnp.zeros((LANES,), F32)
    invf = invf.at[C_NOPE:C_NOPE + half].set(inv_freq).at[C_NOPE + half:C_NOPE + C_ROPE].set(inv_freq)
    return pl.pallas_call(
        _trig_kernel,
        out_shape=(jax.ShapeDtypeStruct((t, LANES), F32),) * 2 + (jax.ShapeDtypeStruct((LANES, t), F32),) * 2,
        grid=(t // tm,),
        in_specs=[
            pl.BlockSpec((tm, 1), lambda i: (i, 0)),
            pl.BlockSpec((1, LANES), lambda i: (0, 0)),
        ],
        out_specs=(pl.BlockSpec((tm, LANES), lambda i: (i, 0)),) * 2
        + (pl.BlockSpec((LANES, tm), lambda i: (0, i)),) * 2,
        compiler_params=_cparams(("parallel",)),
        name="rope_tables",
    )(positions.reshape(t, 1), invf.reshape(1, LANES))


def _band_attn_kernel(q_ref, k_ref, v_ref, bias_ref, o_ref):
    lane = lax.broadcasted_iota(jnp.int32, (A_QTILE, LANES), 1)
    n_shift = A_PAD // A_QTILE
    starts, scores = [], []
    for t in range(A_TILES_PER_STEP):
        c2 = pl.program_id(2) * A_TILES_PER_STEP + t
        start = pl.multiple_of(jnp.maximum(c2 * A_QTILE - A_PAD, 0), A_QTILE)
        shift = jnp.minimum(c2, n_shift)
        kwin = k_ref[pl.ds(start, A_BAND), :]
        q = q_ref[t * A_QTILE:(t + 1) * A_QTILE, :].astype(F32) * (HEAD_DIM ** -0.5)
        qst = jnp.concatenate([jnp.where(lane < HEAD_DIM, q, 0.0), jnp.where(lane < HEAD_DIM, 0.0, q)],
                              axis=0).astype(BF16)
        s = lax.dot_general(qst, kwin, _NT, preferred_element_type=F32)
        starts.append(start)
        scores.append(s + bias_ref[shift, 0])
    for t in range(A_TILES_PER_STEP):
        s = scores[t]
        vwin = v_ref[pl.ds(starts[t], A_BAND), :]
        m = jnp.max(s, axis=-1, keepdims=True)
        p = jnp.exp(s - m)
        l = jnp.sum(p, axis=-1, keepdims=True)
        pv = jnp.dot(p.astype(BF16), vwin, preferred_element_type=F32) / l
        o_ref[t * A_QTILE:(t + 1) * A_QTILE, :] = jnp.where(lane < HEAD_DIM, pv[:A_QTILE], pv[A_QTILE:]).astype(o_ref.dtype)


def band_attention(proj, bias, bsz, seq, q_col, k_col, v_col):
    t = bsz * seq
    rows = A_QTILE * A_TILES_PER_STEP
    n_steps = seq // rows
    n_shift = A_PAD // A_QTILE
    return pl.pallas_call(
        _band_attn_kernel,
        out_shape=jax.ShapeDtypeStruct((t, WIDTH), BF16),
        grid=(bsz, N_PAIRS, n_steps),
        in_specs=[
            pl.BlockSpec((rows, LANES), lambda b, p, c: (b * n_steps + c, q_col + p)),
            pl.BlockSpec((seq, LANES), lambda b, p, c: (b, k_col + p)),
            pl.BlockSpec((seq, LANES), lambda b, p, c: (b, v_col + p)),
            pl.BlockSpec((n_shift + 1, 1, 2 * A_QTILE, A_BAND), lambda b, p, c: (0, p, 0, 0)),
        ],
        out_specs=pl.BlockSpec((rows, LANES), lambda b, p, c: (b * n_steps + c, p)),
        compiler_params=_cparams(("parallel", "parallel", "arbitrary")),
        name="band_attention",
    )(proj, proj, proj, bias.reshape(n_shift + 1, N_PAIRS, 2 * A_QTILE, A_BAND))


A_ROLL = 768


def _band_bias_kernel(r_ref, o_ref):
    d = pl.program_id(0) * A_QTILE
    prof = jnp.broadcast_to(r_ref[...], (A_QTILE, A_ROLL))
    b = pltpu.roll(prof, 0, 1, stride=1, stride_axis=0)[:, :A_BAND]
    i = lax.broadcasted_iota(jnp.int32, (A_QTILE, A_BAND), 0)
    j = lax.broadcasted_iota(jnp.int32, (A_QTILE, A_BAND), 1)
    gap = (d + i) // CHUNK - j // CHUNK
    o_ref[...] = jnp.where((gap >= 0) & (gap <= A_LEFT_CHUNKS), b, NEG_BIG)


def band_bias_table(rel_table):
    n_shift = A_PAD // A_QTILE
    heads = rel_table.shape[0]
    pad = A_ROLL + A_PAD
    ext = jnp.pad(rel_table.astype(F32)[:, ::-1], ((0, 0), (pad, pad)), mode="edge")

    def seg(d, u0, n):
        o = u0 - d + A_MAX_REL + pad
        return ext[:, o:o + n]

    n_neg = A_ROLL - (A_BAND + 1)
    prof = jnp.stack([jnp.concatenate([seg(s * A_QTILE, 0, A_BAND + 1), seg(s * A_QTILE, -n_neg, n_neg)], axis=1)
                      for s in range(n_shift + 1)])
    return pl.pallas_call(
        _band_bias_kernel,
        out_shape=jax.ShapeDtypeStruct((n_shift + 1, heads, A_QTILE, A_BAND), F32),
        grid=(n_shift + 1, heads),
        in_specs=[pl.BlockSpec((None, None, 1, A_ROLL), lambda s, h: (s, h, 0, 0))],
        out_specs=pl.BlockSpec((None, None, A_QTILE, A_BAND), lambda s, h: (s, h, 0, 0)),
        compiler_params=_cparams(("parallel", "parallel")),
        name="band_bias",
    )(prof.reshape(n_shift + 1, heads, 1, A_ROLL))


def _split3(x):
    hi = x.astype(BF16)
    r1 = x - hi.astype(F32)
    mid = r1.astype(BF16)
    lo = (r1 - mid.astype(F32)).astype(BF16)
    return hi, mid, lo


HGRN_STAGES = 6
HGRN_GROUP = 2


def _hgrn_stages(z, qraw, v, graw, log_lb, log1m_lb, norm_g, st_ref, y_out):
    rows, w = z.shape
    n_pairs = w // LANES
    n_chunks = rows // CHUNK
    n_blk = rows // SUB

    lane = lax.broadcasted_iota(jnp.int32, (rows, w), 1)
    head0 = (lane // HEAD_DIM) % 2 == 0
    head0_c = (lax.broadcasted_iota(jnp.int32, (CHUNK, w), 1) // HEAD_DIM) % 2 == 0
    r64 = lax.broadcasted_iota(jnp.int32, (CHUNK, CHUNK), 0)
    c64 = lax.broadcasted_iota(jnp.int32, (CHUNK, CHUNK), 1)
    tril = (c64 <= r64).astype(BF16)
    c64s = lax.broadcasted_iota(jnp.int32, (2 * CHUNK, n_pairs * CHUNK), 1) % CHUNK
    rl = lax.broadcasted_iota(jnp.int32, (w, w), 0)
    cl = lax.broadcasted_iota(jnp.int32, (w, w), 1)
    same_head = (rl // HEAD_DIM) == (cl // HEAD_DIM)
    bd = same_head.astype(BF16)
    ri = lax.broadcasted_iota(jnp.int32, (CHUNK, CHUNK * SUB), 0)
    ci = lax.broadcasted_iota(jnp.int32, (CHUNK, CHUNK * SUB), 1)
    pick = (ci // SUB == ri).astype(BF16)
    row_s = lax.broadcasted_iota(jnp.int32, (SUB, w), 0)

    def chunk(x, c):
        return x[c * CHUNK:(c + 1) * CHUNK]

    log_sig = jnp.minimum(z, 0.0) - jnp.log(1.0 + jnp.exp(-jnp.abs(z)))
    bterm = log1m_lb + log_sig
    log_f = jnp.maximum(log_lb, bterm) + jnp.log(1.0 + jnp.exp(-jnp.abs(log_lb - bterm)))
    log_k = bterm - z
    qs = qraw * (1.0 / (1.0 + jnp.exp(-qraw)))

    x3 = jnp.concatenate(_split3(log_f), axis=1)
    cum3 = [jnp.dot(tril, chunk(x3, c), preferred_element_type=F32) for c in range(n_chunks)]
    yield
    cum = jnp.concatenate([t[:, :w] + t[:, w:2 * w] + t[:, 2 * w:] for t in cum3], axis=0)

    zero_row = jnp.zeros((1, w), F32)
    c_end = [cum[SUB * b + SUB - 1:SUB * b + SUB, :] for b in range(n_blk)]
    c_start = [zero_row if b % N_SUB == 0 else c_end[b - 1] for b in range(n_blk)]
    c_last = [c_end[c * N_SUB + N_SUB - 1] for c in range(n_chunks)]

    def rows_of(blocks):
        return jnp.concatenate([jnp.broadcast_to(r, (SUB, w)) for r in blocks], axis=0)

    cstart_full = rows_of(c_start)
    cend_full = rows_of(c_end)
    clast_full = jnp.concatenate([jnp.broadcast_to(r, (CHUNK, w)) for r in c_last], axis=0)
    q1 = qs * jnp.exp(cum - cstart_full)
    lk = log_k - cum
    k2b = jnp.exp(cend_full + lk).astype(BF16)
    k_end = jnp.exp(clast_full + lk).astype(BF16)
    q_state = (qs * jnp.exp(cum)).astype(BF16)
    vb = v.astype(BF16)

    def pair_diag(x):
        zero = jnp.zeros((CHUNK, LANES), x.dtype)
        return jnp.concatenate(
            [jnp.concatenate([x[:, q * LANES:(q + 1) * LANES] if q == p else zero for q in range(n_pairs)], axis=1)
             for p in range(n_pairs)], axis=0)

    q_parts = []
    for j in range(N_SUB - 1):
        d_rows = []
        for b in range(n_blk):
            if b % N_SUB > j:
                d_rows.append(jnp.exp(c_start[b] - c_end[(b // N_SUB) * N_SUB + j]))
            else:
                d_rows.append(zero_row)
        qj = q1 * rows_of(d_rows)
        q_parts.append((jnp.where(head0, qj, 0.0).astype(BF16), jnp.where(head0, 0.0, qj).astype(BF16)))
    s_cross = []
    for c in range(n_chunks):
        qst = jnp.concatenate([chunk(qh, c) for pair_q in q_parts for qh in pair_q], axis=0)
        s_cross.append(lax.dot_general(qst, pair_diag(chunk(k2b, c)), _NT, preferred_element_type=F32))
    yield
    o_cross = []
    for c in range(n_chunks):
        sc = None
        for j in range(N_SUB - 1):
            col_in_j = (c64s >= j * SUB) & (c64s < (j + 1) * SUB)
            s = jnp.where(col_in_j, s_cross[c][2 * j * CHUNK:2 * (j + 1) * CHUNK], 0.0)
            sc = s if sc is None else sc + s
        o2 = jnp.dot(sc.astype(BF16), pair_diag(chunk(vb, c)), preferred_element_type=F32)
        o_cross.append(jnp.where(head0_c, o2[:CHUNK], o2[CHUNK:]))

    yield
    a = cum - log_k
    w_rows = []
    for b in range(n_blk):
        a_b = a[SUB * b:SUB * (b + 1), :]
        for i in range(SUB):
            r = SUB * b + i
            arg = jnp.where(row_s <= i, cum[r:r + 1, :] - a_b, NEG_BIG)
            w_rows.append(jnp.exp(arg) * qs[r:r + 1, :])
    w_all = jnp.concatenate(w_rows, axis=0).astype(BF16)
    sb = jnp.dot(w_all, bd, preferred_element_type=F32)
    yield
    o_diag = []
    for c in range(n_chunks):
        v_rep = jnp.concatenate([v[SUB * (r // SUB):SUB * (r // SUB + 1), :]
                                 for r in range(c * CHUNK, (c + 1) * CHUNK)], axis=0)
        sb_c = sb[c * CHUNK * SUB:(c + 1) * CHUNK * SUB]
        o_diag.append(jnp.dot(pick, (sb_c * v_rep).astype(BF16), preferred_element_type=F32))

    yield
    upd = [jnp.where(same_head,
                     lax.dot_general(chunk(vb, c), chunk(k_end, c), (((0,), (0,)), ((), ())),
                                     preferred_element_type=F32), 0.0) for c in range(n_chunks)]
    st = st_ref[...]
    o_state = []
    for c in range(n_chunks):
        o_state.append(lax.dot_general(chunk(q_state, c), st.astype(BF16), _NT, preferred_element_type=F32))
        st = st * jnp.exp(c_last[c]) + upd[c]
    st_ref[...] = st

    o = jnp.concatenate([o_cross[c] + o_diag[c] + o_state[c] for c in range(n_chunks)], axis=0)
    osq = o * o
    ms = jnp.zeros_like(o)
    for hd in range(w // HEAD_DIM):
        in_head = lane // HEAD_DIM == hd
        ms = jnp.where(in_head, jnp.sum(jnp.where(in_head, osq, 0.0), axis=-1, keepdims=True), ms)
    ms = ms * (1.0 / HEAD_DIM)
    y = o * lax.rsqrt(ms + EPS) * norm_g
    y_out.append(y * (graw * (1.0 / (1.0 + jnp.exp(-graw)))))
    yield


def _mla_flash_kernel(qt_ref, k_ref, vt_ref, o_ref, acc_ref, m_ref, l_ref, s_ref, *, n_qt):
    c = C_QSCALE

    def scores(qi, j, e):
        qoff = pl.multiple_of(qi * C_TILE, C_TILE)
        koff = pl.multiple_of(j * C_TILE, C_TILE)
        k = k_ref[pl.ds(koff, C_TILE), e * C_SLOT:(e + 1) * C_SLOT]
        qt = qt_ref[e * C_SLOT:(e + 1) * C_SLOT, pl.ds(qoff, C_TILE)]
        return jnp.dot(k, qt, preferred_element_type=F32)

    def softmax_pv(j, e, s):
        off = pl.multiple_of(j * C_TILE, C_TILE)
        m_prev = m_ref[e]
        m_new = jnp.maximum(m_prev, jnp.max(s, axis=0, keepdims=True))
        p = jnp.exp2((s - m_new) * c).astype(BF16)
        alpha = jnp.exp2((m_prev - m_new) * c)
        vt = vt_ref[e * HEAD_DIM:(e + 1) * HEAD_DIM, pl.ds(off, C_TILE)]
        vt_aug = jnp.concatenate([vt, jnp.ones((ONES_ROWS, C_TILE), BF16)], axis=0)
        pv = jnp.dot(vt_aug, p, preferred_element_type=F32)
        l_ref[e] = alpha * l_ref[e] + pv[HEAD_DIM:HEAD_DIM + 1]
        acc_ref[e] = alpha * acc_ref[e] + pv[:HEAD_DIM]
        m_ref[e] = m_new

    kc = lax.broadcasted_iota(jnp.int32, (C_TILE, C_TILE), 0) // CHUNK
    qc = lax.broadcasted_iota(jnp.int32, (C_TILE, C_TILE), 1) // CHUNK

    for e in range(2):
        s_ref[e] = scores(0, 0, e)

    def q_body(qi, carry):
        m_ref[...] = jnp.full_like(m_ref, -jnp.inf)
        l_ref[...] = jnp.zeros_like(l_ref)
        acc_ref[...] = jnp.zeros_like(acc_ref)

        def body(j, carry2):
            for e in range(2):
                s = s_ref[e]
                s_ref[e] = scores(qi, j + 1, e)
                softmax_pv(j, e, s)
            return carry2

        lax.fori_loop(0, qi, body, 0)
        q_next = jnp.minimum(qi + 1, n_qt - 1)
        for e in range(2):
            s = jnp.where(kc <= qc, s_ref[e], NEG_BIG)
            s_ref[e] = scores(q_next, 0, e)
            softmax_pv(qi, e, s)

        o_t = jnp.concatenate([acc_ref[0] / l_ref[0], acc_ref[1] / l_ref[1]], axis=0)
        o_ref[pl.ds(pl.multiple_of(qi * C_TILE, C_TILE), C_TILE), :] = o_t.T.astype(o_ref.dtype)
        return carry

    lax.fori_loop(0, n_qt, q_body, 0)


def mla_flash(qt, k, vt, bsz, seq):
    t = bsz * seq
    return pl.pallas_call(
        functools.partial(_mla_flash_kernel, n_qt=seq // C_TILE),
        out_shape=jax.ShapeDtypeStruct((t, WIDTH), BF16),
        grid=(bsz, N_PAIRS),
        in_specs=[
            pl.BlockSpec((2 * C_SLOT, seq), lambda b, p: (p, b)),
            pl.BlockSpec((seq, 2 * C_SLOT), lambda b, p: (b, p)),
            pl.BlockSpec((LANES, seq), lambda b, p: (p, b)),
        ],
        out_specs=pl.BlockSpec((seq, LANES), lambda b, p: (b, p)),
        scratch_shapes=[pltpu.VMEM((2, HEAD_DIM, C_TILE), F32),
                        pltpu.VMEM((2, 1, C_TILE), F32),
                        pltpu.VMEM((2, 1, C_TILE), F32),
                        pltpu.VMEM((2, C_TILE, C_TILE), F32)],
        compiler_params=_cparams(("parallel", "parallel")),
        name="mla_flash",
    )(qt, k, vt)


def _merge_kernel(x_ref, ya_ref, yb_ref, yc_ref, g_ref, wbr_ref, wout_ref, o_ref):
    merged = None
    for n, y_ref in enumerate((ya_ref, yb_ref, yc_ref)):
        up = jnp.dot(y_ref[...], wbr_ref[n], preferred_element_type=F32)
        gl = g_ref[:, n * D_MODEL:(n + 1) * D_MODEL].astype(F32)
        term = (1.0 / (1.0 + jnp.exp(-gl))) * up
        merged = term if merged is None else merged + term
    o_ref[...] = x_ref[...] + jnp.dot(merged.astype(BF16), wout_ref[...], preferred_element_type=F32)


def merge_out(x, ya, yb, yc, proj16, gate_blk, wbr, wout, tm):
    t = x.shape[0]
    return pl.pallas_call(
        _merge_kernel,
        out_shape=jax.ShapeDtypeStruct((t, D_MODEL), F32),
        grid=(t // tm,),
        in_specs=[
            pl.BlockSpec((tm, D_MODEL), lambda i: (i, 0)),
            pl.BlockSpec((tm, WIDTH), lambda i: (i, 0)),
            pl.BlockSpec((tm, WIDTH), lambda i: (i, 0)),
            pl.BlockSpec((tm, WIDTH), lambda i: (i, 0)),
            pl.BlockSpec((tm, N_BRANCH * D_MODEL), lambda i: (i, gate_blk)),
            _resident(wbr),
            _resident(wout),
        ],
        out_specs=pl.BlockSpec((tm, D_MODEL), lambda i: (i, 0)),
        compiler_params=_cparams(("parallel",)),
        name="merge_out",
    )(x, ya, yb, yc, proj16, wbr, wout)


def _ffn_kernel(x_ref, g_ref, w1_ref, w2_ref, gf_ref, o_ref, *, final_norm, tf):
    x = x_ref[...]
    h = _rms(x, g_ref[...]).astype(BF16)
    acc = None
    for k in range(w1_ref.shape[1] // tf):
        u = jnp.maximum(jnp.dot(h, w1_ref[:, k * tf:(k + 1) * tf], preferred_element_type=F32), 0.0)
        part = jnp.dot((u * u).astype(BF16), w2_ref[k * tf:(k + 1) * tf, :], preferred_element_type=F32)
        acc = part if acc is None else acc + part
    y = x + acc
    if final_norm:
        y = _rms(y, gf_ref[...])
    o_ref[...] = y


def ffn(x, g, w1, w2, gf, final_norm, tm, tf):
    t, d = x.shape
    return pl.pallas_call(
        functools.partial(_ffn_kernel, final_norm=final_norm, tf=tf),
        out_shape=jax.ShapeDtypeStruct((t, d), F32),
        grid=(t // tm,),
        in_specs=[
            pl.BlockSpec((tm, d), lambda i: (i, 0)),
            pl.BlockSpec((1, d), lambda i: (0, 0)),
            _resident(w1),
            _resident(w2),
            pl.BlockSpec((1, d), lambda i: (0, 0)),
        ],
        out_specs=pl.BlockSpec((tm, d), lambda i: (i, 0)),
        compiler_params=_cparams(("parallel",)),
        name="ffn",
    )(x, g.reshape(1, d), w1, w2, gf.reshape(1, d))


COL_AQ, COL_AK, COL_AV = 0, 4, 8
COL_BQ, COL_BI, COL_BG = 12, 16, 20
GATE_BLK = 1


def _prep_w_in(w):
    o = 0
    parts = {}
    for name, size in (("aq", 512), ("ak", 512), ("av", 512), ("bq", 512), ("bf", 512), ("bi", 512),
                       ("bg", 512), ("cq", C_Q_RANK), ("ckv", C_KV_RANK), ("ckr", C_ROPE),
                       ("gate", N_BRANCH * D_MODEL)):
        parts[name] = w[:, o:o + size]
        o += size
    w16 = jnp.concatenate([parts[n] for n in ("aq", "ak", "av", "bq", "bi", "bg", "gate")], axis=1)
    half = C_ROPE // 2
    x1, x2 = parts["ckr"][:, :half], parts["ckr"][:, half:]
    zl = jnp.zeros((w.shape[0], C_NOPE), w.dtype)
    zr = jnp.zeros((w.shape[0], C_SLOT - C_NOPE - C_ROPE), w.dtype)
    kr_slot = jnp.concatenate([zl, x1, x2, zr], axis=1)
    kr_sw = jnp.concatenate([zl, -x2, x1, zr], axis=1)
    w32 = jnp.concatenate([parts["cq"], parts["ckv"], kr_slot, kr_sw, parts["bf"]], axis=1)
    return w16.astype(BF16), w32.astype(BF16)


def _prep_w_mla(w_uq, w_ukv):
    half = C_ROPE // 2
    per_q = C_NOPE + C_ROPE
    wq = w_uq.reshape(C_Q_RANK, N_HEADS, per_q)
    nope, x1, x2 = wq[..., :C_NOPE], wq[..., C_NOPE:C_NOPE + half], wq[..., C_NOPE + half:]
    zpad = jnp.zeros((C_Q_RANK, N_HEADS, C_SLOT - per_q), w_uq.dtype)
    wq1 = jnp.concatenate([nope, x1, x2, zpad], axis=-1).reshape(C_Q_RANK, N_HEADS * C_SLOT)
    wq2 = jnp.concatenate([jnp.zeros_like(nope), -x2, x1, zpad], axis=-1).reshape(C_Q_RANK, N_HEADS * C_SLOT)
    wkv = w_ukv.reshape(C_KV_RANK, N_HEADS, C_NOPE + HEAD_DIM)
    k_nope, v = wkv[..., :C_NOPE], wkv[..., C_NOPE:]
    wk = jnp.concatenate([k_nope, jnp.zeros((C_KV_RANK, N_HEADS, C_SLOT - C_NOPE), w_ukv.dtype)],
                         axis=-1).reshape(C_KV_RANK, N_HEADS * C_SLOT)
    wv = v.reshape(C_KV_RANK, WIDTH)
    return wq1.T.astype(BF16), wq2.T.astype(BF16), wk.astype(BF16), wv.T.astype(BF16)


def kernel(x, positions, norm_mix_g, w_in, rel_bias, hgrn_lb_logits, hgrn_norm_g, mla_q_norm_g,
           mla_kv_norm_g, mla_w_uq, mla_w_ukv, w_branch, w_out, norm_ffn_g, w_ff1, w_ff2, final_norm_g):
    bsz, seq, d = x.shape
    depth = w_in.shape[0]
    t = bsz * seq
    assert d == D_MODEL and seq % C_TILE == 0 and seq >= A_BAND
    tm = min(1024, t)

    p_lb = jax.nn.softmax(hgrn_lb_logits.astype(F32), axis=0)
    lb_all = jnp.cumsum(p_lb, axis=0)
    lb_all = lb_all - lb_all[0:1]

    tabs = rope_tables(positions, tm)
    xf = x.reshape(t, d)
    for l in range(depth):
        w16, w32 = _prep_w_in(w_in[l])
        wq1t, wq2t, wk, wvt = _prep_w_mla(mla_w_uq[l], mla_w_ukv[l])
        proj16, y_b, qt_c, k_c, vt_c = in_proj(xf, norm_mix_g[l], w16, w32, tabs, mla_q_norm_g[l], mla_kv_norm_g[l],
                                               wq1t, wq2t, wk, wvt, lb_all[l], hgrn_norm_g[l], seq, min(512, seq), 1024)

        y_a = band_attention(proj16, band_bias_table(rel_bias[l]), bsz, seq, COL_AQ, COL_AK, COL_AV)
        y_c = mla_flash(qt_c, k_c, vt_c, bsz, seq)

        xf = merge_out(xf, y_a, y_b, y_c, proj16, GATE_BLK, w_branch[l].astype(BF16),
                       w_out[l].astype(BF16), tm)
        xf = ffn(xf, norm_ffn_g[l], w_ff1[l].astype(BF16), w_ff2[l].astype(BF16), final_norm_g,
                 l == depth - 1, tm, 1024)
    return xf.reshape(bsz, seq, d)
```

```python
import functools

import jax
import jax.numpy as jnp
from jax import lax
from jax.experimental import pallas as pl
from jax.experimental.pallas import tpu as pltpu

F32 = jnp.float32
BF16 = jnp.bfloat16

D_MODEL = 1024
CHUNK = 64
EPS = 1e-6
N_HEADS = 8
HEAD_DIM = 64
LANES = 128
N_PAIRS = N_HEADS * HEAD_DIM // LANES
WIDTH = N_HEADS * HEAD_DIM

A_LEFT_CHUNKS = 8
A_MAX_REL = 128
A_QTILE = 2 * CHUNK
A_BAND = (A_LEFT_CHUNKS + 2) * CHUNK
A_PAD = A_LEFT_CHUNKS * CHUNK
A_TILES_PER_STEP = 8

C_Q_RANK = 256
C_KV_RANK = 128
C_ROPE = 32
C_NOPE = 64
C_SLOT = LANES
C_LATENT = C_Q_RANK + C_KV_RANK + 2 * C_SLOT
ROPE_BASE = 10000.0
C_TILE = 512
ONES_ROWS = 16
LOG2E = 1.4426950408889634
C_QSCALE = (C_NOPE + C_ROPE) ** -0.5 * LOG2E

N_BRANCH = 3

SUB = 8
N_SUB = CHUNK // SUB
NEG_BIG = -1e30

VMEM_LIMIT = 48 * 1024 * 1024
VMEM_LIMIT_IN_PROJ = 58 * 1024 * 1024

TM_IN_PROJ = 512
TN_IN_PROJ = 1024
TM_DENSE = 1024
TF_FFN = 1024

_NT = (((1,), (1,)), ((), ()))


def _cparams(sem, vmem_limit=VMEM_LIMIT):
    return pltpu.CompilerParams(dimension_semantics=sem, vmem_limit_bytes=vmem_limit)


def _resident(a):
    return pl.BlockSpec(a.shape, lambda *_: (0,) * a.ndim, pipeline_mode=pl.Buffered(1))


def _rms(x, g):
    return x * lax.rsqrt(jnp.mean(x * x, axis=-1, keepdims=True) + EPS) * g


def _in_proj_kernel(x_ref, g_ref, wa_ref, wb_ref, ct_ref, st_ref, ctt_ref, stt_ref, gq_ref, gkv_ref,
                    wq1t_ref, wq2t_ref, wk_ref, wvt_ref, loglb_ref, log1mlb_ref, ng_ref,
                    oa_ref, yb_ref, qt_ref, k_ref, vt_ref, state_ref, *, tn, tiles_per_seq):
    @pl.when(pl.program_id(0) % tiles_per_seq == 0)
    def _():
        state_ref[...] = jnp.zeros_like(state_ref)

    h = _rms(x_ref[...], g_ref[...]).astype(BF16)

    def proj_tile(j):
        cols = slice(j * tn, (j + 1) * tn)
        t = jnp.dot(h, wa_ref[:, cols], preferred_element_type=F32)
        oa_ref[:, cols] = t.astype(oa_ref.dtype)
        return t

    tq = proj_tile(COL_BQ * LANES // tn)
    tig = proj_tile(COL_BI * LANES // tn)
    pb = jnp.dot(h, wb_ref[...], preferred_element_type=F32)
    q_off = COL_BQ * LANES % tn
    i_off = COL_BI * LANES % tn
    g_off = COL_BG * LANES - (COL_BI * LANES // tn) * tn
    y_out = []
    gens = []
    for p in range(N_PAIRS):
        lanes = lambda off: slice(off + p * LANES, off + (p + 1) * LANES)
        gens.append(_hgrn_stages(pb[:, lanes(C_LATENT)], tq[:, lanes(q_off)], tig[:, lanes(i_off)],
                                 tig[:, lanes(g_off)], loglb_ref[p], log1mlb_ref[p], ng_ref[...],
                                 state_ref.at[p], y_out))

    qn = _rms(pb[:, :C_Q_RANK], gq_ref[...]).astype(BF16)
    kvn = _rms(pb[:, C_Q_RANK:C_Q_RANK + C_KV_RANK], gkv_ref[...]).astype(BF16)
    kr = pb[:, C_Q_RANK + C_KV_RANK:C_Q_RANK + C_KV_RANK + C_SLOT]
    kr_sw = pb[:, C_Q_RANK + C_KV_RANK + C_SLOT:C_LATENT]
    kr_rot = kr * ct_ref[...] + kr_sw * st_ref[...]

    def mla_q():
        a_t = lax.dot_general(wq1t_ref[...], qn, _NT, preferred_element_type=F32)
        b_t = lax.dot_general(wq2t_ref[...], qn, _NT, preferred_element_type=F32)
        ctt = ctt_ref[...]
        stt = stt_ref[...]
        for hd in range(N_HEADS):
            sl = slice(hd * C_SLOT, (hd + 1) * C_SLOT)
            qt_ref[sl, :] = (a_t[sl, :] * ctt + b_t[sl, :] * stt).astype(qt_ref.dtype)

    def mla_kv():
        kn = jnp.dot(kvn, wk_ref[...], preferred_element_type=F32)
        vt_ref[...] = lax.dot_general(wvt_ref[...], kvn, _NT, preferred_element_type=F32).astype(vt_ref.dtype)
        for hd in range(N_HEADS):
            sl = slice(hd * C_SLOT, (hd + 1) * C_SLOT)
            k_ref[:, sl] = (kn[:, sl] + kr_rot).astype(k_ref.dtype)

    done = {COL_BQ * LANES // tn, COL_BI * LANES // tn}
    big = [functools.partial(proj_tile, j) for j in range(wa_ref.shape[1] // tn) if j not in done] + [mla_q, mla_kv]
    assert len(big) == HGRN_STAGES
    for work in big:
        work()
        for gen in gens:
            next(gen)
    for p in range(N_PAIRS):
        yb_ref[:, p * LANES:(p + 1) * LANES] = y_out[p].astype(yb_ref.dtype)


def in_proj(x, g, wa, wb, tabs, gq, gkv, wq1t, wq2t, wk, wvt, lb, hgrn_norm_g, seq, tm, tn):
    t, d = x.shape
    na = wa.shape[1]
    ctab, stab, ctab_t, stab_t = tabs
    gq = gq.reshape(1, -1)
    gkv = gkv.reshape(1, -1)
    log_lb = jnp.log(lb).reshape(N_PAIRS, 1, LANES)
    log1m_lb = jnp.log1p(-lb).reshape(N_PAIRS, 1, LANES)
    ng = jnp.tile(hgrn_norm_g.astype(F32), LANES // HEAD_DIM).reshape(1, LANES)
    rows = lambda n: pl.BlockSpec((tm, n), lambda i: (i, 0))
    cols = lambda n: pl.BlockSpec((n, tm), lambda i: (0, i))
    return pl.pallas_call(
        functools.partial(_in_proj_kernel, tn=tn, tiles_per_seq=seq // tm),
        out_shape=(jax.ShapeDtypeStruct((t, na), BF16),
                   jax.ShapeDtypeStruct((t, WIDTH), BF16),
                   jax.ShapeDtypeStruct((N_HEADS * C_SLOT, t), BF16),
                   jax.ShapeDtypeStruct((t, N_HEADS * C_SLOT), BF16),
                   jax.ShapeDtypeStruct((WIDTH, t), BF16)),
        grid=(t // tm,),
        in_specs=[rows(d), pl.BlockSpec((1, d), lambda i: (0, 0)), _resident(wa), _resident(wb),
                  rows(LANES), rows(LANES), cols(LANES), cols(LANES),
                  _resident(gq), _resident(gkv), _resident(wq1t), _resident(wq2t), _resident(wk), _resident(wvt),
                  _resident(log_lb), _resident(log1m_lb), _resident(ng)],
        out_specs=(rows(na), rows(WIDTH), cols(N_HEADS * C_SLOT), rows(N_HEADS * C_SLOT), cols(WIDTH)),
        scratch_shapes=[pltpu.VMEM((N_PAIRS, LANES, LANES), F32)],
        compiler_params=_cparams(("arbitrary",), VMEM_LIMIT_IN_PROJ),
        name="in_proj",
    )(x, g.reshape(1, d), wa, wb, ctab, stab, ctab_t, stab_t, gq, gkv, wq1t, wq2t, wk, wvt, log_lb, log1m_lb, ng)


def _trig_kernel(pos_ref, invf_ref, c_ref, s_ref, ct_ref, st_ref):
    ang = pos_ref[...].astype(F32) * invf_ref[...]
    c = jnp.cos(ang)
    s = jnp.sin(ang)
    c_ref[...] = c
    s_ref[...] = s
    ct_ref[...] = c.T
    st_ref[...] = s.T


def rope_tables(positions, tm):
    t = positions.size
    inv_freq = ROPE_BASE ** (-jnp.arange(0, C_ROPE, 2, dtype=F32) / C_ROPE)
    half = C_ROPE // 2
    invf = jnp.zeros((LANES,), F32)
    invf = invf.at[C_NOPE:C_NOPE + half].set(inv_freq).at[C_NOPE + half:C_NOPE + C_ROPE].set(inv_freq)
    return pl.pallas_call(
        _trig_kernel,
        out_shape=(jax.ShapeDtypeStruct((t, LANES), F32),) * 2 + (jax.ShapeDtypeStruct((LANES, t), F32),) * 2,
        grid=(t // tm,),
        in_specs=[
            pl.BlockSpec((tm, 1), lambda i: (i, 0)),
            pl.BlockSpec((1, LANES), lambda i: (0, 0)),
        ],
        out_specs=(pl.BlockSpec((tm, LANES), lambda i: (i, 0)),) * 2
        + (pl.BlockSpec((LANES, tm), lambda i: (0, i)),) * 2,
        compiler_params=_cparams(("parallel",)),
        name="rope_tables",
    )(positions.reshape(t, 1), invf.reshape(1, LANES))


def _band_attn_kernel(q_ref, k_ref, v_ref, bias_ref, o_ref):
    lane = lax.broadcasted_iota(jnp.int32, (A_QTILE, LANES), 1)
    n_shift = A_PAD // A_QTILE
    starts, scores = [], []
    for t in range(A_TILES_PER_STEP):
        c2 = pl.program_id(2) * A_TILES_PER_STEP + t
        start = pl.multiple_of(jnp.maximum(c2 * A_QTILE - A_PAD, 0), A_QTILE)
        shift = jnp.minimum(c2, n_shift)
        kwin = k_ref[pl.ds(start, A_BAND), :]
        q = q_ref[t * A_QTILE:(t + 1) * A_QTILE, :].astype(F32) * (HEAD_DIM ** -0.5)
        qst = jnp.concatenate([jnp.where(lane < HEAD_DIM, q, 0.0), jnp.where(lane < HEAD_DIM, 0.0, q)],
                              axis=0).astype(BF16)
        s = lax.dot_general(qst, kwin, _NT, preferred_element_type=F32)
        starts.append(start)
        scores.append(s + bias_ref[shift, 0])
    for t in range(A_TILES_PER_STEP):
        s = scores[t]
        vwin = v_ref[pl.ds(starts[t], A_BAND), :]
        m = jnp.max(s, axis=-1, keepdims=True)
        p = jnp.exp(s - m)
        l = jnp.sum(p, axis=-1, keepdims=True)
        pv = jnp.dot(p.astype(BF16), vwin, preferred_element_type=F32) / l
        o_ref[t * A_QTILE:(t + 1) * A_QTILE, :] = jnp.where(lane < HEAD_DIM, pv[:A_QTILE], pv[A_QTILE:]).astype(o_ref.dtype)


def band_attention(proj, bias, bsz, seq, q_col, k_col, v_col):
    t = bsz * seq
    rows = A_QTILE * A_TILES_PER_STEP
    n_steps = seq // rows
    n_shift = A_PAD // A_QTILE
    return pl.pallas_call(
        _band_attn_kernel,
        out_shape=jax.ShapeDtypeStruct((t, WIDTH), BF16),
        grid=(bsz, N_PAIRS, n_steps),
        in_specs=[
            pl.BlockSpec((rows, LANES), lambda b, p, c: (b * n_steps + c, q_col + p)),
            pl.BlockSpec((seq, LANES), lambda b, p, c: (b, k_col + p)),
            pl.BlockSpec((seq, LANES), lambda b, p, c: (b, v_col + p)),
            pl.BlockSpec((n_shift + 1, 1, 2 * A_QTILE, A_BAND), lambda b, p, c: (0, p, 0, 0)),
        ],
        out_specs=pl.BlockSpec((rows, LANES), lambda b, p, c: (b * n_steps + c, p)),
        compiler_params=_cparams(("parallel", "parallel", "arbitrary")),
        name="band_attention",
    )(proj, proj, proj, bias.reshape(n_shift + 1, N_PAIRS, 2 * A_QTILE, A_BAND))


A_ROLL = 768


def _band_bias_kernel(r_ref, o_ref):
    d = pl.program_id(0) * A_QTILE
    prof = jnp.broadcast_to(r_ref[...], (A_QTILE, A_ROLL))
    b = pltpu.roll(prof, 0, 1, stride=1, stride_axis=0)[:, :A_BAND]
    i = lax.broadcasted_iota(jnp.int32, (A_QTILE, A_BAND), 0)
    j = lax.broadcasted_iota(jnp.int32, (A_QTILE, A_BAND), 1)
    gap = (d + i) // CHUNK - j // CHUNK
    o_ref[...] = jnp.where((gap >= 0) & (gap <= A_LEFT_CHUNKS), b, NEG_BIG)


def band_bias_table(rel_table):
    n_shift = A_PAD // A_QTILE
    heads = rel_table.shape[0]
    pad = A_ROLL + A_PAD
    ext = jnp.pad(rel_table.astype(F32)[:, ::-1], ((0, 0), (pad, pad)), mode="edge")

    def seg(d, u0, n):
        o = u0 - d + A_MAX_REL + pad
        return ext[:, o:o + n]

    n_neg = A_ROLL - (A_BAND + 1)
    prof = jnp.stack([jnp.concatenate([seg(s * A_QTILE, 0, A_BAND + 1), seg(s * A_QTILE, -n_neg, n_neg)], axis=1)
                      for s in range(n_shift + 1)])
    return pl.pallas_call(
        _band_bias_kernel,
        out_shape=jax.ShapeDtypeStruct((n_shift + 1, heads, A_QTILE, A_BAND), F32),
        grid=(n_shift + 1, heads),
        in_specs=[pl.BlockSpec((None, None, 1, A_ROLL), lambda s, h: (s, h, 0, 0))],
        out_specs=pl.BlockSpec((None, None, A_QTILE, A_BAND), lambda s, h: (s, h, 0, 0)),
        compiler_params=_cparams(("parallel", "parallel")),
        name="band_bias",
    )(prof.reshape(n_shift + 1, heads, 1, A_ROLL))


def _split3(x):
    hi = x.astype(BF16)
    r1 = x - hi.astype(F32)
    mid = r1.astype(BF16)
    lo = (r1 - mid.astype(F32)).astype(BF16)
    return hi, mid, lo


HGRN_STAGES = 6


def _hgrn_stages(z, qraw, v, graw, log_lb, log1m_lb, norm_g, st_ref, y_out):
    rows = z.shape[0]
    n_chunks = rows // CHUNK
    n_blk = rows // SUB

    head0 = lax.broadcasted_iota(jnp.int32, (rows, LANES), 1) < HEAD_DIM
    head0_c = lax.broadcasted_iota(jnp.int32, (CHUNK, LANES), 1) < HEAD_DIM
    r64 = lax.broadcasted_iota(jnp.int32, (CHUNK, CHUNK), 0)
    c64 = lax.broadcasted_iota(jnp.int32, (CHUNK, CHUNK), 1)
    tril = (c64 <= r64).astype(BF16)
    c64s = lax.broadcasted_iota(jnp.int32, (2 * CHUNK, CHUNK), 1)
    rl = lax.broadcasted_iota(jnp.int32, (LANES, LANES), 0)
    cl = lax.broadcasted_iota(jnp.int32, (LANES, LANES), 1)
    same_head = (rl // HEAD_DIM) == (cl // HEAD_DIM)
    bd = same_head.astype(BF16)
    ri = lax.broadcasted_iota(jnp.int32, (CHUNK, CHUNK * SUB), 0)
    ci = lax.broadcasted_iota(jnp.int32, (CHUNK, CHUNK * SUB), 1)
    pick = (ci // SUB == ri).astype(BF16)
    row_s = lax.broadcasted_iota(jnp.int32, (SUB, LANES), 0)

    def chunk(x, c):
        return x[c * CHUNK:(c + 1) * CHUNK]

    log_sig = jnp.minimum(z, 0.0) - jnp.log(1.0 + jnp.exp(-jnp.abs(z)))
    bterm = log1m_lb + log_sig
    log_f = jnp.maximum(log_lb, bterm) + jnp.log(1.0 + jnp.exp(-jnp.abs(log_lb - bterm)))
    log_k = bterm - z
    qs = qraw * (1.0 / (1.0 + jnp.exp(-qraw)))

    x3 = jnp.concatenate(_split3(log_f), axis=1)
    cum3 = [jnp.dot(tril, chunk(x3, c), preferred_element_type=F32) for c in range(n_chunks)]
    yield
    cum = jnp.concatenate([t[:, :LANES] + t[:, LANES:2 * LANES] + t[:, 2 * LANES:] for t in cum3], axis=0)

    zero_row = jnp.zeros((1, LANES), F32)
    c_end = [cum[SUB * b + SUB - 1:SUB * b + SUB, :] for b in range(n_blk)]
    c_start = [zero_row if b % N_SUB == 0 else c_end[b - 1] for b in range(n_blk)]
    c_last = [c_end[c * N_SUB + N_SUB - 1] for c in range(n_chunks)]

    def rows_of(blocks):
        return jnp.concatenate([jnp.broadcast_to(r, (SUB, LANES)) for r in blocks], axis=0)

    cstart_full = rows_of(c_start)
    cend_full = rows_of(c_end)
    clast_full = jnp.concatenate([jnp.broadcast_to(r, (CHUNK, LANES)) for r in c_last], axis=0)
    q1 = qs * jnp.exp(cum - cstart_full)
    lk = log_k - cum
    k2b = jnp.exp(cend_full + lk).astype(BF16)
    k_end = jnp.exp(clast_full + lk).astype(BF16)
    q_state = (qs * jnp.exp(cum)).astype(BF16)
    vb = v.astype(BF16)

    s_cross = []
    for j in range(N_SUB - 1):
        d_rows = []
        for b in range(n_blk):
            if b % N_SUB > j:
                d_rows.append(jnp.exp(c_start[b] - c_end[(b // N_SUB) * N_SUB + j]))
            else:
                d_rows.append(zero_row)
        qj = q1 * rows_of(d_rows)
        q0 = jnp.where(head0, qj, 0.0).astype(BF16)
        q1h = jnp.where(head0, 0.0, qj).astype(BF16)
        for c in range(n_chunks):
            qst = jnp.concatenate([chunk(q0, c), chunk(q1h, c)], axis=0)
            s_cross.append(lax.dot_general(qst, chunk(k2b, c), _NT, preferred_element_type=F32))
    yield
    o_cross = []
    for c in range(n_chunks):
        sc = None
        for j in range(N_SUB - 1):
            col_in_j = (c64s >= j * SUB) & (c64s < (j + 1) * SUB)
            s = jnp.where(col_in_j, s_cross[j * n_chunks + c], 0.0)
            sc = s if sc is None else sc + s
        o2 = jnp.dot(sc.astype(BF16), chunk(vb, c), preferred_element_type=F32)
        o_cross.append(jnp.where(head0_c, o2[:CHUNK], o2[CHUNK:]))

    yield
    a = cum - log_k
    w_rows = []
    for b in range(n_blk):
        a_b = a[SUB * b:SUB * (b + 1), :]
        for i in range(SUB):
            r = SUB * b + i
            arg = jnp.where(row_s <= i, cum[r:r + 1, :] - a_b, NEG_BIG)
            w_rows.append(jnp.exp(arg) * qs[r:r + 1, :])
    w_all = jnp.concatenate(w_rows, axis=0).astype(BF16)
    sb = jnp.dot(w_all, bd, preferred_element_type=F32)
    yield
    o_diag = []
    for c in range(n_chunks):
        v_rep = jnp.concatenate([v[SUB * (r // SUB):SUB * (r // SUB + 1), :]
                                 for r in range(c * CHUNK, (c + 1) * CHUNK)], axis=0)
        sb_c = sb[c * CHUNK * SUB:(c + 1) * CHUNK * SUB]
        o_diag.append(jnp.dot(pick, (sb_c * v_rep).astype(BF16), preferred_element_type=F32))

    yield
    upd = [jnp.where(same_head,
                     lax.dot_general(chunk(vb, c), chunk(k_end, c), (((0,), (0,)), ((), ())),
                                     preferred_element_type=F32), 0.0) for c in range(n_chunks)]
    st = st_ref[...]
    o_state = []
    for c in range(n_chunks):
        o_state.append(lax.dot_general(chunk(q_state, c), st.astype(BF16), _NT, preferred_element_type=F32))
        st = st * jnp.exp(c_last[c]) + upd[c]
    st_ref[...] = st

    o = jnp.concatenate([o_cross[c] + o_diag[c] + o_state[c] for c in range(n_chunks)], axis=0)
    osq = o * o
    ms0 = jnp.sum(jnp.where(head0, osq, 0.0), axis=-1, keepdims=True)
    ms1 = jnp.sum(jnp.where(head0, 0.0, osq), axis=-1, keepdims=True)
    ms = jnp.where(head0, ms0, ms1) * (1.0 / HEAD_DIM)
    y = o * lax.rsqrt(ms + EPS) * norm_g
    y_out.append(y * (graw * (1.0 / (1.0 + jnp.exp(-graw)))))
    yield


def _mla_flash_kernel(qt_ref, k_ref, vt_ref, o_ref, acc_ref, m_ref, l_ref, s_ref, *, n_qt):
    c = C_QSCALE

    def scores(qi, j, e):
        qoff = pl.multiple_of(qi * C_TILE, C_TILE)
        koff = pl.multiple_of(j * C_TILE, C_TILE)
        k = k_ref[pl.ds(koff, C_TILE), e * C_SLOT:(e + 1) * C_SLOT]
        qt = qt_ref[e * C_SLOT:(e + 1) * C_SLOT, pl.ds(qoff, C_TILE)]
        return jnp.dot(k, qt, preferred_element_type=F32)

    def softmax_pv(j, e, s):
        off = pl.multiple_of(j * C_TILE, C_TILE)
        m_prev = m_ref[e]
        m_new = jnp.maximum(m_prev, jnp.max(s, axis=0, keepdims=True))
        p = jnp.exp2((s - m_new) * c).astype(BF16)
        alpha = jnp.exp2((m_prev - m_new) * c)
        vt = vt_ref[e * HEAD_DIM:(e + 1) * HEAD_DIM, pl.ds(off, C_TILE)]
        vt_aug = jnp.concatenate([vt, jnp.ones((ONES_ROWS, C_TILE), BF16)], axis=0)
        pv = jnp.dot(vt_aug, p, preferred_element_type=F32)
        l_ref[e] = alpha * l_ref[e] + pv[HEAD_DIM:HEAD_DIM + 1]
        acc_ref[e] = alpha * acc_ref[e] + pv[:HEAD_DIM]
        m_ref[e] = m_new

    kc = lax.broadcasted_iota(jnp.int32, (C_TILE, C_TILE), 0) // CHUNK
    qc = lax.broadcasted_iota(jnp.int32, (C_TILE, C_TILE), 1) // CHUNK

    for e in range(2):
        s_ref[e] = scores(0, 0, e)

    def q_body(qi, carry):
        m_ref[...] = jnp.full_like(m_ref, -jnp.inf)
        l_ref[...] = jnp.zeros_like(l_ref)
        acc_ref[...] = jnp.zeros_like(acc_ref)

        def body(j, carry2):
            for e in range(2):
                s = s_ref[e]
                s_ref[e] = scores(qi, j + 1, e)
                softmax_pv(j, e, s)
            return carry2

        lax.fori_loop(0, qi, body, 0)
        q_next = jnp.minimum(qi + 1, n_qt - 1)
        for e in range(2):
            s = jnp.where(kc <= qc, s_ref[e], NEG_BIG)
            s_ref[e] = scores(q_next, 0, e)
            softmax_pv(qi, e, s)

        o_t = jnp.concatenate([acc_ref[0] / l_ref[0], acc_ref[1] / l_ref[1]], axis=0)
        o_ref[pl.ds(pl.multiple_of(qi * C_TILE, C_TILE), C_TILE), :] = o_t.T.astype(o_ref.dtype)
        return carry

    lax.fori_loop(0, n_qt, q_body, 0)


def mla_flash(qt, k, vt, bsz, seq):
    t = bsz * seq
    return pl.pallas_call(
        functools.partial(_mla_flash_kernel, n_qt=seq // C_TILE),
        out_shape=jax.ShapeDtypeStruct((t, WIDTH), BF16),
        grid=(bsz, N_PAIRS),
        in_specs=[
            pl.BlockSpec((2 * C_SLOT, seq), lambda b, p: (p, b)),
            pl.BlockSpec((seq, 2 * C_SLOT), lambda b, p: (b, p)),
            pl.BlockSpec((LANES, seq), lambda b, p: (p, b)),
        ],
        out_specs=pl.BlockSpec((seq, LANES), lambda b, p: (b, p)),
        scratch_shapes=[pltpu.VMEM((2, HEAD_DIM, C_TILE), F32),
                        pltpu.VMEM((2, 1, C_TILE), F32),
                        pltpu.VMEM((2, 1, C_TILE), F32),
                        pltpu.VMEM((2, C_TILE, C_TILE), F32)],
        compiler_params=_cparams(("parallel", "parallel")),
        name="mla_flash",
    )(qt, k, vt)


def _merge_kernel(x_ref, ya_ref, yb_ref, yc_ref, g_ref, wbr_ref, wout_ref, o_ref):
    merged = None
    for n, y_ref in enumerate((ya_ref, yb_ref, yc_ref)):
        up = jnp.dot(y_ref[...], wbr_ref[n], preferred_element_type=F32)
        gl = g_ref[:, n * D_MODEL:(n + 1) * D_MODEL].astype(F32)
        term = (1.0 / (1.0 + jnp.exp(-gl))) * up
        merged = term if merged is None else merged + term
    o_ref[...] = x_ref[...] + jnp.dot(merged.astype(BF16), wout_ref[...], preferred_element_type=F32)


def merge_out(x, ya, yb, yc, proj16, gate_blk, wbr, wout, tm):
    t = x.shape[0]
    return pl.pallas_call(
        _merge_kernel,
        out_shape=jax.ShapeDtypeStruct((t, D_MODEL), F32),
        grid=(t // tm,),
        in_specs=[
            pl.BlockSpec((tm, D_MODEL), lambda i: (i, 0)),
            pl.BlockSpec((tm, WIDTH), lambda i: (i, 0)),
            pl.BlockSpec((tm, WIDTH), lambda i: (i, 0)),
            pl.BlockSpec((tm, WIDTH), lambda i: (i, 0)),
            pl.BlockSpec((tm, N_BRANCH * D_MODEL), lambda i: (i, gate_blk)),
            _resident(wbr),
            _resident(wout),
        ],
        out_specs=pl.BlockSpec((tm, D_MODEL), lambda i: (i, 0)),
        compiler_params=_cparams(("parallel",)),
        name="merge_out",
    )(x, ya, yb, yc, proj16, wbr, wout)


def _ffn_kernel(x_ref, g_ref, w1_ref, w2_ref, gf_ref, o_ref, *, final_norm, tf):
    x = x_ref[...]
    h = _rms(x, g_ref[...]).astype(BF16)
    acc = None
    for k in range(w1_ref.shape[1] // tf):
        u = jnp.maximum(jnp.dot(h, w1_ref[:, k * tf:(k + 1) * tf], preferred_element_type=F32), 0.0)
        part = jnp.dot((u * u).astype(BF16), w2_ref[k * tf:(k + 1) * tf, :], preferred_element_type=F32)
        acc = part if acc is None else acc + part
    y = x + acc
    if final_norm:
        y = _rms(y, gf_ref[...])
    o_ref[...] = y


def ffn(x, g, w1, w2, gf, final_norm, tm, tf):
    t, d = x.shape
    return pl.pallas_call(
        functools.partial(_ffn_kernel, final_norm=final_norm, tf=tf),
        out_shape=jax.ShapeDtypeStruct((t, d), F32),
        grid=(t // tm,),
        in_specs=[
            pl.BlockSpec((tm, d), lambda i: (i, 0)),
            pl.BlockSpec((1, d), lambda i: (0, 0)),
            _resident(w1),
            _resident(w2),
            pl.BlockSpec((1, d), lambda i: (0, 0)),
        ],
        out_specs=pl.BlockSpec((tm, d), lambda i: (i, 0)),
        compiler_params=_cparams(("parallel",)),
        name="ffn",
    )(x, g.reshape(1, d), w1, w2, gf.reshape(1, d))


COL_AQ, COL_AK, COL_AV = 0, 4, 8
COL_BQ, COL_BI, COL_BG = 12, 16, 20
GATE_BLK = 1


def _prep_w_in(w):
    o = 0
    parts = {}
    for name, size in (("aq", 512), ("ak", 512), ("av", 512), ("bq", 512), ("bf", 512), ("bi", 512),
                       ("bg", 512), ("cq", C_Q_RANK), ("ckv", C_KV_RANK), ("ckr", C_ROPE),
                       ("gate", N_BRANCH * D_MODEL)):
        parts[name] = w[:, o:o + size]
        o += size
    w16 = jnp.concatenate([parts[n] for n in ("aq", "ak", "av", "bq", "bi", "bg", "gate")], axis=1)
    half = C_ROPE // 2
    x1, x2 = parts["ckr"][:, :half], parts["ckr"][:, half:]
    zl = jnp.zeros((w.shape[0], C_NOPE), w.dtype)
    zr = jnp.zeros((w.shape[0], C_SLOT - C_NOPE - C_ROPE), w.dtype)
    kr_slot = jnp.concatenate([zl, x1, x2, zr], axis=1)
    kr_sw = jnp.concatenate([zl, -x2, x1, zr], axis=1)
    w32 = jnp.concatenate([parts["cq"], parts["ckv"], kr_slot, kr_sw, parts["bf"]], axis=1)
    return w16.astype(BF16), w32.astype(BF16)


def _prep_w_mla(w_uq, w_ukv):
    half = C_ROPE // 2
    per_q = C_NOPE + C_ROPE
    wq = w_uq.reshape(C_Q_RANK, N_HEADS, per_q)
    nope, x1, x2 = wq[..., :C_NOPE], wq[..., C_NOPE:C_NOPE + half], wq[..., C_NOPE + half:]
    zpad = jnp.zeros((C_Q_RANK, N_HEADS, C_SLOT - per_q), w_uq.dtype)
    wq1 = jnp.concatenate([nope, x1, x2, zpad], axis=-1).reshape(C_Q_RANK, N_HEADS * C_SLOT)
    wq2 = jnp.concatenate([jnp.zeros_like(nope), -x2, x1, zpad], axis=-1).reshape(C_Q_RANK, N_HEADS * C_SLOT)
    wkv = w_ukv.reshape(C_KV_RANK, N_HEADS, C_NOPE + HEAD_DIM)
    k_nope, v = wkv[..., :C_NOPE], wkv[..., C_NOPE:]
    wk = jnp.concatenate([k_nope, jnp.zeros((C_KV_RANK, N_HEADS, C_SLOT - C_NOPE), w_ukv.dtype)],
                         axis=-1).reshape(C_KV_RANK, N_HEADS * C_SLOT)
    wv = v.reshape(C_KV_RANK, WIDTH)
    return wq1.T.astype(BF16), wq2.T.astype(BF16), wk.astype(BF16), wv.T.astype(BF16)


def kernel(x, positions, norm_mix_g, w_in, rel_bias, hgrn_lb_logits, hgrn_norm_g, mla_q_norm_g,
           mla_kv_norm_g, mla_w_uq, mla_w_ukv, w_branch, w_out, norm_ffn_g, w_ff1, w_ff2, final_norm_g):
    bsz, seq, d = x.shape
    depth = w_in.shape[0]
    t = bsz * seq
    assert d == D_MODEL and seq % C_TILE == 0 and seq % (A_QTILE * A_TILES_PER_STEP) == 0
    tm = min(TM_DENSE, t)

    p_lb = jax.nn.softmax(hgrn_lb_logits.astype(F32), axis=0)
    lb_all = jnp.cumsum(p_lb, axis=0)
    lb_all = lb_all - lb_all[0:1]

    tabs = rope_tables(positions, tm)
    xf = x.reshape(t, d)
    for l in range(depth):
        w16, w32 = _prep_w_in(w_in[l])
        wq1t, wq2t, wk, wvt = _prep_w_mla(mla_w_uq[l], mla_w_ukv[l])
        proj16, y_b, qt_c, k_c, vt_c = in_proj(xf, norm_mix_g[l], w16, w32, tabs, mla_q_norm_g[l], mla_kv_norm_g[l],
                                               wq1t, wq2t, wk, wvt, lb_all[l], hgrn_norm_g[l], seq,
                                               min(TM_IN_PROJ, seq), TN_IN_PROJ)

        y_a = band_attention(proj16, band_bias_table(rel_bias[l]), bsz, seq, COL_AQ, COL_AK, COL_AV)
        y_c = mla_flash(qt_c, k_c, vt_c, bsz, seq)

        xf = merge_out(xf, y_a, y_b, y_c, proj16, GATE_BLK, w_branch[l].astype(BF16),
                       w_out[l].astype(BF16), tm)
        xf = ffn(xf, norm_ffn_g[l], w_ff1[l].astype(BF16), w_ff2[l].astype(BF16), final_norm_g,
                 l == depth - 1, tm, TF_FFN)
    return xf.reshape(bsz, seq, d)
```

```python
import functools

import jax
import jax.numpy as jnp
from jax import lax
from jax.experimental import pallas as pl
from jax.experimental.pallas import tpu as pltpu

F32 = jnp.float32
BF16 = jnp.bfloat16

D_MODEL = 1024
CHUNK = 64
EPS = 1e-6
N_HEADS = 8
HEAD_DIM = 64
LANES = 128
N_PAIRS = N_HEADS * HEAD_DIM // LANES
WIDTH = N_HEADS * HEAD_DIM

A_LEFT_CHUNKS = 8
A_MAX_REL = 128
A_QTILE = 2 * CHUNK
A_BAND = (A_LEFT_CHUNKS + 2) * CHUNK
A_PAD = A_LEFT_CHUNKS * CHUNK
A_TILES_PER_STEP = 8

C_Q_RANK = 256
C_KV_RANK = 128
C_ROPE = 32
C_NOPE = 64
C_SLOT = LANES
C_LATENT = C_Q_RANK + C_KV_RANK + 2 * C_SLOT
ROPE_BASE = 10000.0
C_TILE = 512
ONES_ROWS = 16
LOG2E = 1.4426950408889634
C_QSCALE = (C_NOPE + C_ROPE) ** -0.5 * LOG2E

N_BRANCH = 3

SUB = 8
SUP = 2 * SUB
N_SUB = CHUNK // SUB
NEG_BIG = -1e30

VMEM_LIMIT = 48 * 1024 * 1024
VMEM_LIMIT_IN_PROJ = 58 * 1024 * 1024

TM_IN_PROJ = 512
TN_IN_PROJ = 1024
TM_DENSE = 1024
TF_FFN = 1024

_NT = (((1,), (1,)), ((), ()))


def _cparams(sem, vmem_limit=VMEM_LIMIT):
    return pltpu.CompilerParams(dimension_semantics=sem, vmem_limit_bytes=vmem_limit)


def _resident(a):
    return pl.BlockSpec(a.shape, lambda *_: (0,) * a.ndim, pipeline_mode=pl.Buffered(1))


def _rms(x, g):
    return x * lax.rsqrt(jnp.mean(x * x, axis=-1, keepdims=True) + EPS) * g


def _in_proj_kernel(x_ref, g_ref, wa_ref, wb_ref, ct_ref, st_ref, ctt_ref, stt_ref, gq_ref, gkv_ref,
                    wq1t_ref, wq2t_ref, wk_ref, wvt_ref, loglb_ref, log1mlb_ref, ng_ref,
                    oa_ref, yb_ref, qt_ref, k_ref, vt_ref, state_ref, *, tn, tiles_per_seq):
    @pl.when(pl.program_id(0) % tiles_per_seq == 0)
    def _():
        state_ref[...] = jnp.zeros_like(state_ref)

    h = _rms(x_ref[...], g_ref[...]).astype(BF16)

    def proj_tile(j):
        cols = slice(j * tn, (j + 1) * tn)
        t = jnp.dot(h, wa_ref[:, cols], preferred_element_type=F32)
        oa_ref[:, cols] = t.astype(oa_ref.dtype)
        return t

    tq = proj_tile(COL_BQ * LANES // tn)
    tig = proj_tile(COL_BI * LANES // tn)
    pb = jnp.dot(h, wb_ref[...], preferred_element_type=F32)
    q_off = COL_BQ * LANES % tn
    i_off = COL_BI * LANES % tn
    g_off = COL_BG * LANES - (COL_BI * LANES // tn) * tn
    y_out = []
    gens = []
    for p in range(N_PAIRS):
        lanes = lambda off: slice(off + p * LANES, off + (p + 1) * LANES)
        gens.append(_hgrn_stages(pb[:, lanes(C_LATENT)], tq[:, lanes(q_off)], tig[:, lanes(i_off)],
                                 tig[:, lanes(g_off)], loglb_ref[p], log1mlb_ref[p], ng_ref[...],
                                 state_ref.at[p], y_out))

    qn = _rms(pb[:, :C_Q_RANK], gq_ref[...]).astype(BF16)
    kvn = _rms(pb[:, C_Q_RANK:C_Q_RANK + C_KV_RANK], gkv_ref[...]).astype(BF16)
    kr = pb[:, C_Q_RANK + C_KV_RANK:C_Q_RANK + C_KV_RANK + C_SLOT]
    kr_sw = pb[:, C_Q_RANK + C_KV_RANK + C_SLOT:C_LATENT]
    kr_rot = kr * ct_ref[...] + kr_sw * st_ref[...]

    def mla_q():
        a_t = lax.dot_general(wq1t_ref[...], qn, _NT, preferred_element_type=F32)
        b_t = lax.dot_general(wq2t_ref[...], qn, _NT, preferred_element_type=F32)
        ctt = ctt_ref[...]
        stt = stt_ref[...]
        for hd in range(N_HEADS):
            sl = slice(hd * C_SLOT, (hd + 1) * C_SLOT)
            qt_ref[sl, :] = (a_t[sl, :] * ctt + b_t[sl, :] * stt).astype(qt_ref.dtype)

    def mla_kv():
        kn = jnp.dot(kvn, wk_ref[...], preferred_element_type=F32)
        vt_ref[...] = lax.dot_general(wvt_ref[...], kvn, _NT, preferred_element_type=F32).astype(vt_ref.dtype)
        for hd in range(N_HEADS):
            sl = slice(hd * C_SLOT, (hd + 1) * C_SLOT)
            k_ref[:, sl] = (kn[:, sl] + kr_rot).astype(k_ref.dtype)

    done = {COL_BQ * LANES // tn, COL_BI * LANES // tn}
    big = [functools.partial(proj_tile, j) for j in range(wa_ref.shape[1] // tn) if j not in done] + [mla_q, mla_kv]
    assert len(big) == HGRN_STAGES
    for work in big:
        work()
        for gen in gens:
            next(gen)
    for p in range(N_PAIRS):
        yb_ref[:, p * LANES:(p + 1) * LANES] = y_out[p].astype(yb_ref.dtype)


def in_proj(x, g, wa, wb, tabs, gq, gkv, wq1t, wq2t, wk, wvt, lb, hgrn_norm_g, seq, tm, tn):
    t, d = x.shape
    na = wa.shape[1]
    ctab, stab, ctab_t, stab_t = tabs
    gq = gq.reshape(1, -1)
    gkv = gkv.reshape(1, -1)
    log_lb = jnp.log(lb).reshape(N_PAIRS, 1, LANES)
    log1m_lb = jnp.log1p(-lb).reshape(N_PAIRS, 1, LANES)
    ng = jnp.tile(hgrn_norm_g.astype(F32), LANES // HEAD_DIM).reshape(1, LANES)
    rows = lambda n: pl.BlockSpec((tm, n), lambda i: (i, 0))
    cols = lambda n: pl.BlockSpec((n, tm), lambda i: (0, i))
    return pl.pallas_call(
        functools.partial(_in_proj_kernel, tn=tn, tiles_per_seq=seq // tm),
        out_shape=(jax.ShapeDtypeStruct((t, na), BF16),
                   jax.ShapeDtypeStruct((t, WIDTH), BF16),
                   jax.ShapeDtypeStruct((N_HEADS * C_SLOT, t), BF16),
                   jax.ShapeDtypeStruct((t, N_HEADS * C_SLOT), BF16),
                   jax.ShapeDtypeStruct((WIDTH, t), BF16)),
        grid=(t // tm,),
        in_specs=[rows(d), pl.BlockSpec((1, d), lambda i: (0, 0)), _resident(wa), _resident(wb),
                  rows(LANES), rows(LANES), cols(LANES), cols(LANES),
                  _resident(gq), _resident(gkv), _resident(wq1t), _resident(wq2t), _resident(wk), _resident(wvt),
                  _resident(log_lb), _resident(log1m_lb), _resident(ng)],
        out_specs=(rows(na), rows(WIDTH), cols(N_HEADS * C_SLOT), rows(N_HEADS * C_SLOT), cols(WIDTH)),
        scratch_shapes=[pltpu.VMEM((N_PAIRS, LANES, LANES), F32)],
        compiler_params=_cparams(("arbitrary",), VMEM_LIMIT_IN_PROJ),
        name="in_proj",
    )(x, g.reshape(1, d), wa, wb, ctab, stab, ctab_t, stab_t, gq, gkv, wq1t, wq2t, wk, wvt, log_lb, log1m_lb, ng)


def _trig_kernel(pos_ref, invf_ref, c_ref, s_ref, ct_ref, st_ref):
    ang = pos_ref[...].astype(F32) * invf_ref[...]
    c = jnp.cos(ang)
    s = jnp.sin(ang)
    c_ref[...] = c
    s_ref[...] = s
    ct_ref[...] = c.T
    st_ref[...] = s.T


def rope_tables(positions, tm):
    t = positions.size
    inv_freq = ROPE_BASE ** (-jnp.arange(0, C_ROPE, 2, dtype=F32) / C_ROPE)
    half = C_ROPE // 2
    invf = jnp.zeros((LANES,), F32)
    invf = invf.at[C_NOPE:C_NOPE + half].set(inv_freq).at[C_NOPE + half:C_NOPE + C_ROPE].set(inv_freq)
    return pl.pallas_call(
        _trig_kernel,
        out_shape=(jax.ShapeDtypeStruct((t, LANES), F32),) * 2 + (jax.ShapeDtypeStruct((LANES, t), F32),) * 2,
        grid=(t // tm,),
        in_specs=[
            pl.BlockSpec((tm, 1), lambda i: (i, 0)),
            pl.BlockSpec((1, LANES), lambda i: (0, 0)),
        ],
        out_specs=(pl.BlockSpec((tm, LANES), lambda i: (i, 0)),) * 2
        + (pl.BlockSpec((LANES, tm), lambda i: (0, i)),) * 2,
        compiler_params=_cparams(("parallel",)),
        name="rope_tables",
    )(positions.reshape(t, 1), invf.reshape(1, LANES))


def _band_attn_kernel(q_ref, k_ref, v_ref, bias_ref, o_ref):
    lane = lax.broadcasted_iota(jnp.int32, (A_QTILE, LANES), 1)
    n_shift = A_PAD // A_QTILE
    starts, scores = [], []
    for t in range(A_TILES_PER_STEP):
        c2 = pl.program_id(2) * A_TILES_PER_STEP + t
        start = pl.multiple_of(jnp.maximum(c2 * A_QTILE - A_PAD, 0), A_QTILE)
        shift = jnp.minimum(c2, n_shift)
        kwin = k_ref[pl.ds(start, A_BAND), :]
        q = q_ref[t * A_QTILE:(t + 1) * A_QTILE, :].astype(F32) * (HEAD_DIM ** -0.5)
        qst = jnp.concatenate([jnp.where(lane < HEAD_DIM, q, 0.0), jnp.where(lane < HEAD_DIM, 0.0, q)],
                              axis=0).astype(BF16)
        s = lax.dot_general(qst, kwin, _NT, preferred_element_type=F32)
        starts.append(start)
        scores.append(s + bias_ref[shift, 0])
    for t in range(A_TILES_PER_STEP):
        s = scores[t]
        vwin = v_ref[pl.ds(starts[t], A_BAND), :]
        m = jnp.max(s, axis=-1, keepdims=True)
        p = jnp.exp(s - m)
        l = jnp.sum(p, axis=-1, keepdims=True)
        pv = jnp.dot(p.astype(BF16), vwin, preferred_element_type=F32) / l
        o_ref[t * A_QTILE:(t + 1) * A_QTILE, :] = jnp.where(lane < HEAD_DIM, pv[:A_QTILE], pv[A_QTILE:]).astype(o_ref.dtype)


def band_attention(proj, bias, bsz, seq, q_col, k_col, v_col):
    t = bsz * seq
    rows = A_QTILE * A_TILES_PER_STEP
    n_steps = seq // rows
    n_shift = A_PAD // A_QTILE
    return pl.pallas_call(
        _band_attn_kernel,
        out_shape=jax.ShapeDtypeStruct((t, WIDTH), BF16),
        grid=(bsz, N_PAIRS, n_steps),
        in_specs=[
            pl.BlockSpec((rows, LANES), lambda b, p, c: (b * n_steps + c, q_col + p)),
            pl.BlockSpec((seq, LANES), lambda b, p, c: (b, k_col + p)),
            pl.BlockSpec((seq, LANES), lambda b, p, c: (b, v_col + p)),
            pl.BlockSpec((n_shift + 1, 1, 2 * A_QTILE, A_BAND), lambda b, p, c: (0, p, 0, 0)),
        ],
        out_specs=pl.BlockSpec((rows, LANES), lambda b, p, c: (b * n_steps + c, p)),
        compiler_params=_cparams(("parallel", "parallel", "arbitrary")),
        name="band_attention",
    )(proj, proj, proj, bias.reshape(n_shift + 1, N_PAIRS, 2 * A_QTILE, A_BAND))


A_ROLL = 768


def _band_bias_kernel(r_ref, o_ref):
    d = pl.program_id(0) * A_QTILE
    prof = jnp.broadcast_to(r_ref[...], (A_QTILE, A_ROLL))
    b = pltpu.roll(prof, 0, 1, stride=1, stride_axis=0)[:, :A_BAND]
    i = lax.broadcasted_iota(jnp.int32, (A_QTILE, A_BAND), 0)
    j = lax.broadcasted_iota(jnp.int32, (A_QTILE, A_BAND), 1)
    gap = (d + i) // CHUNK - j // CHUNK
    o_ref[...] = jnp.where((gap >= 0) & (gap <= A_LEFT_CHUNKS), b, NEG_BIG)


def band_bias_table(rel_table):
    n_shift = A_PAD // A_QTILE
    heads = rel_table.shape[0]
    pad = A_ROLL + A_PAD
    ext = jnp.pad(rel_table.astype(F32)[:, ::-1], ((0, 0), (pad, pad)), mode="edge")

    def seg(d, u0, n):
        o = u0 - d + A_MAX_REL + pad
        return ext[:, o:o + n]

    n_neg = A_ROLL - (A_BAND + 1)
    prof = jnp.stack([jnp.concatenate([seg(s * A_QTILE, 0, A_BAND + 1), seg(s * A_QTILE, -n_neg, n_neg)], axis=1)
                      for s in range(n_shift + 1)])
    return pl.pallas_call(
        _band_bias_kernel,
        out_shape=jax.ShapeDtypeStruct((n_shift + 1, heads, A_QTILE, A_BAND), F32),
        grid=(n_shift + 1, heads),
        in_specs=[pl.BlockSpec((None, None, 1, A_ROLL), lambda s, h: (s, h, 0, 0))],
        out_specs=pl.BlockSpec((None, None, A_QTILE, A_BAND), lambda s, h: (s, h, 0, 0)),
        compiler_params=_cparams(("parallel", "parallel")),
        name="band_bias",
    )(prof.reshape(n_shift + 1, heads, 1, A_ROLL))


def _split3(x):
    hi = x.astype(BF16)
    r1 = x - hi.astype(F32)
    mid = r1.astype(BF16)
    lo = (r1 - mid.astype(F32)).astype(BF16)
    return hi, mid, lo


HGRN_STAGES = 6


def _hgrn_stages(z, qraw, v, graw, log_lb, log1m_lb, norm_g, st_ref, y_out):
    rows = z.shape[0]
    n_chunks = rows // CHUNK
    n_blk = rows // SUB

    head0 = lax.broadcasted_iota(jnp.int32, (rows, LANES), 1) < HEAD_DIM
    head0_c = lax.broadcasted_iota(jnp.int32, (CHUNK, LANES), 1) < HEAD_DIM
    r64 = lax.broadcasted_iota(jnp.int32, (CHUNK, CHUNK), 0)
    c64 = lax.broadcasted_iota(jnp.int32, (CHUNK, CHUNK), 1)
    tril = (c64 <= r64).astype(BF16)
    c64s = lax.broadcasted_iota(jnp.int32, (2 * CHUNK, CHUNK), 1)
    rl = lax.broadcasted_iota(jnp.int32, (LANES, LANES), 0)
    cl = lax.broadcasted_iota(jnp.int32, (LANES, LANES), 1)
    same_head = (rl // HEAD_DIM) == (cl // HEAD_DIM)
    bd = same_head.astype(BF16)
    ri = lax.broadcasted_iota(jnp.int32, (CHUNK, CHUNK * SUB), 0)
    ci = lax.broadcasted_iota(jnp.int32, (CHUNK, CHUNK * SUB), 1)
    pick = (ci // SUB == ri).astype(BF16)
    row_s = lax.broadcasted_iota(jnp.int32, (SUB, LANES), 0)

    def chunk(x, c):
        return x[c * CHUNK:(c + 1) * CHUNK]

    log_sig = jnp.minimum(z, 0.0) - jnp.log(1.0 + jnp.exp(-jnp.abs(z)))
    bterm = log1m_lb + log_sig
    log_f = jnp.maximum(log_lb, bterm) + jnp.log(1.0 + jnp.exp(-jnp.abs(log_lb - bterm)))
    log_k = bterm - z
    qs = qraw * (1.0 / (1.0 + jnp.exp(-qraw)))

    x3 = jnp.concatenate(_split3(log_f), axis=1)
    cum3 = [jnp.dot(tril, chunk(x3, c), preferred_element_type=F32) for c in range(n_chunks)]
    yield
    cum = jnp.concatenate([t[:, :LANES] + t[:, LANES:2 * LANES] + t[:, 2 * LANES:] for t in cum3], axis=0)

    zero_row = jnp.zeros((1, LANES), F32)
    c_end = [cum[SUB * b + SUB - 1:SUB * b + SUB, :] for b in range(n_blk)]
    c_start = [zero_row if b % N_SUB == 0 else c_end[b - 1] for b in range(n_blk)]
    c_last = [c_end[c * N_SUB + N_SUB - 1] for c in range(n_chunks)]

    def rows_of(blocks):
        return jnp.concatenate([jnp.broadcast_to(r, (SUB, LANES)) for r in blocks], axis=0)

    cstart_full = rows_of(c_start)
    cend_full = rows_of(c_end)
    clast_full = jnp.concatenate([jnp.broadcast_to(r, (CHUNK, LANES)) for r in c_last], axis=0)
    q1 = qs * jnp.exp(cum - cstart_full)
    lk = log_k - cum
    k2b = jnp.exp(cend_full + lk).astype(BF16)
    k_end = jnp.exp(clast_full + lk).astype(BF16)
    q_state = (qs * jnp.exp(cum)).astype(BF16)
    vb = v.astype(BF16)

    n_sup = CHUNK // SUP
    ce_sup = [c_end[(SUP // SUB) * (b + 1) - 1] for b in range(rows // SUP)]
    cs_sup = [zero_row if b % n_sup == 0 else ce_sup[b - 1] for b in range(rows // SUP)]

    def rows_of_sup(blocks):
        return jnp.concatenate([jnp.broadcast_to(r, (SUP, LANES)) for r in blocks], axis=0)

    q1s = qs * jnp.exp(cum - rows_of_sup(cs_sup))
    k2s = jnp.exp(rows_of_sup(ce_sup) + lk).astype(BF16)
    odd_sub = (lax.broadcasted_iota(jnp.int32, (rows, LANES), 0) // SUB) % 2 == 1
    q_adj = jnp.where(odd_sub, q1, 0.0)
    q_adj = (jnp.where(head0, q_adj, 0.0).astype(BF16), jnp.where(head0, 0.0, q_adj).astype(BF16))
    q_sup = []
    for j in range(n_sup - 1):
        d_rows = []
        for b in range(rows // SUP):
            if b % n_sup > j:
                d_rows.append(jnp.exp(cs_sup[b] - ce_sup[(b // n_sup) * n_sup + j]))
            else:
                d_rows.append(zero_row)
        qj = q1s * rows_of_sup(d_rows)
        q_sup.append((jnp.where(head0, qj, 0.0).astype(BF16), jnp.where(head0, 0.0, qj).astype(BF16)))
    s_adj, s_sup = [], []
    for c in range(n_chunks):
        qst = jnp.concatenate([chunk(q_adj[0], c), chunk(q_adj[1], c)], axis=0)
        s_adj.append(lax.dot_general(qst, chunk(k2b, c), _NT, preferred_element_type=F32))
        for j in range(n_sup - 1):
            qst = jnp.concatenate([chunk(q_sup[j][0], c), chunk(q_sup[j][1], c)], axis=0)
            s_sup.append(lax.dot_general(qst, chunk(k2s, c), _NT, preferred_element_type=F32))
    yield
    r64s = lax.broadcasted_iota(jnp.int32, (2 * CHUNK, CHUNK), 0) % CHUNK
    prev_sub = c64s // SUB == r64s // SUB - 1
    o_cross = []
    for c in range(n_chunks):
        sc = jnp.where(prev_sub, s_adj[c], 0.0)
        for j in range(n_sup - 1):
            col_in_j = (c64s >= j * SUP) & (c64s < (j + 1) * SUP)
            sc = sc + jnp.where(col_in_j, s_sup[c * (n_sup - 1) + j], 0.0)
        o2 = jnp.dot(sc.astype(BF16), chunk(vb, c), preferred_element_type=F32)
        o_cross.append(jnp.where(head0_c, o2[:CHUNK], o2[CHUNK:]))

    yield
    a = cum - log_k
    w_rows = []
    for b in range(n_blk):
        a_b = a[SUB * b:SUB * (b + 1), :]
        for i in range(SUB):
            r = SUB * b + i
            arg = jnp.where(row_s <= i, cum[r:r + 1, :] - a_b, NEG_BIG)
            w_rows.append(jnp.exp(arg) * qs[r:r + 1, :])
    w_all = jnp.concatenate(w_rows, axis=0).astype(BF16)
    sb = jnp.dot(w_all, bd, preferred_element_type=F32)
    yield
    o_diag = []
    for c in range(n_chunks):
        v_rep = jnp.concatenate([v[SUB * (r // SUB):SUB * (r // SUB + 1), :]
                                 for r in range(c * CHUNK, (c + 1) * CHUNK)], axis=0)
        sb_c = sb[c * CHUNK * SUB:(c + 1) * CHUNK * SUB]
        o_diag.append(jnp.dot(pick, (sb_c * v_rep).astype(BF16), preferred_element_type=F32))

    yield
    upd = [jnp.where(same_head,
                     lax.dot_general(chunk(vb, c), chunk(k_end, c), (((0,), (0,)), ((), ())),
                                     preferred_element_type=F32), 0.0) for c in range(n_chunks)]
    st = st_ref[...]
    o_state = []
    for c in range(n_chunks):
        o_state.append(lax.dot_general(chunk(q_state, c), st.astype(BF16), _NT, preferred_element_type=F32))
        st = st * jnp.exp(c_last[c]) + upd[c]
    st_ref[...] = st

    o = jnp.concatenate([o_cross[c] + o_diag[c] + o_state[c] for c in range(n_chunks)], axis=0)
    osq = o * o
    ms0 = jnp.sum(jnp.where(head0, osq, 0.0), axis=-1, keepdims=True)
    ms1 = jnp.sum(jnp.where(head0, 0.0, osq), axis=-1, keepdims=True)
    ms = jnp.where(head0, ms0, ms1) * (1.0 / HEAD_DIM)
    y = o * lax.rsqrt(ms + EPS) * norm_g
    y_out.append(y * (graw * (1.0 / (1.0 + jnp.exp(-graw)))))
    yield


def _mla_flash_kernel(qt_ref, k_ref, vt_ref, o_ref, acc_ref, m_ref, l_ref, s_ref, *, n_qt):
    c = C_QSCALE

    def scores(qi, j, e):
        qoff = pl.multiple_of(qi * C_TILE, C_TILE)
        koff = pl.multiple_of(j * C_TILE, C_TILE)
        k = k_ref[pl.ds(koff, C_TILE), e * C_SLOT:(e + 1) * C_SLOT]
        qt = qt_ref[e * C_SLOT:(e + 1) * C_SLOT, pl.ds(qoff, C_TILE)]
        return jnp.dot(k, qt, preferred_element_type=F32)

    def softmax_pv(j, e, s):
        off = pl.multiple_of(j * C_TILE, C_TILE)
        m_prev = m_ref[e]
        m_new = jnp.maximum(m_prev, jnp.max(s, axis=0, keepdims=True))
        p = jnp.exp2((s - m_new) * c).astype(BF16)
        alpha = jnp.exp2((m_prev - m_new) * c)
        vt = vt_ref[e * HEAD_DIM:(e + 1) * HEAD_DIM, pl.ds(off, C_TILE)]
        vt_aug = jnp.concatenate([vt, jnp.ones((ONES_ROWS, C_TILE), BF16)], axis=0)
        pv = jnp.dot(vt_aug, p, preferred_element_type=F32)
        l_ref[e] = alpha * l_ref[e] + pv[HEAD_DIM:HEAD_DIM + 1]
        acc_ref[e] = alpha * acc_ref[e] + pv[:HEAD_DIM]
        m_ref[e] = m_new

    kc = lax.broadcasted_iota(jnp.int32, (C_TILE, C_TILE), 0) // CHUNK
    qc = lax.broadcasted_iota(jnp.int32, (C_TILE, C_TILE), 1) // CHUNK

    for e in range(2):
        s_ref[e] = scores(0, 0, e)

    def q_body(qi, carry):
        m_ref[...] = jnp.full_like(m_ref, -jnp.inf)
        l_ref[...] = jnp.zeros_like(l_ref)
        acc_ref[...] = jnp.zeros_like(acc_ref)

        def body(j, carry2):
            for e in range(2):
                s = s_ref[e]
                s_ref[e] = scores(qi, j + 1, e)
                softmax_pv(j, e, s)
            return carry2

        lax.fori_loop(0, qi, body, 0)
        q_next = jnp.minimum(qi + 1, n_qt - 1)
        for e in range(2):
            s = jnp.where(kc <= qc, s_ref[e], NEG_BIG)
            s_ref[e] = scores(q_next, 0, e)
            softmax_pv(qi, e, s)

        o_t = jnp.concatenate([acc_ref[0] / l_ref[0], acc_ref[1] / l_ref[1]], axis=0)
        o_ref[pl.ds(pl.multiple_of(qi * C_TILE, C_TILE), C_TILE), :] = o_t.T.astype(o_ref.dtype)
        return carry

    lax.fori_loop(0, n_qt, q_body, 0)


def mla_flash(qt, k, vt, bsz, seq):
    t = bsz * seq
    return pl.pallas_call(
        functools.partial(_mla_flash_kernel, n_qt=seq // C_TILE),
        out_shape=jax.ShapeDtypeStruct((t, WIDTH), BF16),
        grid=(bsz, N_PAIRS),
        in_specs=[
            pl.BlockSpec((2 * C_SLOT, seq), lambda b, p: (p, b)),
            pl.BlockSpec((seq, 2 * C_SLOT), lambda b, p: (b, p)),
            pl.BlockSpec((LANES, seq), lambda b, p: (p, b)),
        ],
        out_specs=pl.BlockSpec((seq, LANES), lambda b, p: (b, p)),
        scratch_shapes=[pltpu.VMEM((2, HEAD_DIM, C_TILE), F32),
                        pltpu.VMEM((2, 1, C_TILE), F32),
                        pltpu.VMEM((2, 1, C_TILE), F32),
                        pltpu.VMEM((2, C_TILE, C_TILE), F32)],
        compiler_params=_cparams(("parallel", "parallel")),
        name="mla_flash",
    )(qt, k, vt)


def _merge_kernel(x_ref, ya_ref, yb_ref, yc_ref, g_ref, wbr_ref, wout_ref, o_ref):
    merged = None
    for n, y_ref in enumerate((ya_ref, yb_ref, yc_ref)):
        up = jnp.dot(y_ref[...], wbr_ref[n], preferred_element_type=F32)
        gl = g_ref[:, n * D_MODEL:(n + 1) * D_MODEL].astype(F32)
        term = (1.0 / (1.0 + jnp.exp(-gl))) * up
        merged = term if merged is None else merged + term
    o_ref[...] = x_ref[...] + jnp.dot(merged.astype(BF16), wout_ref[...], preferred_element_type=F32)


def merge_out(x, ya, yb, yc, proj16, gate_blk, wbr, wout, tm):
    t = x.shape[0]
    return pl.pallas_call(
        _merge_kernel,
        out_shape=jax.ShapeDtypeStruct((t, D_MODEL), F32),
        grid=(t // tm,),
        in_specs=[
            pl.BlockSpec((tm, D_MODEL), lambda i: (i, 0)),
            pl.BlockSpec((tm, WIDTH), lambda i: (i, 0)),
            pl.BlockSpec((tm, WIDTH), lambda i: (i, 0)),
            pl.BlockSpec((tm, WIDTH), lambda i: (i, 0)),
            pl.BlockSpec((tm, N_BRANCH * D_MODEL), lambda i: (i, gate_blk)),
            _resident(wbr),
            _resident(wout),
        ],
        out_specs=pl.BlockSpec((tm, D_MODEL), lambda i: (i, 0)),
        compiler_params=_cparams(("parallel",)),
        name="merge_out",
    )(x, ya, yb, yc, proj16, wbr, wout)


def _ffn_kernel(x_ref, g_ref, w1_ref, w2_ref, gf_ref, o_ref, *, final_norm, tf):
    x = x_ref[...]
    h = _rms(x, g_ref[...]).astype(BF16)
    acc = None
    for k in range(w1_ref.shape[1] // tf):
        u = jnp.maximum(jnp.dot(h, w1_ref[:, k * tf:(k + 1) * tf], preferred_element_type=F32), 0.0)
        part = jnp.dot((u * u).astype(BF16), w2_ref[k * tf:(k + 1) * tf, :], preferred_element_type=F32)
        acc = part if acc is None else acc + part
    y = x + acc
    if final_norm:
        y = _rms(y, gf_ref[...])
    o_ref[...] = y


def ffn(x, g, w1, w2, gf, final_norm, tm, tf):
    t, d = x.shape
    return pl.pallas_call(
        functools.partial(_ffn_kernel, final_norm=final_norm, tf=tf),
        out_shape=jax.ShapeDtypeStruct((t, d), F32),
        grid=(t // tm,),
        in_specs=[
            pl.BlockSpec((tm, d), lambda i: (i, 0)),
            pl.BlockSpec((1, d), lambda i: (0, 0)),
            _resident(w1),
            _resident(w2),
            pl.BlockSpec((1, d), lambda i: (0, 0)),
        ],
        out_specs=pl.BlockSpec((tm, d), lambda i: (i, 0)),
        compiler_params=_cparams(("parallel",)),
        name="ffn",
    )(x, g.reshape(1, d), w1, w2, gf.reshape(1, d))


COL_AQ, COL_AK, COL_AV = 0, 4, 8
COL_BQ, COL_BI, COL_BG = 12, 16, 20
GATE_BLK = 1


def _prep_w_in(w):
    o = 0
    parts = {}
    for name, size in (("aq", 512), ("ak", 512), ("av", 512), ("bq", 512), ("bf", 512), ("bi", 512),
                       ("bg", 512), ("cq", C_Q_RANK), ("ckv", C_KV_RANK), ("ckr", C_ROPE),
                       ("gate", N_BRANCH * D_MODEL)):
        parts[name] = w[:, o:o + size]
        o += size
    w16 = jnp.concatenate([parts[n] for n in ("aq", "ak", "av", "bq", "bi", "bg", "gate")], axis=1)
    half = C_ROPE // 2
    x1, x2 = parts["ckr"][:, :half], parts["ckr"][:, half:]
    zl = jnp.zeros((w.shape[0], C_NOPE), w.dtype)
    zr = jnp.zeros((w.shape[0], C_SLOT - C_NOPE - C_ROPE), w.dtype)
    kr_slot = jnp.concatenate([zl, x1, x2, zr], axis=1)
    kr_sw = jnp.concatenate([zl, -x2, x1, zr], axis=1)
    w32 = jnp.concatenate([parts["cq"], parts["ckv"], kr_slot, kr_sw, parts["bf"]], axis=1)
    return w16.astype(BF16), w32.astype(BF16)


def _prep_w_mla(w_uq, w_ukv):
    half = C_ROPE // 2
    per_q = C_NOPE + C_ROPE
    wq = w_uq.reshape(C_Q_RANK, N_HEADS, per_q)
    nope, x1, x2 = wq[..., :C_NOPE], wq[..., C_NOPE:C_NOPE + half], wq[..., C_NOPE + half:]
    zpad = jnp.zeros((C_Q_RANK, N_HEADS, C_SLOT - per_q), w_uq.dtype)
    wq1 = jnp.concatenate([nope, x1, x2, zpad], axis=-1).reshape(C_Q_RANK, N_HEADS * C_SLOT)
    wq2 = jnp.concatenate([jnp.zeros_like(nope), -x2, x1, zpad], axis=-1).reshape(C_Q_RANK, N_HEADS * C_SLOT)
    wkv = w_ukv.reshape(C_KV_RANK, N_HEADS, C_NOPE + HEAD_DIM)
    k_nope, v = wkv[..., :C_NOPE], wkv[..., C_NOPE:]
    wk = jnp.concatenate([k_nope, jnp.zeros((C_KV_RANK, N_HEADS, C_SLOT - C_NOPE), w_ukv.dtype)],
                         axis=-1).reshape(C_KV_RANK, N_HEADS * C_SLOT)
    wv = v.reshape(C_KV_RANK, WIDTH)
    return wq1.T.astype(BF16), wq2.T.astype(BF16), wk.astype(BF16), wv.T.astype(BF16)


def kernel(x, positions, norm_mix_g, w_in, rel_bias, hgrn_lb_logits, hgrn_norm_g, mla_q_norm_g,
           mla_kv_norm_g, mla_w_uq, mla_w_ukv, w_branch, w_out, norm_ffn_g, w_ff1, w_ff2, final_norm_g):
    bsz, seq, d = x.shape
    depth = w_in.shape[0]
    t = bsz * seq
    assert d == D_MODEL and seq % C_TILE == 0 and seq % (A_QTILE * A_TILES_PER_STEP) == 0
    tm = min(TM_DENSE, t)

    p_lb = jax.nn.softmax(hgrn_lb_logits.astype(F32), axis=0)
    lb_all = jnp.cumsum(p_lb, axis=0)
    lb_all = lb_all - lb_all[0:1]

    tabs = rope_tables(positions, tm)
    xf = x.reshape(t, d)
    for l in range(depth):
        w16, w32 = _prep_w_in(w_in[l])
        wq1t, wq2t, wk, wvt = _prep_w_mla(mla_w_uq[l], mla_w_ukv[l])
        proj16, y_b, qt_c, k_c, vt_c = in_proj(xf, norm_mix_g[l], w16, w32, tabs, mla_q_norm_g[l], mla_kv_norm_g[l],
                                               wq1t, wq2t, wk, wvt, lb_all[l], hgrn_norm_g[l], seq,
                                               min(TM_IN_PROJ, seq), TN_IN_PROJ)

        y_a = band_attention(proj16, band_bias_table(rel_bias[l]), bsz, seq, COL_AQ, COL_AK, COL_AV)
        y_c = mla_flash(qt_c, k_c, vt_c, bsz, seq)

        xf = merge_out(xf, y_a, y_b, y_c, proj16, GATE_BLK, w_branch[l].astype(BF16),
                       w_out[l].astype(BF16), tm)
        xf = ffn(xf, norm_ffn_g[l], w_ff1[l].astype(BF16), w_ff2[l].astype(BF16), final_norm_g,
                 l == depth - 1, tm, TF_FFN)
    return xf.reshape(bsz, seq, d)
```

```python
import functools

import jax
import jax.numpy as jnp
from jax import lax
from jax.experimental import pallas as pl
from jax.experimental.pallas import tpu as pltpu

F32 = jnp.float32
BF16 = jnp.bfloat16

D_MODEL = 1024
CHUNK = 64
EPS = 1e-6
N_HEADS = 8
HEAD_DIM = 64
LANES = 128
N_PAIRS = N_HEADS * HEAD_DIM // LANES
WIDTH = N_HEADS * HEAD_DIM

A_LEFT_CHUNKS = 8
A_MAX_REL = 128
A_QTILE = 2 * CHUNK
A_BAND = (A_LEFT_CHUNKS + 2) * CHUNK
A_PAD = A_LEFT_CHUNKS * CHUNK
A_TILES_PER_STEP = 16

C_Q_RANK = 256
C_KV_RANK = 128
C_ROPE = 32
C_NOPE = 64
C_SLOT = LANES
C_LATENT = C_Q_RANK + C_KV_RANK + 2 * C_SLOT
ROPE_BASE = 10000.0
C_TILE = 512
ONES_ROWS = 16
LOG2E = 1.4426950408889634
C_QSCALE = (C_NOPE + C_ROPE) ** -0.5 * LOG2E

N_BRANCH = 3

SUB = 8
SUP = 2 * SUB
N_SUB = CHUNK // SUB
NEG_BIG = -1e30

VMEM_LIMIT = 48 * 1024 * 1024
VMEM_LIMIT_IN_PROJ = 58 * 1024 * 1024

TM_IN_PROJ = 512
TN_IN_PROJ = 1024
TM_DENSE = 1024
TF_FFN = 1024

_NT = (((1,), (1,)), ((), ()))


def _cparams(sem, vmem_limit=VMEM_LIMIT):
    return pltpu.CompilerParams(dimension_semantics=sem, vmem_limit_bytes=vmem_limit)


def _resident(a):
    return pl.BlockSpec(a.shape, lambda *_: (0,) * a.ndim, pipeline_mode=pl.Buffered(1))


def _rms(x, g):
    return x * lax.rsqrt(jnp.mean(x * x, axis=-1, keepdims=True) + EPS) * g


def _in_proj_kernel(x_ref, g_ref, wa_ref, wb_ref, ct_ref, st_ref, ctt_ref, stt_ref, gq_ref, gkv_ref,
                    wq1t_ref, wq2t_ref, wk_ref, wvt_ref, loglb_ref, log1mlb_ref, ng_ref,
                    oa_ref, og_ref, yb_ref, qt_ref, k_ref, vt_ref, state_ref, *, tn, tiles_per_seq):
    @pl.when(pl.program_id(0) % tiles_per_seq == 0)
    def _():
        state_ref[...] = jnp.zeros_like(state_ref)

    h = _rms(x_ref[...], g_ref[...]).astype(BF16)

    a_cols = COL_BQ * LANES
    g_col0 = GATE_BLK * N_BRANCH * D_MODEL

    def proj_tile(j):
        lo, hi = j * tn, (j + 1) * tn
        t = jnp.dot(h, wa_ref[:, lo:hi], preferred_element_type=F32)
        if lo < a_cols:
            oa_ref[:, lo:min(hi, a_cols)] = t[:, :min(hi, a_cols) - lo].astype(oa_ref.dtype)
        if hi > g_col0:
            og_ref[:, max(lo, g_col0) - g_col0:hi - g_col0] = t[:, max(lo, g_col0) - lo:].astype(og_ref.dtype)
        return t

    tq = proj_tile(COL_BQ * LANES // tn)
    tig = proj_tile(COL_BI * LANES // tn)
    pb = jnp.dot(h, wb_ref[...], preferred_element_type=F32)
    q_off = COL_BQ * LANES % tn
    i_off = COL_BI * LANES % tn
    g_off = COL_BG * LANES - (COL_BI * LANES // tn) * tn
    y_out = []
    gens = []
    for p in range(N_PAIRS):
        lanes = lambda off: slice(off + p * LANES, off + (p + 1) * LANES)
        gens.append(_hgrn_stages(pb[:, lanes(C_LATENT)], tq[:, lanes(q_off)], tig[:, lanes(i_off)],
                                 tig[:, lanes(g_off)], loglb_ref[p], log1mlb_ref[p], ng_ref[...],
                                 state_ref.at[p], y_out))

    qn = _rms(pb[:, :C_Q_RANK], gq_ref[...]).astype(BF16)
    kvn = _rms(pb[:, C_Q_RANK:C_Q_RANK + C_KV_RANK], gkv_ref[...]).astype(BF16)
    kr = pb[:, C_Q_RANK + C_KV_RANK:C_Q_RANK + C_KV_RANK + C_SLOT]
    kr_sw = pb[:, C_Q_RANK + C_KV_RANK + C_SLOT:C_LATENT]
    kr_rot = kr * ct_ref[...] + kr_sw * st_ref[...]

    def mla_q():
        a_t = lax.dot_general(wq1t_ref[...], qn, _NT, preferred_element_type=F32)
        b_t = lax.dot_general(wq2t_ref[...], qn, _NT, preferred_element_type=F32)
        ctt = ctt_ref[...]
        stt = stt_ref[...]
        for hd in range(N_HEADS):
            sl = slice(hd * C_SLOT, (hd + 1) * C_SLOT)
            qt_ref[sl, :] = (a_t[sl, :] * ctt + b_t[sl, :] * stt).astype(qt_ref.dtype)

    def mla_kv():
        kn = jnp.dot(kvn, wk_ref[...], preferred_element_type=F32)
        vt_ref[...] = lax.dot_general(wvt_ref[...], kvn, _NT, preferred_element_type=F32).astype(vt_ref.dtype)
        for hd in range(N_HEADS):
            sl = slice(hd * C_SLOT, (hd + 1) * C_SLOT)
            k_ref[:, sl] = (kn[:, sl] + kr_rot).astype(k_ref.dtype)

    done = {COL_BQ * LANES // tn, COL_BI * LANES // tn}
    big = [functools.partial(proj_tile, j) for j in range(wa_ref.shape[1] // tn) if j not in done] + [mla_q, mla_kv]
    assert len(big) == HGRN_STAGES
    for work in big:
        work()
        for gen in gens:
            next(gen)
    for p in range(N_PAIRS):
        yb_ref[:, p * LANES:(p + 1) * LANES] = y_out[p].astype(yb_ref.dtype)


def in_proj(x, g, wa, wb, tabs, gq, gkv, wq1t, wq2t, wk, wvt, lb, hgrn_norm_g, seq, tm, tn):
    t, d = x.shape
    na = COL_BQ * LANES
    ng_cols = N_BRANCH * D_MODEL
    ctab, stab, ctab_t, stab_t = tabs
    gq = gq.reshape(1, -1)
    gkv = gkv.reshape(1, -1)
    log_lb = jnp.log(lb).reshape(N_PAIRS, 1, LANES)
    log1m_lb = jnp.log1p(-lb).reshape(N_PAIRS, 1, LANES)
    ng = jnp.tile(hgrn_norm_g.astype(F32), LANES // HEAD_DIM).reshape(1, LANES)
    rows = lambda n: pl.BlockSpec((tm, n), lambda i: (i, 0))
    cols = lambda n: pl.BlockSpec((n, tm), lambda i: (0, i))
    return pl.pallas_call(
        functools.partial(_in_proj_kernel, tn=tn, tiles_per_seq=seq // tm),
        out_shape=(jax.ShapeDtypeStruct((t, na), BF16),
                   jax.ShapeDtypeStruct((t, ng_cols), BF16),
                   jax.ShapeDtypeStruct((t, WIDTH), BF16),
                   jax.ShapeDtypeStruct((N_HEADS * C_SLOT, t), BF16),
                   jax.ShapeDtypeStruct((t, N_HEADS * C_SLOT), BF16),
                   jax.ShapeDtypeStruct((WIDTH, t), BF16)),
        grid=(t // tm,),
        in_specs=[rows(d), pl.BlockSpec((1, d), lambda i: (0, 0)), _resident(wa), _resident(wb),
                  rows(LANES), rows(LANES), cols(LANES), cols(LANES),
                  _resident(gq), _resident(gkv), _resident(wq1t), _resident(wq2t), _resident(wk), _resident(wvt),
                  _resident(log_lb), _resident(log1m_lb), _resident(ng)],
        out_specs=(rows(na), rows(ng_cols), rows(WIDTH), cols(N_HEADS * C_SLOT), rows(N_HEADS * C_SLOT), cols(WIDTH)),
        scratch_shapes=[pltpu.VMEM((N_PAIRS, LANES, LANES), F32)],
        compiler_params=_cparams(("arbitrary",), VMEM_LIMIT_IN_PROJ),
        name="in_proj",
    )(x, g.reshape(1, d), wa, wb, ctab, stab, ctab_t, stab_t, gq, gkv, wq1t, wq2t, wk, wvt, log_lb, log1m_lb, ng)


def _trig_kernel(pos_ref, invf_ref, c_ref, s_ref, ct_ref, st_ref):
    ang = pos_ref[...].astype(F32) * invf_ref[...]
    c = jnp.cos(ang)
    s = jnp.sin(ang)
    c_ref[...] = c
    s_ref[...] = s
    ct_ref[...] = c.T
    st_ref[...] = s.T


def rope_tables(positions, tm):
    t = positions.size
    inv_freq = ROPE_BASE ** (-jnp.arange(0, C_ROPE, 2, dtype=F32) / C_ROPE)
    half = C_ROPE // 2
    invf = jnp.zeros((LANES,), F32)
    invf = invf.at[C_NOPE:C_NOPE + half].set(inv_freq).at[C_NOPE + half:C_NOPE + C_ROPE].set(inv_freq)
    return pl.pallas_call(
        _trig_kernel,
        out_shape=(jax.ShapeDtypeStruct((t, LANES), F32),) * 2 + (jax.ShapeDtypeStruct((LANES, t), F32),) * 2,
        grid=(t // tm,),
        in_specs=[
            pl.BlockSpec((tm, 1), lambda i: (i, 0)),
            pl.BlockSpec((1, LANES), lambda i: (0, 0)),
        ],
        out_specs=(pl.BlockSpec((tm, LANES), lambda i: (i, 0)),) * 2
        + (pl.BlockSpec((LANES, tm), lambda i: (0, i)),) * 2,
        compiler_params=_cparams(("parallel",)),
        name="rope_tables",
    )(positions.reshape(t, 1), invf.reshape(1, LANES))


def _band_attn_kernel(q_ref, k_ref, v_ref, bias_ref, o_ref):
    lane = lax.broadcasted_iota(jnp.int32, (A_QTILE, LANES), 1)
    n_shift = A_PAD // A_QTILE
    starts, scores = [], []
    for t in range(A_TILES_PER_STEP):
        c2 = pl.program_id(2) * A_TILES_PER_STEP + t
        start = pl.multiple_of(jnp.maximum(c2 * A_QTILE - A_PAD, 0), A_QTILE)
        shift = jnp.minimum(c2, n_shift)
        kwin = k_ref[pl.ds(start, A_BAND), :]
        q = q_ref[t * A_QTILE:(t + 1) * A_QTILE, :].astype(F32) * (HEAD_DIM ** -0.5)
        qst = jnp.concatenate([jnp.where(lane < HEAD_DIM, q, 0.0), jnp.where(lane < HEAD_DIM, 0.0, q)],
                              axis=0).astype(BF16)
        s = lax.dot_general(qst, kwin, _NT, preferred_element_type=F32)
        starts.append(start)
        scores.append(s + bias_ref[shift, 0])
    for t in range(A_TILES_PER_STEP):
        s = scores[t]
        vwin = v_ref[pl.ds(starts[t], A_BAND), :]
        m = jnp.max(s, axis=-1, keepdims=True)
        p = jnp.exp(s - m)
        l = jnp.sum(p, axis=-1, keepdims=True)
        pv = jnp.dot(p.astype(BF16), vwin, preferred_element_type=F32) / l
        o_ref[t * A_QTILE:(t + 1) * A_QTILE, :] = jnp.where(lane < HEAD_DIM, pv[:A_QTILE], pv[A_QTILE:]).astype(o_ref.dtype)


def band_attention(proj, bias, bsz, seq, q_col, k_col, v_col):
    t = bsz * seq
    rows = A_QTILE * A_TILES_PER_STEP
    n_steps = seq // rows
    n_shift = A_PAD // A_QTILE
    return pl.pallas_call(
        _band_attn_kernel,
        out_shape=jax.ShapeDtypeStruct((t, WIDTH), BF16),
        grid=(bsz, N_PAIRS, n_steps),
        in_specs=[
            pl.BlockSpec((rows, LANES), lambda b, p, c: (b * n_steps + c, q_col + p)),
            pl.BlockSpec((seq, LANES), lambda b, p, c: (b, k_col + p)),
            pl.BlockSpec((seq, LANES), lambda b, p, c: (b, v_col + p)),
            pl.BlockSpec((n_shift + 1, 1, 2 * A_QTILE, A_BAND), lambda b, p, c: (0, p, 0, 0)),
        ],
        out_specs=pl.BlockSpec((rows, LANES), lambda b, p, c: (b * n_steps + c, p)),
        compiler_params=_cparams(("parallel", "parallel", "arbitrary")),
        name="band_attention",
    )(proj, proj, proj, bias.reshape(n_shift + 1, N_PAIRS, 2 * A_QTILE, A_BAND))


A_ROLL = 768


def _band_bias_kernel(r_ref, o_ref):
    d = pl.program_id(0) * A_QTILE
    prof = jnp.broadcast_to(r_ref[...], (A_QTILE, A_ROLL))
    b = pltpu.roll(prof, 0, 1, stride=1, stride_axis=0)[:, :A_BAND]
    i = lax.broadcasted_iota(jnp.int32, (A_QTILE, A_BAND), 0)
    j = lax.broadcasted_iota(jnp.int32, (A_QTILE, A_BAND), 1)
    gap = (d + i) // CHUNK - j // CHUNK
    o_ref[...] = jnp.where((gap >= 0) & (gap <= A_LEFT_CHUNKS), b, NEG_BIG)


def band_bias_table(rel_table):
    n_shift = A_PAD // A_QTILE
    heads = rel_table.shape[0]
    pad = A_ROLL + A_PAD
    ext = jnp.pad(rel_table.astype(F32)[:, ::-1], ((0, 0), (pad, pad)), mode="edge")

    def seg(d, u0, n):
        o = u0 - d + A_MAX_REL + pad
        return ext[:, o:o + n]

    n_neg = A_ROLL - (A_BAND + 1)
    prof = jnp.stack([jnp.concatenate([seg(s * A_QTILE, 0, A_BAND + 1), seg(s * A_QTILE, -n_neg, n_neg)], axis=1)
                      for s in range(n_shift + 1)])
    return pl.pallas_call(
        _band_bias_kernel,
        out_shape=jax.ShapeDtypeStruct((n_shift + 1, heads, A_QTILE, A_BAND), F32),
        grid=(n_shift + 1, heads),
        in_specs=[pl.BlockSpec((None, None, 1, A_ROLL), lambda s, h: (s, h, 0, 0))],
        out_specs=pl.BlockSpec((None, None, A_QTILE, A_BAND), lambda s, h: (s, h, 0, 0)),
        compiler_params=_cparams(("parallel", "parallel")),
        name="band_bias",
    )(prof.reshape(n_shift + 1, heads, 1, A_ROLL))


def _split3(x):
    hi = x.astype(BF16)
    r1 = x - hi.astype(F32)
    mid = r1.astype(BF16)
    lo = (r1 - mid.astype(F32)).astype(BF16)
    return hi, mid, lo


HGRN_STAGES = 6


def _hgrn_stages(z, qraw, v, graw, log_lb, log1m_lb, norm_g, st_ref, y_out):
    rows = z.shape[0]
    n_chunks = rows // CHUNK
    n_blk = rows // SUB

    head0 = lax.broadcasted_iota(jnp.int32, (rows, LANES), 1) < HEAD_DIM
    head0_c = lax.broadcasted_iota(jnp.int32, (CHUNK, LANES), 1) < HEAD_DIM
    r64 = lax.broadcasted_iota(jnp.int32, (CHUNK, CHUNK), 0)
    c64 = lax.broadcasted_iota(jnp.int32, (CHUNK, CHUNK), 1)
    tril = (c64 <= r64).astype(BF16)
    c64s = lax.broadcasted_iota(jnp.int32, (2 * CHUNK, CHUNK), 1)
    rl = lax.broadcasted_iota(jnp.int32, (LANES, LANES), 0)
    cl = lax.broadcasted_iota(jnp.int32, (LANES, LANES), 1)
    same_head = (rl // HEAD_DIM) == (cl // HEAD_DIM)
    ri = lax.broadcasted_iota(jnp.int32, (CHUNK, CHUNK * SUB), 0)
    ci = lax.broadcasted_iota(jnp.int32, (CHUNK, CHUNK * SUB), 1)
    pick = (ci // SUB == ri).astype(BF16)
    row_s = lax.broadcasted_iota(jnp.int32, (SUB, LANES), 0)

    def chunk(x, c):
        return x[c * CHUNK:(c + 1) * CHUNK]

    log_sig = jnp.minimum(z, 0.0) - jnp.log(1.0 + jnp.exp(-jnp.abs(z)))
    bterm = log1m_lb + log_sig
    log_f = jnp.maximum(log_lb, bterm) + jnp.log(1.0 + jnp.exp(-jnp.abs(log_lb - bterm)))
    log_k = bterm - z
    qs = qraw * (1.0 / (1.0 + jnp.exp(-qraw)))

    x3 = jnp.concatenate(_split3(log_f), axis=1)
    cum3 = [jnp.dot(tril, chunk(x3, c), preferred_element_type=F32) for c in range(n_chunks)]
    yield
    cum = jnp.concatenate([t[:, :LANES] + t[:, LANES:2 * LANES] + t[:, 2 * LANES:] for t in cum3], axis=0)

    zero_row = jnp.zeros((1, LANES), F32)
    c_end = [cum[SUB * b + SUB - 1:SUB * b + SUB, :] for b in range(n_blk)]
    c_start = [zero_row if b % N_SUB == 0 else c_end[b - 1] for b in range(n_blk)]
    c_last = [c_end[c * N_SUB + N_SUB - 1] for c in range(n_chunks)]

    def rows_of(blocks):
        return jnp.concatenate([jnp.broadcast_to(r, (SUB, LANES)) for r in blocks], axis=0)

    cstart_full = rows_of(c_start)
    cend_full = rows_of(c_end)
    clast_full = jnp.concatenate([jnp.broadcast_to(r, (CHUNK, LANES)) for r in c_last], axis=0)
    q1 = qs * jnp.exp(cum - cstart_full)
    lk = log_k - cum
    k2b = jnp.exp(cend_full + lk).astype(BF16)
    k_end = jnp.exp(clast_full + lk).astype(BF16)
    q_state = (qs * jnp.exp(cum)).astype(BF16)
    vb = v.astype(BF16)

    n_sup = CHUNK // SUP
    ce_sup = [c_end[(SUP // SUB) * (b + 1) - 1] for b in range(rows // SUP)]
    cs_sup = [zero_row if b % n_sup == 0 else ce_sup[b - 1] for b in range(rows // SUP)]

    def rows_of_sup(blocks):
        return jnp.concatenate([jnp.broadcast_to(r, (SUP, LANES)) for r in blocks], axis=0)

    q1s = qs * jnp.exp(cum - rows_of_sup(cs_sup))
    k2s = jnp.exp(rows_of_sup(ce_sup) + lk).astype(BF16)
    odd_sub = (lax.broadcasted_iota(jnp.int32, (rows, LANES), 0) // SUB) % 2 == 1
    q_adj = jnp.where(odd_sub, q1, 0.0)
    q_adj = (jnp.where(head0, q_adj, 0.0).astype(BF16), jnp.where(head0, 0.0, q_adj).astype(BF16))
    q_sup = []
    for j in range(n_sup - 1):
        d_rows = []
        for b in range(rows // SUP):
            if b % n_sup > j:
                d_rows.append(jnp.exp(cs_sup[b] - ce_sup[(b // n_sup) * n_sup + j]))
            else:
                d_rows.append(zero_row)
        qj = q1s * rows_of_sup(d_rows)
        q_sup.append((jnp.where(head0, qj, 0.0).astype(BF16), jnp.where(head0, 0.0, qj).astype(BF16)))
    s_adj, s_sup = [], []
    for c in range(n_chunks):
        qst = jnp.concatenate([chunk(q_adj[0], c), chunk(q_adj[1], c)], axis=0)
        s_adj.append(lax.dot_general(qst, chunk(k2b, c), _NT, preferred_element_type=F32))
        for j in range(n_sup - 1):
            qst = jnp.concatenate([chunk(q_sup[j][0], c), chunk(q_sup[j][1], c)], axis=0)
            s_sup.append(lax.dot_general(qst, chunk(k2s, c), _NT, preferred_element_type=F32))
    yield
    r64s = lax.broadcasted_iota(jnp.int32, (2 * CHUNK, CHUNK), 0) % CHUNK
    prev_sub = c64s // SUB == r64s // SUB - 1
    o_cross = []
    for c in range(n_chunks):
        sc = jnp.where(prev_sub, s_adj[c], 0.0)
        for j in range(n_sup - 1):
            col_in_j = (c64s >= j * SUP) & (c64s < (j + 1) * SUP)
            sc = sc + jnp.where(col_in_j, s_sup[c * (n_sup - 1) + j], 0.0)
        o2 = jnp.dot(sc.astype(BF16), chunk(vb, c), preferred_element_type=F32)
        o_cross.append(jnp.where(head0_c, o2[:CHUNK], o2[CHUNK:]))

    yield
    a = cum - log_k
    w_rows = []
    for b in range(n_blk):
        a_b = a[SUB * b:SUB * (b + 1), :]
        for i in range(SUB):
            r = SUB * b + i
            arg = jnp.where(row_s <= i, cum[r:r + 1, :] - a_b, NEG_BIG)
            w_rows.append(jnp.exp(arg) * qs[r:r + 1, :])
    w_all = jnp.concatenate(w_rows, axis=0)
    head0_w = lax.broadcasted_iota(jnp.int32, w_all.shape, 1) < HEAD_DIM
    sum0 = jnp.sum(jnp.where(head0_w, w_all, 0.0), axis=-1, keepdims=True)
    sum1 = jnp.sum(jnp.where(head0_w, 0.0, w_all), axis=-1, keepdims=True)
    sb = jnp.where(head0_w, sum0, sum1)
    yield
    o_diag = []
    for c in range(n_chunks):
        v_rep = jnp.concatenate([v[SUB * (r // SUB):SUB * (r // SUB + 1), :]
                                 for r in range(c * CHUNK, (c + 1) * CHUNK)], axis=0)
        sb_c = sb[c * CHUNK * SUB:(c + 1) * CHUNK * SUB]
        o_diag.append(jnp.dot(pick, (sb_c * v_rep).astype(BF16), preferred_element_type=F32))

    yield
    upd = [jnp.where(same_head,
                     lax.dot_general(chunk(vb, c), chunk(k_end, c), (((0,), (0,)), ((), ())),
                                     preferred_element_type=F32), 0.0) for c in range(n_chunks)]
    st = st_ref[...]
    o_state = []
    for c in range(n_chunks):
        o_state.append(lax.dot_general(chunk(q_state, c), st.astype(BF16), _NT, preferred_element_type=F32))
        st = st * jnp.exp(c_last[c]) + upd[c]
    st_ref[...] = st

    o = jnp.concatenate([o_cross[c] + o_diag[c] + o_state[c] for c in range(n_chunks)], axis=0)
    osq = o * o
    ms0 = jnp.sum(jnp.where(head0, osq, 0.0), axis=-1, keepdims=True)
    ms1 = jnp.sum(jnp.where(head0, 0.0, osq), axis=-1, keepdims=True)
    ms = jnp.where(head0, ms0, ms1) * (1.0 / HEAD_DIM)
    y = o * lax.rsqrt(ms + EPS) * norm_g
    y_out.append(y * (graw * (1.0 / (1.0 + jnp.exp(-graw)))))
    yield


def _mla_flash_kernel(qt_ref, k_ref, vt_ref, o_ref, acc_ref, m_ref, l_ref, s_ref, *, n_qt):
    c = C_QSCALE

    def scores(qi, j, e):
        qoff = pl.multiple_of(qi * C_TILE, C_TILE)
        koff = pl.multiple_of(j * C_TILE, C_TILE)
        k = k_ref[pl.ds(koff, C_TILE), e * C_SLOT:(e + 1) * C_SLOT]
        qt = qt_ref[e * C_SLOT:(e + 1) * C_SLOT, pl.ds(qoff, C_TILE)]
        return jnp.dot(k, qt, preferred_element_type=F32)

    def softmax_pv(j, e, s):
        off = pl.multiple_of(j * C_TILE, C_TILE)
        m_prev = m_ref[e]
        m_new = jnp.maximum(m_prev, jnp.max(s, axis=0, keepdims=True))
        p = jnp.exp2((s - m_new) * c).astype(BF16)
        alpha = jnp.exp2((m_prev - m_new) * c)
        vt = vt_ref[e * HEAD_DIM:(e + 1) * HEAD_DIM, pl.ds(off, C_TILE)]
        vt_aug = jnp.concatenate([vt, jnp.ones((ONES_ROWS, C_TILE), BF16)], axis=0)
        pv = jnp.dot(vt_aug, p, preferred_element_type=F32)
        l_ref[e] = alpha * l_ref[e] + pv[HEAD_DIM:HEAD_DIM + 1]
        acc_ref[e] = alpha * acc_ref[e] + pv[:HEAD_DIM]
        m_ref[e] = m_new

    kc = lax.broadcasted_iota(jnp.int32, (C_TILE, C_TILE), 0) // CHUNK
    qc = lax.broadcasted_iota(jnp.int32, (C_TILE, C_TILE), 1) // CHUNK

    for e in range(2):
        s_ref[e] = scores(0, 0, e)

    def q_body(qi, carry):
        m_ref[...] = jnp.full_like(m_ref, -jnp.inf)
        l_ref[...] = jnp.zeros_like(l_ref)
        acc_ref[...] = jnp.zeros_like(acc_ref)

        def body(j, carry2):
            for e in range(2):
                s = s_ref[e]
                s_ref[e] = scores(qi, j + 1, e)
                softmax_pv(j, e, s)
            return carry2

        lax.fori_loop(0, qi, body, 0)
        q_next = jnp.minimum(qi + 1, n_qt - 1)
        for e in range(2):
            s = jnp.where(kc <= qc, s_ref[e], NEG_BIG)
            s_ref[e] = scores(q_next, 0, e)
            softmax_pv(qi, e, s)

        o_t = jnp.concatenate([acc_ref[0] / l_ref[0], acc_ref[1] / l_ref[1]], axis=0)
        o_ref[pl.ds(pl.multiple_of(qi * C_TILE, C_TILE), C_TILE), :] = o_t.T.astype(o_ref.dtype)
        return carry

    lax.fori_loop(0, n_qt, q_body, 0)


def mla_flash(qt, k, vt, bsz, seq):
    t = bsz * seq
    return pl.pallas_call(
        functools.partial(_mla_flash_kernel, n_qt=seq // C_TILE),
        out_shape=jax.ShapeDtypeStruct((t, WIDTH), BF16),
        grid=(bsz, N_PAIRS),
        in_specs=[
            pl.BlockSpec((2 * C_SLOT, seq), lambda b, p: (p, b)),
            pl.BlockSpec((seq, 2 * C_SLOT), lambda b, p: (b, p)),
            pl.BlockSpec((LANES, seq), lambda b, p: (p, b)),
        ],
        out_specs=pl.BlockSpec((seq, LANES), lambda b, p: (b, p)),
        scratch_shapes=[pltpu.VMEM((2, HEAD_DIM, C_TILE), F32),
                        pltpu.VMEM((2, 1, C_TILE), F32),
                        pltpu.VMEM((2, 1, C_TILE), F32),
                        pltpu.VMEM((2, C_TILE, C_TILE), F32)],
        compiler_params=_cparams(("parallel", "parallel")),
        name="mla_flash",
    )(qt, k, vt)


def _merge_kernel(x_ref, ya_ref, yb_ref, yc_ref, g_ref, wbr_ref, wout_ref, o_ref):
    merged = None
    for n, y_ref in enumerate((ya_ref, yb_ref, yc_ref)):
        up = jnp.dot(y_ref[...], wbr_ref[n], preferred_element_type=F32)
        gl = g_ref[:, n * D_MODEL:(n + 1) * D_MODEL].astype(F32)
        term = (1.0 / (1.0 + jnp.exp(-gl))) * up
        merged = term if merged is None else merged + term
    o_ref[...] = x_ref[...] + jnp.dot(merged.astype(BF16), wout_ref[...], preferred_element_type=F32)


def merge_out(x, ya, yb, yc, gates, wbr, wout, tm):
    t = x.shape[0]
    return pl.pallas_call(
        _merge_kernel,
        out_shape=jax.ShapeDtypeStruct((t, D_MODEL), F32),
        grid=(t // tm,),
        in_specs=[
            pl.BlockSpec((tm, D_MODEL), lambda i: (i, 0)),
            pl.BlockSpec((tm, WIDTH), lambda i: (i, 0)),
            pl.BlockSpec((tm, WIDTH), lambda i: (i, 0)),
            pl.BlockSpec((tm, WIDTH), lambda i: (i, 0)),
            pl.BlockSpec((tm, N_BRANCH * D_MODEL), lambda i: (i, 0)),
            _resident(wbr),
            _resident(wout),
        ],
        out_specs=pl.BlockSpec((tm, D_MODEL), lambda i: (i, 0)),
        compiler_params=_cparams(("parallel",)),
        name="merge_out",
    )(x, ya, yb, yc, gates, wbr, wout)


def _ffn_kernel(x_ref, g_ref, w1_ref, w2_ref, gf_ref, o_ref, *, final_norm, tf):
    x = x_ref[...]
    h = _rms(x, g_ref[...]).astype(BF16)
    acc = None
    for k in range(w1_ref.shape[1] // tf):
        u = jnp.maximum(jnp.dot(h, w1_ref[:, k * tf:(k + 1) * tf], preferred_element_type=F32), 0.0)
        part = jnp.dot((u * u).astype(BF16), w2_ref[k * tf:(k + 1) * tf, :], preferred_element_type=F32)
        acc = part if acc is None else acc + part
    y = x + acc
    if final_norm:
        y = _rms(y, gf_ref[...])
    o_ref[...] = y


def ffn(x, g, w1, w2, gf, final_norm, tm, tf):
    t, d = x.shape
    return pl.pallas_call(
        functools.partial(_ffn_kernel, final_norm=final_norm, tf=tf),
        out_shape=jax.ShapeDtypeStruct((t, d), F32),
        grid=(t // tm,),
        in_specs=[
            pl.BlockSpec((tm, d), lambda i: (i, 0)),
            pl.BlockSpec((1, d), lambda i: (0, 0)),
            _resident(w1),
            _resident(w2),
            pl.BlockSpec((1, d), lambda i: (0, 0)),
        ],
        out_specs=pl.BlockSpec((tm, d), lambda i: (i, 0)),
        compiler_params=_cparams(("parallel",)),
        name="ffn",
    )(x, g.reshape(1, d), w1, w2, gf.reshape(1, d))


COL_AQ, COL_AK, COL_AV = 0, 4, 8
COL_BQ, COL_BI, COL_BG = 12, 16, 20
GATE_BLK = 1


def _prep_w_in(w):
    o = 0
    parts = {}
    for name, size in (("aq", 512), ("ak", 512), ("av", 512), ("bq", 512), ("bf", 512), ("bi", 512),
                       ("bg", 512), ("cq", C_Q_RANK), ("ckv", C_KV_RANK), ("ckr", C_ROPE),
                       ("gate", N_BRANCH * D_MODEL)):
        parts[name] = w[:, o:o + size]
        o += size
    w16 = jnp.concatenate([parts[n] for n in ("aq", "ak", "av", "bq", "bi", "bg", "gate")], axis=1)
    half = C_ROPE // 2
    x1, x2 = parts["ckr"][:, :half], parts["ckr"][:, half:]
    zl = jnp.zeros((w.shape[0], C_NOPE), w.dtype)
    zr = jnp.zeros((w.shape[0], C_SLOT - C_NOPE - C_ROPE), w.dtype)
    kr_slot = jnp.concatenate([zl, x1, x2, zr], axis=1)
    kr_sw = jnp.concatenate([zl, -x2, x1, zr], axis=1)
    w32 = jnp.concatenate([parts["cq"], parts["ckv"], kr_slot, kr_sw, parts["bf"]], axis=1)
    return w16.astype(BF16), w32.astype(BF16)


def _prep_w_mla(w_uq, w_ukv):
    half = C_ROPE // 2
    per_q = C_NOPE + C_ROPE
    wq = w_uq.reshape(C_Q_RANK, N_HEADS, per_q)
    nope, x1, x2 = wq[..., :C_NOPE], wq[..., C_NOPE:C_NOPE + half], wq[..., C_NOPE + half:]
    zpad = jnp.zeros((C_Q_RANK, N_HEADS, C_SLOT - per_q), w_uq.dtype)
    wq1 = jnp.concatenate([nope, x1, x2, zpad], axis=-1).reshape(C_Q_RANK, N_HEADS * C_SLOT)
    wq2 = jnp.concatenate([jnp.zeros_like(nope), -x2, x1, zpad], axis=-1).reshape(C_Q_RANK, N_HEADS * C_SLOT)
    wkv = w_ukv.reshape(C_KV_RANK, N_HEADS, C_NOPE + HEAD_DIM)
    k_nope, v = wkv[..., :C_NOPE], wkv[..., C_NOPE:]
    wk = jnp.concatenate([k_nope, jnp.zeros((C_KV_RANK, N_HEADS, C_SLOT - C_NOPE), w_ukv.dtype)],
                         axis=-1).reshape(C_KV_RANK, N_HEADS * C_SLOT)
    wv = v.reshape(C_KV_RANK, WIDTH)
    return wq1.T.astype(BF16), wq2.T.astype(BF16), wk.astype(BF16), wv.T.astype(BF16)


def kernel(x, positions, norm_mix_g, w_in, rel_bias, hgrn_lb_logits, hgrn_norm_g, mla_q_norm_g,
           mla_kv_norm_g, mla_w_uq, mla_w_ukv, w_branch, w_out, norm_ffn_g, w_ff1, w_ff2, final_norm_g):
    bsz, seq, d = x.shape
    depth = w_in.shape[0]
    t = bsz * seq
    assert d == D_MODEL and seq % C_TILE == 0 and seq % (A_QTILE * A_TILES_PER_STEP) == 0
    tm = min(TM_DENSE, t)

    p_lb = jax.nn.softmax(hgrn_lb_logits.astype(F32), axis=0)
    lb_all = jnp.cumsum(p_lb, axis=0)
    lb_all = lb_all - lb_all[0:1]

    tabs = rope_tables(positions, tm)
    xf = x.reshape(t, d)
    for l in range(depth):
        w16, w32 = _prep_w_in(w_in[l])
        wq1t, wq2t, wk, wvt = _prep_w_mla(mla_w_uq[l], mla_w_ukv[l])
        proj_a, gates, y_b, qt_c, k_c, vt_c = in_proj(xf, norm_mix_g[l], w16, w32, tabs, mla_q_norm_g[l], mla_kv_norm_g[l],
                                               wq1t, wq2t, wk, wvt, lb_all[l], hgrn_norm_g[l], seq,
                                               min(TM_IN_PROJ, seq), TN_IN_PROJ)

        y_a = band_attention(proj_a, band_bias_table(rel_bias[l]), bsz, seq, COL_AQ, COL_AK, COL_AV)
        y_c = mla_flash(qt_c, k_c, vt_c, bsz, seq)

        xf = merge_out(xf, y_a, y_b, y_c, gates, w_branch[l].astype(BF16), w_out[l].astype(BF16), tm)
        xf = ffn(xf, norm_ffn_g[l], w_ff1[l].astype(BF16), w_ff2[l].astype(BF16), final_norm_g,
                 l == depth - 1, tm, TF_FFN)
    return xf.reshape(bsz, seq, d)
```

```python
import functools

import jax
import jax.numpy as jnp
from jax import lax
from jax.experimental import pallas as pl
from jax.experimental.pallas import tpu as pltpu

F32 = jnp.float32
BF16 = jnp.bfloat16

D_MODEL = 1024
CHUNK = 64
EPS = 1e-6
N_HEADS = 8
HEAD_DIM = 64
LANES = 128
N_PAIRS = N_HEADS * HEAD_DIM // LANES
WIDTH = N_HEADS * HEAD_DIM

A_LEFT_CHUNKS = 8
A_MAX_REL = 128
A_QTILE = 2 * CHUNK
A_BAND = (A_LEFT_CHUNKS + 2) * CHUNK
A_PAD = A_LEFT_CHUNKS * CHUNK
A_TILES_PER_STEP = 16

C_Q_RANK = 256
C_KV_RANK = 128
C_ROPE = 32
C_NOPE = 64
C_SLOT = LANES
C_LATENT = C_Q_RANK + C_KV_RANK + 2 * C_SLOT
ROPE_BASE = 10000.0
C_TILE = 512
ONES_ROWS = 16
LOG2E = 1.4426950408889634
C_QSCALE = (C_NOPE + C_ROPE) ** -0.5 * LOG2E

N_BRANCH = 3

SUB = 8
SUP = 2 * SUB
N_SUB = CHUNK // SUB
NEG_BIG = -1e30

VMEM_LIMIT = 48 * 1024 * 1024
VMEM_LIMIT_IN_PROJ = 58 * 1024 * 1024

TM_IN_PROJ = 512
TN_IN_PROJ = 1024
TM_DENSE = 1024
TF_FFN = 1024

_NT = (((1,), (1,)), ((), ()))


def _cparams(sem, vmem_limit=VMEM_LIMIT):
    return pltpu.CompilerParams(dimension_semantics=sem, vmem_limit_bytes=vmem_limit)


def _resident(a):
    return pl.BlockSpec(a.shape, lambda *_: (0,) * a.ndim, pipeline_mode=pl.Buffered(1))


def _rms(x, g):
    return x * lax.rsqrt(jnp.mean(x * x, axis=-1, keepdims=True) + EPS) * g


def _in_proj_kernel(x_ref, g_ref, wa_ref, wb_ref, ct_ref, st_ref, ctt_ref, stt_ref, gq_ref, gkv_ref,
                    wq1t_ref, wq2t_ref, wk_ref, wvt_ref, loglb_ref, log1mlb_ref, ng_ref,
                    oa_ref, og_ref, yb_ref, qt_ref, k_ref, vt_ref, state_ref, *, tn, tiles_per_seq):
    @pl.when(pl.program_id(0) % tiles_per_seq == 0)
    def _():
        state_ref[...] = jnp.zeros_like(state_ref)

    h = _rms(x_ref[...], g_ref[...]).astype(BF16)

    a_cols = COL_BQ * LANES
    g_col0 = GATE_BLK * N_BRANCH * D_MODEL

    def proj_tile(j):
        lo, hi = j * tn, (j + 1) * tn
        t = jnp.dot(h, wa_ref[:, lo:hi], preferred_element_type=F32)
        if lo < a_cols:
            oa_ref[:, lo:min(hi, a_cols)] = t[:, :min(hi, a_cols) - lo].astype(oa_ref.dtype)
        if hi > g_col0:
            og_ref[:, max(lo, g_col0) - g_col0:hi - g_col0] = t[:, max(lo, g_col0) - lo:].astype(og_ref.dtype)
        return t

    tq = proj_tile(COL_BQ * LANES // tn)
    tig = proj_tile(COL_BI * LANES // tn)
    pb = jnp.dot(h, wb_ref[...], preferred_element_type=F32)
    q_off = COL_BQ * LANES % tn
    i_off = COL_BI * LANES % tn
    g_off = COL_BG * LANES - (COL_BI * LANES // tn) * tn
    y_out = []
    gens = []
    for p in range(N_PAIRS):
        lanes = lambda off: slice(off + p * LANES, off + (p + 1) * LANES)
        gens.append(_hgrn_stages(pb[:, lanes(C_LATENT)], tq[:, lanes(q_off)], tig[:, lanes(i_off)],
                                 tig[:, lanes(g_off)], loglb_ref[p], log1mlb_ref[p], ng_ref[...],
                                 state_ref.at[p], y_out))

    qn = _rms(pb[:, :C_Q_RANK], gq_ref[...]).astype(BF16)
    kvn = _rms(pb[:, C_Q_RANK:C_Q_RANK + C_KV_RANK], gkv_ref[...]).astype(BF16)
    kr = pb[:, C_Q_RANK + C_KV_RANK:C_Q_RANK + C_KV_RANK + C_SLOT]
    kr_sw = pb[:, C_Q_RANK + C_KV_RANK + C_SLOT:C_LATENT]
    kr_rot = kr * ct_ref[...] + kr_sw * st_ref[...]

    def mla_q():
        a_t = lax.dot_general(wq1t_ref[...], qn, _NT, preferred_element_type=F32)
        b_t = lax.dot_general(wq2t_ref[...], qn, _NT, preferred_element_type=F32)
        ctt = ctt_ref[...]
        stt = stt_ref[...]
        for hd in range(N_HEADS):
            sl = slice(hd * C_SLOT, (hd + 1) * C_SLOT)
            qt_ref[sl, :] = (a_t[sl, :] * ctt + b_t[sl, :] * stt).astype(qt_ref.dtype)

    def mla_kv():
        kn = jnp.dot(kvn, wk_ref[...], preferred_element_type=F32)
        vt_ref[...] = lax.dot_general(wvt_ref[...], kvn, _NT, preferred_element_type=F32).astype(vt_ref.dtype)
        for hd in range(N_HEADS):
            sl = slice(hd * C_SLOT, (hd + 1) * C_SLOT)
            k_ref[:, sl] = (kn[:, sl] + kr_rot).astype(k_ref.dtype)

    done = {COL_BQ * LANES // tn, COL_BI * LANES // tn}
    big = [functools.partial(proj_tile, j) for j in range(wa_ref.shape[1] // tn) if j not in done] + [mla_q, mla_kv]
    assert len(big) == HGRN_STAGES
    for work in big:
        work()
        for gen in gens:
            next(gen)
    for p in range(N_PAIRS):
        yb_ref[:, p * LANES:(p + 1) * LANES] = y_out[p].astype(yb_ref.dtype)


def in_proj(x, g, wa, wb, tabs, gq, gkv, wq1t, wq2t, wk, wvt, lb, hgrn_norm_g, seq, tm, tn):
    t, d = x.shape
    na = COL_BQ * LANES
    ng_cols = N_BRANCH * D_MODEL
    ctab, stab, ctab_t, stab_t = tabs
    gq = gq.reshape(1, -1)
    gkv = gkv.reshape(1, -1)
    log_lb = jnp.log(lb).reshape(N_PAIRS, 1, LANES)
    log1m_lb = jnp.log1p(-lb).reshape(N_PAIRS, 1, LANES)
    ng = jnp.tile(hgrn_norm_g.astype(F32), LANES // HEAD_DIM).reshape(1, LANES)
    rows = lambda n: pl.BlockSpec((tm, n), lambda i: (i, 0))
    cols = lambda n: pl.BlockSpec((n, tm), lambda i: (0, i))
    return pl.pallas_call(
        functools.partial(_in_proj_kernel, tn=tn, tiles_per_seq=seq // tm),
        out_shape=(jax.ShapeDtypeStruct((t, na), BF16),
                   jax.ShapeDtypeStruct((t, ng_cols), BF16),
                   jax.ShapeDtypeStruct((t, WIDTH), BF16),
                   jax.ShapeDtypeStruct((N_HEADS * C_SLOT, t), BF16),
                   jax.ShapeDtypeStruct((t, N_HEADS * C_SLOT), BF16),
                   jax.ShapeDtypeStruct((WIDTH, t), BF16)),
        grid=(t // tm,),
        in_specs=[rows(d), pl.BlockSpec((1, d), lambda i: (0, 0)), _resident(wa), _resident(wb),
                  rows(LANES), rows(LANES), cols(LANES), cols(LANES),
                  _resident(gq), _resident(gkv), _resident(wq1t), _resident(wq2t), _resident(wk), _resident(wvt),
                  _resident(log_lb), _resident(log1m_lb), _resident(ng)],
        out_specs=(rows(na), rows(ng_cols), rows(WIDTH), cols(N_HEADS * C_SLOT), rows(N_HEADS * C_SLOT), cols(WIDTH)),
        scratch_shapes=[pltpu.VMEM((N_PAIRS, LANES, LANES), F32)],
        compiler_params=_cparams(("arbitrary",), VMEM_LIMIT_IN_PROJ),
        name="in_proj",
    )(x, g.reshape(1, d), wa, wb, ctab, stab, ctab_t, stab_t, gq, gkv, wq1t, wq2t, wk, wvt, log_lb, log1m_lb, ng)


def _trig_kernel(pos_ref, invf_ref, c_ref, s_ref, ct_ref, st_ref):
    ang = pos_ref[...].astype(F32) * invf_ref[...]
    c = jnp.cos(ang)
    s = jnp.sin(ang)
    c_ref[...] = c
    s_ref[...] = s
    ct_ref[...] = c.T
    st_ref[...] = s.T


def rope_tables(positions, tm):
    t = positions.size
    inv_freq = ROPE_BASE ** (-jnp.arange(0, C_ROPE, 2, dtype=F32) / C_ROPE)
    half = C_ROPE // 2
    invf = jnp.zeros((LANES,), F32)
    invf = invf.at[C_NOPE:C_NOPE + half].set(inv_freq).at[C_NOPE + half:C_NOPE + C_ROPE].set(inv_freq)
    return pl.pallas_call(
        _trig_kernel,
        out_shape=(jax.ShapeDtypeStruct((t, LANES), F32),) * 2 + (jax.ShapeDtypeStruct((LANES, t), F32),) * 2,
        grid=(t // tm,),
        in_specs=[
            pl.BlockSpec((tm, 1), lambda i: (i, 0)),
            pl.BlockSpec((1, LANES), lambda i: (0, 0)),
        ],
        out_specs=(pl.BlockSpec((tm, LANES), lambda i: (i, 0)),) * 2
        + (pl.BlockSpec((LANES, tm), lambda i: (0, i)),) * 2,
        compiler_params=_cparams(("parallel",)),
        name="rope_tables",
    )(positions.reshape(t, 1), invf.reshape(1, LANES))


def _band_attn_kernel(q_ref, k_ref, v_ref, bias_ref, o_ref):
    lane = lax.broadcasted_iota(jnp.int32, (A_QTILE, LANES), 1)
    n_shift = A_PAD // A_QTILE
    starts, scores = [], []
    for t in range(A_TILES_PER_STEP):
        c2 = pl.program_id(2) * A_TILES_PER_STEP + t
        start = pl.multiple_of(jnp.maximum(c2 * A_QTILE - A_PAD, 0), A_QTILE)
        shift = jnp.minimum(c2, n_shift)
        kwin = k_ref[pl.ds(start, A_BAND), :]
        q = q_ref[t * A_QTILE:(t + 1) * A_QTILE, :].astype(F32) * (HEAD_DIM ** -0.5)
        qst = jnp.concatenate([jnp.where(lane < HEAD_DIM, q, 0.0), jnp.where(lane < HEAD_DIM, 0.0, q)],
                              axis=0).astype(BF16)
        s = lax.dot_general(qst, kwin, _NT, preferred_element_type=F32)
        starts.append(start)
        scores.append(s + bias_ref[shift, 0])
    for t in range(A_TILES_PER_STEP):
        s = scores[t]
        vwin = v_ref[pl.ds(starts[t], A_BAND), :]
        m = jnp.max(s, axis=-1, keepdims=True)
        p = jnp.exp(s - m)
        l = jnp.sum(p, axis=-1, keepdims=True)
        pv = jnp.dot(p.astype(BF16), vwin, preferred_element_type=F32) / l
        o_ref[t * A_QTILE:(t + 1) * A_QTILE, :] = jnp.where(lane < HEAD_DIM, pv[:A_QTILE], pv[A_QTILE:]).astype(o_ref.dtype)


def band_attention(proj, bias, bsz, seq, q_col, k_col, v_col):
    t = bsz * seq
    rows = A_QTILE * A_TILES_PER_STEP
    n_steps = seq // rows
    n_shift = A_PAD // A_QTILE
    return pl.pallas_call(
        _band_attn_kernel,
        out_shape=jax.ShapeDtypeStruct((t, WIDTH), BF16),
        grid=(bsz, N_PAIRS, n_steps),
        in_specs=[
            pl.BlockSpec((rows, LANES), lambda b, p, c: (b * n_steps + c, q_col + p)),
            pl.BlockSpec((seq, LANES), lambda b, p, c: (b, k_col + p)),
            pl.BlockSpec((seq, LANES), lambda b, p, c: (b, v_col + p)),
            pl.BlockSpec((n_shift + 1, 1, 2 * A_QTILE, A_BAND), lambda b, p, c: (0, p, 0, 0)),
        ],
        out_specs=pl.BlockSpec((rows, LANES), lambda b, p, c: (b * n_steps + c, p)),
        compiler_params=_cparams(("parallel", "parallel", "arbitrary")),
        name="band_attention",
    )(proj, proj, proj, bias.reshape(n_shift + 1, N_PAIRS, 2 * A_QTILE, A_BAND))


A_ROLL = 768


def _band_bias_kernel(r_ref, o_ref):
    d = pl.program_id(0) * A_QTILE
    prof = jnp.broadcast_to(r_ref[...], (A_QTILE, A_ROLL))
    b = pltpu.roll(prof, 0, 1, stride=1, stride_axis=0)[:, :A_BAND]
    i = lax.broadcasted_iota(jnp.int32, (A_QTILE, A_BAND), 0)
    j = lax.broadcasted_iota(jnp.int32, (A_QTILE, A_BAND), 1)
    gap = (d + i) // CHUNK - j // CHUNK
    o_ref[...] = jnp.where((gap >= 0) & (gap <= A_LEFT_CHUNKS), b, NEG_BIG)


def band_bias_table(rel_table):
    n_shift = A_PAD // A_QTILE
    heads = rel_table.shape[0]
    pad = A_ROLL + A_PAD
    ext = jnp.pad(rel_table.astype(F32)[:, ::-1], ((0, 0), (pad, pad)), mode="edge")

    def seg(d, u0, n):
        o = u0 - d + A_MAX_REL + pad
        return ext[:, o:o + n]

    n_neg = A_ROLL - (A_BAND + 1)
    prof = jnp.stack([jnp.concatenate([seg(s * A_QTILE, 0, A_BAND + 1), seg(s * A_QTILE, -n_neg, n_neg)], axis=1)
                      for s in range(n_shift + 1)])
    return pl.pallas_call(
        _band_bias_kernel,
        out_shape=jax.ShapeDtypeStruct((n_shift + 1, heads, A_QTILE, A_BAND), F32),
        grid=(n_shift + 1, heads),
        in_specs=[pl.BlockSpec((None, None, 1, A_ROLL), lambda s, h: (s, h, 0, 0))],
        out_specs=pl.BlockSpec((None, None, A_QTILE, A_BAND), lambda s, h: (s, h, 0, 0)),
        compiler_params=_cparams(("parallel", "parallel")),
        name="band_bias",
    )(prof.reshape(n_shift + 1, heads, 1, A_ROLL))


def _split3(x):
    hi = x.astype(BF16)
    r1 = x - hi.astype(F32)
    mid = r1.astype(BF16)
    lo = (r1 - mid.astype(F32)).astype(BF16)
    return hi, mid, lo


HGRN_STAGES = 6


def _hgrn_stages(z, qraw, v, graw, log_lb, log1m_lb, norm_g, st_ref, y_out):
    rows = z.shape[0]
    n_chunks = rows // CHUNK
    n_blk = rows // SUB

    head0 = lax.broadcasted_iota(jnp.int32, (rows, LANES), 1) < HEAD_DIM
    head0_c = lax.broadcasted_iota(jnp.int32, (CHUNK, LANES), 1) < HEAD_DIM
    r64 = lax.broadcasted_iota(jnp.int32, (CHUNK, CHUNK), 0)
    c64 = lax.broadcasted_iota(jnp.int32, (CHUNK, CHUNK), 1)
    tril = (c64 <= r64).astype(BF16)
    c64s = lax.broadcasted_iota(jnp.int32, (2 * CHUNK, CHUNK), 1)
    rl = lax.broadcasted_iota(jnp.int32, (LANES, LANES), 0)
    cl = lax.broadcasted_iota(jnp.int32, (LANES, LANES), 1)
    same_head = (rl // HEAD_DIM) == (cl // HEAD_DIM)
    ri = lax.broadcasted_iota(jnp.int32, (CHUNK, CHUNK * SUB), 0)
    ci = lax.broadcasted_iota(jnp.int32, (CHUNK, CHUNK * SUB), 1)
    pick = (ci // SUB == ri).astype(BF16)
    row_s = lax.broadcasted_iota(jnp.int32, (SUB, LANES), 0)

    def chunk(x, c):
        return x[c * CHUNK:(c + 1) * CHUNK]

    log_sig = jnp.minimum(z, 0.0) - jnp.log(1.0 + jnp.exp(-jnp.abs(z)))
    bterm = log1m_lb + log_sig
    log_f = jnp.maximum(log_lb, bterm) + jnp.log(1.0 + jnp.exp(-jnp.abs(log_lb - bterm)))
    log_k = bterm - z
    qs = qraw * (1.0 / (1.0 + jnp.exp(-qraw)))

    x3 = jnp.concatenate(_split3(log_f), axis=1)
    cum3 = [jnp.dot(tril, chunk(x3, c), preferred_element_type=F32) for c in range(n_chunks)]
    yield
    cum = jnp.concatenate([t[:, :LANES] + t[:, LANES:2 * LANES] + t[:, 2 * LANES:] for t in cum3], axis=0)

    zero_row = jnp.zeros((1, LANES), F32)
    c_end = [cum[SUB * b + SUB - 1:SUB * b + SUB, :] for b in range(n_blk)]
    c_start = [zero_row if b % N_SUB == 0 else c_end[b - 1] for b in range(n_blk)]
    c_last = [c_end[c * N_SUB + N_SUB - 1] for c in range(n_chunks)]

    def rows_of(blocks):
        return jnp.concatenate([jnp.broadcast_to(r, (SUB, LANES)) for r in blocks], axis=0)

    cstart_full = rows_of(c_start)
    cend_full = rows_of(c_end)
    clast_full = jnp.concatenate([jnp.broadcast_to(r, (CHUNK, LANES)) for r in c_last], axis=0)
    q1 = qs * jnp.exp(cum - cstart_full)
    lk = log_k - cum
    k2b = jnp.exp(cend_full + lk).astype(BF16)
    k_end = jnp.exp(clast_full + lk).astype(BF16)
    q_state = (qs * jnp.exp(cum)).astype(BF16)
    vb = v.astype(BF16)

    n_sup = CHUNK // SUP
    ce_sup = [c_end[(SUP // SUB) * (b + 1) - 1] for b in range(rows // SUP)]
    cs_sup = [zero_row if b % n_sup == 0 else ce_sup[b - 1] for b in range(rows // SUP)]

    def rows_of_sup(blocks):
        return jnp.concatenate([jnp.broadcast_to(r, (SUP, LANES)) for r in blocks], axis=0)

    q1s = qs * jnp.exp(cum - rows_of_sup(cs_sup))
    k2s = jnp.exp(rows_of_sup(ce_sup) + lk).astype(BF16)
    odd_sub = (lax.broadcasted_iota(jnp.int32, (rows, LANES), 0) // SUB) % 2 == 1
    q_adj = jnp.where(odd_sub, q1, 0.0)
    q_adj = (jnp.where(head0, q_adj, 0.0).astype(BF16), jnp.where(head0, 0.0, q_adj).astype(BF16))
    q_sup = []
    for j in range(n_sup - 1):
        d_rows = []
        for b in range(rows // SUP):
            if b % n_sup > j:
                d_rows.append(jnp.exp(cs_sup[b] - ce_sup[(b // n_sup) * n_sup + j]))
            else:
                d_rows.append(zero_row)
        qj = q1s * rows_of_sup(d_rows)
        q_sup.append((jnp.where(head0, qj, 0.0).astype(BF16), jnp.where(head0, 0.0, qj).astype(BF16)))
    s_adj, s_sup = [], []
    for c in range(n_chunks):
        qst = jnp.concatenate([chunk(q_adj[0], c), chunk(q_adj[1], c)], axis=0)
        s_adj.append(lax.dot_general(qst, chunk(k2b, c), _NT, preferred_element_type=F32))
        for j in range(n_sup - 1):
            qst = jnp.concatenate([chunk(q_sup[j][0], c), chunk(q_sup[j][1], c)], axis=0)
            s_sup.append(lax.dot_general(qst, chunk(k2s, c), _NT, preferred_element_type=F32))
    yield
    r64s = lax.broadcasted_iota(jnp.int32, (2 * CHUNK, CHUNK), 0) % CHUNK
    prev_sub = c64s // SUB == r64s // SUB - 1
    o_cross = []
    for c in range(n_chunks):
        sc = jnp.where(prev_sub, s_adj[c], 0.0)
        for j in range(n_sup - 1):
            col_in_j = (c64s >= j * SUP) & (c64s < (j + 1) * SUP)
            sc = sc + jnp.where(col_in_j, s_sup[c * (n_sup - 1) + j], 0.0)
        o2 = jnp.dot(sc.astype(BF16), chunk(vb, c), preferred_element_type=F32)
        o_cross.append(jnp.where(head0_c, o2[:CHUNK], o2[CHUNK:]))

    yield
    a = cum - log_k
    w_rows = []
    for b in range(n_blk):
        a_b = a[SUB * b:SUB * (b + 1), :]
        for i in range(SUB):
            r = SUB * b + i
            arg = jnp.where(row_s <= i, cum[r:r + 1, :] - a_b, NEG_BIG)
            w_rows.append(jnp.exp(arg) * qs[r:r + 1, :])
    w_all = jnp.concatenate(w_rows, axis=0)
    head0_w = lax.broadcasted_iota(jnp.int32, w_all.shape, 1) < HEAD_DIM
    sum0 = jnp.sum(jnp.where(head0_w, w_all, 0.0), axis=-1, keepdims=True)
    sum1 = jnp.sum(jnp.where(head0_w, 0.0, w_all), axis=-1, keepdims=True)
    sb = jnp.where(head0_w, sum0, sum1)
    yield
    o_diag = []
    for c in range(n_chunks):
        v_rep = jnp.concatenate([v[SUB * (r // SUB):SUB * (r // SUB + 1), :]
                                 for r in range(c * CHUNK, (c + 1) * CHUNK)], axis=0)
        sb_c = sb[c * CHUNK * SUB:(c + 1) * CHUNK * SUB]
        o_diag.append(jnp.dot(pick, (sb_c * v_rep).astype(BF16), preferred_element_type=F32))

    yield
    upd = [jnp.where(same_head,
                     lax.dot_general(chunk(vb, c), chunk(k_end, c), (((0,), (0,)), ((), ())),
                                     preferred_element_type=F32), 0.0) for c in range(n_chunks)]
    st = st_ref[...]
    o_state = []
    for c in range(n_chunks):
        o_state.append(lax.dot_general(chunk(q_state, c), st.astype(BF16), _NT, preferred_element_type=F32))
        st = st * jnp.exp(c_last[c]) + upd[c]
    st_ref[...] = st

    o = jnp.concatenate([o_cross[c] + o_diag[c] + o_state[c] for c in range(n_chunks)], axis=0)
    osq = o * o
    ms0 = jnp.sum(jnp.where(head0, osq, 0.0), axis=-1, keepdims=True)
    ms1 = jnp.sum(jnp.where(head0, 0.0, osq), axis=-1, keepdims=True)
    ms = jnp.where(head0, ms0, ms1) * (1.0 / HEAD_DIM)
    y = o * lax.rsqrt(ms + EPS) * norm_g
    y_out.append(y * (graw * (1.0 / (1.0 + jnp.exp(-graw)))))
    yield


def _mla_flash_kernel(qt_ref, k_ref, vt_ref, o_ref, acc_ref, m_ref, l_ref, s_ref, mt_ref, *, n_qt):
    c = C_QSCALE

    def scores(qi, j, e):
        qoff = pl.multiple_of(qi * C_TILE, C_TILE)
        koff = pl.multiple_of(j * C_TILE, C_TILE)
        k = k_ref[pl.ds(koff, C_TILE), e * C_SLOT:(e + 1) * C_SLOT]
        qt = qt_ref[e * C_SLOT:(e + 1) * C_SLOT, pl.ds(qoff, C_TILE)]
        return jnp.dot(k, qt, preferred_element_type=F32)

    def put_scores(qi, j, e):
        s = scores(qi, j, e)
        s_ref[e] = s
        mt_ref[e] = jnp.max(s, axis=0, keepdims=True)

    def softmax_pv(j, e, s, m_tile):
        off = pl.multiple_of(j * C_TILE, C_TILE)
        m_prev = m_ref[e]
        m_new = jnp.maximum(m_prev, m_tile)
        p = jnp.exp2((s - m_new) * c).astype(BF16)
        alpha = jnp.exp2((m_prev - m_new) * c)
        vt = vt_ref[e * HEAD_DIM:(e + 1) * HEAD_DIM, pl.ds(off, C_TILE)]
        vt_aug = jnp.concatenate([vt, jnp.ones((ONES_ROWS, C_TILE), BF16)], axis=0)
        pv = jnp.dot(vt_aug, p, preferred_element_type=F32)
        l_ref[e] = alpha * l_ref[e] + pv[HEAD_DIM:HEAD_DIM + 1]
        acc_ref[e] = alpha * acc_ref[e] + pv[:HEAD_DIM]
        m_ref[e] = m_new

    kc = lax.broadcasted_iota(jnp.int32, (C_TILE, C_TILE), 0) // CHUNK
    qc = lax.broadcasted_iota(jnp.int32, (C_TILE, C_TILE), 1) // CHUNK

    for e in range(2):
        put_scores(0, 0, e)

    def q_body(qi, carry):
        m_ref[...] = jnp.full_like(m_ref, -jnp.inf)
        l_ref[...] = jnp.zeros_like(l_ref)
        acc_ref[...] = jnp.zeros_like(acc_ref)

        def body(j, carry2):
            for e in range(2):
                s, m_tile = s_ref[e], mt_ref[e]
                put_scores(qi, j + 1, e)
                softmax_pv(j, e, s, m_tile)
            return carry2

        lax.fori_loop(0, qi, body, 0)
        q_next = jnp.minimum(qi + 1, n_qt - 1)
        for e in range(2):
            s = jnp.where(kc <= qc, s_ref[e], NEG_BIG)
            put_scores(q_next, 0, e)
            softmax_pv(qi, e, s, jnp.max(s, axis=0, keepdims=True))

        o_t = jnp.concatenate([acc_ref[0] / l_ref[0], acc_ref[1] / l_ref[1]], axis=0)
        o_ref[pl.ds(pl.multiple_of(qi * C_TILE, C_TILE), C_TILE), :] = o_t.T.astype(o_ref.dtype)
        return carry

    lax.fori_loop(0, n_qt, q_body, 0)


def mla_flash(qt, k, vt, bsz, seq):
    t = bsz * seq
    return pl.pallas_call(
        functools.partial(_mla_flash_kernel, n_qt=seq // C_TILE),
        out_shape=jax.ShapeDtypeStruct((t, WIDTH), BF16),
        grid=(bsz, N_PAIRS),
        in_specs=[
            pl.BlockSpec((2 * C_SLOT, seq), lambda b, p: (p, b)),
            pl.BlockSpec((seq, 2 * C_SLOT), lambda b, p: (b, p)),
            pl.BlockSpec((LANES, seq), lambda b, p: (p, b)),
        ],
        out_specs=pl.BlockSpec((seq, LANES), lambda b, p: (b, p)),
        scratch_shapes=[pltpu.VMEM((2, HEAD_DIM, C_TILE), F32),
                        pltpu.VMEM((2, 1, C_TILE), F32),
                        pltpu.VMEM((2, 1, C_TILE), F32),
                        pltpu.VMEM((2, C_TILE, C_TILE), F32),
                        pltpu.VMEM((2, 1, C_TILE), F32)],
        compiler_params=_cparams(("parallel", "parallel")),
        name="mla_flash",
    )(qt, k, vt)


def _merge_kernel(x_ref, ya_ref, yb_ref, yc_ref, g_ref, wbr_ref, wout_ref, o_ref):
    merged = None
    for n, y_ref in enumerate((ya_ref, yb_ref, yc_ref)):
        up = jnp.dot(y_ref[...], wbr_ref[n], preferred_element_type=F32)
        gl = g_ref[:, n * D_MODEL:(n + 1) * D_MODEL].astype(F32)
        term = (1.0 / (1.0 + jnp.exp(-gl))) * up
        merged = term if merged is None else merged + term
    o_ref[...] = x_ref[...] + jnp.dot(merged.astype(BF16), wout_ref[...], preferred_element_type=F32)


def merge_out(x, ya, yb, yc, gates, wbr, wout, tm):
    t = x.shape[0]
    return pl.pallas_call(
        _merge_kernel,
        out_shape=jax.ShapeDtypeStruct((t, D_MODEL), F32),
        grid=(t // tm,),
        in_specs=[
            pl.BlockSpec((tm, D_MODEL), lambda i: (i, 0)),
            pl.BlockSpec((tm, WIDTH), lambda i: (i, 0)),
            pl.BlockSpec((tm, WIDTH), lambda i: (i, 0)),
            pl.BlockSpec((tm, WIDTH), lambda i: (i, 0)),
            pl.BlockSpec((tm, N_BRANCH * D_MODEL), lambda i: (i, 0)),
            _resident(wbr),
            _resident(wout),
        ],
        out_specs=pl.BlockSpec((tm, D_MODEL), lambda i: (i, 0)),
        compiler_params=_cparams(("parallel",)),
        name="merge_out",
    )(x, ya, yb, yc, gates, wbr, wout)


def _ffn_kernel(x_ref, g_ref, w1_ref, w2_ref, gf_ref, o_ref, *, final_norm, tf):
    x = x_ref[...]
    h = _rms(x, g_ref[...]).astype(BF16)
    acc = None
    for k in range(w1_ref.shape[1] // tf):
        u = jnp.maximum(jnp.dot(h, w1_ref[:, k * tf:(k + 1) * tf], preferred_element_type=F32), 0.0)
        part = jnp.dot((u * u).astype(BF16), w2_ref[k * tf:(k + 1) * tf, :], preferred_element_type=F32)
        acc = part if acc is None else acc + part
    y = x + acc
    if final_norm:
        y = _rms(y, gf_ref[...])
    o_ref[...] = y


def ffn(x, g, w1, w2, gf, final_norm, tm, tf):
    t, d = x.shape
    return pl.pallas_call(
        functools.partial(_ffn_kernel, final_norm=final_norm, tf=tf),
        out_shape=jax.ShapeDtypeStruct((t, d), F32),
        grid=(t // tm,),
        in_specs=[
            pl.BlockSpec((tm, d), lambda i: (i, 0)),
            pl.BlockSpec((1, d), lambda i: (0, 0)),
            _resident(w1),
            _resident(w2),
            pl.BlockSpec((1, d), lambda i: (0, 0)),
        ],
        out_specs=pl.BlockSpec((tm, d), lambda i: (i, 0)),
        compiler_params=_cparams(("parallel",)),
        name="ffn",
    )(x, g.reshape(1, d), w1, w2, gf.reshape(1, d))


COL_AQ, COL_AK, COL_AV = 0, 4, 8
COL_BQ, COL_BI, COL_BG = 12, 16, 20
GATE_BLK = 1


def _prep_w_in(w):
    o = 0
    parts = {}
    for name, size in (("aq", 512), ("ak", 512), ("av", 512), ("bq", 512), ("bf", 512), ("bi", 512),
                       ("bg", 512), ("cq", C_Q_RANK), ("ckv", C_KV_RANK), ("ckr", C_ROPE),
                       ("gate", N_BRANCH * D_MODEL)):
        parts[name] = w[:, o:o + size]
        o += size
    w16 = jnp.concatenate([parts[n] for n in ("aq", "ak", "av", "bq", "bi", "bg", "gate")], axis=1)
    half = C_ROPE // 2
    x1, x2 = parts["ckr"][:, :half], parts["ckr"][:, half:]
    zl = jnp.zeros((w.shape[0], C_NOPE), w.dtype)
    zr = jnp.zeros((w.shape[0], C_SLOT - C_NOPE - C_ROPE), w.dtype)
    kr_slot = jnp.concatenate([zl, x1, x2, zr], axis=1)
    kr_sw = jnp.concatenate([zl, -x2, x1, zr], axis=1)
    w32 = jnp.concatenate([parts["cq"], parts["ckv"], kr_slot, kr_sw, parts["bf"]], axis=1)
    return w16.astype(BF16), w32.astype(BF16)


def _prep_w_mla(w_uq, w_ukv):
    half = C_ROPE // 2
    per_q = C_NOPE + C_ROPE
    wq = w_uq.reshape(C_Q_RANK, N_HEADS, per_q)
    nope, x1, x2 = wq[..., :C_NOPE], wq[..., C_NOPE:C_NOPE + half], wq[..., C_NOPE + half:]
    zpad = jnp.zeros((C_Q_RANK, N_HEADS, C_SLOT - per_q), w_uq.dtype)
    wq1 = jnp.concatenate([nope, x1, x2, zpad], axis=-1).reshape(C_Q_RANK, N_HEADS * C_SLOT)
    wq2 = jnp.concatenate([jnp.zeros_like(nope), -x2, x1, zpad], axis=-1).reshape(C_Q_RANK, N_HEADS * C_SLOT)
    wkv = w_ukv.reshape(C_KV_RANK, N_HEADS, C_NOPE + HEAD_DIM)
    k_nope, v = wkv[..., :C_NOPE], wkv[..., C_NOPE:]
    wk = jnp.concatenate([k_nope, jnp.zeros((C_KV_RANK, N_HEADS, C_SLOT - C_NOPE), w_ukv.dtype)],
                         axis=-1).reshape(C_KV_RANK, N_HEADS * C_SLOT)
    wv = v.reshape(C_KV_RANK, WIDTH)
    return wq1.T.astype(BF16), wq2.T.astype(BF16), wk.astype(BF16), wv.T.astype(BF16)


def kernel(x, positions, norm_mix_g, w_in, rel_bias, hgrn_lb_logits, hgrn_norm_g, mla_q_norm_g,
           mla_kv_norm_g, mla_w_uq, mla_w_ukv, w_branch, w_out, norm_ffn_g, w_ff1, w_ff2, final_norm_g):
    bsz, seq, d = x.shape
    depth = w_in.shape[0]
    t = bsz * seq
    assert d == D_MODEL and seq % C_TILE == 0 and seq % (A_QTILE * A_TILES_PER_STEP) == 0
    tm = min(TM_DENSE, t)

    p_lb = jax.nn.softmax(hgrn_lb_logits.astype(F32), axis=0)
    lb_all = jnp.cumsum(p_lb, axis=0)
    lb_all = lb_all - lb_all[0:1]

    tabs = rope_tables(positions, tm)
    xf = x.reshape(t, d)
    for l in range(depth):
        w16, w32 = _prep_w_in(w_in[l])
        wq1t, wq2t, wk, wvt = _prep_w_mla(mla_w_uq[l], mla_w_ukv[l])
        proj_a, gates, y_b, qt_c, k_c, vt_c = in_proj(xf, norm_mix_g[l], w16, w32, tabs, mla_q_norm_g[l], mla_kv_norm_g[l],
                                               wq1t, wq2t, wk, wvt, lb_all[l], hgrn_norm_g[l], seq,
                                               min(TM_IN_PROJ, seq), TN_IN_PROJ)

        y_a = band_attention(proj_a, band_bias_table(rel_bias[l]), bsz, seq, COL_AQ, COL_AK, COL_AV)
        y_c = mla_flash(qt_c, k_c, vt_c, bsz, seq)

        xf = merge_out(xf, y_a, y_b, y_c, gates, w_branch[l].astype(BF16), w_out[l].astype(BF16), tm)
        xf = ffn(xf, norm_ffn_g[l], w_ff1[l].astype(BF16), w_ff2[l].astype(BF16), final_norm_g,
                 l == depth - 1, tm, TF_FFN)
    return xf.reshape(bsz, seq, d)
```

```python
import functools

import jax
import jax.numpy as jnp
from jax import lax
from jax.experimental import pallas as pl
from jax.experimental.pallas import tpu as pltpu

F32 = jnp.float32
BF16 = jnp.bfloat16

D_MODEL = 1024
CHUNK = 64
EPS = 1e-6
N_HEADS = 8
HEAD_DIM = 64
LANES = 128
N_PAIRS = N_HEADS * HEAD_DIM // LANES
WIDTH = N_HEADS * HEAD_DIM

A_LEFT_CHUNKS = 8
A_MAX_REL = 128
A_QTILE = 2 * CHUNK
A_BAND = (A_LEFT_CHUNKS + 2) * CHUNK
A_PAD = A_LEFT_CHUNKS * CHUNK
A_TILES_PER_STEP = 16

C_Q_RANK = 256
C_KV_RANK = 128
C_ROPE = 32
C_NOPE = 64
C_SLOT = LANES
C_LATENT = C_Q_RANK + C_KV_RANK + 2 * C_SLOT
ROPE_BASE = 10000.0
C_TILE = 512
ONES_ROWS = 16
LOG2E = 1.4426950408889634
C_QSCALE = (C_NOPE + C_ROPE) ** -0.5 * LOG2E

N_BRANCH = 3

SUB = 8
SUP = 2 * SUB
N_SUB = CHUNK // SUB
NEG_BIG = -1e30

VMEM_LIMIT = 48 * 1024 * 1024
VMEM_LIMIT_IN_PROJ = 58 * 1024 * 1024

TM_IN_PROJ = 512
TN_IN_PROJ = 1024
TM_DENSE = 1024
TF_FFN = 1024

_NT = (((1,), (1,)), ((), ()))


def _cparams(sem, vmem_limit=VMEM_LIMIT):
    return pltpu.CompilerParams(dimension_semantics=sem, vmem_limit_bytes=vmem_limit)


def _resident(a):
    return pl.BlockSpec(a.shape, lambda *_: (0,) * a.ndim, pipeline_mode=pl.Buffered(1))


def _rms(x, g):
    return x * lax.rsqrt(jnp.mean(x * x, axis=-1, keepdims=True) + EPS) * g


def _in_proj_kernel(x_ref, g_ref, wa_ref, wb_ref, ct_ref, st_ref, ctt_ref, stt_ref, gq_ref, gkv_ref,
                    wq1t_ref, wq2t_ref, wk_ref, wvt_ref, loglb_ref, log1mlb_ref, ng_ref,
                    oa_ref, og_ref, yb_ref, qt_ref, k_ref, vt_ref, state_ref, *, tn, tiles_per_seq):
    @pl.when(pl.program_id(0) % tiles_per_seq == 0)
    def _():
        state_ref[...] = jnp.zeros_like(state_ref)

    h = _rms(x_ref[...], g_ref[...]).astype(BF16)

    a_cols = COL_BQ * LANES
    g_col0 = GATE_BLK * N_BRANCH * D_MODEL

    def proj_tile(j):
        lo, hi = j * tn, (j + 1) * tn
        t = jnp.dot(h, wa_ref[:, lo:hi], preferred_element_type=F32)
        if lo < a_cols:
            oa_ref[:, lo:min(hi, a_cols)] = t[:, :min(hi, a_cols) - lo].astype(oa_ref.dtype)
        if hi > g_col0:
            og_ref[:, max(lo, g_col0) - g_col0:hi - g_col0] = t[:, max(lo, g_col0) - lo:].astype(og_ref.dtype)
        return t

    tq = proj_tile(COL_BQ * LANES // tn)
    tig = proj_tile(COL_BI * LANES // tn)
    pb = jnp.dot(h, wb_ref[...], preferred_element_type=F32)
    q_off = COL_BQ * LANES % tn
    i_off = COL_BI * LANES % tn
    g_off = COL_BG * LANES - (COL_BI * LANES // tn) * tn
    y_out = []
    gens = []
    for p in range(N_PAIRS):
        lanes = lambda off: slice(off + p * LANES, off + (p + 1) * LANES)
        gens.append(_hgrn_stages(pb[:, lanes(C_LATENT)], tq[:, lanes(q_off)], tig[:, lanes(i_off)],
                                 tig[:, lanes(g_off)], loglb_ref[p], log1mlb_ref[p], ng_ref[...],
                                 state_ref.at[p], y_out))

    qn = _rms(pb[:, :C_Q_RANK], gq_ref[...]).astype(BF16)
    kvn = _rms(pb[:, C_Q_RANK:C_Q_RANK + C_KV_RANK], gkv_ref[...]).astype(BF16)
    kr = pb[:, C_Q_RANK + C_KV_RANK:C_Q_RANK + C_KV_RANK + C_SLOT]
    kr_sw = pb[:, C_Q_RANK + C_KV_RANK + C_SLOT:C_LATENT]
    kr_rot = kr * ct_ref[...] + kr_sw * st_ref[...]

    def mla_q():
        a_t = lax.dot_general(wq1t_ref[...], qn, _NT, preferred_element_type=F32)
        b_t = lax.dot_general(wq2t_ref[...], qn, _NT, preferred_element_type=F32)
        ctt = ctt_ref[...]
        stt = stt_ref[...]
        for hd in range(N_HEADS):
            sl = slice(hd * C_SLOT, (hd + 1) * C_SLOT)
            qt_ref[sl, :] = (a_t[sl, :] * ctt + b_t[sl, :] * stt).astype(qt_ref.dtype)

    def mla_kv():
        kn = jnp.dot(kvn, wk_ref[...], preferred_element_type=F32)
        vt_ref[...] = lax.dot_general(wvt_ref[...], kvn, _NT, preferred_element_type=F32).astype(vt_ref.dtype)
        for hd in range(N_HEADS):
            sl = slice(hd * C_SLOT, (hd + 1) * C_SLOT)
            k_ref[:, sl] = (kn[:, sl] + kr_rot).astype(k_ref.dtype)

    done = {COL_BQ * LANES // tn, COL_BI * LANES // tn}
    big = [functools.partial(proj_tile, j) for j in range(wa_ref.shape[1] // tn) if j not in done] + [mla_q, mla_kv]
    assert len(big) == HGRN_STAGES
    for work in big:
        work()
        for gen in gens:
            next(gen)
    for p in range(N_PAIRS):
        yb_ref[:, p * LANES:(p + 1) * LANES] = y_out[p].astype(yb_ref.dtype)


def in_proj(x, g, wa, wb, tabs, gq, gkv, wq1t, wq2t, wk, wvt, lb, hgrn_norm_g, seq, tm, tn):
    t, d = x.shape
    na = COL_BQ * LANES
    ng_cols = N_BRANCH * D_MODEL
    ctab, stab, ctab_t, stab_t = tabs
    gq = gq.reshape(1, -1)
    gkv = gkv.reshape(1, -1)
    log_lb = jnp.log(lb).reshape(N_PAIRS, 1, LANES)
    log1m_lb = jnp.log1p(-lb).reshape(N_PAIRS, 1, LANES)
    ng = jnp.tile(hgrn_norm_g.astype(F32), LANES // HEAD_DIM).reshape(1, LANES)
    rows = lambda n: pl.BlockSpec((tm, n), lambda i: (i, 0))
    cols = lambda n: pl.BlockSpec((n, tm), lambda i: (0, i))
    return pl.pallas_call(
        functools.partial(_in_proj_kernel, tn=tn, tiles_per_seq=seq // tm),
        out_shape=(jax.ShapeDtypeStruct((t, na), BF16),
                   jax.ShapeDtypeStruct((t, ng_cols), BF16),
                   jax.ShapeDtypeStruct((t, WIDTH), BF16),
                   jax.ShapeDtypeStruct((N_HEADS * C_SLOT, t), BF16),
                   jax.ShapeDtypeStruct((t, N_HEADS * C_SLOT), BF16),
                   jax.ShapeDtypeStruct((WIDTH, t), BF16)),
        grid=(t // tm,),
        in_specs=[rows(d), pl.BlockSpec((1, d), lambda i: (0, 0)), _resident(wa), _resident(wb),
                  rows(LANES), rows(LANES), cols(LANES), cols(LANES),
                  _resident(gq), _resident(gkv), _resident(wq1t), _resident(wq2t), _resident(wk), _resident(wvt),
                  _resident(log_lb), _resident(log1m_lb), _resident(ng)],
        out_specs=(rows(na), rows(ng_cols), rows(WIDTH), cols(N_HEADS * C_SLOT), rows(N_HEADS * C_SLOT), cols(WIDTH)),
        scratch_shapes=[pltpu.VMEM((N_PAIRS, LANES, LANES), F32)],
        compiler_params=_cparams(("arbitrary",), VMEM_LIMIT_IN_PROJ),
        name="in_proj",
    )(x, g.reshape(1, d), wa, wb, ctab, stab, ctab_t, stab_t, gq, gkv, wq1t, wq2t, wk, wvt, log_lb, log1m_lb, ng)


def _trig_kernel(pos_ref, invf_ref, c_ref, s_ref, ct_ref, st_ref):
    ang = pos_ref[...].astype(F32) * invf_ref[...]
    c = jnp.cos(ang)
    s = jnp.sin(ang)
    c_ref[...] = c
    s_ref[...] = s
    ct_ref[...] = c.T
    st_ref[...] = s.T


def rope_tables(positions, tm):
    t = positions.size
    inv_freq = ROPE_BASE ** (-jnp.arange(0, C_ROPE, 2, dtype=F32) / C_ROPE)
    half = C_ROPE // 2
    invf = jnp.zeros((LANES,), F32)
    invf = invf.at[C_NOPE:C_NOPE + half].set(inv_freq).at[C_NOPE + half:C_NOPE + C_ROPE].set(inv_freq)
    return pl.pallas_call(
        _trig_kernel,
        out_shape=(jax.ShapeDtypeStruct((t, LANES), F32),) * 2 + (jax.ShapeDtypeStruct((LANES, t), F32),) * 2,
        grid=(t // tm,),
        in_specs=[
            pl.BlockSpec((tm, 1), lambda i: (i, 0)),
            pl.BlockSpec((1, LANES), lambda i: (0, 0)),
        ],
        out_specs=(pl.BlockSpec((tm, LANES), lambda i: (i, 0)),) * 2
        + (pl.BlockSpec((LANES, tm), lambda i: (0, i)),) * 2,
        compiler_params=_cparams(("parallel",)),
        name="rope_tables",
    )(positions.reshape(t, 1), invf.reshape(1, LANES))


def _band_attn_kernel(q_ref, k_ref, v_ref, bias_ref, o_ref, kt_ref):
    lane = lax.broadcasted_iota(jnp.int32, (A_QTILE, LANES), 1)
    n_shift = A_PAD // A_QTILE

    @pl.when(pl.program_id(2) == 0)
    def _():
        for r in range(0, k_ref.shape[0], C_TILE):
            kt_ref[:, r:r + C_TILE] = k_ref[r:r + C_TILE, :].astype(F32).T.astype(kt_ref.dtype)

    starts, scores = [], []
    for t in range(A_TILES_PER_STEP):
        c2 = pl.program_id(2) * A_TILES_PER_STEP + t
        start = pl.multiple_of(jnp.maximum(c2 * A_QTILE - A_PAD, 0), A_QTILE)
        shift = jnp.minimum(c2, n_shift)
        kt_win = kt_ref[:, pl.ds(start, A_BAND)]
        q = q_ref[t * A_QTILE:(t + 1) * A_QTILE, :].astype(F32) * (HEAD_DIM ** -0.5)
        qst = jnp.concatenate([jnp.where(lane < HEAD_DIM, q, 0.0), jnp.where(lane < HEAD_DIM, 0.0, q)],
                              axis=0).astype(BF16)
        s = jnp.dot(qst, kt_win, preferred_element_type=F32)
        starts.append(start)
        scores.append(s + bias_ref[shift, 0])
    for t in range(A_TILES_PER_STEP):
        s = scores[t]
        vwin = v_ref[pl.ds(starts[t], A_BAND), :]
        m = jnp.max(s, axis=-1, keepdims=True)
        p = jnp.exp(s - m)
        l = jnp.sum(p, axis=-1, keepdims=True)
        pv = jnp.dot(p.astype(BF16), vwin, preferred_element_type=F32) / l
        o_ref[t * A_QTILE:(t + 1) * A_QTILE, :] = jnp.where(lane < HEAD_DIM, pv[:A_QTILE], pv[A_QTILE:]).astype(o_ref.dtype)


def band_attention(proj, bias, bsz, seq, q_col, k_col, v_col):
    t = bsz * seq
    rows = A_QTILE * A_TILES_PER_STEP
    n_steps = seq // rows
    n_shift = A_PAD // A_QTILE
    return pl.pallas_call(
        _band_attn_kernel,
        out_shape=jax.ShapeDtypeStruct((t, WIDTH), BF16),
        grid=(bsz, N_PAIRS, n_steps),
        in_specs=[
            pl.BlockSpec((rows, LANES), lambda b, p, c: (b * n_steps + c, q_col + p)),
            pl.BlockSpec((seq, LANES), lambda b, p, c: (b, k_col + p)),
            pl.BlockSpec((seq, LANES), lambda b, p, c: (b, v_col + p)),
            pl.BlockSpec((n_shift + 1, 1, 2 * A_QTILE, A_BAND), lambda b, p, c: (0, p, 0, 0)),
        ],
        out_specs=pl.BlockSpec((rows, LANES), lambda b, p, c: (b * n_steps + c, p)),
        scratch_shapes=[pltpu.VMEM((LANES, seq), BF16)],
        compiler_params=_cparams(("parallel", "parallel", "arbitrary")),
        name="band_attention",
    )(proj, proj, proj, bias.reshape(n_shift + 1, N_PAIRS, 2 * A_QTILE, A_BAND))


A_ROLL = 768


def _band_bias_kernel(r_ref, o_ref):
    d = pl.program_id(0) * A_QTILE
    prof = jnp.broadcast_to(r_ref[...], (A_QTILE, A_ROLL))
    b = pltpu.roll(prof, 0, 1, stride=1, stride_axis=0)[:, :A_BAND]
    i = lax.broadcasted_iota(jnp.int32, (A_QTILE, A_BAND), 0)
    j = lax.broadcasted_iota(jnp.int32, (A_QTILE, A_BAND), 1)
    gap = (d + i) // CHUNK - j // CHUNK
    o_ref[...] = jnp.where((gap >= 0) & (gap <= A_LEFT_CHUNKS), b, NEG_BIG)


def band_bias_table(rel_table):
    n_shift = A_PAD // A_QTILE
    heads = rel_table.shape[0]
    pad = A_ROLL + A_PAD
    ext = jnp.pad(rel_table.astype(F32)[:, ::-1], ((0, 0), (pad, pad)), mode="edge")

    def seg(d, u0, n):
        o = u0 - d + A_MAX_REL + pad
        return ext[:, o:o + n]

    n_neg = A_ROLL - (A_BAND + 1)
    prof = jnp.stack([jnp.concatenate([seg(s * A_QTILE, 0, A_BAND + 1), seg(s * A_QTILE, -n_neg, n_neg)], axis=1)
                      for s in range(n_shift + 1)])
    return pl.pallas_call(
        _band_bias_kernel,
        out_shape=jax.ShapeDtypeStruct((n_shift + 1, heads, A_QTILE, A_BAND), F32),
        grid=(n_shift + 1, heads),
        in_specs=[pl.BlockSpec((None, None, 1, A_ROLL), lambda s, h: (s, h, 0, 0))],
        out_specs=pl.BlockSpec((None, None, A_QTILE, A_BAND), lambda s, h: (s, h, 0, 0)),
        compiler_params=_cparams(("parallel", "parallel")),
        name="band_bias",
    )(prof.reshape(n_shift + 1, heads, 1, A_ROLL))


def _split3(x):
    hi = x.astype(BF16)
    r1 = x - hi.astype(F32)
    mid = r1.astype(BF16)
    lo = (r1 - mid.astype(F32)).astype(BF16)
    return hi, mid, lo


HGRN_STAGES = 6


def _hgrn_stages(z, qraw, v, graw, log_lb, log1m_lb, norm_g, st_ref, y_out):
    rows = z.shape[0]
    n_chunks = rows // CHUNK
    n_blk = rows // SUB

    head0 = lax.broadcasted_iota(jnp.int32, (rows, LANES), 1) < HEAD_DIM
    head0_c = lax.broadcasted_iota(jnp.int32, (CHUNK, LANES), 1) < HEAD_DIM
    r64 = lax.broadcasted_iota(jnp.int32, (CHUNK, CHUNK), 0)
    c64 = lax.broadcasted_iota(jnp.int32, (CHUNK, CHUNK), 1)
    tril = (c64 <= r64).astype(BF16)
    c64s = lax.broadcasted_iota(jnp.int32, (2 * CHUNK, CHUNK), 1)
    rl = lax.broadcasted_iota(jnp.int32, (LANES, LANES), 0)
    cl = lax.broadcasted_iota(jnp.int32, (LANES, LANES), 1)
    same_head = (rl // HEAD_DIM) == (cl // HEAD_DIM)
    ri = lax.broadcasted_iota(jnp.int32, (CHUNK, CHUNK * SUB), 0)
    ci = lax.broadcasted_iota(jnp.int32, (CHUNK, CHUNK * SUB), 1)
    pick = (ci // SUB == ri).astype(BF16)
    row_s = lax.broadcasted_iota(jnp.int32, (SUB, LANES), 0)

    def chunk(x, c):
        return x[c * CHUNK:(c + 1) * CHUNK]

    log_sig = jnp.minimum(z, 0.0) - jnp.log(1.0 + jnp.exp(-jnp.abs(z)))
    bterm = log1m_lb + log_sig
    log_f = jnp.maximum(log_lb, bterm) + jnp.log(1.0 + jnp.exp(-jnp.abs(log_lb - bterm)))
    log_k = bterm - z
    qs = qraw * (1.0 / (1.0 + jnp.exp(-qraw)))

    x3 = jnp.concatenate(_split3(log_f), axis=1)
    cum3 = [jnp.dot(tril, chunk(x3, c), preferred_element_type=F32) for c in range(n_chunks)]
    yield
    cum = jnp.concatenate([t[:, :LANES] + t[:, LANES:2 * LANES] + t[:, 2 * LANES:] for t in cum3], axis=0)

    zero_row = jnp.zeros((1, LANES), F32)
    c_end = [cum[SUB * b + SUB - 1:SUB * b + SUB, :] for b in range(n_blk)]
    c_start = [zero_row if b % N_SUB == 0 else c_end[b - 1] for b in range(n_blk)]
    c_last = [c_end[c * N_SUB + N_SUB - 1] for c in range(n_chunks)]

    def rows_of(blocks):
        return jnp.concatenate([jnp.broadcast_to(r, (SUB, LANES)) for r in blocks], axis=0)

    cstart_full = rows_of(c_start)
    cend_full = rows_of(c_end)
    clast_full = jnp.concatenate([jnp.broadcast_to(r, (CHUNK, LANES)) for r in c_last], axis=0)
    q1 = qs * jnp.exp(cum - cstart_full)
    lk = log_k - cum
    k2b = jnp.exp(cend_full + lk).astype(BF16)
    k_end = jnp.exp(clast_full + lk).astype(BF16)
    q_state = (qs * jnp.exp(cum)).astype(BF16)
    vb = v.astype(BF16)

    n_sup = CHUNK // SUP
    ce_sup = [c_end[(SUP // SUB) * (b + 1) - 1] for b in range(rows // SUP)]
    cs_sup = [zero_row if b % n_sup == 0 else ce_sup[b - 1] for b in range(rows // SUP)]

    def rows_of_sup(blocks):
        return jnp.concatenate([jnp.broadcast_to(r, (SUP, LANES)) for r in blocks], axis=0)

    q1s = qs * jnp.exp(cum - rows_of_sup(cs_sup))
    k2s = jnp.exp(rows_of_sup(ce_sup) + lk).astype(BF16)
    odd_sub = (lax.broadcasted_iota(jnp.int32, (rows, LANES), 0) // SUB) % 2 == 1
    q_adj = jnp.where(odd_sub, q1, 0.0)
    q_adj = (jnp.where(head0, q_adj, 0.0).astype(BF16), jnp.where(head0, 0.0, q_adj).astype(BF16))
    q_sup = []
    for j in range(n_sup - 1):
        d_rows = []
        for b in range(rows // SUP):
            if b % n_sup > j:
                d_rows.append(jnp.exp(cs_sup[b] - ce_sup[(b // n_sup) * n_sup + j]))
            else:
                d_rows.append(zero_row)
        qj = q1s * rows_of_sup(d_rows)
        q_sup.append((jnp.where(head0, qj, 0.0).astype(BF16), jnp.where(head0, 0.0, qj).astype(BF16)))
    s_adj, s_sup = [], []
    for c in range(n_chunks):
        qst = jnp.concatenate([chunk(q_adj[0], c), chunk(q_adj[1], c)], axis=0)
        s_adj.append(lax.dot_general(qst, chunk(k2b, c), _NT, preferred_element_type=F32))
        for j in range(n_sup - 1):
            qst = jnp.concatenate([chunk(q_sup[j][0], c), chunk(q_sup[j][1], c)], axis=0)
            s_sup.append(lax.dot_general(qst, chunk(k2s, c), _NT, preferred_element_type=F32))
    yield
    r64s = lax.broadcasted_iota(jnp.int32, (2 * CHUNK, CHUNK), 0) % CHUNK
    prev_sub = c64s // SUB == r64s // SUB - 1
    o_cross = []
    for c in range(n_chunks):
        sc = jnp.where(prev_sub, s_adj[c], 0.0)
        for j in range(n_sup - 1):
            col_in_j = (c64s >= j * SUP) & (c64s < (j + 1) * SUP)
            sc = sc + jnp.where(col_in_j, s_sup[c * (n_sup - 1) + j], 0.0)
        o2 = jnp.dot(sc.astype(BF16), chunk(vb, c), preferred_element_type=F32)
        o_cross.append(jnp.where(head0_c, o2[:CHUNK], o2[CHUNK:]))

    yield
    a = cum - log_k
    w_rows = []
    for b in range(n_blk):
        a_b = a[SUB * b:SUB * (b + 1), :]
        for i in range(SUB):
            r = SUB * b + i
            arg = jnp.where(row_s <= i, cum[r:r + 1, :] - a_b, NEG_BIG)
            w_rows.append(jnp.exp(arg) * qs[r:r + 1, :])
    w_all = jnp.concatenate(w_rows, axis=0)
    head0_w = lax.broadcasted_iota(jnp.int32, w_all.shape, 1) < HEAD_DIM
    sum0 = jnp.sum(jnp.where(head0_w, w_all, 0.0), axis=-1, keepdims=True)
    sum1 = jnp.sum(jnp.where(head0_w, 0.0, w_all), axis=-1, keepdims=True)
    sb = jnp.where(head0_w, sum0, sum1)
    yield
    o_diag = []
    for c in range(n_chunks):
        v_rep = jnp.concatenate([v[SUB * (r // SUB):SUB * (r // SUB + 1), :]
                                 for r in range(c * CHUNK, (c + 1) * CHUNK)], axis=0)
        sb_c = sb[c * CHUNK * SUB:(c + 1) * CHUNK * SUB]
        o_diag.append(jnp.dot(pick, (sb_c * v_rep).astype(BF16), preferred_element_type=F32))

    yield
    upd = [jnp.where(same_head,
                     lax.dot_general(chunk(vb, c), chunk(k_end, c), (((0,), (0,)), ((), ())),
                                     preferred_element_type=F32), 0.0) for c in range(n_chunks)]
    st = st_ref[...]
    o_state = []
    for c in range(n_chunks):
        o_state.append(lax.dot_general(chunk(q_state, c), st.astype(BF16), _NT, preferred_element_type=F32))
        st = st * jnp.exp(c_last[c]) + upd[c]
    st_ref[...] = st

    o = jnp.concatenate([o_cross[c] + o_diag[c] + o_state[c] for c in range(n_chunks)], axis=0)
    osq = o * o
    ms0 = jnp.sum(jnp.where(head0, osq, 0.0), axis=-1, keepdims=True)
    ms1 = jnp.sum(jnp.where(head0, 0.0, osq), axis=-1, keepdims=True)
    ms = jnp.where(head0, ms0, ms1) * (1.0 / HEAD_DIM)
    y = o * lax.rsqrt(ms + EPS) * norm_g
    y_out.append(y * (graw * (1.0 / (1.0 + jnp.exp(-graw)))))
    yield


def _mla_flash_kernel(qt_ref, k_ref, vt_ref, o_ref, acc_ref, m_ref, l_ref, s_ref, mt_ref, *, n_qt):
    c = C_QSCALE

    def scores(qi, j, e):
        qoff = pl.multiple_of(qi * C_TILE, C_TILE)
        koff = pl.multiple_of(j * C_TILE, C_TILE)
        k = k_ref[pl.ds(koff, C_TILE), e * C_SLOT:(e + 1) * C_SLOT]
        qt = qt_ref[e * C_SLOT:(e + 1) * C_SLOT, pl.ds(qoff, C_TILE)]
        return jnp.dot(k, qt, preferred_element_type=F32)

    def put_scores(qi, j, e):
        s = scores(qi, j, e)
        s_ref[e] = s
        mt_ref[e] = jnp.max(s, axis=0, keepdims=True)

    def softmax_pv(j, e, s, m_tile):
        off = pl.multiple_of(j * C_TILE, C_TILE)
        m_prev = m_ref[e]
        m_new = jnp.maximum(m_prev, m_tile)
        p = jnp.exp2((s - m_new) * c).astype(BF16)
        alpha = jnp.exp2((m_prev - m_new) * c)
        vt = vt_ref[e * HEAD_DIM:(e + 1) * HEAD_DIM, pl.ds(off, C_TILE)]
        vt_aug = jnp.concatenate([vt, jnp.ones((ONES_ROWS, C_TILE), BF16)], axis=0)
        pv = jnp.dot(vt_aug, p, preferred_element_type=F32)
        l_ref[e] = alpha * l_ref[e] + pv[HEAD_DIM:HEAD_DIM + 1]
        acc_ref[e] = alpha * acc_ref[e] + pv[:HEAD_DIM]
        m_ref[e] = m_new

    kc = lax.broadcasted_iota(jnp.int32, (C_TILE, C_TILE), 0) // CHUNK
    qc = lax.broadcasted_iota(jnp.int32, (C_TILE, C_TILE), 1) // CHUNK

    for e in range(2):
        put_scores(0, 0, e)

    def q_body(qi, carry):
        m_ref[...] = jnp.full_like(m_ref, -jnp.inf)
        l_ref[...] = jnp.zeros_like(l_ref)
        acc_ref[...] = jnp.zeros_like(acc_ref)

        def body(j, carry2):
            for e in range(2):
                s, m_tile = s_ref[e], mt_ref[e]
                put_scores(qi, j + 1, e)
                softmax_pv(j, e, s, m_tile)
            return carry2

        lax.fori_loop(0, qi, body, 0)
        q_next = jnp.minimum(qi + 1, n_qt - 1)
        for e in range(2):
            s = jnp.where(kc <= qc, s_ref[e], NEG_BIG)
            put_scores(q_next, 0, e)
            softmax_pv(qi, e, s, jnp.max(s, axis=0, keepdims=True))

        o_t = jnp.concatenate([acc_ref[0] / l_ref[0], acc_ref[1] / l_ref[1]], axis=0)
        o_ref[pl.ds(pl.multiple_of(qi * C_TILE, C_TILE), C_TILE), :] = o_t.T.astype(o_ref.dtype)
        return carry

    lax.fori_loop(0, n_qt, q_body, 0)


def mla_flash(qt, k, vt, bsz, seq):
    t = bsz * seq
    return pl.pallas_call(
        functools.partial(_mla_flash_kernel, n_qt=seq // C_TILE),
        out_shape=jax.ShapeDtypeStruct((t, WIDTH), BF16),
        grid=(bsz, N_PAIRS),
        in_specs=[
            pl.BlockSpec((2 * C_SLOT, seq), lambda b, p: (p, b)),
            pl.BlockSpec((seq, 2 * C_SLOT), lambda b, p: (b, p)),
            pl.BlockSpec((LANES, seq), lambda b, p: (p, b)),
        ],
        out_specs=pl.BlockSpec((seq, LANES), lambda b, p: (b, p)),
        scratch_shapes=[pltpu.VMEM((2, HEAD_DIM, C_TILE), F32),
                        pltpu.VMEM((2, 1, C_TILE), F32),
                        pltpu.VMEM((2, 1, C_TILE), F32),
                        pltpu.VMEM((2, C_TILE, C_TILE), F32),
                        pltpu.VMEM((2, 1, C_TILE), F32)],
        compiler_params=_cparams(("parallel", "parallel")),
        name="mla_flash",
    )(qt, k, vt)


def _merge_kernel(x_ref, ya_ref, yb_ref, yc_ref, g_ref, wbr_ref, wout_ref, o_ref):
    merged = None
    for n, y_ref in enumerate((ya_ref, yb_ref, yc_ref)):
        up = jnp.dot(y_ref[...], wbr_ref[n], preferred_element_type=F32)
        gl = g_ref[:, n * D_MODEL:(n + 1) * D_MODEL].astype(F32)
        term = (1.0 / (1.0 + jnp.exp(-gl))) * up
        merged = term if merged is None else merged + term
    o_ref[...] = x_ref[...] + jnp.dot(merged.astype(BF16), wout_ref[...], preferred_element_type=F32)


def merge_out(x, ya, yb, yc, gates, wbr, wout, tm):
    t = x.shape[0]
    return pl.pallas_call(
        _merge_kernel,
        out_shape=jax.ShapeDtypeStruct((t, D_MODEL), F32),
        grid=(t // tm,),
        in_specs=[
            pl.BlockSpec((tm, D_MODEL), lambda i: (i, 0)),
            pl.BlockSpec((tm, WIDTH), lambda i: (i, 0)),
            pl.BlockSpec((tm, WIDTH), lambda i: (i, 0)),
            pl.BlockSpec((tm, WIDTH), lambda i: (i, 0)),
            pl.BlockSpec((tm, N_BRANCH * D_MODEL), lambda i: (i, 0)),
            _resident(wbr),
            _resident(wout),
        ],
        out_specs=pl.BlockSpec((tm, D_MODEL), lambda i: (i, 0)),
        compiler_params=_cparams(("parallel",)),
        name="merge_out",
    )(x, ya, yb, yc, gates, wbr, wout)


def _ffn_kernel(x_ref, g_ref, w1_ref, w2_ref, gf_ref, o_ref, *, final_norm, tf):
    x = x_ref[...]
    h = _rms(x, g_ref[...]).astype(BF16)
    acc = None
    for k in range(w1_ref.shape[1] // tf):
        u = jnp.maximum(jnp.dot(h, w1_ref[:, k * tf:(k + 1) * tf], preferred_element_type=F32), 0.0)
        part = jnp.dot((u * u).astype(BF16), w2_ref[k * tf:(k + 1) * tf, :], preferred_element_type=F32)
        acc = part if acc is None else acc + part
    y = x + acc
    if final_norm:
        y = _rms(y, gf_ref[...])
    o_ref[...] = y


def ffn(x, g, w1, w2, gf, final_norm, tm, tf):
    t, d = x.shape
    return pl.pallas_call(
        functools.partial(_ffn_kernel, final_norm=final_norm, tf=tf),
        out_shape=jax.ShapeDtypeStruct((t, d), F32),
        grid=(t // tm,),
        in_specs=[
            pl.BlockSpec((tm, d), lambda i: (i, 0)),
            pl.BlockSpec((1, d), lambda i: (0, 0)),
            _resident(w1),
            _resident(w2),
            pl.BlockSpec((1, d), lambda i: (0, 0)),
        ],
        out_specs=pl.BlockSpec((tm, d), lambda i: (i, 0)),
        compiler_params=_cparams(("parallel",)),
        name="ffn",
    )(x, g.reshape(1, d), w1, w2, gf.reshape(1, d))


COL_AQ, COL_AK, COL_AV = 0, 4, 8
COL_BQ, COL_BI, COL_BG = 12, 16, 20
GATE_BLK = 1


def _prep_w_in(w):
    o = 0
    parts = {}
    for name, size in (("aq", 512), ("ak", 512), ("av", 512), ("bq", 512), ("bf", 512), ("bi", 512),
                       ("bg", 512), ("cq", C_Q_RANK), ("ckv", C_KV_RANK), ("ckr", C_ROPE),
                       ("gate", N_BRANCH * D_MODEL)):
        parts[name] = w[:, o:o + size]
        o += size
    w16 = jnp.concatenate([parts[n] for n in ("aq", "ak", "av", "bq", "bi", "bg", "gate")], axis=1)
    half = C_ROPE // 2
    x1, x2 = parts["ckr"][:, :half], parts["ckr"][:, half:]
    zl = jnp.zeros((w.shape[0], C_NOPE), w.dtype)
    zr = jnp.zeros((w.shape[0], C_SLOT - C_NOPE - C_ROPE), w.dtype)
    kr_slot = jnp.concatenate([zl, x1, x2, zr], axis=1)
    kr_sw = jnp.concatenate([zl, -x2, x1, zr], axis=1)
    w32 = jnp.concatenate([parts["cq"], parts["ckv"], kr_slot, kr_sw, parts["bf"]], axis=1)
    return w16.astype(BF16), w32.astype(BF16)


def _prep_w_mla(w_uq, w_ukv):
    half = C_ROPE // 2
    per_q = C_NOPE + C_ROPE
    wq = w_uq.reshape(C_Q_RANK, N_HEADS, per_q)
    nope, x1, x2 = wq[..., :C_NOPE], wq[..., C_NOPE:C_NOPE + half], wq[..., C_NOPE + half:]
    zpad = jnp.zeros((C_Q_RANK, N_HEADS, C_SLOT - per_q), w_uq.dtype)
    wq1 = jnp.concatenate([nope, x1, x2, zpad], axis=-1).reshape(C_Q_RANK, N_HEADS * C_SLOT)
    wq2 = jnp.concatenate([jnp.zeros_like(nope), -x2, x1, zpad], axis=-1).reshape(C_Q_RANK, N_HEADS * C_SLOT)
    wkv = w_ukv.reshape(C_KV_RANK, N_HEADS, C_NOPE + HEAD_DIM)
    k_nope, v = wkv[..., :C_NOPE], wkv[..., C_NOPE:]
    wk = jnp.concatenate([k_nope, jnp.zeros((C_KV_RANK, N_HEADS, C_SLOT - C_NOPE), w_ukv.dtype)],
                         axis=-1).reshape(C_KV_RANK, N_HEADS * C_SLOT)
    wv = v.reshape(C_KV_RANK, WIDTH)
    return wq1.T.astype(BF16), wq2.T.astype(BF16), wk.astype(BF16), wv.T.astype(BF16)


def kernel(x, positions, norm_mix_g, w_in, rel_bias, hgrn_lb_logits, hgrn_norm_g, mla_q_norm_g,
           mla_kv_norm_g, mla_w_uq, mla_w_ukv, w_branch, w_out, norm_ffn_g, w_ff1, w_ff2, final_norm_g):
    bsz, seq, d = x.shape
    depth = w_in.shape[0]
    t = bsz * seq
    assert d == D_MODEL and seq % C_TILE == 0 and seq % (A_QTILE * A_TILES_PER_STEP) == 0
    tm = min(TM_DENSE, t)

    p_lb = jax.nn.softmax(hgrn_lb_logits.astype(F32), axis=0)
    lb_all = jnp.cumsum(p_lb, axis=0)
    lb_all = lb_all - lb_all[0:1]

    tabs = rope_tables(positions, tm)
    xf = x.reshape(t, d)
    for l in range(depth):
        w16, w32 = _prep_w_in(w_in[l])
        wq1t, wq2t, wk, wvt = _prep_w_mla(mla_w_uq[l], mla_w_ukv[l])
        proj_a, gates, y_b, qt_c, k_c, vt_c = in_proj(xf, norm_mix_g[l], w16, w32, tabs, mla_q_norm_g[l], mla_kv_norm_g[l],
                                               wq1t, wq2t, wk, wvt, lb_all[l], hgrn_norm_g[l], seq,
                                               min(TM_IN_PROJ, seq), TN_IN_PROJ)

        y_a = band_attention(proj_a, band_bias_table(rel_bias[l]), bsz, seq, COL_AQ, COL_AK, COL_AV)
        y_c = mla_flash(qt_c, k_c, vt_c, bsz, seq)

        xf = merge_out(xf, y_a, y_b, y_c, gates, w_branch[l].astype(BF16), w_out[l].astype(BF16), tm)
        xf = ffn(xf, norm_ffn_g[l], w_ff1[l].astype(BF16), w_ff2[l].astype(BF16), final_norm_g,
                 l == depth - 1, tm, TF_FFN)
    return xf.reshape(bsz, seq, d)
```

```python
import functools

import jax
import jax.numpy as jnp
from jax import lax
from jax.experimental import pallas as pl
from jax.experimental.pallas import tpu as pltpu

F32 = jnp.float32
BF16 = jnp.bfloat16

D_MODEL = 1024
CHUNK = 64
EPS = 1e-6
N_HEADS = 8
HEAD_DIM = 64
LANES = 128
N_PAIRS = N_HEADS * HEAD_DIM // LANES
WIDTH = N_HEADS * HEAD_DIM

A_LEFT_CHUNKS = 8
A_MAX_REL = 128
A_QTILE = 2 * CHUNK
A_BAND = (A_LEFT_CHUNKS + 2) * CHUNK
A_PAD = A_LEFT_CHUNKS * CHUNK
A_TILES_PER_STEP = 16

C_Q_RANK = 256
C_KV_RANK = 128
C_ROPE = 32
C_NOPE = 64
C_SLOT = LANES
C_LATENT = C_Q_RANK + C_KV_RANK + 2 * C_SLOT
ROPE_BASE = 10000.0
C_TILE = 512
ONES_ROWS = 16
LOG2E = 1.4426950408889634
C_QSCALE = (C_NOPE + C_ROPE) ** -0.5 * LOG2E

N_BRANCH = 3

SUB = 8
SUP = 2 * SUB
N_SUB = CHUNK // SUB
NEG_BIG = -1e30

VMEM_LIMIT = 48 * 1024 * 1024
VMEM_LIMIT_IN_PROJ = 58 * 1024 * 1024

TM_IN_PROJ = 512
TN_IN_PROJ = 1024
TM_DENSE = 1024
TF_FFN = 1024

_NT = (((1,), (1,)), ((), ()))


def _cparams(sem, vmem_limit=VMEM_LIMIT):
    return pltpu.CompilerParams(dimension_semantics=sem, vmem_limit_bytes=vmem_limit)


def _resident(a):
    return pl.BlockSpec(a.shape, lambda *_: (0,) * a.ndim, pipeline_mode=pl.Buffered(1))


def _rms(x, g):
    return x * lax.rsqrt(jnp.mean(x * x, axis=-1, keepdims=True) + EPS) * g


def _in_proj_kernel(x_ref, g_ref, wa_ref, wb_ref, ct_ref, st_ref, ctt_ref, stt_ref, gq_ref, gkv_ref,
                    wq1t_ref, wq2t_ref, wk_ref, wvt_ref, loglb_ref, log1mlb_ref, ng_ref,
                    oa_ref, og_ref, yb_ref, qt_ref, k_ref, vt_ref, state_ref, *, tn, tiles_per_seq):
    @pl.when(pl.program_id(0) % tiles_per_seq == 0)
    def _():
        state_ref[...] = jnp.zeros_like(state_ref)

    h = _rms(x_ref[...], g_ref[...]).astype(BF16)

    a_cols = COL_BQ * LANES
    g_col0 = GATE_BLK * N_BRANCH * D_MODEL

    def proj_tile(j):
        lo, hi = j * tn, (j + 1) * tn
        t = jnp.dot(h, wa_ref[:, lo:hi], preferred_element_type=F32)
        if lo < a_cols:
            oa_ref[:, lo:min(hi, a_cols)] = t[:, :min(hi, a_cols) - lo].astype(oa_ref.dtype)
        if hi > g_col0:
            og_ref[:, max(lo, g_col0) - g_col0:hi - g_col0] = t[:, max(lo, g_col0) - lo:].astype(og_ref.dtype)
        return t

    tq = proj_tile(COL_BQ * LANES // tn)
    tig = proj_tile(COL_BI * LANES // tn)
    pb = jnp.dot(h, wb_ref[...], preferred_element_type=F32)
    q_off = COL_BQ * LANES % tn
    i_off = COL_BI * LANES % tn
    g_off = COL_BG * LANES - (COL_BI * LANES // tn) * tn
    y_out = []
    gens = []
    for p in range(N_PAIRS):
        lanes = lambda off: slice(off + p * LANES, off + (p + 1) * LANES)
        gens.append(_hgrn_stages(pb[:, lanes(C_LATENT)], tq[:, lanes(q_off)], tig[:, lanes(i_off)],
                                 tig[:, lanes(g_off)], loglb_ref[p], log1mlb_ref[p], ng_ref[...],
                                 state_ref.at[p], y_out))

    qn = _rms(pb[:, :C_Q_RANK], gq_ref[...]).astype(BF16)
    kvn = _rms(pb[:, C_Q_RANK:C_Q_RANK + C_KV_RANK], gkv_ref[...]).astype(BF16)
    kr = pb[:, C_Q_RANK + C_KV_RANK:C_Q_RANK + C_KV_RANK + C_SLOT]
    kr_sw = pb[:, C_Q_RANK + C_KV_RANK + C_SLOT:C_LATENT]
    kr_rot = kr * ct_ref[...] + kr_sw * st_ref[...]

    def mla_q():
        a_t = lax.dot_general(wq1t_ref[...], qn, _NT, preferred_element_type=F32)
        b_t = lax.dot_general(wq2t_ref[...], qn, _NT, preferred_element_type=F32)
        ctt = ctt_ref[...]
        stt = stt_ref[...]
        for hd in range(N_HEADS):
            sl = slice(hd * C_SLOT, (hd + 1) * C_SLOT)
            qt_ref[sl, :] = (a_t[sl, :] * ctt + b_t[sl, :] * stt).astype(qt_ref.dtype)

    def mla_kv():
        kn = jnp.dot(kvn, wk_ref[...], preferred_element_type=F32)
        vt_ref[...] = lax.dot_general(wvt_ref[...], kvn, _NT, preferred_element_type=F32).astype(vt_ref.dtype)
        for hd in range(N_HEADS):
            sl = slice(hd * C_SLOT, (hd + 1) * C_SLOT)
            k_ref[:, sl] = (kn[:, sl] + kr_rot).astype(k_ref.dtype)

    done = {COL_BQ * LANES // tn, COL_BI * LANES // tn}
    big = [functools.partial(proj_tile, j) for j in range(wa_ref.shape[1] // tn) if j not in done] + [mla_q, mla_kv]
    assert len(big) == HGRN_STAGES
    for work in big:
        work()
        for gen in gens:
            next(gen)
    for p in range(N_PAIRS):
        yb_ref[:, p * LANES:(p + 1) * LANES] = y_out[p].astype(yb_ref.dtype)


def in_proj(x, g, wa, wb, tabs, gq, gkv, wq1t, wq2t, wk, wvt, lb, hgrn_norm_g, seq, tm, tn):
    t, d = x.shape
    na = COL_BQ * LANES
    ng_cols = N_BRANCH * D_MODEL
    ctab, stab, ctab_t, stab_t = tabs
    gq = gq.reshape(1, -1)
    gkv = gkv.reshape(1, -1)
    log_lb = jnp.log(lb).reshape(N_PAIRS, 1, LANES)
    log1m_lb = jnp.log1p(-lb).reshape(N_PAIRS, 1, LANES)
    ng = jnp.tile(hgrn_norm_g.astype(F32), LANES // HEAD_DIM).reshape(1, LANES)
    rows = lambda n: pl.BlockSpec((tm, n), lambda i: (i, 0))
    cols = lambda n: pl.BlockSpec((n, tm), lambda i: (0, i))
    return pl.pallas_call(
        functools.partial(_in_proj_kernel, tn=tn, tiles_per_seq=seq // tm),
        out_shape=(jax.ShapeDtypeStruct((t, na), BF16),
                   jax.ShapeDtypeStruct((t, ng_cols), BF16),
                   jax.ShapeDtypeStruct((t, WIDTH), BF16),
                   jax.ShapeDtypeStruct((N_HEADS * C_SLOT, t), BF16),
                   jax.ShapeDtypeStruct((t, N_HEADS * C_SLOT), BF16),
                   jax.ShapeDtypeStruct((WIDTH, t), BF16)),
        grid=(t // tm,),
        in_specs=[rows(d), pl.BlockSpec((1, d), lambda i: (0, 0)), _resident(wa), _resident(wb),
                  rows(LANES), rows(LANES), cols(LANES), cols(LANES),
                  _resident(gq), _resident(gkv), _resident(wq1t), _resident(wq2t), _resident(wk), _resident(wvt),
                  _resident(log_lb), _resident(log1m_lb), _resident(ng)],
        out_specs=(rows(na), rows(ng_cols), rows(WIDTH), cols(N_HEADS * C_SLOT), rows(N_HEADS * C_SLOT), cols(WIDTH)),
        scratch_shapes=[pltpu.VMEM((N_PAIRS, LANES, LANES), F32)],
        compiler_params=_cparams(("arbitrary",), VMEM_LIMIT_IN_PROJ),
        name="in_proj",
    )(x, g.reshape(1, d), wa, wb, ctab, stab, ctab_t, stab_t, gq, gkv, wq1t, wq2t, wk, wvt, log_lb, log1m_lb, ng)


def _trig_kernel(pos_ref, invf_ref, c_ref, s_ref, ct_ref, st_ref):
    ang = pos_ref[...].astype(F32) * invf_ref[...]
    c = jnp.cos(ang)
    s = jnp.sin(ang)
    c_ref[...] = c
    s_ref[...] = s
    ct_ref[...] = c.T
    st_ref[...] = s.T


def rope_tables(positions, tm):
    t = positions.size
    inv_freq = ROPE_BASE ** (-jnp.arange(0, C_ROPE, 2, dtype=F32) / C_ROPE)
    half = C_ROPE // 2
    invf = jnp.zeros((LANES,), F32)
    invf = invf.at[C_NOPE:C_NOPE + half].set(inv_freq).at[C_NOPE + half:C_NOPE + C_ROPE].set(inv_freq)
    return pl.pallas_call(
        _trig_kernel,
        out_shape=(jax.ShapeDtypeStruct((t, LANES), F32),) * 2 + (jax.ShapeDtypeStruct((LANES, t), F32),) * 2,
        grid=(t // tm,),
        in_specs=[
            pl.BlockSpec((tm, 1), lambda i: (i, 0)),
            pl.BlockSpec((1, LANES), lambda i: (0, 0)),
        ],
        out_specs=(pl.BlockSpec((tm, LANES), lambda i: (i, 0)),) * 2
        + (pl.BlockSpec((LANES, tm), lambda i: (0, i)),) * 2,
        compiler_params=_cparams(("parallel",)),
        name="rope_tables",
    )(positions.reshape(t, 1), invf.reshape(1, LANES))


def _band_attn_kernel(q_ref, k_ref, v_ref, bias_ref, o_ref):
    lane = lax.broadcasted_iota(jnp.int32, (A_QTILE, LANES), 1)
    n_shift = A_PAD // A_QTILE
    starts, scores = [], []
    for t in range(A_TILES_PER_STEP):
        c2 = pl.program_id(2) * A_TILES_PER_STEP + t
        start = pl.multiple_of(jnp.maximum(c2 * A_QTILE - A_PAD, 0), A_QTILE)
        shift = jnp.minimum(c2, n_shift)
        kwin = k_ref[pl.ds(start, A_BAND), :]
        q = q_ref[t * A_QTILE:(t + 1) * A_QTILE, :].astype(F32) * (HEAD_DIM ** -0.5)
        qst = jnp.concatenate([jnp.where(lane < HEAD_DIM, q, 0.0), jnp.where(lane < HEAD_DIM, 0.0, q)],
                              axis=0).astype(BF16)
        s = lax.dot_general(qst, kwin, _NT, preferred_element_type=F32)
        starts.append(start)
        scores.append(s + bias_ref[shift, 0])
    for t in range(A_TILES_PER_STEP):
        s = scores[t]
        vwin = v_ref[pl.ds(starts[t], A_BAND), :]
        m = jnp.max(s, axis=-1, keepdims=True)
        p = jnp.exp(s - m)
        l = jnp.sum(p, axis=-1, keepdims=True)
        pv = jnp.dot(p.astype(BF16), vwin, preferred_element_type=F32) / l
        o_ref[t * A_QTILE:(t + 1) * A_QTILE, :] = jnp.where(lane < HEAD_DIM, pv[:A_QTILE], pv[A_QTILE:]).astype(o_ref.dtype)


def band_attention(proj, bias, bsz, seq, q_col, k_col, v_col):
    t = bsz * seq
    rows = A_QTILE * A_TILES_PER_STEP
    n_steps = seq // rows
    n_shift = A_PAD // A_QTILE
    return pl.pallas_call(
        _band_attn_kernel,
        out_shape=jax.ShapeDtypeStruct((t, WIDTH), BF16),
        grid=(bsz, N_PAIRS, n_steps),
        in_specs=[
            pl.BlockSpec((rows, LANES), lambda b, p, c: (b * n_steps + c, q_col + p)),
            pl.BlockSpec((seq, LANES), lambda b, p, c: (b, k_col + p)),
            pl.BlockSpec((seq, LANES), lambda b, p, c: (b, v_col + p)),
            pl.BlockSpec((n_shift + 1, 1, 2 * A_QTILE, A_BAND), lambda b, p, c: (0, p, 0, 0)),
        ],
        out_specs=pl.BlockSpec((rows, LANES), lambda b, p, c: (b * n_steps + c, p)),
        compiler_params=_cparams(("parallel", "parallel", "arbitrary")),
        name="band_attention",
    )(proj, proj, proj, bias.reshape(n_shift + 1, N_PAIRS, 2 * A_QTILE, A_BAND))


A_ROLL = 768


def _band_bias_kernel(r_ref, o_ref):
    d = pl.program_id(0) * A_QTILE
    prof = jnp.broadcast_to(r_ref[...], (A_QTILE, A_ROLL))
    b = pltpu.roll(prof, 0, 1, stride=1, stride_axis=0)[:, :A_BAND]
    i = lax.broadcasted_iota(jnp.int32, (A_QTILE, A_BAND), 0)
    j = lax.broadcasted_iota(jnp.int32, (A_QTILE, A_BAND), 1)
    gap = (d + i) // CHUNK - j // CHUNK
    o_ref[...] = jnp.where((gap >= 0) & (gap <= A_LEFT_CHUNKS), b, NEG_BIG)


def band_bias_table(rel_table):
    n_shift = A_PAD // A_QTILE
    heads = rel_table.shape[0]
    pad = A_ROLL + A_PAD
    ext = jnp.pad(rel_table.astype(F32)[:, ::-1], ((0, 0), (pad, pad)), mode="edge")

    def seg(d, u0, n):
        o = u0 - d + A_MAX_REL + pad
        return ext[:, o:o + n]

    n_neg = A_ROLL - (A_BAND + 1)
    prof = jnp.stack([jnp.concatenate([seg(s * A_QTILE, 0, A_BAND + 1), seg(s * A_QTILE, -n_neg, n_neg)], axis=1)
                      for s in range(n_shift + 1)])
    return pl.pallas_call(
        _band_bias_kernel,
        out_shape=jax.ShapeDtypeStruct((n_shift + 1, heads, A_QTILE, A_BAND), F32),
        grid=(n_shift + 1, heads),
        in_specs=[pl.BlockSpec((None, None, 1, A_ROLL), lambda s, h: (s, h, 0, 0))],
        out_specs=pl.BlockSpec((None, None, A_QTILE, A_BAND), lambda s, h: (s, h, 0, 0)),
        compiler_params=_cparams(("parallel", "parallel")),
        name="band_bias",
    )(prof.reshape(n_shift + 1, heads, 1, A_ROLL))


def _split3(x):
    hi = x.astype(BF16)
    r1 = x - hi.astype(F32)
    mid = r1.astype(BF16)
    lo = (r1 - mid.astype(F32)).astype(BF16)
    return hi, mid, lo


HGRN_STAGES = 6


def _hgrn_stages(z, qraw, v, graw, log_lb, log1m_lb, norm_g, st_ref, y_out):
    rows = z.shape[0]
    n_chunks = rows // CHUNK
    n_blk = rows // SUB

    head0 = lax.broadcasted_iota(jnp.int32, (rows, LANES), 1) < HEAD_DIM
    head0_c = lax.broadcasted_iota(jnp.int32, (CHUNK, LANES), 1) < HEAD_DIM
    r64 = lax.broadcasted_iota(jnp.int32, (CHUNK, CHUNK), 0)
    c64 = lax.broadcasted_iota(jnp.int32, (CHUNK, CHUNK), 1)
    tril = (c64 <= r64).astype(BF16)
    c64s = lax.broadcasted_iota(jnp.int32, (2 * CHUNK, CHUNK), 1)
    rl = lax.broadcasted_iota(jnp.int32, (LANES, LANES), 0)
    cl = lax.broadcasted_iota(jnp.int32, (LANES, LANES), 1)
    same_head = (rl // HEAD_DIM) == (cl // HEAD_DIM)
    ri = lax.broadcasted_iota(jnp.int32, (CHUNK, CHUNK * SUB), 0)
    ci = lax.broadcasted_iota(jnp.int32, (CHUNK, CHUNK * SUB), 1)
    pick = (ci // SUB == ri).astype(BF16)
    row_s = lax.broadcasted_iota(jnp.int32, (SUB, LANES), 0)

    def chunk(x, c):
        return x[c * CHUNK:(c + 1) * CHUNK]

    log_sig = jnp.minimum(z, 0.0) - jnp.log(1.0 + jnp.exp(-jnp.abs(z)))
    bterm = log1m_lb + log_sig
    log_f = jnp.maximum(log_lb, bterm) + jnp.log(1.0 + jnp.exp(-jnp.abs(log_lb - bterm)))
    log_k = bterm - z
    qs = qraw * (1.0 / (1.0 + jnp.exp(-qraw)))

    x3 = jnp.concatenate(_split3(log_f), axis=1)
    cum3 = [jnp.dot(tril, chunk(x3, c), preferred_element_type=F32) for c in range(n_chunks)]
    yield
    cum = jnp.concatenate([t[:, :LANES] + t[:, LANES:2 * LANES] + t[:, 2 * LANES:] for t in cum3], axis=0)

    zero_row = jnp.zeros((1, LANES), F32)
    c_end = [cum[SUB * b + SUB - 1:SUB * b + SUB, :] for b in range(n_blk)]
    c_start = [zero_row if b % N_SUB == 0 else c_end[b - 1] for b in range(n_blk)]
    c_last = [c_end[c * N_SUB + N_SUB - 1] for c in range(n_chunks)]

    def rows_of(blocks):
        return jnp.concatenate([jnp.broadcast_to(r, (SUB, LANES)) for r in blocks], axis=0)

    cstart_full = rows_of(c_start)
    cend_full = rows_of(c_end)
    clast_full = jnp.concatenate([jnp.broadcast_to(r, (CHUNK, LANES)) for r in c_last], axis=0)
    q1 = qs * jnp.exp(cum - cstart_full)
    lk = log_k - cum
    k2b = jnp.exp(cend_full + lk).astype(BF16)
    k_end = jnp.exp(clast_full + lk).astype(BF16)
    q_state = (qs * jnp.exp(cum)).astype(BF16)
    vb = v.astype(BF16)

    n_sup = CHUNK // SUP
    ce_sup = [c_end[(SUP // SUB) * (b + 1) - 1] for b in range(rows // SUP)]
    cs_sup = [zero_row if b % n_sup == 0 else ce_sup[b - 1] for b in range(rows // SUP)]

    def rows_of_sup(blocks):
        return jnp.concatenate([jnp.broadcast_to(r, (SUP, LANES)) for r in blocks], axis=0)

    q1s = qs * jnp.exp(cum - rows_of_sup(cs_sup))
    k2s = jnp.exp(rows_of_sup(ce_sup) + lk).astype(BF16)
    odd_sub = (lax.broadcasted_iota(jnp.int32, (rows, LANES), 0) // SUB) % 2 == 1
    q_adj = jnp.where(odd_sub, q1, 0.0)
    q_adj = (jnp.where(head0, q_adj, 0.0).astype(BF16), jnp.where(head0, 0.0, q_adj).astype(BF16))
    q_sup = []
    for j in range(n_sup - 1):
        d_rows = []
        for b in range(rows // SUP):
            if b % n_sup > j:
                d_rows.append(jnp.exp(cs_sup[b] - ce_sup[(b // n_sup) * n_sup + j]))
            else:
                d_rows.append(zero_row)
        qj = q1s * rows_of_sup(d_rows)
        q_sup.append((jnp.where(head0, qj, 0.0).astype(BF16), jnp.where(head0, 0.0, qj).astype(BF16)))
    s_adj, s_sup = [], []
    for c in range(n_chunks):
        qst = jnp.concatenate([chunk(q_adj[0], c), chunk(q_adj[1], c)], axis=0)
        s_adj.append(lax.dot_general(qst, chunk(k2b, c), _NT, preferred_element_type=F32))
        for j in range(n_sup - 1):
            qst = jnp.concatenate([chunk(q_sup[j][0], c), chunk(q_sup[j][1], c)], axis=0)
            s_sup.append(lax.dot_general(qst, chunk(k2s, c), _NT, preferred_element_type=F32))
    yield
    r64s = lax.broadcasted_iota(jnp.int32, (2 * CHUNK, CHUNK), 0) % CHUNK
    prev_sub = c64s // SUB == r64s // SUB - 1
    o_cross = []
    for c in range(n_chunks):
        sc = jnp.where(prev_sub, s_adj[c], 0.0)
        for j in range(n_sup - 1):
            col_in_j = (c64s >= j * SUP) & (c64s < (j + 1) * SUP)
            sc = sc + jnp.where(col_in_j, s_sup[c * (n_sup - 1) + j], 0.0)
        o2 = jnp.dot(sc.astype(BF16), chunk(vb, c), preferred_element_type=F32)
        o_cross.append(jnp.where(head0_c, o2[:CHUNK], o2[CHUNK:]))

    yield
    a = cum - log_k
    w_rows = []
    for b in range(n_blk):
        a_b = a[SUB * b:SUB * (b + 1), :]
        for i in range(SUB):
            r = SUB * b + i
            arg = jnp.where(row_s <= i, cum[r:r + 1, :] - a_b, NEG_BIG)
            w_rows.append(jnp.exp(arg) * qs[r:r + 1, :])
    w_all = jnp.concatenate(w_rows, axis=0)
    head0_w = lax.broadcasted_iota(jnp.int32, w_all.shape, 1) < HEAD_DIM
    sum0 = jnp.sum(jnp.where(head0_w, w_all, 0.0), axis=-1, keepdims=True)
    sum1 = jnp.sum(jnp.where(head0_w, 0.0, w_all), axis=-1, keepdims=True)
    sb = jnp.where(head0_w, sum0, sum1)
    yield
    o_diag = []
    for c in range(n_chunks):
        v_rep = jnp.concatenate([v[SUB * (r // SUB):SUB * (r // SUB + 1), :]
                                 for r in range(c * CHUNK, (c + 1) * CHUNK)], axis=0)
        sb_c = sb[c * CHUNK * SUB:(c + 1) * CHUNK * SUB]
        o_diag.append(jnp.dot(pick, (sb_c * v_rep).astype(BF16), preferred_element_type=F32))

    yield
    upd = [jnp.where(same_head,
                     lax.dot_general(chunk(vb, c), chunk(k_end, c), (((0,), (0,)), ((), ())),
                                     preferred_element_type=F32), 0.0) for c in range(n_chunks)]
    st = st_ref[...]
    o_state = []
    for c in range(n_chunks):
        o_state.append(lax.dot_general(chunk(q_state, c), st.astype(BF16), _NT, preferred_element_type=F32))
        st = st * jnp.exp(c_last[c]) + upd[c]
    st_ref[...] = st

    o = jnp.concatenate([o_cross[c] + o_diag[c] + o_state[c] for c in range(n_chunks)], axis=0)
    osq = o * o
    ms0 = jnp.sum(jnp.where(head0, osq, 0.0), axis=-1, keepdims=True)
    ms1 = jnp.sum(jnp.where(head0, 0.0, osq), axis=-1, keepdims=True)
    ms = jnp.where(head0, ms0, ms1) * (1.0 / HEAD_DIM)
    y = o * lax.rsqrt(ms + EPS) * norm_g
    y_out.append(y * (graw * (1.0 / (1.0 + jnp.exp(-graw)))))
    yield


def _mla_flash_kernel(qt_ref, k0_ref, k1_ref, vt_ref, o_ref, acc_ref, m_ref, l_ref, s_ref, mt_ref, *, n_qt):
    c = C_QSCALE

    def scores(qi, j, e):
        qoff = pl.multiple_of(qi * C_TILE, C_TILE)
        koff = pl.multiple_of(j * C_TILE, C_TILE)
        k = (k0_ref, k1_ref)[e][pl.ds(koff, C_TILE), :]
        qt = qt_ref[e * C_SLOT:(e + 1) * C_SLOT, pl.ds(qoff, C_TILE)]
        return jnp.dot(k, qt, preferred_element_type=F32)

    def put_scores(qi, j, e):
        s = scores(qi, j, e)
        s_ref[e] = s
        mt_ref[e] = jnp.max(s, axis=0, keepdims=True)

    def softmax_pv(j, e, s, m_tile):
        off = pl.multiple_of(j * C_TILE, C_TILE)
        m_prev = m_ref[e]
        m_new = jnp.maximum(m_prev, m_tile)
        p = jnp.exp2((s - m_new) * c).astype(BF16)
        alpha = jnp.exp2((m_prev - m_new) * c)
        vt = vt_ref[e * HEAD_DIM:(e + 1) * HEAD_DIM, pl.ds(off, C_TILE)]
        vt_aug = jnp.concatenate([vt, jnp.ones((ONES_ROWS, C_TILE), BF16)], axis=0)
        pv = jnp.dot(vt_aug, p, preferred_element_type=F32)
        l_ref[e] = alpha * l_ref[e] + pv[HEAD_DIM:HEAD_DIM + 1]
        acc_ref[e] = alpha * acc_ref[e] + pv[:HEAD_DIM]
        m_ref[e] = m_new

    kc = lax.broadcasted_iota(jnp.int32, (C_TILE, C_TILE), 0) // CHUNK
    qc = lax.broadcasted_iota(jnp.int32, (C_TILE, C_TILE), 1) // CHUNK

    for e in range(2):
        put_scores(0, 0, e)

    def q_body(qi, carry):
        m_ref[...] = jnp.full_like(m_ref, -jnp.inf)
        l_ref[...] = jnp.zeros_like(l_ref)
        acc_ref[...] = jnp.zeros_like(acc_ref)

        def body(j, carry2):
            for e in range(2):
                s, m_tile = s_ref[e], mt_ref[e]
                put_scores(qi, j + 1, e)
                softmax_pv(j, e, s, m_tile)
            return carry2

        lax.fori_loop(0, qi, body, 0)
        q_next = jnp.minimum(qi + 1, n_qt - 1)
        for e in range(2):
            s = jnp.where(kc <= qc, s_ref[e], NEG_BIG)
            put_scores(q_next, 0, e)
            softmax_pv(qi, e, s, jnp.max(s, axis=0, keepdims=True))

        o_t = jnp.concatenate([acc_ref[0] / l_ref[0], acc_ref[1] / l_ref[1]], axis=0)
        o_ref[pl.ds(pl.multiple_of(qi * C_TILE, C_TILE), C_TILE), :] = o_t.T.astype(o_ref.dtype)
        return carry

    lax.fori_loop(0, n_qt, q_body, 0)


def mla_flash(qt, k, vt, bsz, seq):
    t = bsz * seq
    return pl.pallas_call(
        functools.partial(_mla_flash_kernel, n_qt=seq // C_TILE),
        out_shape=jax.ShapeDtypeStruct((t, WIDTH), BF16),
        grid=(bsz, N_PAIRS),
        in_specs=[
            pl.BlockSpec((2 * C_SLOT, seq), lambda b, p: (p, b)),
            pl.BlockSpec((seq, C_SLOT), lambda b, p: (b, 2 * p)),
            pl.BlockSpec((seq, C_SLOT), lambda b, p: (b, 2 * p + 1)),
            pl.BlockSpec((LANES, seq), lambda b, p: (p, b)),
        ],
        out_specs=pl.BlockSpec((seq, LANES), lambda b, p: (b, p)),
        scratch_shapes=[pltpu.VMEM((2, HEAD_DIM, C_TILE), F32),
                        pltpu.VMEM((2, 1, C_TILE), F32),
                        pltpu.VMEM((2, 1, C_TILE), F32),
                        pltpu.VMEM((2, C_TILE, C_TILE), F32),
                        pltpu.VMEM((2, 1, C_TILE), F32)],
        compiler_params=_cparams(("parallel", "parallel")),
        name="mla_flash",
    )(qt, k, k, vt)


def _merge_kernel(x_ref, ya_ref, yb_ref, yc_ref, g_ref, wbr_ref, wout_ref, o_ref):
    merged = None
    for n, y_ref in enumerate((ya_ref, yb_ref, yc_ref)):
        up = jnp.dot(y_ref[...], wbr_ref[n], preferred_element_type=F32)
        gl = g_ref[:, n * D_MODEL:(n + 1) * D_MODEL].astype(F32)
        term = (1.0 / (1.0 + jnp.exp(-gl))) * up
        merged = term if merged is None else merged + term
    o_ref[...] = x_ref[...] + jnp.dot(merged.astype(BF16), wout_ref[...], preferred_element_type=F32)


def merge_out(x, ya, yb, yc, gates, wbr, wout, tm):
    t = x.shape[0]
    return pl.pallas_call(
        _merge_kernel,
        out_shape=jax.ShapeDtypeStruct((t, D_MODEL), F32),
        grid=(t // tm,),
        in_specs=[
            pl.BlockSpec((tm, D_MODEL), lambda i: (i, 0)),
            pl.BlockSpec((tm, WIDTH), lambda i: (i, 0)),
            pl.BlockSpec((tm, WIDTH), lambda i: (i, 0)),
            pl.BlockSpec((tm, WIDTH), lambda i: (i, 0)),
            pl.BlockSpec((tm, N_BRANCH * D_MODEL), lambda i: (i, 0)),
            _resident(wbr),
            _resident(wout),
        ],
        out_specs=pl.BlockSpec((tm, D_MODEL), lambda i: (i, 0)),
        compiler_params=_cparams(("parallel",)),
        name="merge_out",
    )(x, ya, yb, yc, gates, wbr, wout)


def _ffn_kernel(x_ref, g_ref, w1_ref, w2_ref, gf_ref, o_ref, *, final_norm, tf):
    x = x_ref[...]
    h = _rms(x, g_ref[...]).astype(BF16)
    acc = None
    for k in range(w1_ref.shape[1] // tf):
        u = jnp.maximum(jnp.dot(h, w1_ref[:, k * tf:(k + 1) * tf], preferred_element_type=F32), 0.0)
        part = jnp.dot((u * u).astype(BF16), w2_ref[k * tf:(k + 1) * tf, :], preferred_element_type=F32)
        acc = part if acc is None else acc + part
    y = x + acc
    if final_norm:
        y = _rms(y, gf_ref[...])
    o_ref[...] = y


def ffn(x, g, w1, w2, gf, final_norm, tm, tf):
    t, d = x.shape
    return pl.pallas_call(
        functools.partial(_ffn_kernel, final_norm=final_norm, tf=tf),
        out_shape=jax.ShapeDtypeStruct((t, d), F32),
        grid=(t // tm,),
        in_specs=[
            pl.BlockSpec((tm, d), lambda i: (i, 0)),
            pl.BlockSpec((1, d), lambda i: (0, 0)),
            _resident(w1),
            _resident(w2),
            pl.BlockSpec((1, d), lambda i: (0, 0)),
        ],
        out_specs=pl.BlockSpec((tm, d), lambda i: (i, 0)),
        compiler_params=_cparams(("parallel",)),
        name="ffn",
    )(x, g.reshape(1, d), w1, w2, gf.reshape(1, d))


COL_AQ, COL_AK, COL_AV = 0, 4, 8
COL_BQ, COL_BI, COL_BG = 12, 16, 20
GATE_BLK = 1


def _prep_w_in(w):
    o = 0
    parts = {}
    for name, size in (("aq", 512), ("ak", 512), ("av", 512), ("bq", 512), ("bf", 512), ("bi", 512),
                       ("bg", 512), ("cq", C_Q_RANK), ("ckv", C_KV_RANK), ("ckr", C_ROPE),
                       ("gate", N_BRANCH * D_MODEL)):
        parts[name] = w[:, o:o + size]
        o += size
    w16 = jnp.concatenate([parts[n] for n in ("aq", "ak", "av", "bq", "bi", "bg", "gate")], axis=1)
    half = C_ROPE // 2
    x1, x2 = parts["ckr"][:, :half], parts["ckr"][:, half:]
    zl = jnp.zeros((w.shape[0], C_NOPE), w.dtype)
    zr = jnp.zeros((w.shape[0], C_SLOT - C_NOPE - C_ROPE), w.dtype)
    kr_slot = jnp.concatenate([zl, x1, x2, zr], axis=1)
    kr_sw = jnp.concatenate([zl, -x2, x1, zr], axis=1)
    w32 = jnp.concatenate([parts["cq"], parts["ckv"], kr_slot, kr_sw, parts["bf"]], axis=1)
    return w16.astype(BF16), w32.astype(BF16)


def _prep_w_mla(w_uq, w_ukv):
    half = C_ROPE // 2
    per_q = C_NOPE + C_ROPE
    wq = w_uq.reshape(C_Q_RANK, N_HEADS, per_q)
    nope, x1, x2 = wq[..., :C_NOPE], wq[..., C_NOPE:C_NOPE + half], wq[..., C_NOPE + half:]
    zpad = jnp.zeros((C_Q_RANK, N_HEADS, C_SLOT - per_q), w_uq.dtype)
    wq1 = jnp.concatenate([nope, x1, x2, zpad], axis=-1).reshape(C_Q_RANK, N_HEADS * C_SLOT)
    wq2 = jnp.concatenate([jnp.zeros_like(nope), -x2, x1, zpad], axis=-1).reshape(C_Q_RANK, N_HEADS * C_SLOT)
    wkv = w_ukv.reshape(C_KV_RANK, N_HEADS, C_NOPE + HEAD_DIM)
    k_nope, v = wkv[..., :C_NOPE], wkv[..., C_NOPE:]
    wk = jnp.concatenate([k_nope, jnp.zeros((C_KV_RANK, N_HEADS, C_SLOT - C_NOPE), w_ukv.dtype)],
                         axis=-1).reshape(C_KV_RANK, N_HEADS * C_SLOT)
    wv = v.reshape(C_KV_RANK, WIDTH)
    return wq1.T.astype(BF16), wq2.T.astype(BF16), wk.astype(BF16), wv.T.astype(BF16)


def kernel(x, positions, norm_mix_g, w_in, rel_bias, hgrn_lb_logits, hgrn_norm_g, mla_q_norm_g,
           mla_kv_norm_g, mla_w_uq, mla_w_ukv, w_branch, w_out, norm_ffn_g, w_ff1, w_ff2, final_norm_g):
    bsz, seq, d = x.shape
    depth = w_in.shape[0]
    t = bsz * seq
    assert d == D_MODEL and seq % C_TILE == 0 and seq % (A_QTILE * A_TILES_PER_STEP) == 0
    tm = min(TM_DENSE, t)

    p_lb = jax.nn.softmax(hgrn_lb_logits.astype(F32), axis=0)
    lb_all = jnp.cumsum(p_lb, axis=0)
    lb_all = lb_all - lb_all[0:1]

    tabs = rope_tables(positions, tm)
    xf = x.reshape(t, d)
    for l in range(depth):
        w16, w32 = _prep_w_in(w_in[l])
        wq1t, wq2t, wk, wvt = _prep_w_mla(mla_w_uq[l], mla_w_ukv[l])
        proj_a, gates, y_b, qt_c, k_c, vt_c = in_proj(xf, norm_mix_g[l], w16, w32, tabs, mla_q_norm_g[l], mla_kv_norm_g[l],
                                               wq1t, wq2t, wk, wvt, lb_all[l], hgrn_norm_g[l], seq,
                                               min(TM_IN_PROJ, seq), TN_IN_PROJ)

        y_a = band_attention(proj_a, band_bias_table(rel_bias[l]), bsz, seq, COL_AQ, COL_AK, COL_AV)
        y_c = mla_flash(qt_c, k_c, vt_c, bsz, seq)

        xf = merge_out(xf, y_a, y_b, y_c, gates, w_branch[l].astype(BF16), w_out[l].astype(BF16), tm)
        xf = ffn(xf, norm_ffn_g[l], w_ff1[l].astype(BF16), w_ff2[l].astype(BF16), final_norm_g,
                 l == depth - 1, tm, TF_FFN)
    return xf.reshape(bsz, seq, d)
```

```python
import functools

import jax
import jax.numpy as jnp
from jax import lax
from jax.experimental import pallas as pl
from jax.experimental.pallas import tpu as pltpu

F32 = jnp.float32
BF16 = jnp.bfloat16

D_MODEL = 1024
CHUNK = 64
EPS = 1e-6
N_HEADS = 8
HEAD_DIM = 64
LANES = 128
N_PAIRS = N_HEADS * HEAD_DIM // LANES
WIDTH = N_HEADS * HEAD_DIM

A_LEFT_CHUNKS = 8
A_MAX_REL = 128
A_QTILE = 2 * CHUNK
A_BAND = (A_LEFT_CHUNKS + 2) * CHUNK
A_PAD = A_LEFT_CHUNKS * CHUNK
A_TILES_PER_STEP = 16

C_Q_RANK = 256
C_KV_RANK = 128
C_ROPE = 32
C_NOPE = 64
C_SLOT = LANES
C_LATENT = C_Q_RANK + C_KV_RANK + 2 * C_SLOT
ROPE_BASE = 10000.0
C_TILE = 512
ONES_ROWS = 16
V_ROWS = HEAD_DIM + ONES_ROWS
LOG2E = 1.4426950408889634
C_QSCALE = (C_NOPE + C_ROPE) ** -0.5 * LOG2E

N_BRANCH = 3

SUB = 8
SUP = 2 * SUB
N_SUB = CHUNK // SUB
NEG_BIG = -1e30

VMEM_LIMIT = 48 * 1024 * 1024
VMEM_LIMIT_IN_PROJ = 58 * 1024 * 1024

TM_IN_PROJ = 512
TN_IN_PROJ = 1024
TM_DENSE = 1024
TF_FFN = 1024

_NT = (((1,), (1,)), ((), ()))


def _cparams(sem, vmem_limit=VMEM_LIMIT):
    return pltpu.CompilerParams(dimension_semantics=sem, vmem_limit_bytes=vmem_limit)


def _resident(a):
    return pl.BlockSpec(a.shape, lambda *_: (0,) * a.ndim, pipeline_mode=pl.Buffered(1))


def _rms(x, g):
    return x * lax.rsqrt(jnp.mean(x * x, axis=-1, keepdims=True) + EPS) * g


def _in_proj_kernel(x_ref, g_ref, wa_ref, wb_ref, ct_ref, st_ref, ctt_ref, stt_ref, gq_ref, gkv_ref,
                    wq1t_ref, wq2t_ref, wk_ref, wvt_ref, loglb_ref, log1mlb_ref, ng_ref,
                    oa_ref, og_ref, yb_ref, qt_ref, k_ref, vt_ref, state_ref, *, tn, tiles_per_seq):
    @pl.when(pl.program_id(0) % tiles_per_seq == 0)
    def _():
        state_ref[...] = jnp.zeros_like(state_ref)

    h = _rms(x_ref[...], g_ref[...]).astype(BF16)

    a_cols = COL_BQ * LANES
    g_col0 = GATE_BLK * N_BRANCH * D_MODEL

    def proj_tile(j):
        lo, hi = j * tn, (j + 1) * tn
        t = jnp.dot(h, wa_ref[:, lo:hi], preferred_element_type=F32)
        if lo < a_cols:
            oa_ref[:, lo:min(hi, a_cols)] = t[:, :min(hi, a_cols) - lo].astype(oa_ref.dtype)
        if hi > g_col0:
            og_ref[:, max(lo, g_col0) - g_col0:hi - g_col0] = t[:, max(lo, g_col0) - lo:].astype(og_ref.dtype)
        return t

    tq = proj_tile(COL_BQ * LANES // tn)
    tig = proj_tile(COL_BI * LANES // tn)
    pb = jnp.dot(h, wb_ref[...], preferred_element_type=F32)
    q_off = COL_BQ * LANES % tn
    i_off = COL_BI * LANES % tn
    g_off = COL_BG * LANES - (COL_BI * LANES // tn) * tn
    y_out = []
    gens = []
    for p in range(N_PAIRS):
        lanes = lambda off: slice(off + p * LANES, off + (p + 1) * LANES)
        gens.append(_hgrn_stages(pb[:, lanes(C_LATENT)], tq[:, lanes(q_off)], tig[:, lanes(i_off)],
                                 tig[:, lanes(g_off)], loglb_ref[p], log1mlb_ref[p], ng_ref[...],
                                 state_ref.at[p], y_out))

    qn = _rms(pb[:, :C_Q_RANK], gq_ref[...]).astype(BF16)
    kvn = _rms(pb[:, C_Q_RANK:C_Q_RANK + C_KV_RANK], gkv_ref[...]).astype(BF16)
    kr = pb[:, C_Q_RANK + C_KV_RANK:C_Q_RANK + C_KV_RANK + C_SLOT]
    kr_sw = pb[:, C_Q_RANK + C_KV_RANK + C_SLOT:C_LATENT]
    kr_rot = kr * ct_ref[...] + kr_sw * st_ref[...]

    def mla_q():
        a_t = lax.dot_general(wq1t_ref[...], qn, _NT, preferred_element_type=F32)
        b_t = lax.dot_general(wq2t_ref[...], qn, _NT, preferred_element_type=F32)
        ctt = ctt_ref[...]
        stt = stt_ref[...]
        for hd in range(N_HEADS):
            sl = slice(hd * C_SLOT, (hd + 1) * C_SLOT)
            qt_ref[sl, :] = (a_t[sl, :] * ctt + b_t[sl, :] * stt).astype(qt_ref.dtype)

    def mla_kv():
        kn = jnp.dot(kvn, wk_ref[...], preferred_element_type=F32)
        v_t = lax.dot_general(wvt_ref[...], kvn, _NT, preferred_element_type=F32).astype(vt_ref.dtype)
        ones = jnp.ones((ONES_ROWS, v_t.shape[1]), vt_ref.dtype)
        for hd in range(N_HEADS):
            vt_ref[hd * V_ROWS:hd * V_ROWS + HEAD_DIM, :] = v_t[hd * HEAD_DIM:(hd + 1) * HEAD_DIM, :]
            vt_ref[hd * V_ROWS + HEAD_DIM:(hd + 1) * V_ROWS, :] = ones
        for hd in range(N_HEADS):
            sl = slice(hd * C_SLOT, (hd + 1) * C_SLOT)
            k_ref[:, sl] = (kn[:, sl] + kr_rot).astype(k_ref.dtype)

    done = {COL_BQ * LANES // tn, COL_BI * LANES // tn}
    big = [functools.partial(proj_tile, j) for j in range(wa_ref.shape[1] // tn) if j not in done] + [mla_q, mla_kv]
    assert len(big) == HGRN_STAGES
    for work in big:
        work()
        for gen in gens:
            next(gen)
    for p in range(N_PAIRS):
        yb_ref[:, p * LANES:(p + 1) * LANES] = y_out[p].astype(yb_ref.dtype)


def in_proj(x, g, wa, wb, tabs, gq, gkv, wq1t, wq2t, wk, wvt, lb, hgrn_norm_g, seq, tm, tn):
    t, d = x.shape
    na = COL_BQ * LANES
    ng_cols = N_BRANCH * D_MODEL
    ctab, stab, ctab_t, stab_t = tabs
    gq = gq.reshape(1, -1)
    gkv = gkv.reshape(1, -1)
    log_lb = jnp.log(lb).reshape(N_PAIRS, 1, LANES)
    log1m_lb = jnp.log1p(-lb).reshape(N_PAIRS, 1, LANES)
    ng = jnp.tile(hgrn_norm_g.astype(F32), LANES // HEAD_DIM).reshape(1, LANES)
    rows = lambda n: pl.BlockSpec((tm, n), lambda i: (i, 0))
    cols = lambda n: pl.BlockSpec((n, tm), lambda i: (0, i))
    return pl.pallas_call(
        functools.partial(_in_proj_kernel, tn=tn, tiles_per_seq=seq // tm),
        out_shape=(jax.ShapeDtypeStruct((t, na), BF16),
                   jax.ShapeDtypeStruct((t, ng_cols), BF16),
                   jax.ShapeDtypeStruct((t, WIDTH), BF16),
                   jax.ShapeDtypeStruct((N_HEADS * C_SLOT, t), BF16),
                   jax.ShapeDtypeStruct((t, N_HEADS * C_SLOT), BF16),
                   jax.ShapeDtypeStruct((N_HEADS * V_ROWS, t), BF16)),
        grid=(t // tm,),
        in_specs=[rows(d), pl.BlockSpec((1, d), lambda i: (0, 0)), _resident(wa), _resident(wb),
                  rows(LANES), rows(LANES), cols(LANES), cols(LANES),
                  _resident(gq), _resident(gkv), _resident(wq1t), _resident(wq2t), _resident(wk), _resident(wvt),
                  _resident(log_lb), _resident(log1m_lb), _resident(ng)],
        out_specs=(rows(na), rows(ng_cols), rows(WIDTH), cols(N_HEADS * C_SLOT), rows(N_HEADS * C_SLOT), cols(N_HEADS * V_ROWS)),
        scratch_shapes=[pltpu.VMEM((N_PAIRS, LANES, LANES), F32)],
        compiler_params=_cparams(("arbitrary",), VMEM_LIMIT_IN_PROJ),
        name="in_proj",
    )(x, g.reshape(1, d), wa, wb, ctab, stab, ctab_t, stab_t, gq, gkv, wq1t, wq2t, wk, wvt, log_lb, log1m_lb, ng)


def _trig_kernel(pos_ref, invf_ref, c_ref, s_ref, ct_ref, st_ref):
    ang = pos_ref[...].astype(F32) * invf_ref[...]
    c = jnp.cos(ang)
    s = jnp.sin(ang)
    c_ref[...] = c
    s_ref[...] = s
    ct_ref[...] = c.T
    st_ref[...] = s.T


def rope_tables(positions, tm):
    t = positions.size
    inv_freq = ROPE_BASE ** (-jnp.arange(0, C_ROPE, 2, dtype=F32) / C_ROPE)
    half = C_ROPE // 2
    invf = jnp.zeros((LANES,), F32)
    invf = invf.at[C_NOPE:C_NOPE + half].set(inv_freq).at[C_NOPE + half:C_NOPE + C_ROPE].set(inv_freq)
    return pl.pallas_call(
        _trig_kernel,
        out_shape=(jax.ShapeDtypeStruct((t, LANES), F32),) * 2 + (jax.ShapeDtypeStruct((LANES, t), F32),) * 2,
        grid=(t // tm,),
        in_specs=[
            pl.BlockSpec((tm, 1), lambda i: (i, 0)),
            pl.BlockSpec((1, LANES), lambda i: (0, 0)),
        ],
        out_specs=(pl.BlockSpec((tm, LANES), lambda i: (i, 0)),) * 2
        + (pl.BlockSpec((LANES, tm), lambda i: (0, i)),) * 2,
        compiler_params=_cparams(("parallel",)),
        name="rope_tables",
    )(positions.reshape(t, 1), invf.reshape(1, LANES))


def _band_attn_kernel(q_ref, k_ref, v_ref, bias_ref, o_ref):
    lane = lax.broadcasted_iota(jnp.int32, (A_QTILE, LANES), 1)
    n_shift = A_PAD // A_QTILE
    starts, scores = [], []
    for t in range(A_TILES_PER_STEP):
        c2 = pl.program_id(2) * A_TILES_PER_STEP + t
        start = pl.multiple_of(jnp.maximum(c2 * A_QTILE - A_PAD, 0), A_QTILE)
        shift = jnp.minimum(c2, n_shift)
        kwin = k_ref[pl.ds(start, A_BAND), :]
        q = q_ref[t * A_QTILE:(t + 1) * A_QTILE, :].astype(F32) * (HEAD_DIM ** -0.5)
        qst = jnp.concatenate([jnp.where(lane < HEAD_DIM, q, 0.0), jnp.where(lane < HEAD_DIM, 0.0, q)],
                              axis=0).astype(BF16)
        s = lax.dot_general(qst, kwin, _NT, preferred_element_type=F32)
        starts.append(start)
        scores.append(s + bias_ref[shift, 0])
    for t in range(A_TILES_PER_STEP):
        s = scores[t]
        vwin = v_ref[pl.ds(starts[t], A_BAND), :]
        m = jnp.max(s, axis=-1, keepdims=True)
        p = jnp.exp(s - m)
        l = jnp.sum(p, axis=-1, keepdims=True)
        pv = jnp.dot(p.astype(BF16), vwin, preferred_element_type=F32) / l
        o_ref[t * A_QTILE:(t + 1) * A_QTILE, :] = jnp.where(lane < HEAD_DIM, pv[:A_QTILE], pv[A_QTILE:]).astype(o_ref.dtype)


def band_attention(proj, bias, bsz, seq, q_col, k_col, v_col):
    t = bsz * seq
    rows = A_QTILE * A_TILES_PER_STEP
    n_steps = seq // rows
    n_shift = A_PAD // A_QTILE
    return pl.pallas_call(
        _band_attn_kernel,
        out_shape=jax.ShapeDtypeStruct((t, WIDTH), BF16),
        grid=(bsz, N_PAIRS, n_steps),
        in_specs=[
            pl.BlockSpec((rows, LANES), lambda b, p, c: (b * n_steps + c, q_col + p)),
            pl.BlockSpec((seq, LANES), lambda b, p, c: (b, k_col + p)),
            pl.BlockSpec((seq, LANES), lambda b, p, c: (b, v_col + p)),
            pl.BlockSpec((n_shift + 1, 1, 2 * A_QTILE, A_BAND), lambda b, p, c: (0, p, 0, 0)),
        ],
        out_specs=pl.BlockSpec((rows, LANES), lambda b, p, c: (b * n_steps + c, p)),
        compiler_params=_cparams(("parallel", "parallel", "arbitrary")),
        name="band_attention",
    )(proj, proj, proj, bias.reshape(n_shift + 1, N_PAIRS, 2 * A_QTILE, A_BAND))


A_ROLL = 768


def _band_bias_kernel(r_ref, o_ref):
    d = pl.program_id(0) * A_QTILE
    prof = jnp.broadcast_to(r_ref[...], (A_QTILE, A_ROLL))
    b = pltpu.roll(prof, 0, 1, stride=1, stride_axis=0)[:, :A_BAND]
    i = lax.broadcasted_iota(jnp.int32, (A_QTILE, A_BAND), 0)
    j = lax.broadcasted_iota(jnp.int32, (A_QTILE, A_BAND), 1)
    gap = (d + i) // CHUNK - j // CHUNK
    o_ref[...] = jnp.where((gap >= 0) & (gap <= A_LEFT_CHUNKS), b, NEG_BIG)


def band_bias_table(rel_table):
    n_shift = A_PAD // A_QTILE
    heads = rel_table.shape[0]
    pad = A_ROLL + A_PAD
    ext = jnp.pad(rel_table.astype(F32)[:, ::-1], ((0, 0), (pad, pad)), mode="edge")

    def seg(d, u0, n):
        o = u0 - d + A_MAX_REL + pad
        return ext[:, o:o + n]

    n_neg = A_ROLL - (A_BAND + 1)
    prof = jnp.stack([jnp.concatenate([seg(s * A_QTILE, 0, A_BAND + 1), seg(s * A_QTILE, -n_neg, n_neg)], axis=1)
                      for s in range(n_shift + 1)])
    return pl.pallas_call(
        _band_bias_kernel,
        out_shape=jax.ShapeDtypeStruct((n_shift + 1, heads, A_QTILE, A_BAND), F32),
        grid=(n_shift + 1, heads),
        in_specs=[pl.BlockSpec((None, None, 1, A_ROLL), lambda s, h: (s, h, 0, 0))],
        out_specs=pl.BlockSpec((None, None, A_QTILE, A_BAND), lambda s, h: (s, h, 0, 0)),
        compiler_params=_cparams(("parallel", "parallel")),
        name="band_bias",
    )(prof.reshape(n_shift + 1, heads, 1, A_ROLL))


def _split3(x):
    hi = x.astype(BF16)
    r1 = x - hi.astype(F32)
    mid = r1.astype(BF16)
    lo = (r1 - mid.astype(F32)).astype(BF16)
    return hi, mid, lo


HGRN_STAGES = 6


def _hgrn_stages(z, qraw, v, graw, log_lb, log1m_lb, norm_g, st_ref, y_out):
    rows = z.shape[0]
    n_chunks = rows // CHUNK
    n_blk = rows // SUB

    head0 = lax.broadcasted_iota(jnp.int32, (rows, LANES), 1) < HEAD_DIM
    head0_c = lax.broadcasted_iota(jnp.int32, (CHUNK, LANES), 1) < HEAD_DIM
    r64 = lax.broadcasted_iota(jnp.int32, (CHUNK, CHUNK), 0)
    c64 = lax.broadcasted_iota(jnp.int32, (CHUNK, CHUNK), 1)
    tril = (c64 <= r64).astype(BF16)
    c64s = lax.broadcasted_iota(jnp.int32, (2 * CHUNK, CHUNK), 1)
    rl = lax.broadcasted_iota(jnp.int32, (LANES, LANES), 0)
    cl = lax.broadcasted_iota(jnp.int32, (LANES, LANES), 1)
    same_head = (rl // HEAD_DIM) == (cl // HEAD_DIM)
    ri = lax.broadcasted_iota(jnp.int32, (CHUNK, CHUNK * SUB), 0)
    ci = lax.broadcasted_iota(jnp.int32, (CHUNK, CHUNK * SUB), 1)
    pick = (ci // SUB == ri).astype(BF16)
    row_s = lax.broadcasted_iota(jnp.int32, (SUB, LANES), 0)

    def chunk(x, c):
        return x[c * CHUNK:(c + 1) * CHUNK]

    log_sig = jnp.minimum(z, 0.0) - jnp.log(1.0 + jnp.exp(-jnp.abs(z)))
    bterm = log1m_lb + log_sig
    log_f = jnp.maximum(log_lb, bterm) + jnp.log(1.0 + jnp.exp(-jnp.abs(log_lb - bterm)))
    log_k = bterm - z
    qs = qraw * (1.0 / (1.0 + jnp.exp(-qraw)))

    x3 = jnp.concatenate(_split3(log_f), axis=1)
    cum3 = [jnp.dot(tril, chunk(x3, c), preferred_element_type=F32) for c in range(n_chunks)]
    yield
    cum = jnp.concatenate([t[:, :LANES] + t[:, LANES:2 * LANES] + t[:, 2 * LANES:] for t in cum3], axis=0)

    zero_row = jnp.zeros((1, LANES), F32)
    c_end = [cum[SUB * b + SUB - 1:SUB * b + SUB, :] for b in range(n_blk)]
    c_start = [zero_row if b % N_SUB == 0 else c_end[b - 1] for b in range(n_blk)]
    c_last = [c_end[c * N_SUB + N_SUB - 1] for c in range(n_chunks)]

    def rows_of(blocks):
        return jnp.concatenate([jnp.broadcast_to(r, (SUB, LANES)) for r in blocks], axis=0)

    cstart_full = rows_of(c_start)
    cend_full = rows_of(c_end)
    clast_full = jnp.concatenate([jnp.broadcast_to(r, (CHUNK, LANES)) for r in c_last], axis=0)
    q1 = qs * jnp.exp(cum - cstart_full)
    lk = log_k - cum
    k2b = jnp.exp(cend_full + lk).astype(BF16)
    k_end = jnp.exp(clast_full + lk).astype(BF16)
    q_state = (qs * jnp.exp(cum)).astype(BF16)
    vb = v.astype(BF16)

    n_sup = CHUNK // SUP
    ce_sup = [c_end[(SUP // SUB) * (b + 1) - 1] for b in range(rows // SUP)]
    cs_sup = [zero_row if b % n_sup == 0 else ce_sup[b - 1] for b in range(rows // SUP)]

    def rows_of_sup(blocks):
        return jnp.concatenate([jnp.broadcast_to(r, (SUP, LANES)) for r in blocks], axis=0)

    q1s = qs * jnp.exp(cum - rows_of_sup(cs_sup))
    k2s = jnp.exp(rows_of_sup(ce_sup) + lk).astype(BF16)
    odd_sub = (lax.broadcasted_iota(jnp.int32, (rows, LANES), 0) // SUB) % 2 == 1
    q_adj = jnp.where(odd_sub, q1, 0.0)
    q_adj = (jnp.where(head0, q_adj, 0.0).astype(BF16), jnp.where(head0, 0.0, q_adj).astype(BF16))
    q_sup = []
    for j in range(n_sup - 1):
        d_rows = []
        for b in range(rows // SUP):
            if b % n_sup > j:
                d_rows.append(jnp.exp(cs_sup[b] - ce_sup[(b // n_sup) * n_sup + j]))
            else:
                d_rows.append(zero_row)
        qj = q1s * rows_of_sup(d_rows)
        q_sup.append((jnp.where(head0, qj, 0.0).astype(BF16), jnp.where(head0, 0.0, qj).astype(BF16)))
    s_adj, s_sup = [], []
    for c in range(n_chunks):
        qst = jnp.concatenate([chunk(q_adj[0], c), chunk(q_adj[1], c)], axis=0)
        s_adj.append(lax.dot_general(qst, chunk(k2b, c), _NT, preferred_element_type=F32))
        for j in range(n_sup - 1):
            qst = jnp.concatenate([chunk(q_sup[j][0], c), chunk(q_sup[j][1], c)], axis=0)
            s_sup.append(lax.dot_general(qst, chunk(k2s, c), _NT, preferred_element_type=F32))
    yield
    r64s = lax.broadcasted_iota(jnp.int32, (2 * CHUNK, CHUNK), 0) % CHUNK
    prev_sub = c64s // SUB == r64s // SUB - 1
    o_cross = []
    for c in range(n_chunks):
        sc = jnp.where(prev_sub, s_adj[c], 0.0)
        for j in range(n_sup - 1):
            col_in_j = (c64s >= j * SUP) & (c64s < (j + 1) * SUP)
            sc = sc + jnp.where(col_in_j, s_sup[c * (n_sup - 1) + j], 0.0)
        o2 = jnp.dot(sc.astype(BF16), chunk(vb, c), preferred_element_type=F32)
        o_cross.append(jnp.where(head0_c, o2[:CHUNK], o2[CHUNK:]))

    yield
    a = cum - log_k
    w_rows = []
    for b in range(n_blk):
        a_b = a[SUB * b:SUB * (b + 1), :]
        for i in range(SUB):
            r = SUB * b + i
            arg = jnp.where(row_s <= i, cum[r:r + 1, :] - a_b, NEG_BIG)
            w_rows.append(jnp.exp(arg) * qs[r:r + 1, :])
    w_all = jnp.concatenate(w_rows, axis=0)
    head0_w = lax.broadcasted_iota(jnp.int32, w_all.shape, 1) < HEAD_DIM
    sum0 = jnp.sum(jnp.where(head0_w, w_all, 0.0), axis=-1, keepdims=True)
    sum1 = jnp.sum(jnp.where(head0_w, 0.0, w_all), axis=-1, keepdims=True)
    sb = jnp.where(head0_w, sum0, sum1)
    yield
    o_diag = []
    for c in range(n_chunks):
        v_rep = jnp.concatenate([v[SUB * (r // SUB):SUB * (r // SUB + 1), :]
                                 for r in range(c * CHUNK, (c + 1) * CHUNK)], axis=0)
        sb_c = sb[c * CHUNK * SUB:(c + 1) * CHUNK * SUB]
        o_diag.append(jnp.dot(pick, (sb_c * v_rep).astype(BF16), preferred_element_type=F32))

    yield
    upd = [jnp.where(same_head,
                     lax.dot_general(chunk(vb, c), chunk(k_end, c), (((0,), (0,)), ((), ())),
                                     preferred_element_type=F32), 0.0) for c in range(n_chunks)]
    st = st_ref[...]
    o_state = []
    for c in range(n_chunks):
        o_state.append(lax.dot_general(chunk(q_state, c), st.astype(BF16), _NT, preferred_element_type=F32))
        st = st * jnp.exp(c_last[c]) + upd[c]
    st_ref[...] = st

    o = jnp.concatenate([o_cross[c] + o_diag[c] + o_state[c] for c in range(n_chunks)], axis=0)
    osq = o * o
    ms0 = jnp.sum(jnp.where(head0, osq, 0.0), axis=-1, keepdims=True)
    ms1 = jnp.sum(jnp.where(head0, 0.0, osq), axis=-1, keepdims=True)
    ms = jnp.where(head0, ms0, ms1) * (1.0 / HEAD_DIM)
    y = o * lax.rsqrt(ms + EPS) * norm_g
    y_out.append(y * (graw * (1.0 / (1.0 + jnp.exp(-graw)))))
    yield


def _mla_flash_kernel(qt_ref, k0_ref, k1_ref, vt_ref, o_ref, acc_ref, m_ref, l_ref, s_ref, mt_ref, *, n_qt):
    c = C_QSCALE

    def scores(qi, j, e):
        qoff = pl.multiple_of(qi * C_TILE, C_TILE)
        koff = pl.multiple_of(j * C_TILE, C_TILE)
        k = (k0_ref, k1_ref)[e][pl.ds(koff, C_TILE), :]
        qt = qt_ref[e * C_SLOT:(e + 1) * C_SLOT, pl.ds(qoff, C_TILE)]
        return jnp.dot(k, qt, preferred_element_type=F32)

    def put_scores(qi, j, e):
        s = scores(qi, j, e)
        s_ref[e] = s
        mt_ref[e] = jnp.max(s, axis=0, keepdims=True)

    def softmax_pv(j, e, s, m_tile):
        off = pl.multiple_of(j * C_TILE, C_TILE)
        m_prev = m_ref[e]
        m_new = jnp.maximum(m_prev, m_tile)
        p = jnp.exp2((s - m_new) * c).astype(BF16)
        alpha = jnp.exp2((m_prev - m_new) * c)
        vt_aug = vt_ref[e * V_ROWS:(e + 1) * V_ROWS, pl.ds(off, C_TILE)]
        pv = jnp.dot(vt_aug, p, preferred_element_type=F32)
        l_ref[e] = alpha * l_ref[e] + pv[HEAD_DIM:HEAD_DIM + 1]
        acc_ref[e] = alpha * acc_ref[e] + pv[:HEAD_DIM]
        m_ref[e] = m_new

    kc = lax.broadcasted_iota(jnp.int32, (C_TILE, C_TILE), 0) // CHUNK
    qc = lax.broadcasted_iota(jnp.int32, (C_TILE, C_TILE), 1) // CHUNK

    for e in range(2):
        put_scores(0, 0, e)

    def q_body(qi, carry):
        m_ref[...] = jnp.full_like(m_ref, -jnp.inf)
        l_ref[...] = jnp.zeros_like(l_ref)
        acc_ref[...] = jnp.zeros_like(acc_ref)

        def body(j, carry2):
            for e in range(2):
                s, m_tile = s_ref[e], mt_ref[e]
                put_scores(qi, j + 1, e)
                softmax_pv(j, e, s, m_tile)
            return carry2

        lax.fori_loop(0, qi, body, 0)
        q_next = jnp.minimum(qi + 1, n_qt - 1)
        for e in range(2):
            s = jnp.where(kc <= qc, s_ref[e], NEG_BIG)
            put_scores(q_next, 0, e)
            softmax_pv(qi, e, s, jnp.max(s, axis=0, keepdims=True))

        o_t = jnp.concatenate([acc_ref[0] / l_ref[0], acc_ref[1] / l_ref[1]], axis=0)
        o_ref[pl.ds(pl.multiple_of(qi * C_TILE, C_TILE), C_TILE), :] = o_t.T.astype(o_ref.dtype)
        return carry

    lax.fori_loop(0, n_qt, q_body, 0)


def mla_flash(qt, k, vt, bsz, seq):
    t = bsz * seq
    return pl.pallas_call(
        functools.partial(_mla_flash_kernel, n_qt=seq // C_TILE),
        out_shape=jax.ShapeDtypeStruct((t, WIDTH), BF16),
        grid=(bsz, N_PAIRS),
        in_specs=[
            pl.BlockSpec((2 * C_SLOT, seq), lambda b, p: (p, b)),
            pl.BlockSpec((seq, C_SLOT), lambda b, p: (b, 2 * p)),
            pl.BlockSpec((seq, C_SLOT), lambda b, p: (b, 2 * p + 1)),
            pl.BlockSpec((2 * V_ROWS, seq), lambda b, p: (p, b)),
        ],
        out_specs=pl.BlockSpec((seq, LANES), lambda b, p: (b, p)),
        scratch_shapes=[pltpu.VMEM((2, HEAD_DIM, C_TILE), F32),
                        pltpu.VMEM((2, 1, C_TILE), F32),
                        pltpu.VMEM((2, 1, C_TILE), F32),
                        pltpu.VMEM((2, C_TILE, C_TILE), F32),
                        pltpu.VMEM((2, 1, C_TILE), F32)],
        compiler_params=_cparams(("parallel", "parallel")),
        name="mla_flash",
    )(qt, k, k, vt)


def _merge_kernel(x_ref, ya_ref, yb_ref, yc_ref, g_ref, wbr_ref, wout_ref, o_ref):
    merged = None
    for n, y_ref in enumerate((ya_ref, yb_ref, yc_ref)):
        up = jnp.dot(y_ref[...], wbr_ref[n], preferred_element_type=F32)
        gl = g_ref[:, n * D_MODEL:(n + 1) * D_MODEL].astype(F32)
        term = (1.0 / (1.0 + jnp.exp(-gl))) * up
        merged = term if merged is None else merged + term
    o_ref[...] = x_ref[...] + jnp.dot(merged.astype(BF16), wout_ref[...], preferred_element_type=F32)


def merge_out(x, ya, yb, yc, gates, wbr, wout, tm):
    t = x.shape[0]
    return pl.pallas_call(
        _merge_kernel,
        out_shape=jax.ShapeDtypeStruct((t, D_MODEL), F32),
        grid=(t // tm,),
        in_specs=[
            pl.BlockSpec((tm, D_MODEL), lambda i: (i, 0)),
            pl.BlockSpec((tm, WIDTH), lambda i: (i, 0)),
            pl.BlockSpec((tm, WIDTH), lambda i: (i, 0)),
            pl.BlockSpec((tm, WIDTH), lambda i: (i, 0)),
            pl.BlockSpec((tm, N_BRANCH * D_MODEL), lambda i: (i, 0)),
            _resident(wbr),
            _resident(wout),
        ],
        out_specs=pl.BlockSpec((tm, D_MODEL), lambda i: (i, 0)),
        compiler_params=_cparams(("parallel",)),
        name="merge_out",
    )(x, ya, yb, yc, gates, wbr, wout)


def _ffn_kernel(x_ref, g_ref, w1_ref, w2_ref, gf_ref, o_ref, *, final_norm, tf):
    x = x_ref[...]
    h = _rms(x, g_ref[...]).astype(BF16)
    acc = None
    for k in range(w1_ref.shape[1] // tf):
        u = jnp.maximum(jnp.dot(h, w1_ref[:, k * tf:(k + 1) * tf], preferred_element_type=F32), 0.0)
        part = jnp.dot((u * u).astype(BF16), w2_ref[k * tf:(k + 1) * tf, :], preferred_element_type=F32)
        acc = part if acc is None else acc + part
    y = x + acc
    if final_norm:
        y = _rms(y, gf_ref[...])
    o_ref[...] = y


def ffn(x, g, w1, w2, gf, final_norm, tm, tf):
    t, d = x.shape
    return pl.pallas_call(
        functools.partial(_ffn_kernel, final_norm=final_norm, tf=tf),
        out_shape=jax.ShapeDtypeStruct((t, d), F32),
        grid=(t // tm,),
        in_specs=[
            pl.BlockSpec((tm, d), lambda i: (i, 0)),
            pl.BlockSpec((1, d), lambda i: (0, 0)),
            _resident(w1),
            _resident(w2),
            pl.BlockSpec((1, d), lambda i: (0, 0)),
        ],
        out_specs=pl.BlockSpec((tm, d), lambda i: (i, 0)),
        compiler_params=_cparams(("parallel",)),
        name="ffn",
    )(x, g.reshape(1, d), w1, w2, gf.reshape(1, d))


COL_AQ, COL_AK, COL_AV = 0, 4, 8
COL_BQ, COL_BI, COL_BG = 12, 16, 20
GATE_BLK = 1


def _prep_w_in(w):
    o = 0
    parts = {}
    for name, size in (("aq", 512), ("ak", 512), ("av", 512), ("bq", 512), ("bf", 512), ("bi", 512),
                       ("bg", 512), ("cq", C_Q_RANK), ("ckv", C_KV_RANK), ("ckr", C_ROPE),
                       ("gate", N_BRANCH * D_MODEL)):
        parts[name] = w[:, o:o + size]
        o += size
    w16 = jnp.concatenate([parts[n] for n in ("aq", "ak", "av", "bq", "bi", "bg", "gate")], axis=1)
    half = C_ROPE // 2
    x1, x2 = parts["ckr"][:, :half], parts["ckr"][:, half:]
    zl = jnp.zeros((w.shape[0], C_NOPE), w.dtype)
    zr = jnp.zeros((w.shape[0], C_SLOT - C_NOPE - C_ROPE), w.dtype)
    kr_slot = jnp.concatenate([zl, x1, x2, zr], axis=1)
    kr_sw = jnp.concatenate([zl, -x2, x1, zr], axis=1)
    w32 = jnp.concatenate([parts["cq"], parts["ckv"], kr_slot, kr_sw, parts["bf"]], axis=1)
    return w16.astype(BF16), w32.astype(BF16)


def _prep_w_mla(w_uq, w_ukv):
    half = C_ROPE // 2
    per_q = C_NOPE + C_ROPE
    wq = w_uq.reshape(C_Q_RANK, N_HEADS, per_q)
    nope, x1, x2 = wq[..., :C_NOPE], wq[..., C_NOPE:C_NOPE + half], wq[..., C_NOPE + half:]
    zpad = jnp.zeros((C_Q_RANK, N_HEADS, C_SLOT - per_q), w_uq.dtype)
    wq1 = jnp.concatenate([nope, x1, x2, zpad], axis=-1).reshape(C_Q_RANK, N_HEADS * C_SLOT)
    wq2 = jnp.concatenate([jnp.zeros_like(nope), -x2, x1, zpad], axis=-1).reshape(C_Q_RANK, N_HEADS * C_SLOT)
    wkv = w_ukv.reshape(C_KV_RANK, N_HEADS, C_NOPE + HEAD_DIM)
    k_nope, v = wkv[..., :C_NOPE], wkv[..., C_NOPE:]
    wk = jnp.concatenate([k_nope, jnp.zeros((C_KV_RANK, N_HEADS, C_SLOT - C_NOPE), w_ukv.dtype)],
                         axis=-1).reshape(C_KV_RANK, N_HEADS * C_SLOT)
    wv = v.reshape(C_KV_RANK, WIDTH)
    return wq1.T.astype(BF16), wq2.T.astype(BF16), wk.astype(BF16), wv.T.astype(BF16)


def kernel(x, positions, norm_mix_g, w_in, rel_bias, hgrn_lb_logits, hgrn_norm_g, mla_q_norm_g,
           mla_kv_norm_g, mla_w_uq, mla_w_ukv, w_branch, w_out, norm_ffn_g, w_ff1, w_ff2, final_norm_g):
    bsz, seq, d = x.shape
    depth = w_in.shape[0]
    t = bsz * seq
    assert d == D_MODEL and seq % C_TILE == 0 and seq % (A_QTILE * A_TILES_PER_STEP) == 0
    tm = min(TM_DENSE, t)

    p_lb = jax.nn.softmax(hgrn_lb_logits.astype(F32), axis=0)
    lb_all = jnp.cumsum(p_lb, axis=0)
    lb_all = lb_all - lb_all[0:1]

    tabs = rope_tables(positions, tm)
    xf = x.reshape(t, d)
    for l in range(depth):
        w16, w32 = _prep_w_in(w_in[l])
        wq1t, wq2t, wk, wvt = _prep_w_mla(mla_w_uq[l], mla_w_ukv[l])
        proj_a, gates, y_b, qt_c, k_c, vt_c = in_proj(xf, norm_mix_g[l], w16, w32, tabs, mla_q_norm_g[l], mla_kv_norm_g[l],
                                               wq1t, wq2t, wk, wvt, lb_all[l], hgrn_norm_g[l], seq,
                                               min(TM_IN_PROJ, seq), TN_IN_PROJ)

        y_a = band_attention(proj_a, band_bias_table(rel_bias[l]), bsz, seq, COL_AQ, COL_AK, COL_AV)
        y_c = mla_flash(qt_c, k_c, vt_c, bsz, seq)

        xf = merge_out(xf, y_a, y_b, y_c, gates, w_branch[l].astype(BF16), w_out[l].astype(BF16), tm)
        xf = ffn(xf, norm_ffn_g[l], w_ff1[l].astype(BF16), w_ff2[l].astype(BF16), final_norm_g,
                 l == depth - 1, tm, TF_FFN)
    return xf.reshape(bsz, seq, d)
```

```python
import functools

import jax
import jax.numpy as jnp
from jax import lax
from jax.experimental import pallas as pl
from jax.experimental.pallas import tpu as pltpu

F32 = jnp.float32
BF16 = jnp.bfloat16

D_MODEL = 1024
CHUNK = 64
EPS = 1e-6
N_HEADS = 8
HEAD_DIM = 64
LANES = 128
N_PAIRS = N_HEADS * HEAD_DIM // LANES
WIDTH = N_HEADS * HEAD_DIM

A_LEFT_CHUNKS = 8
A_MAX_REL = 128
A_QTILE = 2 * CHUNK
A_BAND = (A_LEFT_CHUNKS + 2) * CHUNK
A_PAD = A_LEFT_CHUNKS * CHUNK
A_TILES_PER_STEP = 32

C_Q_RANK = 256
C_KV_RANK = 128
C_ROPE = 32
C_NOPE = 64
C_SLOT = LANES
C_LATENT = C_Q_RANK + C_KV_RANK + 2 * C_SLOT
ROPE_BASE = 10000.0
C_TILE = 512
ONES_ROWS = 16
LOG2E = 1.4426950408889634
C_QSCALE = (C_NOPE + C_ROPE) ** -0.5 * LOG2E

N_BRANCH = 3

SUB = 8
SUP = 2 * SUB
N_SUB = CHUNK // SUB
NEG_BIG = -1e30

VMEM_LIMIT = 48 * 1024 * 1024
VMEM_LIMIT_IN_PROJ = 58 * 1024 * 1024

TM_IN_PROJ = 512
TN_IN_PROJ = 1024
TM_DENSE = 1024
TF_FFN = 1024

_NT = (((1,), (1,)), ((), ()))


def _cparams(sem, vmem_limit=VMEM_LIMIT):
    return pltpu.CompilerParams(dimension_semantics=sem, vmem_limit_bytes=vmem_limit)


def _resident(a):
    return pl.BlockSpec(a.shape, lambda *_: (0,) * a.ndim, pipeline_mode=pl.Buffered(1))


def _rms(x, g):
    return x * lax.rsqrt(jnp.mean(x * x, axis=-1, keepdims=True) + EPS) * g


def _in_proj_kernel(x_ref, g_ref, wa_ref, wb_ref, ct_ref, st_ref, ctt_ref, stt_ref, gq_ref, gkv_ref,
                    wq1t_ref, wq2t_ref, wk_ref, wvt_ref, loglb_ref, log1mlb_ref, ng_ref,
                    oa_ref, og_ref, yb_ref, qt_ref, k_ref, vt_ref, state_ref, *, tn, tiles_per_seq):
    @pl.when(pl.program_id(0) % tiles_per_seq == 0)
    def _():
        state_ref[...] = jnp.zeros_like(state_ref)

    h = _rms(x_ref[...], g_ref[...]).astype(BF16)

    a_cols = COL_BQ * LANES
    g_col0 = GATE_BLK * N_BRANCH * D_MODEL

    def proj_tile(j):
        lo, hi = j * tn, (j + 1) * tn
        t = jnp.dot(h, wa_ref[:, lo:hi], preferred_element_type=F32)
        if lo < a_cols:
            oa_ref[:, lo:min(hi, a_cols)] = t[:, :min(hi, a_cols) - lo].astype(oa_ref.dtype)
        if hi > g_col0:
            og_ref[:, max(lo, g_col0) - g_col0:hi - g_col0] = t[:, max(lo, g_col0) - lo:].astype(og_ref.dtype)
        return t

    tq = proj_tile(COL_BQ * LANES // tn)
    tig = proj_tile(COL_BI * LANES // tn)
    pb = jnp.dot(h, wb_ref[...], preferred_element_type=F32)
    q_off = COL_BQ * LANES % tn
    i_off = COL_BI * LANES % tn
    g_off = COL_BG * LANES - (COL_BI * LANES // tn) * tn
    y_out = []
    gens = []
    for p in range(N_PAIRS):
        lanes = lambda off: slice(off + p * LANES, off + (p + 1) * LANES)
        gens.append(_hgrn_stages(pb[:, lanes(C_LATENT)], tq[:, lanes(q_off)], tig[:, lanes(i_off)],
                                 tig[:, lanes(g_off)], loglb_ref[p], log1mlb_ref[p], ng_ref[...],
                                 state_ref.at[p], y_out))

    qn = _rms(pb[:, :C_Q_RANK], gq_ref[...]).astype(BF16)
    kvn = _rms(pb[:, C_Q_RANK:C_Q_RANK + C_KV_RANK], gkv_ref[...]).astype(BF16)
    kr = pb[:, C_Q_RANK + C_KV_RANK:C_Q_RANK + C_KV_RANK + C_SLOT]
    kr_sw = pb[:, C_Q_RANK + C_KV_RANK + C_SLOT:C_LATENT]
    kr_rot = kr * ct_ref[...] + kr_sw * st_ref[...]

    def mla_q():
        a_t = lax.dot_general(wq1t_ref[...], qn, _NT, preferred_element_type=F32)
        b_t = lax.dot_general(wq2t_ref[...], qn, _NT, preferred_element_type=F32)
        ctt = ctt_ref[...]
        stt = stt_ref[...]
        for hd in range(N_HEADS):
            sl = slice(hd * C_SLOT, (hd + 1) * C_SLOT)
            qt_ref[sl, :] = (a_t[sl, :] * ctt + b_t[sl, :] * stt).astype(qt_ref.dtype)

    def mla_kv():
        kn = jnp.dot(kvn, wk_ref[...], preferred_element_type=F32)
        vt_ref[...] = lax.dot_general(wvt_ref[...], kvn, _NT, preferred_element_type=F32).astype(vt_ref.dtype)
        for hd in range(N_HEADS):
            sl = slice(hd * C_SLOT, (hd + 1) * C_SLOT)
            k_ref[:, sl] = (kn[:, sl] + kr_rot).astype(k_ref.dtype)

    done = {COL_BQ * LANES // tn, COL_BI * LANES // tn}
    big = [functools.partial(proj_tile, j) for j in range(wa_ref.shape[1] // tn) if j not in done] + [mla_q, mla_kv]
    assert len(big) == HGRN_STAGES
    for work in big:
        work()
        for gen in gens:
            next(gen)
    for p in range(N_PAIRS):
        yb_ref[:, p * LANES:(p + 1) * LANES] = y_out[p].astype(yb_ref.dtype)


def in_proj(x, g, wa, wb, tabs, gq, gkv, wq1t, wq2t, wk, wvt, lb, hgrn_norm_g, seq, tm, tn):
    t, d = x.shape
    na = COL_BQ * LANES
    ng_cols = N_BRANCH * D_MODEL
    ctab, stab, ctab_t, stab_t = tabs
    gq = gq.reshape(1, -1)
    gkv = gkv.reshape(1, -1)
    log_lb = jnp.log(lb).reshape(N_PAIRS, 1, LANES)
    log1m_lb = jnp.log1p(-lb).reshape(N_PAIRS, 1, LANES)
    ng = jnp.tile(hgrn_norm_g.astype(F32), LANES // HEAD_DIM).reshape(1, LANES)
    rows = lambda n: pl.BlockSpec((tm, n), lambda i: (i, 0))
    cols = lambda n: pl.BlockSpec((n, tm), lambda i: (0, i))
    return pl.pallas_call(
        functools.partial(_in_proj_kernel, tn=tn, tiles_per_seq=seq // tm),
        out_shape=(jax.ShapeDtypeStruct((t, na), BF16),
                   jax.ShapeDtypeStruct((t, ng_cols), BF16),
                   jax.ShapeDtypeStruct((t, WIDTH), BF16),
                   jax.ShapeDtypeStruct((N_HEADS * C_SLOT, t), BF16),
                   jax.ShapeDtypeStruct((t, N_HEADS * C_SLOT), BF16),
                   jax.ShapeDtypeStruct((WIDTH, t), BF16)),
        grid=(t // tm,),
        in_specs=[rows(d), pl.BlockSpec((1, d), lambda i: (0, 0)), _resident(wa), _resident(wb),
                  rows(LANES), rows(LANES), cols(LANES), cols(LANES),
                  _resident(gq), _resident(gkv), _resident(wq1t), _resident(wq2t), _resident(wk), _resident(wvt),
                  _resident(log_lb), _resident(log1m_lb), _resident(ng)],
        out_specs=(rows(na), rows(ng_cols), rows(WIDTH), cols(N_HEADS * C_SLOT), rows(N_HEADS * C_SLOT), cols(WIDTH)),
        scratch_shapes=[pltpu.VMEM((N_PAIRS, LANES, LANES), F32)],
        compiler_params=_cparams(("arbitrary",), VMEM_LIMIT_IN_PROJ),
        name="in_proj",
    )(x, g.reshape(1, d), wa, wb, ctab, stab, ctab_t, stab_t, gq, gkv, wq1t, wq2t, wk, wvt, log_lb, log1m_lb, ng)


def _trig_kernel(pos_ref, invf_ref, c_ref, s_ref, ct_ref, st_ref):
    ang = pos_ref[...].astype(F32) * invf_ref[...]
    c = jnp.cos(ang)
    s = jnp.sin(ang)
    c_ref[...] = c
    s_ref[...] = s
    ct_ref[...] = c.T
    st_ref[...] = s.T


def rope_tables(positions, tm):
    t = positions.size
    inv_freq = ROPE_BASE ** (-jnp.arange(0, C_ROPE, 2, dtype=F32) / C_ROPE)
    half = C_ROPE // 2
    invf = jnp.zeros((LANES,), F32)
    invf = invf.at[C_NOPE:C_NOPE + half].set(inv_freq).at[C_NOPE + half:C_NOPE + C_ROPE].set(inv_freq)
    return pl.pallas_call(
        _trig_kernel,
        out_shape=(jax.ShapeDtypeStruct((t, LANES), F32),) * 2 + (jax.ShapeDtypeStruct((LANES, t), F32),) * 2,
        grid=(t // tm,),
        in_specs=[
            pl.BlockSpec((tm, 1), lambda i: (i, 0)),
            pl.BlockSpec((1, LANES), lambda i: (0, 0)),
        ],
        out_specs=(pl.BlockSpec((tm, LANES), lambda i: (i, 0)),) * 2
        + (pl.BlockSpec((LANES, tm), lambda i: (0, i)),) * 2,
        compiler_params=_cparams(("parallel",)),
        name="rope_tables",
    )(positions.reshape(t, 1), invf.reshape(1, LANES))


def _band_attn_kernel(q_ref, k_ref, v_ref, bias_ref, o_ref):
    lane = lax.broadcasted_iota(jnp.int32, (A_QTILE, LANES), 1)
    n_shift = A_PAD // A_QTILE
    starts, scores = [], []
    for t in range(A_TILES_PER_STEP):
        c2 = pl.program_id(2) * A_TILES_PER_STEP + t
        start = pl.multiple_of(jnp.maximum(c2 * A_QTILE - A_PAD, 0), A_QTILE)
        shift = jnp.minimum(c2, n_shift)
        kwin = k_ref[pl.ds(start, A_BAND), :]
        q = q_ref[t * A_QTILE:(t + 1) * A_QTILE, :].astype(F32) * (HEAD_DIM ** -0.5)
        qst = jnp.concatenate([jnp.where(lane < HEAD_DIM, q, 0.0), jnp.where(lane < HEAD_DIM, 0.0, q)],
                              axis=0).astype(BF16)
        s = lax.dot_general(qst, kwin, _NT, preferred_element_type=F32)
        starts.append(start)
        scores.append(s + bias_ref[shift, 0])
    for t in range(A_TILES_PER_STEP):
        s = scores[t]
        vwin = v_ref[pl.ds(starts[t], A_BAND), :]
        m = jnp.max(s, axis=-1, keepdims=True)
        p = jnp.exp(s - m)
        l = jnp.sum(p, axis=-1, keepdims=True)
        pv = jnp.dot(p.astype(BF16), vwin, preferred_element_type=F32) / l
        o_ref[t * A_QTILE:(t + 1) * A_QTILE, :] = jnp.where(lane < HEAD_DIM, pv[:A_QTILE], pv[A_QTILE:]).astype(o_ref.dtype)


def band_attention(proj, bias, bsz, seq, q_col, k_col, v_col):
    t = bsz * seq
    rows = A_QTILE * A_TILES_PER_STEP
    n_steps = seq // rows
    n_shift = A_PAD // A_QTILE
    return pl.pallas_call(
        _band_attn_kernel,
        out_shape=jax.ShapeDtypeStruct((t, WIDTH), BF16),
        grid=(bsz, N_PAIRS, n_steps),
        in_specs=[
            pl.BlockSpec((rows, LANES), lambda b, p, c: (b * n_steps + c, q_col + p)),
            pl.BlockSpec((seq, LANES), lambda b, p, c: (b, k_col + p)),
            pl.BlockSpec((seq, LANES), lambda b, p, c: (b, v_col + p)),
            pl.BlockSpec((n_shift + 1, 1, 2 * A_QTILE, A_BAND), lambda b, p, c: (0, p, 0, 0)),
        ],
        out_specs=pl.BlockSpec((rows, LANES), lambda b, p, c: (b * n_steps + c, p)),
        compiler_params=_cparams(("parallel", "parallel", "arbitrary")),
        name="band_attention",
    )(proj, proj, proj, bias.reshape(n_shift + 1, N_PAIRS, 2 * A_QTILE, A_BAND))


A_ROLL = 768


def _band_bias_kernel(r_ref, o_ref):
    d = pl.program_id(0) * A_QTILE
    prof = jnp.broadcast_to(r_ref[...], (A_QTILE, A_ROLL))
    b = pltpu.roll(prof, 0, 1, stride=1, stride_axis=0)[:, :A_BAND]
    i = lax.broadcasted_iota(jnp.int32, (A_QTILE, A_BAND), 0)
    j = lax.broadcasted_iota(jnp.int32, (A_QTILE, A_BAND), 1)
    gap = (d + i) // CHUNK - j // CHUNK
    o_ref[...] = jnp.where((gap >= 0) & (gap <= A_LEFT_CHUNKS), b, NEG_BIG)


def band_bias_table(rel_table):
    n_shift = A_PAD // A_QTILE
    heads = rel_table.shape[0]
    pad = A_ROLL + A_PAD
    ext = jnp.pad(rel_table.astype(F32)[:, ::-1], ((0, 0), (pad, pad)), mode="edge")

    def seg(d, u0, n):
        o = u0 - d + A_MAX_REL + pad
        return ext[:, o:o + n]

    n_neg = A_ROLL - (A_BAND + 1)
    prof = jnp.stack([jnp.concatenate([seg(s * A_QTILE, 0, A_BAND + 1), seg(s * A_QTILE, -n_neg, n_neg)], axis=1)
                      for s in range(n_shift + 1)])
    return pl.pallas_call(
        _band_bias_kernel,
        out_shape=jax.ShapeDtypeStruct((n_shift + 1, heads, A_QTILE, A_BAND), F32),
        grid=(n_shift + 1, heads),
        in_specs=[pl.BlockSpec((None, None, 1, A_ROLL), lambda s, h: (s, h, 0, 0))],
        out_specs=pl.BlockSpec((None, None, A_QTILE, A_BAND), lambda s, h: (s, h, 0, 0)),
        compiler_params=_cparams(("parallel", "parallel")),
        name="band_bias",
    )(prof.reshape(n_shift + 1, heads, 1, A_ROLL))


def _split3(x):
    hi = x.astype(BF16)
    r1 = x - hi.astype(F32)
    mid = r1.astype(BF16)
    lo = (r1 - mid.astype(F32)).astype(BF16)
    return hi, mid, lo


HGRN_STAGES = 6


def _hgrn_stages(z, qraw, v, graw, log_lb, log1m_lb, norm_g, st_ref, y_out):
    rows = z.shape[0]
    n_chunks = rows // CHUNK
    n_blk = rows // SUB

    head0 = lax.broadcasted_iota(jnp.int32, (rows, LANES), 1) < HEAD_DIM
    head0_c = lax.broadcasted_iota(jnp.int32, (CHUNK, LANES), 1) < HEAD_DIM
    r64 = lax.broadcasted_iota(jnp.int32, (CHUNK, CHUNK), 0)
    c64 = lax.broadcasted_iota(jnp.int32, (CHUNK, CHUNK), 1)
    tril = (c64 <= r64).astype(BF16)
    c64s = lax.broadcasted_iota(jnp.int32, (2 * CHUNK, CHUNK), 1)
    rl = lax.broadcasted_iota(jnp.int32, (LANES, LANES), 0)
    cl = lax.broadcasted_iota(jnp.int32, (LANES, LANES), 1)
    same_head = (rl // HEAD_DIM) == (cl // HEAD_DIM)
    ri = lax.broadcasted_iota(jnp.int32, (CHUNK, CHUNK * SUB), 0)
    ci = lax.broadcasted_iota(jnp.int32, (CHUNK, CHUNK * SUB), 1)
    pick = (ci // SUB == ri).astype(BF16)
    row_s = lax.broadcasted_iota(jnp.int32, (SUB, LANES), 0)

    def chunk(x, c):
        return x[c * CHUNK:(c + 1) * CHUNK]

    log_sig = jnp.minimum(z, 0.0) - jnp.log(1.0 + jnp.exp(-jnp.abs(z)))
    bterm = log1m_lb + log_sig
    log_f = jnp.maximum(log_lb, bterm) + jnp.log(1.0 + jnp.exp(-jnp.abs(log_lb - bterm)))
    log_k = bterm - z
    qs = qraw * (1.0 / (1.0 + jnp.exp(-qraw)))

    x3 = jnp.concatenate(_split3(log_f), axis=1)
    cum3 = [jnp.dot(tril, chunk(x3, c), preferred_element_type=F32) for c in range(n_chunks)]
    yield
    cum = jnp.concatenate([t[:, :LANES] + t[:, LANES:2 * LANES] + t[:, 2 * LANES:] for t in cum3], axis=0)

    zero_row = jnp.zeros((1, LANES), F32)
    c_end = [cum[SUB * b + SUB - 1:SUB * b + SUB, :] for b in range(n_blk)]
    c_start = [zero_row if b % N_SUB == 0 else c_end[b - 1] for b in range(n_blk)]
    c_last = [c_end[c * N_SUB + N_SUB - 1] for c in range(n_chunks)]

    def rows_of(blocks):
        return jnp.concatenate([jnp.broadcast_to(r, (SUB, LANES)) for r in blocks], axis=0)

    cstart_full = rows_of(c_start)
    cend_full = rows_of(c_end)
    clast_full = jnp.concatenate([jnp.broadcast_to(r, (CHUNK, LANES)) for r in c_last], axis=0)
    q1 = qs * jnp.exp(cum - cstart_full)
    lk = log_k - cum
    k2b = jnp.exp(cend_full + lk).astype(BF16)
    k_end = jnp.exp(clast_full + lk).astype(BF16)
    q_state = (qs * jnp.exp(cum)).astype(BF16)
    vb = v.astype(BF16)

    n_sup = CHUNK // SUP
    ce_sup = [c_end[(SUP // SUB) * (b + 1) - 1] for b in range(rows // SUP)]
    cs_sup = [zero_row if b % n_sup == 0 else ce_sup[b - 1] for b in range(rows // SUP)]

    def rows_of_sup(blocks):
        return jnp.concatenate([jnp.broadcast_to(r, (SUP, LANES)) for r in blocks], axis=0)

    q1s = qs * jnp.exp(cum - rows_of_sup(cs_sup))
    k2s = jnp.exp(rows_of_sup(ce_sup) + lk).astype(BF16)
    odd_sub = (lax.broadcasted_iota(jnp.int32, (rows, LANES), 0) // SUB) % 2 == 1
    q_adj = jnp.where(odd_sub, q1, 0.0)
    q_adj = (jnp.where(head0, q_adj, 0.0).astype(BF16), jnp.where(head0, 0.0, q_adj).astype(BF16))
    q_sup = []
    for j in range(n_sup - 1):
        d_rows = []
        for b in range(rows // SUP):
            if b % n_sup > j:
                d_rows.append(jnp.exp(cs_sup[b] - ce_sup[(b // n_sup) * n_sup + j]))
            else:
                d_rows.append(zero_row)
        qj = q1s * rows_of_sup(d_rows)
        q_sup.append((jnp.where(head0, qj, 0.0).astype(BF16), jnp.where(head0, 0.0, qj).astype(BF16)))
    s_adj, s_sup = [], []
    for c in range(n_chunks):
        qst = jnp.concatenate([chunk(q_adj[0], c), chunk(q_adj[1], c)], axis=0)
        s_adj.append(lax.dot_general(qst, chunk(k2b, c), _NT, preferred_element_type=F32))
        for j in range(n_sup - 1):
            qst = jnp.concatenate([chunk(q_sup[j][0], c), chunk(q_sup[j][1], c)], axis=0)
            s_sup.append(lax.dot_general(qst, chunk(k2s, c), _NT, preferred_element_type=F32))
    yield
    r64s = lax.broadcasted_iota(jnp.int32, (2 * CHUNK, CHUNK), 0) % CHUNK
    prev_sub = c64s // SUB == r64s // SUB - 1
    o_cross = []
    for c in range(n_chunks):
        sc = jnp.where(prev_sub, s_adj[c], 0.0)
        for j in range(n_sup - 1):
            col_in_j = (c64s >= j * SUP) & (c64s < (j + 1) * SUP)
            sc = sc + jnp.where(col_in_j, s_sup[c * (n_sup - 1) + j], 0.0)
        o2 = jnp.dot(sc.astype(BF16), chunk(vb, c), preferred_element_type=F32)
        o_cross.append(jnp.where(head0_c, o2[:CHUNK], o2[CHUNK:]))

    yield
    a = cum - log_k
    w_rows = []
    for b in range(n_blk):
        a_b = a[SUB * b:SUB * (b + 1), :]
        for i in range(SUB):
            r = SUB * b + i
            arg = jnp.where(row_s <= i, cum[r:r + 1, :] - a_b, NEG_BIG)
            w_rows.append(jnp.exp(arg) * qs[r:r + 1, :])
    w_all = jnp.concatenate(w_rows, axis=0)
    head0_w = lax.broadcasted_iota(jnp.int32, w_all.shape, 1) < HEAD_DIM
    sum0 = jnp.sum(jnp.where(head0_w, w_all, 0.0), axis=-1, keepdims=True)
    sum1 = jnp.sum(jnp.where(head0_w, 0.0, w_all), axis=-1, keepdims=True)
    sb = jnp.where(head0_w, sum0, sum1)
    yield
    o_diag = []
    for c in range(n_chunks):
        v_rep = jnp.concatenate([v[SUB * (r // SUB):SUB * (r // SUB + 1), :]
                                 for r in range(c * CHUNK, (c + 1) * CHUNK)], axis=0)
        sb_c = sb[c * CHUNK * SUB:(c + 1) * CHUNK * SUB]
        o_diag.append(jnp.dot(pick, (sb_c * v_rep).astype(BF16), preferred_element_type=F32))

    yield
    upd = [jnp.where(same_head,
                     lax.dot_general(chunk(vb, c), chunk(k_end, c), (((0,), (0,)), ((), ())),
                                     preferred_element_type=F32), 0.0) for c in range(n_chunks)]
    st = st_ref[...]
    o_state = []
    for c in range(n_chunks):
        o_state.append(lax.dot_general(chunk(q_state, c), st.astype(BF16), _NT, preferred_element_type=F32))
        st = st * jnp.exp(c_last[c]) + upd[c]
    st_ref[...] = st

    o = jnp.concatenate([o_cross[c] + o_diag[c] + o_state[c] for c in range(n_chunks)], axis=0)
    osq = o * o
    ms0 = jnp.sum(jnp.where(head0, osq, 0.0), axis=-1, keepdims=True)
    ms1 = jnp.sum(jnp.where(head0, 0.0, osq), axis=-1, keepdims=True)
    ms = jnp.where(head0, ms0, ms1) * (1.0 / HEAD_DIM)
    y = o * lax.rsqrt(ms + EPS) * norm_g
    y_out.append(y * (graw * (1.0 / (1.0 + jnp.exp(-graw)))))
    yield


def _mla_flash_kernel(qt_ref, k0_ref, k1_ref, vt_ref, o_ref, acc_ref, m_ref, l_ref, s_ref, mt_ref, *, n_qt):
    c = C_QSCALE

    def scores(qi, j, e):
        qoff = pl.multiple_of(qi * C_TILE, C_TILE)
        koff = pl.multiple_of(j * C_TILE, C_TILE)
        k = (k0_ref, k1_ref)[e][pl.ds(koff, C_TILE), :]
        qt = qt_ref[e * C_SLOT:(e + 1) * C_SLOT, pl.ds(qoff, C_TILE)]
        return jnp.dot(k, qt, preferred_element_type=F32)

    def put_scores(qi, j, e):
        s = scores(qi, j, e)
        s_ref[e] = s
        mt_ref[e] = jnp.max(s, axis=0, keepdims=True)

    def softmax_pv(j, e, s, m_tile):
        off = pl.multiple_of(j * C_TILE, C_TILE)
        m_prev = m_ref[e]
        m_new = jnp.maximum(m_prev, m_tile)
        p = jnp.exp2((s - m_new) * c).astype(BF16)
        alpha = jnp.exp2((m_prev - m_new) * c)
        vt = vt_ref[e * HEAD_DIM:(e + 1) * HEAD_DIM, pl.ds(off, C_TILE)]
        vt_aug = jnp.concatenate([vt, jnp.ones((ONES_ROWS, C_TILE), BF16)], axis=0)
        pv = jnp.dot(vt_aug, p, preferred_element_type=F32)
        l_ref[e] = alpha * l_ref[e] + pv[HEAD_DIM:HEAD_DIM + 1]
        acc_ref[e] = alpha * acc_ref[e] + pv[:HEAD_DIM]
        m_ref[e] = m_new

    kc = lax.broadcasted_iota(jnp.int32, (C_TILE, C_TILE), 0) // CHUNK
    qc = lax.broadcasted_iota(jnp.int32, (C_TILE, C_TILE), 1) // CHUNK

    for e in range(2):
        put_scores(0, 0, e)

    def q_body(qi, carry):
        m_ref[...] = jnp.full_like(m_ref, -jnp.inf)
        l_ref[...] = jnp.zeros_like(l_ref)
        acc_ref[...] = jnp.zeros_like(acc_ref)

        def body(j, carry2):
            for e in range(2):
                s, m_tile = s_ref[e], mt_ref[e]
                put_scores(qi, j + 1, e)
                softmax_pv(j, e, s, m_tile)
            return carry2

        lax.fori_loop(0, qi, body, 0)
        q_next = jnp.minimum(qi + 1, n_qt - 1)
        for e in range(2):
            s = jnp.where(kc <= qc, s_ref[e], NEG_BIG)
            put_scores(q_next, 0, e)
            softmax_pv(qi, e, s, jnp.max(s, axis=0, keepdims=True))

        o_t = jnp.concatenate([acc_ref[0] / l_ref[0], acc_ref[1] / l_ref[1]], axis=0)
        o_ref[pl.ds(pl.multiple_of(qi * C_TILE, C_TILE), C_TILE), :] = o_t.T.astype(o_ref.dtype)
        return carry

    lax.fori_loop(0, n_qt, q_body, 0)


def mla_flash(qt, k, vt, bsz, seq):
    t = bsz * seq
    return pl.pallas_call(
        functools.partial(_mla_flash_kernel, n_qt=seq // C_TILE),
        out_shape=jax.ShapeDtypeStruct((t, WIDTH), BF16),
        grid=(bsz, N_PAIRS),
        in_specs=[
            pl.BlockSpec((2 * C_SLOT, seq), lambda b, p: (p, b)),
            pl.BlockSpec((seq, C_SLOT), lambda b, p: (b, 2 * p)),
            pl.BlockSpec((seq, C_SLOT), lambda b, p: (b, 2 * p + 1)),
            pl.BlockSpec((LANES, seq), lambda b, p: (p, b)),
        ],
        out_specs=pl.BlockSpec((seq, LANES), lambda b, p: (b, p)),
        scratch_shapes=[pltpu.VMEM((2, HEAD_DIM, C_TILE), F32),
                        pltpu.VMEM((2, 1, C_TILE), F32),
                        pltpu.VMEM((2, 1, C_TILE), F32),
                        pltpu.VMEM((2, C_TILE, C_TILE), F32),
                        pltpu.VMEM((2, 1, C_TILE), F32)],
        compiler_params=_cparams(("parallel", "parallel")),
        name="mla_flash",
    )(qt, k, k, vt)


def _merge_kernel(x_ref, ya_ref, yb_ref, yc_ref, ga_ref, gb_ref, gc_ref, wbr_ref, wout_ref, o_ref):
    merged = None
    for n, (y_ref, g_ref) in enumerate(((ya_ref, ga_ref), (yb_ref, gb_ref), (yc_ref, gc_ref))):
        up = jnp.dot(y_ref[...], wbr_ref[n], preferred_element_type=F32)
        gl = g_ref[...].astype(F32)
        term = (0.5 * jnp.tanh(0.5 * gl) + 0.5) * up
        merged = term if merged is None else merged + term
    o_ref[...] = x_ref[...] + jnp.dot(merged.astype(BF16), wout_ref[...], preferred_element_type=F32)


def merge_out(x, ya, yb, yc, gates, wbr, wout, tm):
    t = x.shape[0]
    return pl.pallas_call(
        _merge_kernel,
        out_shape=jax.ShapeDtypeStruct((t, D_MODEL), F32),
        grid=(t // tm,),
        in_specs=[
            pl.BlockSpec((tm, D_MODEL), lambda i: (i, 0)),
            pl.BlockSpec((tm, WIDTH), lambda i: (i, 0)),
            pl.BlockSpec((tm, WIDTH), lambda i: (i, 0)),
            pl.BlockSpec((tm, WIDTH), lambda i: (i, 0)),
            pl.BlockSpec((tm, D_MODEL), lambda i: (i, 0)),
            pl.BlockSpec((tm, D_MODEL), lambda i: (i, 1)),
            pl.BlockSpec((tm, D_MODEL), lambda i: (i, 2)),
            _resident(wbr),
            _resident(wout),
        ],
        out_specs=pl.BlockSpec((tm, D_MODEL), lambda i: (i, 0)),
        compiler_params=_cparams(("parallel",)),
        name="merge_out",
    )(x, ya, yb, yc, gates, gates, gates, wbr, wout)


def _ffn_kernel(x_ref, g_ref, w1_ref, w2_ref, gf_ref, o_ref, *, final_norm, tf):
    x = x_ref[...]
    h = _rms(x, g_ref[...]).astype(BF16)
    acc = None
    for k in range(w1_ref.shape[1] // tf):
        u = jnp.maximum(jnp.dot(h, w1_ref[:, k * tf:(k + 1) * tf], preferred_element_type=F32), 0.0)
        part = jnp.dot((u * u).astype(BF16), w2_ref[k * tf:(k + 1) * tf, :], preferred_element_type=F32)
        acc = part if acc is None else acc + part
    y = x + acc
    if final_norm:
        y = _rms(y, gf_ref[...])
    o_ref[...] = y


def ffn(x, g, w1, w2, gf, final_norm, tm, tf):
    t, d = x.shape
    return pl.pallas_call(
        functools.partial(_ffn_kernel, final_norm=final_norm, tf=tf),
        out_shape=jax.ShapeDtypeStruct((t, d), F32),
        grid=(t // tm,),
        in_specs=[
            pl.BlockSpec((tm, d), lambda i: (i, 0)),
            pl.BlockSpec((1, d), lambda i: (0, 0)),
            _resident(w1),
            _resident(w2),
            pl.BlockSpec((1, d), lambda i: (0, 0)),
        ],
        out_specs=pl.BlockSpec((tm, d), lambda i: (i, 0)),
        compiler_params=_cparams(("parallel",)),
        name="ffn",
    )(x, g.reshape(1, d), w1, w2, gf.reshape(1, d))


COL_AQ, COL_AK, COL_AV = 0, 4, 8
COL_BQ, COL_BI, COL_BG = 12, 16, 20
GATE_BLK = 1


def _prep_w_in(w):
    o = 0
    parts = {}
    for name, size in (("aq", 512), ("ak", 512), ("av", 512), ("bq", 512), ("bf", 512), ("bi", 512),
                       ("bg", 512), ("cq", C_Q_RANK), ("ckv", C_KV_RANK), ("ckr", C_ROPE),
                       ("gate", N_BRANCH * D_MODEL)):
        parts[name] = w[:, o:o + size]
        o += size
    w16 = jnp.concatenate([parts[n] for n in ("aq", "ak", "av", "bq", "bi", "bg", "gate")], axis=1)
    half = C_ROPE // 2
    x1, x2 = parts["ckr"][:, :half], parts["ckr"][:, half:]
    zl = jnp.zeros((w.shape[0], C_NOPE), w.dtype)
    zr = jnp.zeros((w.shape[0], C_SLOT - C_NOPE - C_ROPE), w.dtype)
    kr_slot = jnp.concatenate([zl, x1, x2, zr], axis=1)
    kr_sw = jnp.concatenate([zl, -x2, x1, zr], axis=1)
    w32 = jnp.concatenate([parts["cq"], parts["ckv"], kr_slot, kr_sw, parts["bf"]], axis=1)
    return w16.astype(BF16), w32.astype(BF16)


def _prep_w_mla(w_uq, w_ukv):
    half = C_ROPE // 2
    per_q = C_NOPE + C_ROPE
    wq = w_uq.reshape(C_Q_RANK, N_HEADS, per_q)
    nope, x1, x2 = wq[..., :C_NOPE], wq[..., C_NOPE:C_NOPE + half], wq[..., C_NOPE + half:]
    zpad = jnp.zeros((C_Q_RANK, N_HEADS, C_SLOT - per_q), w_uq.dtype)
    wq1 = jnp.concatenate([nope, x1, x2, zpad], axis=-1).reshape(C_Q_RANK, N_HEADS * C_SLOT)
    wq2 = jnp.concatenate([jnp.zeros_like(nope), -x2, x1, zpad], axis=-1).reshape(C_Q_RANK, N_HEADS * C_SLOT)
    wkv = w_ukv.reshape(C_KV_RANK, N_HEADS, C_NOPE + HEAD_DIM)
    k_nope, v = wkv[..., :C_NOPE], wkv[..., C_NOPE:]
    wk = jnp.concatenate([k_nope, jnp.zeros((C_KV_RANK, N_HEADS, C_SLOT - C_NOPE), w_ukv.dtype)],
                         axis=-1).reshape(C_KV_RANK, N_HEADS * C_SLOT)
    wv = v.reshape(C_KV_RANK, WIDTH)
    return wq1.T.astype(BF16), wq2.T.astype(BF16), wk.astype(BF16), wv.T.astype(BF16)


def kernel(x, positions, norm_mix_g, w_in, rel_bias, hgrn_lb_logits, hgrn_norm_g, mla_q_norm_g,
           mla_kv_norm_g, mla_w_uq, mla_w_ukv, w_branch, w_out, norm_ffn_g, w_ff1, w_ff2, final_norm_g):
    bsz, seq, d = x.shape
    depth = w_in.shape[0]
    t = bsz * seq
    assert d == D_MODEL and seq % C_TILE == 0 and seq % (A_QTILE * A_TILES_PER_STEP) == 0
    tm = min(TM_DENSE, t)

    p_lb = jax.nn.softmax(hgrn_lb_logits.astype(F32), axis=0)
    lb_all = jnp.cumsum(p_lb, axis=0)
    lb_all = lb_all - lb_all[0:1]

    tabs = rope_tables(positions, tm)
    xf = x.reshape(t, d)
    for l in range(depth):
        w16, w32 = _prep_w_in(w_in[l])
        wq1t, wq2t, wk, wvt = _prep_w_mla(mla_w_uq[l], mla_w_ukv[l])
        proj_a, gates, y_b, qt_c, k_c, vt_c = in_proj(xf, norm_mix_g[l], w16, w32, tabs, mla_q_norm_g[l], mla_kv_norm_g[l],
                                               wq1t, wq2t, wk, wvt, lb_all[l], hgrn_norm_g[l], seq,
                                               min(TM_IN_PROJ, seq), TN_IN_PROJ)

        y_a = band_attention(proj_a, band_bias_table(rel_bias[l]), bsz, seq, COL_AQ, COL_AK, COL_AV)
        y_c = mla_flash(qt_c, k_c, vt_c, bsz, seq)

        xf = merge_out(xf, y_a, y_b, y_c, gates, w_branch[l].astype(BF16), w_out[l].astype(BF16), tm)
        xf = ffn(xf, norm_ffn_g[l], w_ff1[l].astype(BF16), w_ff2[l].astype(BF16), final_norm_g,
                 l == depth - 1, tm, TF_FFN)
    return xf.reshape(bsz, seq, d)
```

```python
import functools

import jax
import jax.numpy as jnp
from jax import lax
from jax.experimental import pallas as pl
from jax.experimental.pallas import tpu as pltpu

F32 = jnp.float32
BF16 = jnp.bfloat16

D_MODEL = 1024
CHUNK = 64
EPS = 1e-6
N_HEADS = 8
HEAD_DIM = 64
LANES = 128
N_PAIRS = N_HEADS * HEAD_DIM // LANES
WIDTH = N_HEADS * HEAD_DIM

A_LEFT_CHUNKS = 8
A_MAX_REL = 128
A_QTILE = 2 * CHUNK
A_BAND = (A_LEFT_CHUNKS + 2) * CHUNK
A_PAD = A_LEFT_CHUNKS * CHUNK
A_TILES_PER_STEP = 32

C_Q_RANK = 256
C_KV_RANK = 128
C_ROPE = 32
C_NOPE = 64
C_SLOT = LANES
C_LATENT = C_Q_RANK + C_KV_RANK + 2 * C_SLOT
ROPE_BASE = 10000.0
C_TILE = 512
ONES_ROWS = 16
LOG2E = 1.4426950408889634
C_QSCALE = (C_NOPE + C_ROPE) ** -0.5 * LOG2E

N_BRANCH = 3

SUB = 8
SUP = 2 * SUB
N_SUB = CHUNK // SUB
NEG_BIG = -1e30

VMEM_LIMIT = 48 * 1024 * 1024
VMEM_LIMIT_IN_PROJ = 58 * 1024 * 1024

TM_IN_PROJ = 512
TN_IN_PROJ = 1024
TM_DENSE = 1024
TF_FFN = 1024

_NT = (((1,), (1,)), ((), ()))


def _cparams(sem, vmem_limit=VMEM_LIMIT):
    return pltpu.CompilerParams(dimension_semantics=sem, vmem_limit_bytes=vmem_limit)


def _resident(a):
    return pl.BlockSpec(a.shape, lambda *_: (0,) * a.ndim, pipeline_mode=pl.Buffered(1))


def _rms(x, g):
    return x * lax.rsqrt(jnp.mean(x * x, axis=-1, keepdims=True) + EPS) * g


def _in_proj_kernel(x_ref, g_ref, wa_ref, wb_ref, ct_ref, st_ref, ctt_ref, stt_ref, gq_ref, gkv_ref,
                    wq1t_ref, wq2t_ref, wk_ref, wvt_ref, loglb_ref, log1mlb_ref, ng_ref,
                    oa_ref, og_ref, yb_ref, qt_ref, k_ref, vt_ref, state_ref, *, tn, tiles_per_seq):
    @pl.when(pl.program_id(0) % tiles_per_seq == 0)
    def _():
        state_ref[...] = jnp.zeros_like(state_ref)

    h = _rms(x_ref[...], g_ref[...]).astype(BF16)

    a_cols = COL_BQ * LANES
    g_col0 = GATE_BLK * N_BRANCH * D_MODEL

    def proj_tile(j):
        lo, hi = j * tn, (j + 1) * tn
        t = jnp.dot(h, wa_ref[:, lo:hi], preferred_element_type=F32)
        if lo < a_cols:
            oa_ref[:, lo:min(hi, a_cols)] = t[:, :min(hi, a_cols) - lo].astype(oa_ref.dtype)
        if hi > g_col0:
            og_ref[:, max(lo, g_col0) - g_col0:hi - g_col0] = t[:, max(lo, g_col0) - lo:].astype(og_ref.dtype)
        return t

    tq = proj_tile(COL_BQ * LANES // tn)
    tig = proj_tile(COL_BI * LANES // tn)
    pb = jnp.dot(h, wb_ref[...], preferred_element_type=F32)
    q_off = COL_BQ * LANES % tn
    i_off = COL_BI * LANES % tn
    g_off = COL_BG * LANES - (COL_BI * LANES // tn) * tn
    y_out = []
    gens = []
    for p in range(N_PAIRS):
        lanes = lambda off: slice(off + p * LANES, off + (p + 1) * LANES)
        gens.append(_hgrn_stages(pb[:, lanes(C_LATENT)], tq[:, lanes(q_off)], tig[:, lanes(i_off)],
                                 tig[:, lanes(g_off)], loglb_ref[p], log1mlb_ref[p], ng_ref[...],
                                 state_ref.at[p], y_out))

    qn = _rms(pb[:, :C_Q_RANK], gq_ref[...]).astype(BF16)
    kvn = _rms(pb[:, C_Q_RANK:C_Q_RANK + C_KV_RANK], gkv_ref[...]).astype(BF16)
    kr = pb[:, C_Q_RANK + C_KV_RANK:C_Q_RANK + C_KV_RANK + C_SLOT]
    kr_sw = pb[:, C_Q_RANK + C_KV_RANK + C_SLOT:C_LATENT]
    kr_rot = kr * ct_ref[...] + kr_sw * st_ref[...]

    def mla_q():
        a_t = lax.dot_general(wq1t_ref[...], qn, _NT, preferred_element_type=F32)
        b_t = lax.dot_general(wq2t_ref[...], qn, _NT, preferred_element_type=F32)
        ctt = ctt_ref[...]
        stt = stt_ref[...]
        for hd in range(N_HEADS):
            sl = slice(hd * C_SLOT, (hd + 1) * C_SLOT)
            qt_ref[sl, :] = (a_t[sl, :] * ctt + b_t[sl, :] * stt).astype(qt_ref.dtype)

    def mla_kv():
        kn = jnp.dot(kvn, wk_ref[...], preferred_element_type=F32)
        vt_ref[...] = lax.dot_general(wvt_ref[...], kvn, _NT, preferred_element_type=F32).astype(vt_ref.dtype)
        for hd in range(N_HEADS):
            sl = slice(hd * C_SLOT, (hd + 1) * C_SLOT)
            k_ref[:, sl] = (kn[:, sl] + kr_rot).astype(k_ref.dtype)

    done = {COL_BQ * LANES // tn, COL_BI * LANES // tn}
    big = [functools.partial(proj_tile, j) for j in range(wa_ref.shape[1] // tn) if j not in done] + [mla_q, mla_kv]
    assert len(big) == HGRN_STAGES
    for work in big:
        work()
        for gen in gens:
            next(gen)
    for p in range(N_PAIRS):
        yb_ref[:, p * LANES:(p + 1) * LANES] = y_out[p].astype(yb_ref.dtype)


def in_proj(x, g, wa, wb, tabs, gq, gkv, wq1t, wq2t, wk, wvt, lb, hgrn_norm_g, seq, tm, tn):
    t, d = x.shape
    na = COL_BQ * LANES
    ng_cols = N_BRANCH * D_MODEL
    ctab, stab, ctab_t, stab_t = tabs
    gq = gq.reshape(1, -1)
    gkv = gkv.reshape(1, -1)
    log_lb = jnp.log(lb).reshape(N_PAIRS, 1, LANES)
    log1m_lb = jnp.log1p(-lb).reshape(N_PAIRS, 1, LANES)
    ng = jnp.tile(hgrn_norm_g.astype(F32), LANES // HEAD_DIM).reshape(1, LANES)
    rows = lambda n: pl.BlockSpec((tm, n), lambda i: (i, 0))
    cols = lambda n: pl.BlockSpec((n, tm), lambda i: (0, i))
    return pl.pallas_call(
        functools.partial(_in_proj_kernel, tn=tn, tiles_per_seq=seq // tm),
        out_shape=(jax.ShapeDtypeStruct((t, na), BF16),
                   jax.ShapeDtypeStruct((t, ng_cols), BF16),
                   jax.ShapeDtypeStruct((t, WIDTH), BF16),
                   jax.ShapeDtypeStruct((N_HEADS * C_SLOT, t), BF16),
                   jax.ShapeDtypeStruct((t, N_HEADS * C_SLOT), BF16),
                   jax.ShapeDtypeStruct((WIDTH, t), BF16)),
        grid=(t // tm,),
        in_specs=[rows(d), pl.BlockSpec((1, d), lambda i: (0, 0)), _resident(wa), _resident(wb),
                  rows(LANES), rows(LANES), cols(LANES), cols(LANES),
                  _resident(gq), _resident(gkv), _resident(wq1t), _resident(wq2t), _resident(wk), _resident(wvt),
                  _resident(log_lb), _resident(log1m_lb), _resident(ng)],
        out_specs=(rows(na), rows(ng_cols), rows(WIDTH), cols(N_HEADS * C_SLOT), rows(N_HEADS * C_SLOT), cols(WIDTH)),
        scratch_shapes=[pltpu.VMEM((N_PAIRS, LANES, LANES), F32)],
        compiler_params=_cparams(("arbitrary",), VMEM_LIMIT_IN_PROJ),
        name="in_proj",
    )(x, g.reshape(1, d), wa, wb, ctab, stab, ctab_t, stab_t, gq, gkv, wq1t, wq2t, wk, wvt, log_lb, log1m_lb, ng)


def _trig_kernel(pos_ref, invf_ref, c_ref, s_ref, ct_ref, st_ref):
    ang = pos_ref[...].astype(F32) * invf_ref[...]
    c = jnp.cos(ang)
    s = jnp.sin(ang)
    c_ref[...] = c
    s_ref[...] = s
    ct_ref[...] = c.T
    st_ref[...] = s.T


def rope_tables(positions, tm):
    t = positions.size
    inv_freq = ROPE_BASE ** (-jnp.arange(0, C_ROPE, 2, dtype=F32) / C_ROPE)
    half = C_ROPE // 2
    invf = jnp.zeros((LANES,), F32)
    invf = invf.at[C_NOPE:C_NOPE + half].set(inv_freq).at[C_NOPE + half:C_NOPE + C_ROPE].set(inv_freq)
    return pl.pallas_call(
        _trig_kernel,
        out_shape=(jax.ShapeDtypeStruct((t, LANES), F32),) * 2 + (jax.ShapeDtypeStruct((LANES, t), F32),) * 2,
        grid=(t // tm,),
        in_specs=[
            pl.BlockSpec((tm, 1), lambda i: (i, 0)),
            pl.BlockSpec((1, LANES), lambda i: (0, 0)),
        ],
        out_specs=(pl.BlockSpec((tm, LANES), lambda i: (i, 0)),) * 2
        + (pl.BlockSpec((LANES, tm), lambda i: (0, i)),) * 2,
        compiler_params=_cparams(("parallel",)),
        name="rope_tables",
    )(positions.reshape(t, 1), invf.reshape(1, LANES))


def _band_attn_kernel(q_ref, k_ref, v_ref, bias_ref, o_ref):
    lane = lax.broadcasted_iota(jnp.int32, (A_QTILE, LANES), 1)
    n_shift = A_PAD // A_QTILE
    starts, scores = [], []
    for t in range(A_TILES_PER_STEP):
        c2 = pl.program_id(2) * A_TILES_PER_STEP + t
        start = pl.multiple_of(jnp.maximum(c2 * A_QTILE - A_PAD, 0), A_QTILE)
        shift = jnp.minimum(c2, n_shift)
        kwin = k_ref[pl.ds(start, A_BAND), :]
        q = q_ref[t * A_QTILE:(t + 1) * A_QTILE, :].astype(F32) * (HEAD_DIM ** -0.5)
        qst = jnp.concatenate([jnp.where(lane < HEAD_DIM, q, 0.0), jnp.where(lane < HEAD_DIM, 0.0, q)],
                              axis=0).astype(BF16)
        s = lax.dot_general(qst, kwin, _NT, preferred_element_type=F32)
        starts.append(start)
        scores.append(s + bias_ref[shift, 0])
    for t in range(A_TILES_PER_STEP):
        s = scores[t]
        vwin = v_ref[pl.ds(starts[t], A_BAND), :]
        m = jnp.max(s, axis=-1, keepdims=True)
        p = jnp.exp(s - m)
        l = jnp.sum(p, axis=-1, keepdims=True)
        pv = jnp.dot(p.astype(BF16), vwin, preferred_element_type=F32) / l
        o_ref[t * A_QTILE:(t + 1) * A_QTILE, :] = jnp.where(lane < HEAD_DIM, pv[:A_QTILE], pv[A_QTILE:]).astype(o_ref.dtype)


def band_attention(proj, bias, bsz, seq, q_col, k_col, v_col):
    t = bsz * seq
    rows = A_QTILE * A_TILES_PER_STEP
    n_steps = seq // rows
    n_shift = A_PAD // A_QTILE
    return pl.pallas_call(
        _band_attn_kernel,
        out_shape=jax.ShapeDtypeStruct((t, WIDTH), BF16),
        grid=(bsz, N_PAIRS, n_steps),
        in_specs=[
            pl.BlockSpec((rows, LANES), lambda b, p, c: (b * n_steps + c, q_col + p)),
            pl.BlockSpec((seq, LANES), lambda b, p, c: (b, k_col + p)),
            pl.BlockSpec((seq, LANES), lambda b, p, c: (b, v_col + p)),
            pl.BlockSpec((n_shift + 1, 1, 2 * A_QTILE, A_BAND), lambda b, p, c: (0, p, 0, 0)),
        ],
        out_specs=pl.BlockSpec((rows, LANES), lambda b, p, c: (b * n_steps + c, p)),
        compiler_params=_cparams(("parallel", "parallel", "arbitrary")),
        name="band_attention",
    )(proj, proj, proj, bias.reshape(n_shift + 1, N_PAIRS, 2 * A_QTILE, A_BAND))


A_ROLL = 768


def _band_bias_kernel(r_ref, o_ref):
    d = pl.program_id(0) * A_QTILE
    prof = jnp.broadcast_to(r_ref[...], (A_QTILE, A_ROLL))
    b = pltpu.roll(prof, 0, 1, stride=1, stride_axis=0)[:, :A_BAND]
    i = lax.broadcasted_iota(jnp.int32, (A_QTILE, A_BAND), 0)
    j = lax.broadcasted_iota(jnp.int32, (A_QTILE, A_BAND), 1)
    gap = (d + i) // CHUNK - j // CHUNK
    o_ref[...] = jnp.where((gap >= 0) & (gap <= A_LEFT_CHUNKS), b, NEG_BIG)


def band_bias_table(rel_table):
    n_shift = A_PAD // A_QTILE
    heads = rel_table.shape[0]
    pad = A_ROLL + A_PAD
    ext = jnp.pad(rel_table.astype(F32)[:, ::-1], ((0, 0), (pad, pad)), mode="edge")

    def seg(d, u0, n):
        o = u0 - d + A_MAX_REL + pad
        return ext[:, o:o + n]

    n_neg = A_ROLL - (A_BAND + 1)
    prof = jnp.stack([jnp.concatenate([seg(s * A_QTILE, 0, A_BAND + 1), seg(s * A_QTILE, -n_neg, n_neg)], axis=1)
                      for s in range(n_shift + 1)])
    return pl.pallas_call(
        _band_bias_kernel,
        out_shape=jax.ShapeDtypeStruct((n_shift + 1, heads, A_QTILE, A_BAND), F32),
        grid=(n_shift + 1, heads),
        in_specs=[pl.BlockSpec((None, None, 1, A_ROLL), lambda s, h: (s, h, 0, 0))],
        out_specs=pl.BlockSpec((None, None, A_QTILE, A_BAND), lambda s, h: (s, h, 0, 0)),
        compiler_params=_cparams(("parallel", "parallel")),
        name="band_bias",
    )(prof.reshape(n_shift + 1, heads, 1, A_ROLL))


def _split3(x):
    hi = x.astype(BF16)
    r1 = x - hi.astype(F32)
    mid = r1.astype(BF16)
    lo = (r1 - mid.astype(F32)).astype(BF16)
    return hi, mid, lo


HGRN_STAGES = 6


def _hgrn_stages(z, qraw, v, graw, log_lb, log1m_lb, norm_g, st_ref, y_out):
    rows = z.shape[0]
    n_chunks = rows // CHUNK
    n_blk = rows // SUB

    head0 = lax.broadcasted_iota(jnp.int32, (rows, LANES), 1) < HEAD_DIM
    head0_c = lax.broadcasted_iota(jnp.int32, (CHUNK, LANES), 1) < HEAD_DIM
    r64 = lax.broadcasted_iota(jnp.int32, (CHUNK, CHUNK), 0)
    c64 = lax.broadcasted_iota(jnp.int32, (CHUNK, CHUNK), 1)
    tril = (c64 <= r64).astype(BF16)
    c64s = lax.broadcasted_iota(jnp.int32, (2 * CHUNK, CHUNK), 1)
    rl = lax.broadcasted_iota(jnp.int32, (LANES, LANES), 0)
    cl = lax.broadcasted_iota(jnp.int32, (LANES, LANES), 1)
    same_head = (rl // HEAD_DIM) == (cl // HEAD_DIM)
    ri = lax.broadcasted_iota(jnp.int32, (CHUNK, CHUNK * SUB), 0)
    ci = lax.broadcasted_iota(jnp.int32, (CHUNK, CHUNK * SUB), 1)
    pick = (ci // SUB == ri).astype(BF16)
    row_s = lax.broadcasted_iota(jnp.int32, (SUB, LANES), 0)

    def chunk(x, c):
        return x[c * CHUNK:(c + 1) * CHUNK]

    log_sig = jnp.minimum(z, 0.0) - jnp.log(1.0 + jnp.exp(-jnp.abs(z)))
    bterm = log1m_lb + log_sig
    log_f = jnp.maximum(log_lb, bterm) + jnp.log(1.0 + jnp.exp(-jnp.abs(log_lb - bterm)))
    log_k = bterm - z
    qs = qraw * (1.0 / (1.0 + jnp.exp(-qraw)))

    x3 = jnp.concatenate(_split3(log_f), axis=1)
    cum3 = [jnp.dot(tril, chunk(x3, c), preferred_element_type=F32) for c in range(n_chunks)]
    yield
    cum = jnp.concatenate([t[:, :LANES] + t[:, LANES:2 * LANES] + t[:, 2 * LANES:] for t in cum3], axis=0)

    zero_row = jnp.zeros((1, LANES), F32)
    c_end = [cum[SUB * b + SUB - 1:SUB * b + SUB, :] for b in range(n_blk)]
    c_start = [zero_row if b % N_SUB == 0 else c_end[b - 1] for b in range(n_blk)]
    c_last = [c_end[c * N_SUB + N_SUB - 1] for c in range(n_chunks)]

    def rows_of(blocks):
        return jnp.concatenate([jnp.broadcast_to(r, (SUB, LANES)) for r in blocks], axis=0)

    cstart_full = rows_of(c_start)
    cend_full = rows_of(c_end)
    clast_full = jnp.concatenate([jnp.broadcast_to(r, (CHUNK, LANES)) for r in c_last], axis=0)
    q1 = qs * jnp.exp(cum - cstart_full)
    lk = log_k - cum
    k2b = jnp.exp(cend_full + lk).astype(BF16)
    k_end = jnp.exp(clast_full + lk).astype(BF16)
    q_state = (qs * jnp.exp(cum)).astype(BF16)
    vb = v.astype(BF16)

    n_sup = CHUNK // SUP
    ce_sup = [c_end[(SUP // SUB) * (b + 1) - 1] for b in range(rows // SUP)]
    cs_sup = [zero_row if b % n_sup == 0 else ce_sup[b - 1] for b in range(rows // SUP)]

    def rows_of_sup(blocks):
        return jnp.concatenate([jnp.broadcast_to(r, (SUP, LANES)) for r in blocks], axis=0)

    q1s = qs * jnp.exp(cum - rows_of_sup(cs_sup))
    k2s = jnp.exp(rows_of_sup(ce_sup) + lk).astype(BF16)
    odd_sub = (lax.broadcasted_iota(jnp.int32, (rows, LANES), 0) // SUB) % 2 == 1
    q_adj = jnp.where(odd_sub, q1, 0.0)
    q_adj = (jnp.where(head0, q_adj, 0.0).astype(BF16), jnp.where(head0, 0.0, q_adj).astype(BF16))
    q_sup = []
    for j in range(n_sup - 1):
        d_rows = []
        for b in range(rows // SUP):
            if b % n_sup > j:
                d_rows.append(jnp.exp(cs_sup[b] - ce_sup[(b // n_sup) * n_sup + j]))
            else:
                d_rows.append(zero_row)
        qj = q1s * rows_of_sup(d_rows)
        q_sup.append((jnp.where(head0, qj, 0.0).astype(BF16), jnp.where(head0, 0.0, qj).astype(BF16)))
    s_adj, s_sup = [], []
    for c in range(n_chunks):
        qst = jnp.concatenate([chunk(q_adj[0], c), chunk(q_adj[1], c)], axis=0)
        s_adj.append(lax.dot_general(qst, chunk(k2b, c), _NT, preferred_element_type=F32))
        for j in range(n_sup - 1):
            qst = jnp.concatenate([chunk(q_sup[j][0], c), chunk(q_sup[j][1], c)], axis=0)
            s_sup.append(lax.dot_general(qst, chunk(k2s, c), _NT, preferred_element_type=F32))
    yield
    r64s = lax.broadcasted_iota(jnp.int32, (2 * CHUNK, CHUNK), 0) % CHUNK
    prev_sub = c64s // SUB == r64s // SUB - 1
    o_cross = []
    for c in range(n_chunks):
        sc = jnp.where(prev_sub, s_adj[c], 0.0)
        for j in range(n_sup - 1):
            col_in_j = (c64s >= j * SUP) & (c64s < (j + 1) * SUP)
            sc = sc + jnp.where(col_in_j, s_sup[c * (n_sup - 1) + j], 0.0)
        o2 = jnp.dot(sc.astype(BF16), chunk(vb, c), preferred_element_type=F32)
        o_cross.append(jnp.where(head0_c, o2[:CHUNK], o2[CHUNK:]))

    yield
    a = cum - log_k
    w_rows = []
    for b in range(n_blk):
        a_b = a[SUB * b:SUB * (b + 1), :]
        for i in range(SUB):
            r = SUB * b + i
            arg = jnp.where(row_s <= i, cum[r:r + 1, :] - a_b, NEG_BIG)
            w_rows.append(jnp.exp(arg) * qs[r:r + 1, :])
    w_all = jnp.concatenate(w_rows, axis=0)
    head0_w = lax.broadcasted_iota(jnp.int32, w_all.shape, 1) < HEAD_DIM
    sum0 = jnp.sum(jnp.where(head0_w, w_all, 0.0), axis=-1, keepdims=True)
    sum1 = jnp.sum(jnp.where(head0_w, 0.0, w_all), axis=-1, keepdims=True)
    sb = jnp.where(head0_w, sum0, sum1)
    yield
    o_diag = []
    for c in range(n_chunks):
        v_rep = jnp.concatenate([v[SUB * (r // SUB):SUB * (r // SUB + 1), :]
                                 for r in range(c * CHUNK, (c + 1) * CHUNK)], axis=0)
        sb_c = sb[c * CHUNK * SUB:(c + 1) * CHUNK * SUB]
        o_diag.append(jnp.dot(pick, (sb_c * v_rep).astype(BF16), preferred_element_type=F32))

    yield
    upd = [jnp.where(same_head,
                     lax.dot_general(chunk(vb, c), chunk(k_end, c), (((0,), (0,)), ((), ())),
                                     preferred_element_type=F32), 0.0) for c in range(n_chunks)]
    st = st_ref[...]
    o_state = []
    for c in range(n_chunks):
        o_state.append(lax.dot_general(chunk(q_state, c), st.astype(BF16), _NT, preferred_element_type=F32))
        st = st * jnp.exp(c_last[c]) + upd[c]
    st_ref[...] = st

    o = jnp.concatenate([o_cross[c] + o_diag[c] + o_state[c] for c in range(n_chunks)], axis=0)
    osq = o * o
    ms0 = jnp.sum(jnp.where(head0, osq, 0.0), axis=-1, keepdims=True)
    ms1 = jnp.sum(jnp.where(head0, 0.0, osq), axis=-1, keepdims=True)
    ms = jnp.where(head0, ms0, ms1) * (1.0 / HEAD_DIM)
    y = o * lax.rsqrt(ms + EPS) * norm_g
    y_out.append(y * (graw * (1.0 / (1.0 + jnp.exp(-graw)))))
    yield


def _mla_flash_kernel(qt_ref, k0_ref, k1_ref, vt_ref, o_ref, acc_ref, m_ref, l_ref, s_ref, mt_ref, *, n_qt):
    c = C_QSCALE

    def scores(qi, j, e):
        qoff = pl.multiple_of(qi * C_TILE, C_TILE)
        koff = pl.multiple_of(j * C_TILE, C_TILE)
        k = (k0_ref, k1_ref)[e][pl.ds(koff, C_TILE), :]
        qt = qt_ref[e * C_SLOT:(e + 1) * C_SLOT, pl.ds(qoff, C_TILE)]
        return jnp.dot(k, qt, preferred_element_type=F32)

    def put_scores(qi, j, e):
        s = scores(qi, j, e)
        s_ref[e] = s
        mt_ref[e] = jnp.max(s, axis=0, keepdims=True)

    def softmax_pv(j, e, s, m_tile):
        off = pl.multiple_of(j * C_TILE, C_TILE)
        m_prev = m_ref[e]
        m_new = jnp.maximum(m_prev, m_tile)
        p = jnp.exp2((s - m_new) * c)
        alpha = jnp.exp2((m_prev - m_new) * c)
        vt = vt_ref[e * HEAD_DIM:(e + 1) * HEAD_DIM, pl.ds(off, C_TILE)]
        l_ref[e] = alpha * l_ref[e] + jnp.sum(p, axis=0, keepdims=True)
        acc_ref[e] = alpha * acc_ref[e] + jnp.dot(vt, p.astype(BF16), preferred_element_type=F32)
        m_ref[e] = m_new

    kc = lax.broadcasted_iota(jnp.int32, (C_TILE, C_TILE), 0) // CHUNK
    qc = lax.broadcasted_iota(jnp.int32, (C_TILE, C_TILE), 1) // CHUNK

    for e in range(2):
        put_scores(0, 0, e)

    def q_body(qi, carry):
        m_ref[...] = jnp.full_like(m_ref, -jnp.inf)
        l_ref[...] = jnp.zeros_like(l_ref)
        acc_ref[...] = jnp.zeros_like(acc_ref)

        def body(j, carry2):
            for e in range(2):
                s, m_tile = s_ref[e], mt_ref[e]
                put_scores(qi, j + 1, e)
                softmax_pv(j, e, s, m_tile)
            return carry2

        lax.fori_loop(0, qi, body, 0)
        q_next = jnp.minimum(qi + 1, n_qt - 1)
        for e in range(2):
            s = jnp.where(kc <= qc, s_ref[e], NEG_BIG)
            put_scores(q_next, 0, e)
            softmax_pv(qi, e, s, jnp.max(s, axis=0, keepdims=True))

        o_t = jnp.concatenate([acc_ref[0] / l_ref[0], acc_ref[1] / l_ref[1]], axis=0)
        o_ref[pl.ds(pl.multiple_of(qi * C_TILE, C_TILE), C_TILE), :] = o_t.T.astype(o_ref.dtype)
        return carry

    lax.fori_loop(0, n_qt, q_body, 0)


def mla_flash(qt, k, vt, bsz, seq):
    t = bsz * seq
    return pl.pallas_call(
        functools.partial(_mla_flash_kernel, n_qt=seq // C_TILE),
        out_shape=jax.ShapeDtypeStruct((t, WIDTH), BF16),
        grid=(bsz, N_PAIRS),
        in_specs=[
            pl.BlockSpec((2 * C_SLOT, seq), lambda b, p: (p, b)),
            pl.BlockSpec((seq, C_SLOT), lambda b, p: (b, 2 * p)),
            pl.BlockSpec((seq, C_SLOT), lambda b, p: (b, 2 * p + 1)),
            pl.BlockSpec((LANES, seq), lambda b, p: (p, b)),
        ],
        out_specs=pl.BlockSpec((seq, LANES), lambda b, p: (b, p)),
        scratch_shapes=[pltpu.VMEM((2, HEAD_DIM, C_TILE), F32),
                        pltpu.VMEM((2, 1, C_TILE), F32),
                        pltpu.VMEM((2, 1, C_TILE), F32),
                        pltpu.VMEM((2, C_TILE, C_TILE), F32),
                        pltpu.VMEM((2, 1, C_TILE), F32)],
        compiler_params=_cparams(("parallel", "parallel")),
        name="mla_flash",
    )(qt, k, k, vt)


def _merge_kernel(x_ref, ya_ref, yb_ref, yc_ref, ga_ref, gb_ref, gc_ref, wbr_ref, wout_ref, o_ref):
    merged = None
    for n, (y_ref, g_ref) in enumerate(((ya_ref, ga_ref), (yb_ref, gb_ref), (yc_ref, gc_ref))):
        up = jnp.dot(y_ref[...], wbr_ref[n], preferred_element_type=F32)
        gl = g_ref[...].astype(F32)
        term = (0.5 * jnp.tanh(0.5 * gl) + 0.5) * up
        merged = term if merged is None else merged + term
    o_ref[...] = x_ref[...] + jnp.dot(merged.astype(BF16), wout_ref[...], preferred_element_type=F32)


def merge_out(x, ya, yb, yc, gates, wbr, wout, tm):
    t = x.shape[0]
    return pl.pallas_call(
        _merge_kernel,
        out_shape=jax.ShapeDtypeStruct((t, D_MODEL), F32),
        grid=(t // tm,),
        in_specs=[
            pl.BlockSpec((tm, D_MODEL), lambda i: (i, 0)),
            pl.BlockSpec((tm, WIDTH), lambda i: (i, 0)),
            pl.BlockSpec((tm, WIDTH), lambda i: (i, 0)),
            pl.BlockSpec((tm, WIDTH), lambda i: (i, 0)),
            pl.BlockSpec((tm, D_MODEL), lambda i: (i, 0)),
            pl.BlockSpec((tm, D_MODEL), lambda i: (i, 1)),
            pl.BlockSpec((tm, D_MODEL), lambda i: (i, 2)),
            _resident(wbr),
            _resident(wout),
        ],
        out_specs=pl.BlockSpec((tm, D_MODEL), lambda i: (i, 0)),
        compiler_params=_cparams(("parallel",)),
        name="merge_out",
    )(x, ya, yb, yc, gates, gates, gates, wbr, wout)


def _ffn_kernel(x_ref, g_ref, w1_ref, w2_ref, gf_ref, o_ref, *, final_norm, tf):
    x = x_ref[...]
    h = _rms(x, g_ref[...]).astype(BF16)
    acc = None
    for k in range(w1_ref.shape[1] // tf):
        u = jnp.maximum(jnp.dot(h, w1_ref[:, k * tf:(k + 1) * tf], preferred_element_type=F32), 0.0)
        part = jnp.dot((u * u).astype(BF16), w2_ref[k * tf:(k + 1) * tf, :], preferred_element_type=F32)
        acc = part if acc is None else acc + part
    y = x + acc
    if final_norm:
        y = _rms(y, gf_ref[...])
    o_ref[...] = y


def ffn(x, g, w1, w2, gf, final_norm, tm, tf):
    t, d = x.shape
    return pl.pallas_call(
        functools.partial(_ffn_kernel, final_norm=final_norm, tf=tf),
        out_shape=jax.ShapeDtypeStruct((t, d), F32),
        grid=(t // tm,),
        in_specs=[
            pl.BlockSpec((tm, d), lambda i: (i, 0)),
            pl.BlockSpec((1, d), lambda i: (0, 0)),
            _resident(w1),
            _resident(w2),
            pl.BlockSpec((1, d), lambda i: (0, 0)),
        ],
        out_specs=pl.BlockSpec((tm, d), lambda i: (i, 0)),
        compiler_params=_cparams(("parallel",)),
        name="ffn",
    )(x, g.reshape(1, d), w1, w2, gf.reshape(1, d))


COL_AQ, COL_AK, COL_AV = 0, 4, 8
COL_BQ, COL_BI, COL_BG = 12, 16, 20
GATE_BLK = 1


def _prep_w_in(w):
    o = 0
    parts = {}
    for name, size in (("aq", 512), ("ak", 512), ("av", 512), ("bq", 512), ("bf", 512), ("bi", 512),
                       ("bg", 512), ("cq", C_Q_RANK), ("ckv", C_KV_RANK), ("ckr", C_ROPE),
                       ("gate", N_BRANCH * D_MODEL)):
        parts[name] = w[:, o:o + size]
        o += size
    w16 = jnp.concatenate([parts[n] for n in ("aq", "ak", "av", "bq", "bi", "bg", "gate")], axis=1)
    half = C_ROPE // 2
    x1, x2 = parts["ckr"][:, :half], parts["ckr"][:, half:]
    zl = jnp.zeros((w.shape[0], C_NOPE), w.dtype)
    zr = jnp.zeros((w.shape[0], C_SLOT - C_NOPE - C_ROPE), w.dtype)
    kr_slot = jnp.concatenate([zl, x1, x2, zr], axis=1)
    kr_sw = jnp.concatenate([zl, -x2, x1, zr], axis=1)
    w32 = jnp.concatenate([parts["cq"], parts["ckv"], kr_slot, kr_sw, parts["bf"]], axis=1)
    return w16.astype(BF16), w32.astype(BF16)


def _prep_w_mla(w_uq, w_ukv):
    half = C_ROPE // 2
    per_q = C_NOPE + C_ROPE
    wq = w_uq.reshape(C_Q_RANK, N_HEADS, per_q)
    nope, x1, x2 = wq[..., :C_NOPE], wq[..., C_NOPE:C_NOPE + half], wq[..., C_NOPE + half:]
    zpad = jnp.zeros((C_Q_RANK, N_HEADS, C_SLOT - per_q), w_uq.dtype)
    wq1 = jnp.concatenate([nope, x1, x2, zpad], axis=-1).reshape(C_Q_RANK, N_HEADS * C_SLOT)
    wq2 = jnp.concatenate([jnp.zeros_like(nope), -x2, x1, zpad], axis=-1).reshape(C_Q_RANK, N_HEADS * C_SLOT)
    wkv = w_ukv.reshape(C_KV_RANK, N_HEADS, C_NOPE + HEAD_DIM)
    k_nope, v = wkv[..., :C_NOPE], wkv[..., C_NOPE:]
    wk = jnp.concatenate([k_nope, jnp.zeros((C_KV_RANK, N_HEADS, C_SLOT - C_NOPE), w_ukv.dtype)],
                         axis=-1).reshape(C_KV_RANK, N_HEADS * C_SLOT)
    wv = v.reshape(C_KV_RANK, WIDTH)
    return wq1.T.astype(BF16), wq2.T.astype(BF16), wk.astype(BF16), wv.T.astype(BF16)


def kernel(x, positions, norm_mix_g, w_in, rel_bias, hgrn_lb_logits, hgrn_norm_g, mla_q_norm_g,
           mla_kv_norm_g, mla_w_uq, mla_w_ukv, w_branch, w_out, norm_ffn_g, w_ff1, w_ff2, final_norm_g):
    bsz, seq, d = x.shape
    depth = w_in.shape[0]
    t = bsz * seq
    assert d == D_MODEL and seq % C_TILE == 0 and seq % (A_QTILE * A_TILES_PER_STEP) == 0
    tm = min(TM_DENSE, t)

    p_lb = jax.nn.softmax(hgrn_lb_logits.astype(F32), axis=0)
    lb_all = jnp.cumsum(p_lb, axis=0)
    lb_all = lb_all - lb_all[0:1]

    tabs = rope_tables(positions, tm)
    xf = x.reshape(t, d)
    for l in range(depth):
        w16, w32 = _prep_w_in(w_in[l])
        wq1t, wq2t, wk, wvt = _prep_w_mla(mla_w_uq[l], mla_w_ukv[l])
        proj_a, gates, y_b, qt_c, k_c, vt_c = in_proj(xf, norm_mix_g[l], w16, w32, tabs, mla_q_norm_g[l], mla_kv_norm_g[l],
                                               wq1t, wq2t, wk, wvt, lb_all[l], hgrn_norm_g[l], seq,
                                               min(TM_IN_PROJ, seq), TN_IN_PROJ)

        y_a = band_attention(proj_a, band_bias_table(rel_bias[l]), bsz, seq, COL_AQ, COL_AK, COL_AV)
        y_c = mla_flash(qt_c, k_c, vt_c, bsz, seq)

        xf = merge_out(xf, y_a, y_b, y_c, gates, w_branch[l].astype(BF16), w_out[l].astype(BF16), tm)
        xf = ffn(xf, norm_ffn_g[l], w_ff1[l].astype(BF16), w_ff2[l].astype(BF16), final_norm_g,
                 l == depth - 1, tm, TF_FFN)
    return xf.reshape(bsz, seq, d)
```
